```python
import math
import jax, jax.numpy as jnp
from jax import lax
import numpy as np

D_MODEL = 1024
BATCH = 4
SEQ = 8192
DEPTH = 1

GRID_W = 64
HEAD_DIM = 64
N_Q_HEADS = 8
N_KV_HEADS = 2
Q_PER_KV = N_Q_HEADS // N_KV_HEADS
ATTN_WIDTH = N_Q_HEADS * HEAD_DIM
KV_WIDTH = N_KV_HEADS * HEAD_DIM
ROPE_THETA = 10000.0
Q_BLOCK = 128
HYENA_WIDTH = D_MODEL // 2
HYENA_ORDER = 2
SHORT_CONV = 3
FILTER_EMB = 33
FILTER_HIDDEN = 64
FAST_DECAY_PCT = 0.3
SLOW_DECAY_PCT = 1.5
DECAY_TARGET = 1e-2
N_BRANCHES = 2
IN_COLS = 3 * HYENA_WIDTH + ATTN_WIDTH + 2 * KV_WIDTH + N_BRANCHES * D_MODEL
N_EXPERTS = 32
TOP_K = 4
D_FF = D_MODEL
SWIGLU_LIMIT = 7.0
SWIGLU_ALPHA = 1.702
MOE_BLOCK = 256
EPS = 1e-6

kernel_name = "hybrid_hyena_gqa_axialrope_moe_encoder"


def rmsnorm(x, g):
    xf = x.astype(jnp.float32)
    y = xf * lax.rsqrt(jnp.mean(xf * xf, axis=-1, keepdims=True) + EPS)
    return (y * g.astype(jnp.float32)).astype(x.dtype)


def short_conv(z, w, b):
    L = z.shape[1]
    pad = SHORT_CONV // 2
    zp = jnp.pad(z, ((0, 0), (pad, pad), (0, 0)))
    out = b
    for k in range(SHORT_CONV):
        out = out + zp[:, k:k + L] * w[k]
    return out


def hyena_filters(L, fw1, fb1, ff1, fw2, fb2, ff2, fw3, fb3):
    f32 = jnp.float32
    t = jnp.linspace(0.0, 1.0, L, dtype=f32)[:, None]
    bands = (FILTER_EMB - 1) // 2
    w = 2.0 * math.pi * jnp.arange(L, dtype=f32)[:, None] / L
    fr = jnp.linspace(1e-4, bands - 1, bands, dtype=f32)[None, :]
    feats = jnp.concatenate([t, jnp.cos(w * fr), -jnp.sin(w * fr)], axis=-1)
    h = jnp.sin(ff1.astype(f32) * (feats @ fw1.astype(f32) + fb1.astype(f32)))
    h = jnp.sin(ff2.astype(f32) * (h @ fw2.astype(f32) + fb2.astype(f32)))
    h = h @ fw3.astype(f32) + fb3.astype(f32)
    h = h.reshape(L, HYENA_ORDER, 2, HYENA_WIDTH)
    max_decay = math.log(DECAY_TARGET) / FAST_DECAY_PCT
    min_decay = math.log(DECAY_TARGET) / SLOW_DECAY_PCT
    deltas = jnp.abs(jnp.linspace(min_decay, max_decay, HYENA_WIDTH, dtype=f32))
    h = h * jnp.exp(-t[:, :, None, None] * deltas)
    fwd, bwd = h[:, :, 0], h[:, :, 1]
    kern = jnp.concatenate([fwd, jnp.zeros((1, HYENA_ORDER, HYENA_WIDTH), f32), bwd[1:][::-1]], axis=0)
    return kern / jnp.sum(jnp.abs(kern), axis=0, keepdims=True)


def hyena_mixer(z, conv_w, conv_b, fw1, fb1, ff1, fw2, fb2, ff2, fw3, fb3, hyena_d):
    L = z.shape[1]
    zc = short_conv(z, conv_w, conv_b)
    v, x1, x2 = jnp.split(zc, 3, axis=-1)
    kern = hyena_filters(L, fw1, fb1, ff1, fw2, fb2, ff2, fw3, fb3)
    K = jnp.fft.rfft(kern, axis=0)
    y = v.astype(jnp.float32)
    for o, gate in enumerate((x1, x2)):
        Y = jnp.fft.rfft(y, n=2 * L, axis=1)
        conv = jnp.fft.irfft(Y * K[None, :, o, :], n=2 * L, axis=1)[:, :L]
        y = gate.astype(jnp.float32) * (conv + y * hyena_d[o].astype(jnp.float32))
    return y.astype(z.dtype)


def axial_rope_tables(L):
    rows = L // GRID_W
    row = jnp.repeat(jnp.arange(rows, dtype=jnp.float32), GRID_W)
    col = jnp.tile(jnp.arange(GRID_W, dtype=jnp.float32), rows)
    half = HEAD_DIM // 2
    freqs = ROPE_THETA ** (-jnp.arange(0, half, 2, dtype=jnp.float32) / half)
    ang = jnp.concatenate([row[:, None] * freqs, col[:, None] * freqs], axis=-1)
    return jnp.cos(ang), jnp.sin(ang)


def apply_rope(x, cos, sin):
    xr = x.reshape(x.shape[:-1] + (HEAD_DIM // 2, 2))
    x0, x1 = xr[..., 0], xr[..., 1]
    c, s = cos[None, :, None, :], sin[None, :, None, :]
    return jnp.stack([x0 * c - x1 * s, x0 * s + x1 * c], axis=-1).reshape(x.shape)


def head_rmsnorm(x, g):
    xf = x.astype(jnp.float32)
    return xf * lax.rsqrt(jnp.mean(xf * xf, axis=-1, keepdims=True) + EPS) * g.astype(jnp.float32)


def attention_mixer(q, k, v, q_norm_g, k_norm_g):
    B, L, _ = q.shape
    q = head_rmsnorm(q.reshape(B, L, N_Q_HEADS, HEAD_DIM), q_norm_g)
    k = head_rmsnorm(k.reshape(B, L, N_KV_HEADS, HEAD_DIM), k_norm_g)
    v = v.reshape(B, L, N_KV_HEADS, HEAD_DIM).astype(jnp.float32)
    cos, sin = axial_rope_tables(L)
    q = apply_rope(q, cos, sin) * (HEAD_DIM ** -0.5)
    k = apply_rope(k, cos, sin)
    nb = L // Q_BLOCK
    qb = q.reshape(B, nb, Q_BLOCK, N_KV_HEADS, Q_PER_KV, HEAD_DIM).transpose(1, 0, 2, 3, 4, 5)

    def attend(q_blk):
        s = jnp.einsum('bqkgd,bskd->bkgqs', q_blk, k)
        p = jax.nn.softmax(s, axis=-1)
        return jnp.einsum('bkgqs,bskd->bqkgd', p, v)

    o = lax.map(attend, qb)
    o = o.transpose(1, 0, 2, 3, 4, 5).reshape(B, L, ATTN_WIDTH)
    return o


def clamped_swiglu(h):
    x_glu, x_lin = h[..., ::2], h[..., 1::2]
    x_glu = jnp.minimum(x_glu, SWIGLU_LIMIT)
    x_lin = jnp.clip(x_lin, -SWIGLU_LIMIT, SWIGLU_LIMIT)
    return x_glu * jax.nn.sigmoid(SWIGLU_ALPHA * x_glu) * (x_lin + 1.0)


def moe(h, router_w, router_b, w1, b1, w2, b2):
    B, L, D = h.shape
    T = B * L
    xt = h.reshape(T, D)
    logits = (xt @ router_w + router_b).astype(jnp.float32)
    top_val, top_idx = lax.top_k(logits, TOP_K)
    gates = jax.nn.softmax(top_val, axis=-1)
    TK = T * TOP_K
    e_flat = top_idx.reshape(TK)
    tok_flat = jnp.arange(TK, dtype=jnp.int32) // TOP_K
    g_flat = gates.reshape(TK)
    order = jnp.argsort(e_flat)
    e_s, tok_s, g_s = e_flat[order], tok_flat[order], g_flat[order]
    counts = jnp.bincount(e_flat, length=N_EXPERTS)
    starts = jnp.cumsum(counts) - counts
    padded = ((counts + MOE_BLOCK - 1) // MOE_BLOCK) * MOE_BLOCK
    pad_ends = jnp.cumsum(padded)
    pad_starts = pad_ends - padded
    dest = pad_starts[e_s] + jnp.arange(TK, dtype=jnp.int32) - starts[e_s]
    n_blocks = (TK + N_EXPERTS * (MOE_BLOCK - 1) + MOE_BLOCK - 1) // MOE_BLOCK
    P = n_blocks * MOE_BLOCK
    src = jnp.zeros((P,), jnp.int32).at[dest].set(tok_s)
    gpad = jnp.zeros((P,), jnp.float32).at[dest].set(g_s)
    block_start = jnp.arange(n_blocks, dtype=jnp.int32) * MOE_BLOCK
    block_expert = jnp.minimum(jnp.sum(block_start[:, None] >= pad_ends[None, :], axis=1),
                               N_EXPERTS - 1).astype(jnp.int32)
    xg = xt[src].reshape(n_blocks, MOE_BLOCK, D)

    def expert_block(args):
        xb, e = args
        hb = clamped_swiglu(xb @ w1[e] + b1[e])
        return hb @ w2[e] + b2[e]

    y = lax.map(expert_block, (xg, block_expert)).reshape(P, D)
    y = y.astype(jnp.float32) * gpad[:, None]
    out = jax.ops.segment_sum(y, src, num_segments=T)
    return out.reshape(B, L, D).astype(h.dtype)


def setup_inputs(seed: int = 0) -> dict:
    key = jax.random.key(seed)
    ks = jax.random.split(key, 32)
    f32 = jnp.float32

    def nrm(k, shape, scale):
        return jax.random.normal(k, shape, f32) * scale

    Dp = DEPTH
    return {
        "x": jax.random.normal(ks[0], (BATCH, SEQ, D_MODEL), f32),
        "norm1_g": 1.0 + nrm(ks[1], (Dp, D_MODEL), 0.02),
        "w_in": nrm(ks[2], (Dp, D_MODEL, IN_COLS), D_MODEL ** -0.5),
        "conv_w": nrm(ks[3], (Dp, SHORT_CONV, 3 * HYENA_WIDTH), SHORT_CONV ** -0.5),
        "conv_b": nrm(ks[4], (Dp, 3 * HYENA_WIDTH), 0.02),
        "filt_w1": nrm(ks[5], (Dp, FILTER_EMB, FILTER_HIDDEN), FILTER_EMB ** -0.5),
        "filt_b1": nrm(ks[6], (Dp, FILTER_HIDDEN), 0.1),
        "filt_freq1": 1.0 + nrm(ks[7], (Dp, FILTER_HIDDEN), 0.05),
        "filt_w2": nrm(ks[8], (Dp, FILTER_HIDDEN, FILTER_HIDDEN), FILTER_HIDDEN ** -0.5),
        "filt_b2": nrm(ks[9], (Dp, FILTER_HIDDEN), 0.1),
        "filt_freq2": 1.0 + nrm(ks[10], (Dp, FILTER_HIDDEN), 0.05),
        "filt_w3": nrm(ks[11], (Dp, FILTER_HIDDEN, HYENA_ORDER * 2 * HYENA_WIDTH), FILTER_HIDDEN ** -0.5),
        "filt_b3": nrm(ks[12], (Dp, HYENA_ORDER * 2 * HYENA_WIDTH), 0.02),
        "hyena_d": nrm(ks[13], (Dp, HYENA_ORDER, HYENA_WIDTH), 0.5),
        "q_norm_g": 1.0 + nrm(ks[14], (Dp, HEAD_DIM), 0.02),
        "k_norm_g": 1.0 + nrm(ks[15], (Dp, HEAD_DIM), 0.02),
        "w_hyena_up": nrm(ks[16], (Dp, HYENA_WIDTH, D_MODEL), HYENA_WIDTH ** -0.5),
        "w_attn_up": nrm(ks[17], (Dp, ATTN_WIDTH, D_MODEL), ATTN_WIDTH ** -0.5),
        "w_out": nrm(ks[18], (Dp, D_MODEL, D_MODEL), D_MODEL ** -0.5),
        "norm2_g": 1.0 + nrm(ks[19], (Dp, D_MODEL), 0.02),
        "router_w": nrm(ks[20], (Dp, D_MODEL, N_EXPERTS), D_MODEL ** -0.5),
        "router_b": nrm(ks[21], (Dp, N_EXPERTS), 0.01),
        "expert_w1": nrm(ks[22], (Dp, N_EXPERTS, D_MODEL, 2 * D_FF), D_MODEL ** -0.5),
        "expert_b1": nrm(ks[23], (Dp, N_EXPERTS, 2 * D_FF), 0.02),
        "expert_w2": nrm(ks[24], (Dp, N_EXPERTS, D_FF, D_MODEL), D_FF ** -0.5),
        "expert_b2": nrm(ks[25], (Dp, N_EXPERTS, D_MODEL), 0.02),
    }


def reference(x, norm1_g, w_in, conv_w, conv_b, filt_w1, filt_b1, filt_freq1, filt_w2, filt_b2,
              filt_freq2, filt_w3, filt_b3, hyena_d, q_norm_g, k_norm_g, w_hyena_up, w_attn_up,
              w_out, norm2_g, router_w, router_b, expert_w1, expert_b1, expert_w2, expert_b2):
    o_q = 3 * HYENA_WIDTH
    o_k = o_q + ATTN_WIDTH
    o_v = o_k + KV_WIDTH
    o_g = o_v + KV_WIDTH
    for layer in range(DEPTH):
        u = rmsnorm(x, norm1_g[layer])
        proj = u @ w_in[layer]
        y_hy = hyena_mixer(proj[..., :o_q], conv_w[layer], conv_b[layer],
                           filt_w1[layer], filt_b1[layer], filt_freq1[layer],
                           filt_w2[layer], filt_b2[layer], filt_freq2[layer],
                           filt_w3[layer], filt_b3[layer], hyena_d[layer])
        y_at = attention_mixer(proj[..., o_q:o_k], proj[..., o_k:o_v], proj[..., o_v:o_g],
                               q_norm_g[layer], k_norm_g[layer]).astype(x.dtype)
        g_hy = jax.nn.sigmoid(proj[..., o_g:o_g + D_MODEL])
        g_at = jax.nn.sigmoid(proj[..., o_g + D_MODEL:])
        merged = g_hy * (y_hy @ w_hyena_up[layer]) + g_at * (y_at @ w_attn_up[layer])
        x = x + merged @ w_out[layer]
        hn = rmsnorm(x, norm2_g[layer])
        x = x + moe(hn, router_w[layer], router_b[layer], expert_w1[layer], expert_b1[layer],
                    expert_w2[layer], expert_b2[layer])
    return x
```

```python
import functools
import math

import jax
import jax.numpy as jnp
from jax import lax
from jax.experimental import pallas as pl
from jax.experimental.pallas import tpu as pltpu

F32 = jnp.float32
BF16 = jnp.bfloat16

GRID_W = 64
HEAD_DIM = 64
N_Q_HEADS = 8
N_KV_HEADS = 2
Q_PER_KV = N_Q_HEADS // N_KV_HEADS
ROPE_THETA = 10000.0
HYENA_ORDER = 2
SHORT_CONV = 3
FILTER_EMB = 33
FAST_DECAY_PCT = 0.3
SLOW_DECAY_PCT = 1.5
DECAY_TARGET = 1e-2
N_EXPERTS = 32
TOP_K = 4
SWIGLU_LIMIT = 7.0
SWIGLU_ALPHA = 1.702
EPS = 1e-6

LANES = 128
SUBLANES = 8
VMEM_LIMIT = 56 * 1024 * 1024

FFT_N2 = LANES
FFT_PITCH = FFT_N2 + SUBLANES
FFT_K1_CHUNK = 16
MOE_TM = 512
LOGIT_PAD = LANES


def _cparams(sem):
    return pltpu.CompilerParams(dimension_semantics=sem, vmem_limit_bytes=VMEM_LIMIT)


def _split_bf16(x):
    hi = x.astype(BF16)
    lo = (x - hi.astype(F32)).astype(BF16)
    return hi, lo


def _inproj_body(x_ref, g_ref, w_ref, z_ref, q_ref, kv_ref, gate_ref, *, widths):
    x = x_ref[...]
    ms = jnp.mean(x * x, axis=-1, keepdims=True)
    u = (x * lax.rsqrt(ms + EPS) * g_ref[...]).astype(BF16)
    off = 0
    for ref, w in zip((z_ref, q_ref, kv_ref, gate_ref), widths):
        ref[...] = jnp.dot(u, w_ref[:, off:off + w], preferred_element_type=F32)
        off += w


def _inproj(xt, g, w_bf16, widths, tm=512):
    T, D = xt.shape
    n = w_bf16.shape[1]
    return pl.pallas_call(
        functools.partial(_inproj_body, widths=widths),
        grid=(T // tm,),
        in_specs=[
            pl.BlockSpec((tm, D), lambda i: (i, 0)),
            pl.BlockSpec((1, D), lambda i: (0, 0)),
            pl.BlockSpec((D, n), lambda i: (0, 0)),
        ],
        out_specs=[pl.BlockSpec((tm, w), lambda i: (i, 0)) for w in widths],
        out_shape=[jax.ShapeDtypeStruct((T, w), F32) for w in widths],
        compiler_params=_cparams(("parallel",)),
        name="inproj",
    )(xt, g.reshape(1, D), w_bf16)


def _head_norm_rope(x, gain, cosf, sinf, ones_blk):
    w = x.shape[-1]
    hi, lo = _split_bf16(x * x)
    ss = (jnp.dot(hi, ones_blk, preferred_element_type=F32)
          + jnp.dot(lo, ones_blk, preferred_element_type=F32))
    xn = x * lax.rsqrt(ss * (1.0 / HEAD_DIM) + EPS) * gain
    lane = lax.broadcasted_iota(jnp.int32, (x.shape[0], LANES), 1)
    cols = []
    for c in range(w // LANES):
        col = xn[:, c * LANES:(c + 1) * LANES]
        cols.append(jnp.where(lane % 2 == 0, pltpu.roll(col, LANES - 1, 1), pltpu.roll(col, 1, 1)))
    swapped = cols[0] if len(cols) == 1 else jnp.concatenate(cols, axis=1)
    return xn * cosf + swapped * sinf


def _qkrope_body(q_ref, kv_ref, gq_ref, gk_ref, cos_ref, sin_ref, oq_ref, ok_ref,
                 qo_ref, ko_ref, vo_ref):
    cosf = cos_ref[...]
    sinf = sin_ref[...]
    nq = q_ref.shape[-1] // LANES
    q = _head_norm_rope(q_ref[...], gq_ref[...], jnp.tile(cosf, (1, nq)), jnp.tile(sinf, (1, nq)),
                        oq_ref[...])
    qo_ref[...] = (q * (HEAD_DIM ** -0.5)).astype(BF16)
    kv = kv_ref[...]
    kw = ko_ref.shape[-1]
    k = _head_norm_rope(kv[:, :kw], gk_ref[...], cosf, sinf, ok_ref[...])
    ko_ref[...] = k.astype(BF16)
    vo_ref[...] = kv[:, kw:].astype(BF16)


def _qkrope(q, kv, gq, gk, cosf, sinf, seq_len, tm=512):
    T, qw = q.shape
    kw = kv.shape[1] // 2
    nl = seq_len // tm

    def blk_ones(w):
        r = jnp.arange(w) // HEAD_DIM
        return (r[:, None] == r[None, :]).astype(BF16)

    return pl.pallas_call(
        _qkrope_body,
        grid=(T // tm,),
        in_specs=[
            pl.BlockSpec((tm, qw), lambda i: (i, 0)),
            pl.BlockSpec((tm, 2 * kw), lambda i: (i, 0)),
            pl.BlockSpec((1, qw), lambda i: (0, 0)),
            pl.BlockSpec((1, kw), lambda i: (0, 0)),
            pl.BlockSpec((tm, LANES), lambda i: (i % nl, 0)),
            pl.BlockSpec((tm, LANES), lambda i: (i % nl, 0)),
            pl.BlockSpec((qw, qw), lambda i: (0, 0)),
            pl.BlockSpec((kw, kw), lambda i: (0, 0)),
        ],
        out_specs=[
            pl.BlockSpec((tm, qw), lambda i: (i, 0)),
            pl.BlockSpec((tm, kw), lambda i: (i, 0)),
            pl.BlockSpec((tm, kw), lambda i: (i, 0)),
        ],
        out_shape=[
            jax.ShapeDtypeStruct((T, qw), BF16),
            jax.ShapeDtypeStruct((T, kw), BF16),
            jax.ShapeDtypeStruct((T, kw), BF16),
        ],
        compiler_params=_cparams(("parallel",)),
        name="qkrope",
    )(q, kv, gq, gk, cosf, sinf, blk_ones(qw), blk_ones(kw))


def _attn_body(q_ref, k_ref, v_ref, o_ref, qs_ref, m_ref, l_ref, acc_ref):
    kvh = pl.program_id(1)
    j = pl.program_id(3)
    tq = q_ref.shape[0]
    lane = lax.broadcasted_iota(jnp.int32, (tq, LANES), 1)
    in_head = (lane // HEAD_DIM) == kvh

    @pl.when(j == 0)
    def _():
        for g in range(Q_PER_KV):
            col = q_ref[:, (g // 2) * LANES:(g // 2 + 1) * LANES].astype(F32)
            col = jnp.where((g % 2) == kvh, col, pltpu.roll(col, HEAD_DIM, 1))
            qs_ref[g * tq:(g + 1) * tq, :] = jnp.where(in_head, col, 0.0).astype(BF16)
        m_ref[...] = jnp.full(m_ref.shape, -jnp.inf, F32)
        l_ref[...] = jnp.zeros(l_ref.shape, F32)
        acc_ref[...] = jnp.zeros(acc_ref.shape, F32)

    s = lax.dot_general(qs_ref[...], k_ref[...], (((1,), (1,)), ((), ())),
                        preferred_element_type=F32)
    m_prev = m_ref[...]
    m_new = jnp.maximum(m_prev, jnp.max(s, axis=-1, keepdims=True))
    alpha = jnp.exp(m_prev - m_new)
    p = jnp.exp(s - m_new)
    l_ref[...] = alpha * l_ref[...] + jnp.sum(p, axis=-1, keepdims=True)
    acc_ref[...] = alpha * acc_ref[...] + jnp.dot(p.astype(BF16), v_ref[...],
                                                  preferred_element_type=F32)
    m_ref[...] = m_new

    @pl.when(j == pl.num_programs(3) - 1)
    def _():
        o = acc_ref[...] / l_ref[...]
        for c in range(Q_PER_KV // 2):
            even = o[(2 * c) * tq:(2 * c + 1) * tq, :]
            odd = o[(2 * c + 1) * tq:(2 * c + 2) * tq, :]
            even = jnp.where(kvh == 0, even, pltpu.roll(even, HEAD_DIM, 1))
            odd = jnp.where(kvh == 1, odd, pltpu.roll(odd, HEAD_DIM, 1))
            o_ref[:, c * LANES:(c + 1) * LANES] = jnp.where(lane < HEAD_DIM, even, odd)


def _attention(q, k, v, tq=256, tk=1024):
    B, L, qw = q.shape
    gw = Q_PER_KV * HEAD_DIM
    kw = k.shape[-1]
    return pl.pallas_call(
        _attn_body,
        grid=(B, N_KV_HEADS, L // tq, L // tk),
        in_specs=[
            pl.BlockSpec((None, tq, gw), lambda b, h, i, j: (b, i, h)),
            pl.BlockSpec((None, tk, kw), lambda b, h, i, j: (b, j, 0)),
            pl.BlockSpec((None, tk, kw), lambda b, h, i, j: (b, j, 0)),
        ],
        out_specs=pl.BlockSpec((None, tq, gw), lambda b, h, i, j: (b, i, h)),
        out_shape=jax.ShapeDtypeStruct((B, L, qw), F32),
        scratch_shapes=[
            pltpu.VMEM((Q_PER_KV * tq, kw), BF16),
            pltpu.VMEM((Q_PER_KV * tq, 1), F32),
            pltpu.VMEM((Q_PER_KV * tq, 1), F32),
            pltpu.VMEM((Q_PER_KV * tq, kw), F32),
        ],
        compiler_params=_cparams(("parallel", "parallel", "parallel", "arbitrary")),
        name="attention",
    )(q, k, v)


def _hdot(a, b):
    ah, al = _split_bf16(a)
    bh, bl = _split_bf16(b)
    return (jnp.dot(ah, bh, preferred_element_type=F32)
            + jnp.dot(al, bh, preferred_element_type=F32)
            + jnp.dot(ah, bl, preferred_element_type=F32))


def _filter_body(feat_ref, t_ref, w1_ref, b1_ref, f1_ref, w2_ref, b2_ref, f2_ref, w3_ref, b3_ref,
                 delta_ref, bwd_ref, h_ref, sum_ref):
    i = pl.program_id(0)
    h = jnp.sin(f1_ref[...] * (_hdot(feat_ref[...], w1_ref[...]) + b1_ref[...]))
    h = jnp.sin(f2_ref[...] * (_hdot(h, w2_ref[...]) + b2_ref[...]))
    h = _hdot(h, w3_ref[...]) + b3_ref[...]
    t = t_ref[...]
    h = h * jnp.exp(-t * delta_ref[...])
    h_ref[...] = h
    row = lax.broadcasted_iota(jnp.int32, h.shape, 0) + i * h.shape[0]
    a = jnp.where((row == 0) & (bwd_ref[...] > 0.5), 0.0, jnp.abs(h))
    part = jnp.sum(a, axis=0, keepdims=True)

    @pl.when(i == 0)
    def _():
        sum_ref[...] = jnp.zeros(sum_ref.shape, F32)

    sum_ref[...] += jnp.broadcast_to(part, sum_ref.shape)


def _hyena_filter(L, width, fw1, fb1, ff1, fw2, fb2, ff2, fw3, fb3, tl=512):
    bands = (FILTER_EMB - 1) // 2
    t = jnp.linspace(0.0, 1.0, L, dtype=F32)[:, None]
    w = 2.0 * math.pi * jnp.arange(L, dtype=F32)[:, None] / L
    fr = jnp.linspace(1e-4, bands - 1, bands, dtype=F32)[None, :]
    feats = jnp.concatenate([t, jnp.cos(w * fr), -jnp.sin(w * fr)], axis=-1)
    max_decay = math.log(DECAY_TARGET) / FAST_DECAY_PCT
    min_decay = math.log(DECAY_TARGET) / SLOW_DECAY_PCT
    deltas = jnp.abs(jnp.linspace(min_decay, max_decay, width, dtype=F32))
    ncol = fw3.shape[1]
    delta_cols = jnp.tile(deltas, ncol // width)[None, :]
    is_bwd = ((jnp.arange(ncol) // width) % 2).astype(F32)[None, :]
    emb = hid = LANES

    def pad2(a, r, c):
        a = a.astype(F32)
        return jnp.zeros((r, c), F32).at[:a.shape[0], :a.shape[1]].set(a)

    row = lambda a: pad2(a.reshape(1, -1), 1, hid)
    feats = pad2(feats, L, emb)
    fw1, fw2, fw3 = pad2(fw1, emb, hid), pad2(fw2, hid, hid), pad2(fw3, hid, ncol)
    fb3 = fb3.reshape(1, ncol).astype(F32)
    const = lambda shape: pl.BlockSpec(shape, lambda i: (0, 0))
    h, sums = pl.pallas_call(
        _filter_body,
        grid=(L // tl,),
        in_specs=[
            pl.BlockSpec((tl, emb), lambda i: (i, 0)),
            pl.BlockSpec((tl, 1), lambda i: (i, 0)),
            const((emb, hid)), const((1, hid)), const((1, hid)),
            const((hid, hid)), const((1, hid)), const((1, hid)),
            const((hid, ncol)), const((1, ncol)), const((1, ncol)), const((1, ncol)),
        ],
        out_specs=[pl.BlockSpec((tl, ncol), lambda i: (i, 0)),
                   pl.BlockSpec((SUBLANES, ncol), lambda i: (0, 0))],
        out_shape=[jax.ShapeDtypeStruct((L, ncol), F32),
                   jax.ShapeDtypeStruct((SUBLANES, ncol), F32)],
        compiler_params=_cparams(("arbitrary",)),
        name="hyena_filter",
    )(feats, t, fw1, row(fb1), row(ff1), fw2, row(fb2), row(ff2), fw3, fb3, delta_cols, is_bwd)
    return h, sums[0]


def _shortconv_body(z_ref, w_ref, b_ref, o_ref, *, chunk):
    L = z_ref.shape[0]
    w0, w1, w2 = w_ref[0:1, :], w_ref[1:2, :], w_ref[2:3, :]
    bias = b_ref[...]
    row = lax.broadcasted_iota(jnp.int32, (chunk, z_ref.shape[1]), 0)

    def step(c, carry):
        r0 = pl.multiple_of(c * chunk, chunk)
        cur = z_ref[pl.ds(r0, chunk), :]
        prev_row = z_ref[pl.ds(jnp.maximum(r0 - 1, 0), 1), :]
        next_row = z_ref[pl.ds(jnp.minimum(r0 + chunk, L - 1), 1), :]
        prev_row = jnp.where(c == 0, 0.0, prev_row)
        next_row = jnp.where(c == L // chunk - 1, 0.0, next_row)
        down = jnp.where(row == 0, prev_row, pltpu.roll(cur, 1, 0))
        up = jnp.where(row == chunk - 1, next_row, pltpu.roll(cur, chunk - 1, 0))
        o_ref[pl.ds(r0, chunk), :] = bias + down * w0 + cur * w1 + up * w2
        return carry

    lax.fori_loop(0, L // chunk, step, 0)


def _shortconv(z, conv_w, conv_b, chunk=256):
    B, L, C = z.shape
    return pl.pallas_call(
        functools.partial(_shortconv_body, chunk=chunk),
        grid=(B, C // LANES),
        in_specs=[
            pl.BlockSpec((None, L, LANES), lambda b, c: (b, 0, c)),
            pl.BlockSpec((SHORT_CONV, LANES), lambda b, c: (0, c)),
            pl.BlockSpec((1, LANES), lambda b, c: (0, c)),
        ],
        out_specs=pl.BlockSpec((None, L, LANES), lambda b, c: (b, 0, c)),
        out_shape=jax.ShapeDtypeStruct((B, L, C), F32),
        compiler_params=_cparams(("parallel", "parallel")),
        name="shortconv",
    )(z, conv_w, conv_b.reshape(1, C))


def _dft_tables(n1_len, n2_len):
    n = n1_len * n2_len

    def root(num, den):
        ang = (2.0 * math.pi / den) * (num % den).astype(F32)
        return jnp.cos(ang), -jnp.sin(ang)

    i1 = jnp.arange(n1_len, dtype=jnp.int32)
    i2 = jnp.arange(n2_len, dtype=jnp.int32)
    f1r, f1i = root(i1[:, None] * i1[None, :], n1_len)
    f2r, f2i = root(i2[:, None] * i2[None, :], n2_len)
    twr, twi = root(i1[:, None] * i2[None, :], n)
    return (f1r, f1i), (f2r, f2i), (twr, twi)


def _stacked_inner_dft(f2r, f2i, twr_row, twi_row):
    gr = f2r * twr_row - f2i * twi_row
    gi = f2r * twi_row + f2i * twr_row
    top = jnp.concatenate([gr, -gi], axis=1)
    bot = jnp.concatenate([gi, gr], axis=1)
    return jnp.concatenate([top, bot], axis=0)


def _outer_dft_to_scratch(load_rows, fa_ref, ar_ref, ai_ref, n1_len):
    fa = fa_ref[...]

    def step(n2, carry):
        a = jnp.dot(fa, load_rows(n2).astype(BF16), preferred_element_type=F32)
        ar_ref[pl.ds(n2, n1_len, stride=FFT_PITCH), :] = a[:n1_len]
        ai_ref[pl.ds(n2, n1_len, stride=FFT_PITCH), :] = a[n1_len:]
        return carry

    lax.fori_loop(0, FFT_N2, step, 0)


def _spec_body(x_ref, inv_ref, fa_ref, f2r_ref, f2i_ref, twr_ref, twi_ref, kr_ref, ki_ref,
               ar_ref, ai_ref, *, n1_len):
    kc = pl.program_id(2)

    @pl.when(kc == 0)
    def _():
        inv = inv_ref[...]
        _outer_dft_to_scratch(lambda n2: x_ref[pl.ds(n2, n1_len, stride=FFT_N2), :] * inv,
                              fa_ref, ar_ref, ai_ref, n1_len)

    f2r = f2r_ref[...]
    f2i = f2i_ref[...]

    def step(t, carry):
        k1 = kc * FFT_K1_CHUNK + t
        base = pl.multiple_of(k1 * FFT_PITCH, SUBLANES)
        rhs = jnp.concatenate([ar_ref[pl.ds(base, FFT_N2), :], ai_ref[pl.ds(base, FFT_N2), :]],
                              axis=0).astype(BF16)
        mf = _stacked_inner_dft(f2r, f2i, twr_ref[pl.ds(k1, 1), :], twi_ref[pl.ds(k1, 1), :])
        x = jnp.dot(mf.astype(BF16), rhs, preferred_element_type=F32)
        o = pl.multiple_of(t * FFT_N2, FFT_N2)
        kr_ref[pl.ds(o, FFT_N2), :] = x[:FFT_N2]
        ki_ref[pl.ds(o, FFT_N2), :] = x[FFT_N2:]
        return carry

    lax.fori_loop(0, FFT_K1_CHUNK, step, 0)


def _filter_spectrum(kern, inv_norm, tables):
    O, N, C = kern.shape
    n1_len = N // FFT_N2
    (f1r, f1i), (f2r, f2i), (twr, twi) = tables
    fa = jnp.concatenate([f1r, f1i], axis=0).astype(BF16)
    rows = FFT_K1_CHUNK * FFT_N2
    const = lambda shape: pl.BlockSpec(shape, lambda o, c, k: (0, 0))
    return pl.pallas_call(
        functools.partial(_spec_body, n1_len=n1_len),
        grid=(O, C // LANES, n1_len // FFT_K1_CHUNK),
        in_specs=[
            pl.BlockSpec((None, N, LANES), lambda o, c, k: (o, 0, c)),
            pl.BlockSpec((None, 1, LANES), lambda o, c, k: (o, 0, c)),
            const(fa.shape), const(f2r.shape), const(f2i.shape), const(twr.shape), const(twi.shape),
        ],
        out_specs=[pl.BlockSpec((None, rows, LANES), lambda o, c, k: (o, k, c)),
                   pl.BlockSpec((None, rows, LANES), lambda o, c, k: (o, k, c))],
        out_shape=[jax.ShapeDtypeStruct((O, N, C), F32), jax.ShapeDtypeStruct((O, N, C), F32)],
        scratch_shapes=[pltpu.VMEM((n1_len * FFT_PITCH, LANES), F32),
                        pltpu.VMEM((n1_len * FFT_PITCH, LANES), F32)],
        compiler_params=_cparams(("parallel", "parallel", "arbitrary")),
        name="filter_spectrum",
    )(kern, inv_norm, fa, f2r, f2i, twr, twi)


def _fftconv_body(y_ref, gate_ref, d_ref, kr_ref, ki_ref, fa_ref, fs_ref, f2r_ref, f2i_ref,
                  twr_ref, twi_ref, o_ref, xs_ref, ar_ref, ai_ref, *, n1_len):
    kc = pl.program_id(2)
    half = n1_len // 2

    @pl.when(kc == 0)
    def _():
        def copy(n1, carry):
            src = pl.multiple_of(n1 * FFT_N2, FFT_N2)
            dst = pl.multiple_of(n1 * FFT_PITCH, SUBLANES)
            xs_ref[pl.ds(dst, FFT_N2), :] = y_ref[pl.ds(src, FFT_N2), :]
            return carry

        lax.fori_loop(0, half, copy, 0)
        _outer_dft_to_scratch(lambda n2: xs_ref[pl.ds(n2, half, stride=FFT_PITCH), :],
                              fa_ref, ar_ref, ai_ref, n1_len)

    f2r = f2r_ref[...]
    f2i = f2i_ref[...]

    def step(t, carry):
        k1 = kc * FFT_K1_CHUNK + t
        base = pl.multiple_of(k1 * FFT_PITCH, SUBLANES)
        rhs = jnp.concatenate([ar_ref[pl.ds(base, FFT_N2), :], ai_ref[pl.ds(base, FFT_N2), :]],
                              axis=0).astype(BF16)
        mf = _stacked_inner_dft(f2r, f2i, twr_ref[pl.ds(k1, 1), :], twi_ref[pl.ds(k1, 1), :])
        x = jnp.dot(mf.astype(BF16), rhs, preferred_element_type=F32)
        xr, xi = x[:FFT_N2], x[FFT_N2:]
        o = pl.multiple_of(t * FFT_N2, FFT_N2)
        kr = kr_ref[pl.ds(o, FFT_N2), :]
        ki = ki_ref[pl.ds(o, FFT_N2), :]
        z = jnp.concatenate([xr * kr - xi * ki, xr * ki + xi * kr], axis=0).astype(BF16)
        b = jnp.dot(mf.T.astype(BF16), z, preferred_element_type=F32)
        ar_ref[pl.ds(base, FFT_N2), :] = b[:FFT_N2]
        ai_ref[pl.ds(base, FFT_N2), :] = b[FFT_N2:]
        return carry

    lax.fori_loop(0, FFT_K1_CHUNK, step, 0)

    @pl.when(kc == pl.num_programs(2) - 1)
    def _():
        fs = fs_ref[...]

        def inv_outer(n2, carry):
            rhs = jnp.concatenate([ar_ref[pl.ds(n2, n1_len, stride=FFT_PITCH), :],
                                   ai_ref[pl.ds(n2, n1_len, stride=FFT_PITCH), :]],
                                  axis=0).astype(BF16)
            conv = jnp.dot(fs, rhs, preferred_element_type=F32)
            ar_ref[pl.ds(n2, half, stride=FFT_PITCH), :] = conv
            return carry

        lax.fori_loop(0, FFT_N2, inv_outer, 0)
        d = d_ref[...]

        def finish(n1, carry):
            src = pl.multiple_of(n1 * FFT_PITCH, SUBLANES)
            dst = pl.multiple_of(n1 * FFT_N2, FFT_N2)
            y = y_ref[pl.ds(dst, FFT_N2), :]
            o_ref[pl.ds(dst, FFT_N2), :] = gate_ref[pl.ds(dst, FFT_N2), :] * (
                ar_ref[pl.ds(src, FFT_N2), :] + y * d)
            return carry

        lax.fori_loop(0, half, finish, 0)


def _fftconv_gate(y, gate, d, kr, ki, tables):
    B, L, C = y.shape
    N = kr.shape[0]
    n1_len = N // FFT_N2
    half = n1_len // 2
    (f1r, f1i), (f2r, f2i), (twr, twi) = tables
    fa = jnp.concatenate([f1r[:, :half], f1i[:, :half]], axis=0).astype(BF16)
    fs = (jnp.concatenate([f1r[:half, :], f1i[:half, :]], axis=1) * (1.0 / N)).astype(BF16)
    rows = FFT_K1_CHUNK * FFT_N2
    const = lambda shape: pl.BlockSpec(shape, lambda c, b, k: (0, 0))
    return pl.pallas_call(
        functools.partial(_fftconv_body, n1_len=n1_len),
        grid=(C // LANES, B, n1_len // FFT_K1_CHUNK),
        in_specs=[
            pl.BlockSpec((None, L, LANES), lambda c, b, k: (b, 0, c)),
            pl.BlockSpec((None, L, LANES), lambda c, b, k: (b, 0, c)),
            pl.BlockSpec((1, LANES), lambda c, b, k: (0, c)),
            pl.BlockSpec((rows, LANES), lambda c, b, k: (k, c)),
            pl.BlockSpec((rows, LANES), lambda c, b, k: (k, c)),
            const(fa.shape), const(fs.shape), const(f2r.shape), const(f2i.shape),
            const(twr.shape), const(twi.shape),
        ],
        out_specs=pl.BlockSpec((None, L, LANES), lambda c, b, k: (b, 0, c)),
        out_shape=jax.ShapeDtypeStruct((B, L, C), F32),
        scratch_shapes=[pltpu.VMEM((half * FFT_PITCH, LANES), F32),
                        pltpu.VMEM((n1_len * FFT_PITCH, LANES), F32),
                        pltpu.VMEM((n1_len * FFT_PITCH, LANES), F32)],
        compiler_params=_cparams(("parallel", "parallel", "arbitrary")),
        name="fftconv_gate",
    )(y, gate, d, kr, ki, fa, fs, f2r, f2i, twr, twi)


def _hyena(z, conv_w, conv_b, fw1, fb1, ff1, fw2, fb2, ff2, fw3, fb3, hyena_d):
    B, L, C3 = z.shape
    W = C3 // 3
    zc = _shortconv(z, conv_w, conv_b)
    h, sums = _hyena_filter(L, W, fw1, fb1, ff1, fw2, fb2, ff2, fw3, fb3)
    h = h.reshape(L, HYENA_ORDER, 2, W)
    fwd, bwd = h[:, :, 0], h[:, :, 1]
    kern = jnp.concatenate([fwd, jnp.zeros((1, HYENA_ORDER, W), F32), bwd[1:][::-1]], axis=0)
    kern = kern.transpose(1, 0, 2)
    sums = sums.reshape(HYENA_ORDER, 2, W)
    inv_norm = (1.0 / (sums[:, 0] + sums[:, 1]))[:, None, :]
    tables = _dft_tables(2 * L // FFT_N2, FFT_N2)
    kr, ki = _filter_spectrum(kern, inv_norm, tables)
    y = zc[..., :W]
    for o in range(HYENA_ORDER):
        gate = zc[..., (o + 1) * W:(o + 2) * W]
        y = _fftconv_gate(y, gate, hyena_d[o].reshape(1, W).astype(F32), kr[o], ki[o], tables)
    return y


def _merge_body(yh_ref, ya_ref, g_ref, x_ref, whu_ref, wau_ref, wo_ref, n2_ref, rwh_ref, rwl_ref,
                rb_ref, h_ref, hn_ref, lg_ref):
    D = x_ref.shape[-1]
    up_h = jnp.dot(yh_ref[...].astype(BF16), whu_ref[...], preferred_element_type=F32)
    up_a = jnp.dot(ya_ref[...].astype(BF16), wau_ref[...], preferred_element_type=F32)
    merged = jax.nn.sigmoid(g_ref[:, :D]) * up_h + jax.nn.sigmoid(g_ref[:, D:]) * up_a
    h = x_ref[...] + jnp.dot(merged.astype(BF16), wo_ref[...], preferred_element_type=F32)
    h_ref[...] = h
    ms = jnp.mean(h * h, axis=-1, keepdims=True)
    hn = h * lax.rsqrt(ms + EPS) * n2_ref[...]
    hn_ref[...] = hn.astype(BF16)
    hh, hl = _split_bf16(hn)
    lg_ref[...] = (jnp.dot(hh, rwh_ref[...], preferred_element_type=F32)
                   + jnp.dot(hl, rwh_ref[...], preferred_element_type=F32)
                   + jnp.dot(hh, rwl_ref[...], preferred_element_type=F32)) + rb_ref[...]


def _merge(yh, ya, gates, xt, whu, wau, wo, n2g, rw, rb, tm=256):
    T, D = xt.shape
    W = yh.shape[1]
    E = rw.shape[1]
    rwp = jnp.zeros((D, LOGIT_PAD), F32).at[:, :E].set(rw)
    rwh, rwl = _split_bf16(rwp)
    rbp = jnp.zeros((1, LOGIT_PAD), F32).at[0, :E].set(rb)
    rowblk = lambda w: pl.BlockSpec((tm, w), lambda i: (i, 0))
    const = lambda shape: pl.BlockSpec(shape, lambda i: (0, 0))
    return pl.pallas_call(
        _merge_body,
        grid=(T // tm,),
        in_specs=[rowblk(W), rowblk(W), rowblk(2 * D), rowblk(D),
                  const((W, D)), const((W, D)), const((D, D)), const((1, D)),
                  const((D, LOGIT_PAD)), const((D, LOGIT_PAD)), const((1, LOGIT_PAD))],
        out_specs=[rowblk(D), rowblk(D), rowblk(LOGIT_PAD)],
        out_shape=[jax.ShapeDtypeStruct((T, D), F32), jax.ShapeDtypeStruct((T, D), BF16),
                   jax.ShapeDtypeStruct((T, LOGIT_PAD), F32)],
        compiler_params=_cparams(("parallel",)),
        name="merge",
    )(yh, ya, gates, xt, whu.astype(BF16), wau.astype(BF16), wo.astype(BF16), n2g.reshape(1, D),
      rwh, rwl, rbp)


def _moe_body(be_ref, nused_ref, x_ref, w1g_ref, w1l_ref, b1g_ref, b1l_ref, w2_ref, b2_ref, g_ref,
              o_ref):
    i = pl.program_id(0)

    @pl.when(i < nused_ref[0])
    def _():
        x = x_ref[...]
        glu = jnp.dot(x, w1g_ref[...], preferred_element_type=F32) + b1g_ref[...]
        lin = jnp.dot(x, w1l_ref[...], preferred_element_type=F32) + b1l_ref[...]
        glu = jnp.minimum(glu, SWIGLU_LIMIT)
        lin = jnp.clip(lin, -SWIGLU_LIMIT, SWIGLU_LIMIT)
        act = glu * jax.nn.sigmoid(SWIGLU_ALPHA * glu) * (lin + 1.0)
        y = jnp.dot(act.astype(BF16), w2_ref[...], preferred_element_type=F32) + b2_ref[...]
        o_ref[...] = y * g_ref[...]

    @pl.when(i >= nused_ref[0])
    def _():
        o_ref[...] = jnp.zeros(o_ref.shape, F32)


def _moe_experts(xg, block_expert, n_used, w1g, w1l, b1g, b1l, w2, b2, gpad):
    P, D = xg.shape
    dff = w2.shape[1]
    nb = P // MOE_TM
    wspec = lambda k, n: pl.BlockSpec((None, k, n), lambda i, be, nu: (be[i], 0, 0))
    grid_spec = pltpu.PrefetchScalarGridSpec(
        num_scalar_prefetch=2,
        grid=(nb,),
        in_specs=[
            pl.BlockSpec((MOE_TM, D), lambda i, be, nu: (i, 0)),
            wspec(D, dff), wspec(D, dff), wspec(1, dff), wspec(1, dff),
            wspec(dff, D), wspec(1, D),
            pl.BlockSpec((MOE_TM, 1), lambda i, be, nu: (i, 0)),
        ],
        out_specs=pl.BlockSpec((MOE_TM, D), lambda i, be, nu: (i, 0)),
    )
    return pl.pallas_call(
        _moe_body,
        grid_spec=grid_spec,
        out_shape=jax.ShapeDtypeStruct((P, D), F32),
        compiler_params=_cparams(("arbitrary",)),
        name="moe_experts",
    )(block_expert, n_used, xg, w1g, w1l, b1g, b1l, w2, b2, gpad)


def _moe(hn_bf16, logits, w1, b1, w2, b2):
    T, D = hn_bf16.shape
    E = w1.shape[0]
    top_val, top_idx = lax.top_k(logits, TOP_K)
    gates = jax.nn.softmax(top_val, axis=-1)
    TK = T * TOP_K
    e_flat = top_idx.reshape(TK).astype(jnp.int32)
    tok_flat = jnp.arange(TK, dtype=jnp.int32) // TOP_K
    g_flat = gates.reshape(TK)
    order = jnp.argsort(e_flat)
    e_s, tok_s, g_s = e_flat[order], tok_flat[order], g_flat[order]
    counts = jnp.bincount(e_flat, length=E).astype(jnp.int32)
    starts = jnp.cumsum(counts) - counts
    padded = ((counts + MOE_TM - 1) // MOE_TM) * MOE_TM
    pad_ends = jnp.cumsum(padded)
    pad_starts = pad_ends - padded
    dest = pad_starts[e_s] + jnp.arange(TK, dtype=jnp.int32) - starts[e_s]
    nb = (TK + E * (MOE_TM - 1) + MOE_TM - 1) // MOE_TM
    P = nb * MOE_TM
    src = jnp.zeros((P,), jnp.int32).at[dest].set(tok_s)
    gpad = jnp.zeros((P,), F32).at[dest].set(g_s)
    pos = jnp.zeros((TK,), jnp.int32).at[order].set(dest)
    block_start = jnp.arange(nb, dtype=jnp.int32) * MOE_TM
    block_expert = jnp.minimum(jnp.sum(block_start[:, None] >= pad_ends[None, :], axis=1),
                               E - 1).astype(jnp.int32)
    n_used = (pad_ends[-1] // MOE_TM).astype(jnp.int32).reshape(1)
    xg = hn_bf16[src]
    y = _moe_experts(xg, block_expert, n_used,
                     w1[:, :, 0::2].astype(BF16), w1[:, :, 1::2].astype(BF16),
                     b1[:, None, 0::2].astype(F32), b1[:, None, 1::2].astype(F32),
                     w2.astype(BF16), b2[:, None, :].astype(F32), gpad[:, None])
    return y[pos].reshape(T, TOP_K, D).sum(axis=1)


def _rope_tables(L):
    rows = L // GRID_W
    row = jnp.repeat(jnp.arange(rows, dtype=F32), GRID_W)
    col = jnp.tile(jnp.arange(GRID_W, dtype=F32), rows)
    half = HEAD_DIM // 2
    freqs = ROPE_THETA ** (-jnp.arange(0, half, 2, dtype=F32) / half)
    ang = jnp.concatenate([row[:, None] * freqs, col[:, None] * freqs], axis=-1)
    cos = jnp.repeat(jnp.cos(ang), 2, axis=-1)
    sin = jnp.repeat(jnp.sin(ang), 2, axis=-1)
    sign = jnp.tile(jnp.array([-1.0, 1.0], F32), HEAD_DIM // 2)
    reps = LANES // HEAD_DIM
    return jnp.tile(cos, (1, reps)), jnp.tile(sin * sign, (1, reps))


def kernel(x, norm1_g, w_in, conv_w, conv_b, filt_w1, filt_b1, filt_freq1, filt_w2, filt_b2, filt_freq2, filt_w3, filt_b3, hyena_d, q_norm_g, k_norm_g, w_hyena_up, w_attn_up, w_out, norm2_g, router_w, router_b, expert_w1, expert_b1, expert_w2, expert_b2):
    B, L, D = x.shape
    T = B * L
    depth = w_in.shape[0]
    hw = conv_w.shape[-1] // 3
    aw = N_Q_HEADS * HEAD_DIM
    kvw = N_KV_HEADS * HEAD_DIM
    widths = (3 * hw, aw, 2 * kvw, 2 * D)
    cosf, sinf = _rope_tables(L)
    for l in range(depth):
        xt = x.reshape(T, D)
        z, q, kv, gates = _inproj(xt, norm1_g[l], w_in[l].astype(BF16), widths)
        y_hy = _hyena(z.reshape(B, L, 3 * hw), conv_w[l], conv_b[l], filt_w1[l], filt_b1[l],
                      filt_freq1[l], filt_w2[l], filt_b2[l], filt_freq2[l], filt_w3[l], filt_b3[l],
                      hyena_d[l])
        gq = jnp.tile(q_norm_g[l].astype(F32), N_Q_HEADS)[None, :]
        gk = jnp.tile(k_norm_g[l].astype(F32), N_KV_HEADS)[None, :]
        qr, kr, vr = _qkrope(q, kv, gq, gk, cosf, sinf, L)
        y_at = _attention(qr.reshape(B, L, aw), kr.reshape(B, L, kvw), vr.reshape(B, L, kvw))
        h, hn, logits = _merge(y_hy.reshape(T, hw), y_at.reshape(T, aw), gates, xt,
                               w_hyena_up[l], w_attn_up[l], w_out[l], norm2_g[l],
                               router_w[l], router_b[l])
        mo = _moe(hn, logits[:, :N_EXPERTS], expert_w1[l], expert_b1[l], expert_w2[l], expert_b2[l])
        x = (h + mo).reshape(B, L, D)
    return x
```

```python
import functools
import math

import jax
import jax.numpy as jnp
from jax import lax
from jax.experimental import pallas as pl
from jax.experimental.pallas import tpu as pltpu

F32 = jnp.float32
BF16 = jnp.bfloat16

GRID_W = 64
HEAD_DIM = 64
N_Q_HEADS = 8
N_KV_HEADS = 2
Q_PER_KV = N_Q_HEADS // N_KV_HEADS
ROPE_THETA = 10000.0
HYENA_ORDER = 2
SHORT_CONV = 3
FILTER_EMB = 33
FAST_DECAY_PCT = 0.3
SLOW_DECAY_PCT = 1.5
DECAY_TARGET = 1e-2
N_EXPERTS = 32
TOP_K = 4
SWIGLU_LIMIT = 7.0
SWIGLU_ALPHA = 1.702
EPS = 1e-6

LANES = 128
SUBLANES = 8
VMEM_LIMIT = 56 * 1024 * 1024

FFT_N2 = LANES
FFT_PITCH = FFT_N2 + SUBLANES
FFT_K1_CHUNK_MAX = 16
FFT_UNROLL_OUTER = 16
FFT_UNROLL_INNER = True
MOE_TM = 512
LOGIT_PAD = LANES


def _cparams(sem):
    return pltpu.CompilerParams(dimension_semantics=sem, vmem_limit_bytes=VMEM_LIMIT)


def _split_bf16(x):
    hi = x.astype(BF16)
    lo = (x - hi.astype(F32)).astype(BF16)
    return hi, lo


def _inproj_body(x_ref, g_ref, w_ref, z_ref, q_ref, kv_ref, gate_ref, *, widths):
    x = x_ref[...]
    ms = jnp.mean(x * x, axis=-1, keepdims=True)
    u = (x * lax.rsqrt(ms + EPS) * g_ref[...]).astype(BF16)
    off = 0
    for ref, w in zip((z_ref, q_ref, kv_ref, gate_ref), widths):
        ref[...] = jnp.dot(u, w_ref[:, off:off + w], preferred_element_type=F32)
        off += w


def _inproj(xt, g, w_bf16, widths, tm=512):
    T, D = xt.shape
    n = w_bf16.shape[1]
    return pl.pallas_call(
        functools.partial(_inproj_body, widths=widths),
        grid=(T // tm,),
        in_specs=[
            pl.BlockSpec((tm, D), lambda i: (i, 0)),
            pl.BlockSpec((1, D), lambda i: (0, 0)),
            pl.BlockSpec((D, n), lambda i: (0, 0)),
        ],
        out_specs=[pl.BlockSpec((tm, w), lambda i: (i, 0)) for w in widths],
        out_shape=[jax.ShapeDtypeStruct((T, w), F32) for w in widths],
        compiler_params=_cparams(("parallel",)),
        name="inproj",
    )(xt, g.reshape(1, D), w_bf16)


def _head_norm_rope(x, gain, cosf, sinf, ones_blk):
    w = x.shape[-1]
    hi, lo = _split_bf16(x * x)
    ss = (jnp.dot(hi, ones_blk, preferred_element_type=F32)
          + jnp.dot(lo, ones_blk, preferred_element_type=F32))
    xn = x * lax.rsqrt(ss * (1.0 / HEAD_DIM) + EPS) * gain
    lane = lax.broadcasted_iota(jnp.int32, (x.shape[0], LANES), 1)
    cols = []
    for c in range(w // LANES):
        col = xn[:, c * LANES:(c + 1) * LANES]
        cols.append(jnp.where(lane % 2 == 0, pltpu.roll(col, LANES - 1, 1), pltpu.roll(col, 1, 1)))
    swapped = cols[0] if len(cols) == 1 else jnp.concatenate(cols, axis=1)
    return xn * cosf + swapped * sinf


def _qkrope_body(q_ref, kv_ref, gq_ref, gk_ref, cos_ref, sin_ref, oq_ref, ok_ref,
                 qo_ref, ko_ref, vo_ref):
    cosf = cos_ref[...]
    sinf = sin_ref[...]
    nq = q_ref.shape[-1] // LANES
    q = _head_norm_rope(q_ref[...], gq_ref[...], jnp.tile(cosf, (1, nq)), jnp.tile(sinf, (1, nq)),
                        oq_ref[...])
    qo_ref[...] = (q * (HEAD_DIM ** -0.5 * math.log2(math.e))).astype(BF16)
    kv = kv_ref[...]
    kw = ko_ref.shape[-1]
    k = _head_norm_rope(kv[:, :kw], gk_ref[...], cosf, sinf, ok_ref[...])
    ko_ref[...] = k.astype(BF16)
    vo_ref[...] = kv[:, kw:].astype(BF16)


def _qkrope(q, kv, gq, gk, cosf, sinf, seq_len, tm=512):
    T, qw = q.shape
    kw = kv.shape[1] // 2
    nl = seq_len // tm

    def blk_ones(w):
        r = jnp.arange(w) // HEAD_DIM
        return (r[:, None] == r[None, :]).astype(BF16)

    return pl.pallas_call(
        _qkrope_body,
        grid=(T // tm,),
        in_specs=[
            pl.BlockSpec((tm, qw), lambda i: (i, 0)),
            pl.BlockSpec((tm, 2 * kw), lambda i: (i, 0)),
            pl.BlockSpec((1, qw), lambda i: (0, 0)),
            pl.BlockSpec((1, kw), lambda i: (0, 0)),
            pl.BlockSpec((tm, LANES), lambda i: (i % nl, 0)),
            pl.BlockSpec((tm, LANES), lambda i: (i % nl, 0)),
            pl.BlockSpec((qw, qw), lambda i: (0, 0)),
            pl.BlockSpec((kw, kw), lambda i: (0, 0)),
        ],
        out_specs=[
            pl.BlockSpec((tm, qw), lambda i: (i, 0)),
            pl.BlockSpec((tm, kw), lambda i: (i, 0)),
            pl.BlockSpec((tm, kw), lambda i: (i, 0)),
        ],
        out_shape=[
            jax.ShapeDtypeStruct((T, qw), BF16),
            jax.ShapeDtypeStruct((T, kw), BF16),
            jax.ShapeDtypeStruct((T, kw), BF16),
        ],
        compiler_params=_cparams(("parallel",)),
        name="qkrope",
    )(q, kv, gq, gk, cosf, sinf, blk_ones(qw), blk_ones(kw))


def _attn_body(q_ref, k_ref, v_ref, o_ref, qs_ref, m_ref, l_ref, acc_ref, *, tk, nsplit):
    kvh = pl.program_id(1)
    tq = q_ref.shape[0]
    seq = k_ref.shape[0]
    rows = Q_PER_KV * tq
    lane = lax.broadcasted_iota(jnp.int32, (tq, LANES), 1)
    in_head = (lane // HEAD_DIM) == kvh

    for g in range(Q_PER_KV):
        col = q_ref[:, (g // 2) * LANES:(g // 2 + 1) * LANES].astype(F32)
        col = jnp.where((g % 2) == kvh, col, pltpu.roll(col, HEAD_DIM, 1))
        qs_ref[g * tq:(g + 1) * tq, :] = jnp.where(in_head, col, 0.0).astype(BF16)
    m_ref[...] = jnp.full(m_ref.shape, -jnp.inf, F32)
    l_ref[...] = jnp.zeros(l_ref.shape, F32)
    acc_ref[...] = jnp.zeros(acc_ref.shape, F32)
    part = rows // nsplit

    def step(c, carry):
        r0 = pl.multiple_of(c * tk, tk)
        kc = k_ref[pl.ds(r0, tk), :]
        vc = v_ref[pl.ds(r0, tk), :]
        for h in range(nsplit):
            sl = slice(h * part, (h + 1) * part)
            s = lax.dot_general(qs_ref[sl, :], kc, (((1,), (1,)), ((), ())),
                                preferred_element_type=F32)
            m_prev = m_ref[sl, :]
            m_new = jnp.maximum(m_prev, jnp.max(s, axis=-1, keepdims=True))
            alpha = jnp.exp2(m_prev - m_new)
            p = jnp.exp2(s - pltpu.repeat(m_new, tk // LANES, axis=1))
            l_ref[sl, :] = alpha * l_ref[sl, :] + jnp.sum(p, axis=-1, keepdims=True)
            acc_ref[sl, :] = alpha * acc_ref[sl, :] + jnp.dot(p.astype(BF16), vc,
                                                              preferred_element_type=F32)
            m_ref[sl, :] = m_new
        return carry

    lax.fori_loop(0, seq // tk, step, 0)

    o = acc_ref[...] / l_ref[...]
    for c in range(Q_PER_KV // 2):
        even = o[(2 * c) * tq:(2 * c + 1) * tq, :]
        odd = o[(2 * c + 1) * tq:(2 * c + 2) * tq, :]
        even = jnp.where(kvh == 0, even, pltpu.roll(even, HEAD_DIM, 1))
        odd = jnp.where(kvh == 1, odd, pltpu.roll(odd, HEAD_DIM, 1))
        o_ref[:, c * LANES:(c + 1) * LANES] = jnp.where(lane < HEAD_DIM, even, odd)


def _attention(q, k, v, tq=512, tk=2048, nsplit=2):
    B, L, qw = q.shape
    gw = Q_PER_KV * HEAD_DIM
    kw = k.shape[-1]
    rows = Q_PER_KV * tq
    return pl.pallas_call(
        functools.partial(_attn_body, tk=tk, nsplit=nsplit),
        grid=(B, N_KV_HEADS, L // tq),
        in_specs=[
            pl.BlockSpec((None, tq, gw), lambda b, h, i: (b, i, h)),
            pl.BlockSpec((None, L, kw), lambda b, h, i: (b, 0, 0)),
            pl.BlockSpec((None, L, kw), lambda b, h, i: (b, 0, 0)),
        ],
        out_specs=pl.BlockSpec((None, tq, gw), lambda b, h, i: (b, i, h)),
        out_shape=jax.ShapeDtypeStruct((B, L, qw), F32),
        scratch_shapes=[
            pltpu.VMEM((rows, kw), BF16),
            pltpu.VMEM((rows, LANES), F32),
            pltpu.VMEM((rows, LANES), F32),
            pltpu.VMEM((rows, kw), F32),
        ],
        compiler_params=_cparams(("parallel", "parallel", "parallel")),
        name="attention",
    )(q, k, v)


def _hdot(a, b):
    ah, al = _split_bf16(a)
    bh, bl = _split_bf16(b)
    return (jnp.dot(ah, bh, preferred_element_type=F32)
            + jnp.dot(al, bh, preferred_element_type=F32)
            + jnp.dot(ah, bl, preferred_element_type=F32))


def _filter_body(feat_ref, t_ref, w1_ref, b1_ref, f1_ref, w2_ref, b2_ref, f2_ref, w3_ref, b3_ref,
                 delta_ref, bwd_ref, h_ref, sum_ref):
    i = pl.program_id(0)
    h = jnp.sin(f1_ref[...] * (_hdot(feat_ref[...], w1_ref[...]) + b1_ref[...]))
    h = jnp.sin(f2_ref[...] * (_hdot(h, w2_ref[...]) + b2_ref[...]))
    h = _hdot(h, w3_ref[...]) + b3_ref[...]
    t = t_ref[...]
    h = h * jnp.exp(-t * delta_ref[...])
    width = h_ref.shape[-1]
    for j in range(h_ref.shape[0]):
        h_ref[j] = h[:, j * width:(j + 1) * width]
    row = lax.broadcasted_iota(jnp.int32, h.shape, 0) + i * h.shape[0]
    a = jnp.where((row == 0) & (bwd_ref[...] > 0.5), 0.0, jnp.abs(h))
    part = jnp.sum(a, axis=0, keepdims=True)

    @pl.when(i == 0)
    def _():
        sum_ref[...] = jnp.zeros(sum_ref.shape, F32)

    sum_ref[...] += jnp.broadcast_to(part, sum_ref.shape)


def _hyena_filter(L, width, fw1, fb1, ff1, fw2, fb2, ff2, fw3, fb3, tl=512):
    bands = (FILTER_EMB - 1) // 2
    t = jnp.linspace(0.0, 1.0, L, dtype=F32)[:, None]
    w = 2.0 * math.pi * jnp.arange(L, dtype=F32)[:, None] / L
    fr = jnp.linspace(1e-4, bands - 1, bands, dtype=F32)[None, :]
    feats = jnp.concatenate([t, jnp.cos(w * fr), -jnp.sin(w * fr)], axis=-1)
    max_decay = math.log(DECAY_TARGET) / FAST_DECAY_PCT
    min_decay = math.log(DECAY_TARGET) / SLOW_DECAY_PCT
    deltas = jnp.abs(jnp.linspace(min_decay, max_decay, width, dtype=F32))
    ncol = fw3.shape[1]
    delta_cols = jnp.tile(deltas, ncol // width)[None, :]
    is_bwd = ((jnp.arange(ncol) // width) % 2).astype(F32)[None, :]
    emb = hid = LANES

    def pad2(a, r, c):
        a = a.astype(F32)
        return jnp.zeros((r, c), F32).at[:a.shape[0], :a.shape[1]].set(a)

    row = lambda a: pad2(a.reshape(1, -1), 1, hid)
    feats = pad2(feats, L, emb)
    fw1, fw2, fw3 = pad2(fw1, emb, hid), pad2(fw2, hid, hid), pad2(fw3, hid, ncol)
    fb3 = fb3.reshape(1, ncol).astype(F32)
    const = lambda shape: pl.BlockSpec(shape, lambda i: (0, 0))
    h, sums = pl.pallas_call(
        _filter_body,
        grid=(L // tl,),
        in_specs=[
            pl.BlockSpec((tl, emb), lambda i: (i, 0)),
            pl.BlockSpec((tl, 1), lambda i: (i, 0)),
            const((emb, hid)), const((1, hid)), const((1, hid)),
            const((hid, hid)), const((1, hid)), const((1, hid)),
            const((hid, ncol)), const((1, ncol)), const((1, ncol)), const((1, ncol)),
        ],
        out_specs=[pl.BlockSpec((ncol // width, tl, width), lambda i: (0, i, 0)),
                   pl.BlockSpec((SUBLANES, ncol), lambda i: (0, 0))],
        out_shape=[jax.ShapeDtypeStruct((ncol // width, L, width), F32),
                   jax.ShapeDtypeStruct((SUBLANES, ncol), F32)],
        compiler_params=_cparams(("arbitrary",)),
        name="hyena_filter",
    )(feats, t, fw1, row(fb1), row(ff1), fw2, row(fb2), row(ff2), fw3, fb3, delta_cols, is_bwd)
    return h, sums[0]


def _shortconv_body(z_ref, w_ref, b_ref, o_ref, *, chunk):
    L = z_ref.shape[0]
    w0, w1, w2 = w_ref[0:1, :], w_ref[1:2, :], w_ref[2:3, :]
    bias = b_ref[...]
    row = lax.broadcasted_iota(jnp.int32, (chunk, z_ref.shape[1]), 0)

    def step(c, carry):
        r0 = pl.multiple_of(c * chunk, chunk)
        cur = z_ref[pl.ds(r0, chunk), :]
        prev_row = z_ref[pl.ds(jnp.maximum(r0 - 1, 0), 1), :]
        next_row = z_ref[pl.ds(jnp.minimum(r0 + chunk, L - 1), 1), :]
        prev_row = jnp.where(c == 0, 0.0, prev_row)
        next_row = jnp.where(c == L // chunk - 1, 0.0, next_row)
        down = jnp.where(row == 0, prev_row, pltpu.roll(cur, 1, 0))
        up = jnp.where(row == chunk - 1, next_row, pltpu.roll(cur, chunk - 1, 0))
        o_ref[pl.ds(r0, chunk), :] = bias + down * w0 + cur * w1 + up * w2
        return carry

    lax.fori_loop(0, L // chunk, step, 0)


def _shortconv(z, conv_w, conv_b, chunk=256):
    B, L, C = z.shape
    return pl.pallas_call(
        functools.partial(_shortconv_body, chunk=chunk),
        grid=(B, C // LANES),
        in_specs=[
            pl.BlockSpec((None, L, LANES), lambda b, c: (b, 0, c)),
            pl.BlockSpec((SHORT_CONV, LANES), lambda b, c: (0, c)),
            pl.BlockSpec((1, LANES), lambda b, c: (0, c)),
        ],
        out_specs=pl.BlockSpec((None, L, LANES), lambda b, c: (b, 0, c)),
        out_shape=jax.ShapeDtypeStruct((B, L, C), F32),
        compiler_params=_cparams(("parallel", "parallel")),
        name="shortconv",
    )(z, conv_w, conv_b.reshape(1, C))


def _dft_tables(n1_len, n2_len):
    n = n1_len * n2_len

    def root(num, den):
        ang = (2.0 * math.pi / den) * (num % den).astype(F32)
        return jnp.cos(ang), -jnp.sin(ang)

    i1 = jnp.arange(n1_len, dtype=jnp.int32)
    i2 = jnp.arange(n2_len, dtype=jnp.int32)
    f1r, f1i = root(i1[:, None] * i1[None, :], n1_len)
    f2r, f2i = root(i2[:, None] * i2[None, :], n2_len)
    twr, twi = root(i1[:, None] * i2[None, :], n)
    return (f1r, f1i), (f2r, f2i), (twr, twi)


def _stacked_inner_dft(f2r, f2i, twr_row, twi_row):
    gr = f2r * twr_row - f2i * twi_row
    gi = f2r * twi_row + f2i * twr_row
    top = jnp.concatenate([gr, -gi], axis=1)
    bot = jnp.concatenate([gi, gr], axis=1)
    return jnp.concatenate([top, bot], axis=0)


def _fft_plan(n):
    n1_len = n // FFT_N2
    nk1 = n1_len // 2 + 1
    nk1_pad = -(-nk1 // SUBLANES) * SUBLANES
    chunk = max(c for c in range(1, FFT_K1_CHUNK_MAX + 1) if nk1 % c == 0)
    return n1_len, nk1, nk1_pad, chunk


def _outer_dft_to_scratch(load_rows, fa_ref, ar_ref, ai_ref, nk1_pad):
    fa = fa_ref[...]

    def step(n2, carry):
        a = jnp.dot(fa, load_rows(n2).astype(BF16), preferred_element_type=F32)
        ar_ref[pl.ds(n2, nk1_pad, stride=FFT_PITCH), :] = a[:nk1_pad]
        ai_ref[pl.ds(n2, nk1_pad, stride=FFT_PITCH), :] = a[nk1_pad:]
        return carry

    lax.fori_loop(0, FFT_N2, step, 0, unroll=FFT_UNROLL_OUTER)


def _spec_body(f_ref, b_ref, inv_ref, fa_ref, f2r_ref, f2i_ref, twr_ref, twi_ref, kr_ref, ki_ref,
               fr_ref, fi_ref, br_ref, bi_ref, *, n1_len, nk1_pad, chunk):
    kc = pl.program_id(2)
    half = n1_len // 2

    @pl.when(kc == 0)
    def _():
        inv = inv_ref[...]
        row = lax.broadcasted_iota(jnp.int32, (half, LANES), 0)
        _outer_dft_to_scratch(lambda n2: f_ref[pl.ds(n2, half, stride=FFT_N2), :] * inv,
                              fa_ref, fr_ref, fi_ref, nk1_pad)
        _outer_dft_to_scratch(
            lambda n2: jnp.where((row == 0) & (n2 == 0), 0.0,
                                 b_ref[pl.ds(n2, half, stride=FFT_N2), :] * inv),
            fa_ref, br_ref, bi_ref, nk1_pad)

    f2r = f2r_ref[...]
    f2i = f2i_ref[...]

    def step(t, carry):
        k1 = kc * chunk + t
        base = pl.multiple_of(k1 * FFT_PITCH, SUBLANES)
        mf = _stacked_inner_dft(f2r, f2i, twr_ref[pl.ds(k1, 1), :],
                                twi_ref[pl.ds(k1, 1), :]).astype(BF16)
        rows = pl.ds(base, FFT_N2)
        xf = jnp.dot(mf, jnp.concatenate([fr_ref[rows, :], fi_ref[rows, :]], axis=0).astype(BF16),
                     preferred_element_type=F32)
        xb = jnp.dot(mf, jnp.concatenate([br_ref[rows, :], bi_ref[rows, :]], axis=0).astype(BF16),
                     preferred_element_type=F32)
        o = pl.multiple_of(t * FFT_N2, FFT_N2)
        kr_ref[pl.ds(o, FFT_N2), :] = xf[:FFT_N2] + xb[:FFT_N2]
        ki_ref[pl.ds(o, FFT_N2), :] = xf[FFT_N2:] - xb[FFT_N2:]
        return carry

    lax.fori_loop(0, chunk, step, 0)


def _filter_spectrum(h, inv_norm, tables):
    O2, L, C = h.shape
    O = O2 // 2
    n1_len, nk1, nk1_pad, chunk = _fft_plan(2 * L)
    half = n1_len // 2
    (f1r, f1i), (f2r, f2i), (twr, twi) = tables
    fa = jnp.concatenate([f1r[:nk1_pad, :half], f1i[:nk1_pad, :half]], axis=0).astype(BF16)
    rows = chunk * FFT_N2
    const = lambda shape: pl.BlockSpec(shape, lambda o, c, k: (0, 0))
    out = jax.ShapeDtypeStruct((O, nk1 * FFT_N2, C), F32)
    scratch = pltpu.VMEM((nk1_pad * FFT_PITCH, LANES), F32)
    return pl.pallas_call(
        functools.partial(_spec_body, n1_len=n1_len, nk1_pad=nk1_pad, chunk=chunk),
        grid=(O, C // LANES, nk1 // chunk),
        in_specs=[
            pl.BlockSpec((None, L, LANES), lambda o, c, k: (2 * o, 0, c)),
            pl.BlockSpec((None, L, LANES), lambda o, c, k: (2 * o + 1, 0, c)),
            pl.BlockSpec((None, 1, LANES), lambda o, c, k: (o, 0, c)),
            const(fa.shape), const(f2r.shape), const(f2i.shape), const(twr.shape), const(twi.shape),
        ],
        out_specs=[pl.BlockSpec((None, rows, LANES), lambda o, c, k: (o, k, c)),
                   pl.BlockSpec((None, rows, LANES), lambda o, c, k: (o, k, c))],
        out_shape=[out, out],
        scratch_shapes=[scratch, scratch, scratch, scratch],
        compiler_params=_cparams(("parallel", "parallel", "arbitrary")),
        name="filter_spectrum",
    )(h, h, inv_norm, fa, f2r, f2i, twr, twi)


def _fftconv_body(y_ref, gate_ref, d_ref, kr_ref, ki_ref, fa_ref, fs_ref, f2r_ref, f2i_ref,
                  twr_ref, twi_ref, o_ref, xs_ref, ar_ref, ai_ref, *, n1_len, nk1_pad, chunk):
    kc = pl.program_id(2)
    half = n1_len // 2

    @pl.when(kc == 0)
    def _():
        def copy(n1, carry):
            src = pl.multiple_of(n1 * FFT_N2, FFT_N2)
            dst = pl.multiple_of(n1 * FFT_PITCH, SUBLANES)
            xs_ref[pl.ds(dst, FFT_N2), :] = y_ref[pl.ds(src, FFT_N2), :]
            return carry

        lax.fori_loop(0, half, copy, 0)
        _outer_dft_to_scratch(lambda n2: xs_ref[pl.ds(n2, half, stride=FFT_PITCH), :],
                              fa_ref, ar_ref, ai_ref, nk1_pad)

    f2r = f2r_ref[...]
    f2i = f2i_ref[...]

    def step(t, carry):
        k1 = kc * chunk + t
        base = pl.multiple_of(k1 * FFT_PITCH, SUBLANES)
        rhs = jnp.concatenate([ar_ref[pl.ds(base, FFT_N2), :], ai_ref[pl.ds(base, FFT_N2), :]],
                              axis=0).astype(BF16)
        mf = _stacked_inner_dft(f2r, f2i, twr_ref[pl.ds(k1, 1), :], twi_ref[pl.ds(k1, 1), :])
        x = jnp.dot(mf.astype(BF16), rhs, preferred_element_type=F32)
        xr, xi = x[:FFT_N2], x[FFT_N2:]
        o = pl.multiple_of(t * FFT_N2, FFT_N2)
        kr = kr_ref[pl.ds(o, FFT_N2), :]
        ki = ki_ref[pl.ds(o, FFT_N2), :]
        z = jnp.concatenate([xr * kr - xi * ki, xr * ki + xi * kr], axis=0).astype(BF16)
        b = jnp.dot(mf.T.astype(BF16), z, preferred_element_type=F32)
        ar_ref[pl.ds(base, FFT_N2), :] = b[:FFT_N2]
        ai_ref[pl.ds(base, FFT_N2), :] = b[FFT_N2:]
        return carry

    lax.fori_loop(0, chunk, step, 0, unroll=FFT_UNROLL_INNER)

    @pl.when(kc == pl.num_programs(2) - 1)
    def _():
        fs = fs_ref[...]

        def inv_outer(n2, carry):
            rhs = jnp.concatenate([ar_ref[pl.ds(n2, nk1_pad, stride=FFT_PITCH), :],
                                   ai_ref[pl.ds(n2, nk1_pad, stride=FFT_PITCH), :]],
                                  axis=0).astype(BF16)
            conv = jnp.dot(fs, rhs, preferred_element_type=F32)
            ar_ref[pl.ds(n2, half, stride=FFT_PITCH), :] = conv
            return carry

        lax.fori_loop(0, FFT_N2, inv_outer, 0, unroll=FFT_UNROLL_OUTER)
        d = d_ref[...]

        def finish(n1, carry):
            src = pl.multiple_of(n1 * FFT_PITCH, SUBLANES)
            dst = pl.multiple_of(n1 * FFT_N2, FFT_N2)
            y = y_ref[pl.ds(dst, FFT_N2), :]
            o_ref[pl.ds(dst, FFT_N2), :] = gate_ref[pl.ds(dst, FFT_N2), :] * (
                ar_ref[pl.ds(src, FFT_N2), :] + y * d)
            return carry

        lax.fori_loop(0, half, finish, 0)


def _fftconv_gate(y, y_off, gate, gate_off, d, kr, ki, tables):
    B, L, _ = y.shape
    C = d.shape[-1]
    yo, go = y_off // LANES, gate_off // LANES
    N = 2 * L
    n1_len, nk1, nk1_pad, chunk = _fft_plan(N)
    half = n1_len // 2
    (f1r, f1i), (f2r, f2i), (twr, twi) = tables
    fa = jnp.concatenate([f1r[:nk1_pad, :half], f1i[:nk1_pad, :half]], axis=0).astype(BF16)
    wts = jnp.concatenate([jnp.ones((1,), F32), jnp.full((nk1 - 2,), 2.0, F32), jnp.ones((1,), F32),
                           jnp.zeros((nk1_pad - nk1,), F32)]) * (1.0 / N)
    fs = jnp.concatenate([f1r[:half, :nk1_pad] * wts, f1i[:half, :nk1_pad] * wts],
                         axis=1).astype(BF16)
    rows = chunk * FFT_N2
    const = lambda shape: pl.BlockSpec(shape, lambda c, b, k: (0, 0))
    return pl.pallas_call(
        functools.partial(_fftconv_body, n1_len=n1_len, nk1_pad=nk1_pad, chunk=chunk),
        grid=(C // LANES, B, nk1 // chunk),
        in_specs=[
            pl.BlockSpec((None, L, LANES), lambda c, b, k: (b, 0, c + yo)),
            pl.BlockSpec((None, L, LANES), lambda c, b, k: (b, 0, c + go)),
            pl.BlockSpec((1, LANES), lambda c, b, k: (0, c)),
            pl.BlockSpec((rows, LANES), lambda c, b, k: (k, c)),
            pl.BlockSpec((rows, LANES), lambda c, b, k: (k, c)),
            const(fa.shape), const(fs.shape), const(f2r.shape), const(f2i.shape),
            const(twr.shape), const(twi.shape),
        ],
        out_specs=pl.BlockSpec((None, L, LANES), lambda c, b, k: (b, 0, c)),
        out_shape=jax.ShapeDtypeStruct((B, L, C), F32),
        scratch_shapes=[pltpu.VMEM((half * FFT_PITCH, LANES), F32),
                        pltpu.VMEM((nk1_pad * FFT_PITCH, LANES), F32),
                        pltpu.VMEM((nk1_pad * FFT_PITCH, LANES), F32)],
        compiler_params=_cparams(("parallel", "parallel", "arbitrary")),
        name="fftconv_gate",
    )(y, gate, d, kr, ki, fa, fs, f2r, f2i, twr, twi)


def _hyena(z, conv_w, conv_b, fw1, fb1, ff1, fw2, fb2, ff2, fw3, fb3, hyena_d):
    B, L, C3 = z.shape
    W = C3 // 3
    zc = _shortconv(z, conv_w, conv_b)
    h, sums = _hyena_filter(L, W, fw1, fb1, ff1, fw2, fb2, ff2, fw3, fb3)
    sums = sums.reshape(HYENA_ORDER, 2, W)
    inv_norm = (1.0 / (sums[:, 0] + sums[:, 1]))[:, None, :]
    tables = _dft_tables(2 * L // FFT_N2, FFT_N2)
    kr, ki = _filter_spectrum(h, inv_norm, tables)
    y = zc
    for o in range(HYENA_ORDER):
        y = _fftconv_gate(y, 0, zc, (o + 1) * W, hyena_d[o].reshape(1, W).astype(F32),
                          kr[o], ki[o], tables)
    return y


def _merge_body(yh_ref, ya_ref, g_ref, x_ref, whu_ref, wau_ref, wo_ref, n2_ref, rwh_ref, rwl_ref,
                rb_ref, h_ref, hn_ref, lg_ref):
    D = x_ref.shape[-1]
    up_h = jnp.dot(yh_ref[...].astype(BF16), whu_ref[...], preferred_element_type=F32)
    up_a = jnp.dot(ya_ref[...].astype(BF16), wau_ref[...], preferred_element_type=F32)
    merged = jax.nn.sigmoid(g_ref[:, :D]) * up_h + jax.nn.sigmoid(g_ref[:, D:]) * up_a
    h = x_ref[...] + jnp.dot(merged.astype(BF16), wo_ref[...], preferred_element_type=F32)
    h_ref[...] = h
    ms = jnp.mean(h * h, axis=-1, keepdims=True)
    hn = h * lax.rsqrt(ms + EPS) * n2_ref[...]
    hn_ref[...] = hn.astype(BF16)
    hh, hl = _split_bf16(hn)
    lg_ref[...] = (jnp.dot(hh, rwh_ref[...], preferred_element_type=F32)
                   + jnp.dot(hl, rwh_ref[...], preferred_element_type=F32)
                   + jnp.dot(hh, rwl_ref[...], preferred_element_type=F32)) + rb_ref[...]


def _merge(yh, ya, gates, xt, whu, wau, wo, n2g, rw, rb, tm=256):
    T, D = xt.shape
    W = yh.shape[1]
    E = rw.shape[1]
    rwp = jnp.zeros((D, LOGIT_PAD), F32).at[:, :E].set(rw)
    rwh, rwl = _split_bf16(rwp)
    rbp = jnp.zeros((1, LOGIT_PAD), F32).at[0, :E].set(rb)
    rowblk = lambda w: pl.BlockSpec((tm, w), lambda i: (i, 0))
    const = lambda shape: pl.BlockSpec(shape, lambda i: (0, 0))
    return pl.pallas_call(
        _merge_body,
        grid=(T // tm,),
        in_specs=[rowblk(W), rowblk(W), rowblk(2 * D), rowblk(D),
                  const((W, D)), const((W, D)), const((D, D)), const((1, D)),
                  const((D, LOGIT_PAD)), const((D, LOGIT_PAD)), const((1, LOGIT_PAD))],
        out_specs=[rowblk(D), rowblk(D), rowblk(LOGIT_PAD)],
        out_shape=[jax.ShapeDtypeStruct((T, D), F32), jax.ShapeDtypeStruct((T, D), BF16),
                   jax.ShapeDtypeStruct((T, LOGIT_PAD), F32)],
        compiler_params=_cparams(("parallel",)),
        name="merge",
    )(yh, ya, gates, xt, whu.astype(BF16), wau.astype(BF16), wo.astype(BF16), n2g.reshape(1, D),
      rwh, rwl, rbp)


def _moe_body(be_ref, nused_ref, x_ref, w1g_ref, w1l_ref, b1g_ref, b1l_ref, w2_ref, b2_ref, g_ref,
              o_ref):
    i = pl.program_id(0)

    @pl.when(i < nused_ref[0])
    def _():
        x = x_ref[...]
        glu = jnp.dot(x, w1g_ref[...], preferred_element_type=F32) + b1g_ref[...]
        lin = jnp.dot(x, w1l_ref[...], preferred_element_type=F32) + b1l_ref[...]
        glu = jnp.minimum(glu, SWIGLU_LIMIT)
        lin = jnp.clip(lin, -SWIGLU_LIMIT, SWIGLU_LIMIT)
        act = glu * jax.nn.sigmoid(SWIGLU_ALPHA * glu) * (lin + 1.0)
        y = jnp.dot(act.astype(BF16), w2_ref[...], preferred_element_type=F32) + b2_ref[...]
        o_ref[...] = (y * g_ref[...]).astype(o_ref.dtype)

    @pl.when(i >= nused_ref[0])
    def _():
        o_ref[...] = jnp.zeros(o_ref.shape, o_ref.dtype)


def _moe_experts(xg, block_expert, n_used, w1g, w1l, b1g, b1l, w2, b2, gpad):
    P, D = xg.shape
    dff = w2.shape[1]
    nb = P // MOE_TM
    wspec = lambda k, n: pl.BlockSpec((None, k, n), lambda i, be, nu: (be[i], 0, 0))
    grid_spec = pltpu.PrefetchScalarGridSpec(
        num_scalar_prefetch=2,
        grid=(nb,),
        in_specs=[
            pl.BlockSpec((MOE_TM, D), lambda i, be, nu: (i, 0)),
            wspec(D, dff), wspec(D, dff), wspec(1, dff), wspec(1, dff),
            wspec(dff, D), wspec(1, D),
            pl.BlockSpec((MOE_TM, 1), lambda i, be, nu: (i, 0)),
        ],
        out_specs=pl.BlockSpec((MOE_TM, D), lambda i, be, nu: (i, 0)),
    )
    return pl.pallas_call(
        _moe_body,
        grid_spec=grid_spec,
        out_shape=jax.ShapeDtypeStruct((P, D), BF16),
        compiler_params=_cparams(("arbitrary",)),
        name="moe_experts",
    )(block_expert, n_used, xg, w1g, w1l, b1g, b1l, w2, b2, gpad)


def _prep_w1_body(w_ref, g_ref, l_ref, t_ref):
    half = g_ref.shape[-1]
    wt = w_ref[...].T
    for c in range(t_ref.shape[0]):
        rows = slice(c * LANES, (c + 1) * LANES)
        t_ref[c] = wt[:, rows]
        g_ref[rows, :] = t_ref[c, pl.ds(0, half, stride=2), :].T.astype(BF16)
        l_ref[rows, :] = t_ref[c, pl.ds(1, half, stride=2), :].T.astype(BF16)


def _prep_w1(w1, tc=2 * LANES):
    E, D, F2 = w1.shape
    out = jax.ShapeDtypeStruct((E, D, F2 // 2), BF16)
    return pl.pallas_call(
        _prep_w1_body,
        grid=(E, F2 // tc),
        in_specs=[pl.BlockSpec((None, D, tc), lambda e, j: (e, 0, j))],
        out_specs=[pl.BlockSpec((None, D, tc // 2), lambda e, j: (e, 0, j)),
                   pl.BlockSpec((None, D, tc // 2), lambda e, j: (e, 0, j))],
        out_shape=[out, out],
        scratch_shapes=[pltpu.VMEM((D // LANES, tc, LANES), F32)],
        compiler_params=_cparams(("parallel", "parallel")),
        name="prep_w1",
    )(w1)


def _moe(hn_bf16, logits, w1, b1, w2, b2):
    T, D = hn_bf16.shape
    E = w1.shape[0]
    top_val, top_idx = lax.top_k(logits, TOP_K)
    gates = jax.nn.softmax(top_val, axis=-1)
    TK = T * TOP_K
    e_flat = top_idx.reshape(TK).astype(jnp.int32)
    g_flat = gates.reshape(TK)
    order = jnp.argsort(e_flat).astype(jnp.int32)
    rank = jnp.argsort(order).astype(jnp.int32)
    counts = jnp.sum(e_flat[:, None] == jnp.arange(E, dtype=jnp.int32)[None, :], axis=0,
                     dtype=jnp.int32)
    starts = jnp.cumsum(counts) - counts
    padded = ((counts + MOE_TM - 1) // MOE_TM) * MOE_TM
    pad_ends = jnp.cumsum(padded)
    pad_starts = pad_ends - padded
    nb = (TK + E * (MOE_TM - 1) + MOE_TM - 1) // MOE_TM
    P = nb * MOE_TM
    block_start = jnp.arange(nb, dtype=jnp.int32) * MOE_TM
    block_expert = jnp.minimum(jnp.sum(block_start[:, None] >= pad_ends[None, :], axis=1),
                               E - 1).astype(jnp.int32)
    n_used = (pad_ends[-1] // MOE_TM).astype(jnp.int32).reshape(1)
    row_e = jnp.repeat(block_expert, MOE_TM)
    within = jnp.arange(P, dtype=jnp.int32) - pad_starts[row_e]
    valid = within < counts[row_e]
    sorted_idx = jnp.where(valid, starts[row_e] + within, 0)
    flat_idx = order[sorted_idx]
    src = jnp.where(valid, flat_idx // TOP_K, 0)
    gpad = jnp.where(valid, g_flat[flat_idx], 0.0)
    pos = pad_starts[e_flat] + rank - starts[e_flat]
    xg = hn_bf16[src]
    w1g, w1l = _prep_w1(w1)
    y = _moe_experts(xg, block_expert, n_used, w1g, w1l,
                     b1[:, None, 0::2].astype(F32), b1[:, None, 1::2].astype(F32),
                     w2.astype(BF16), b2[:, None, :].astype(F32), gpad[:, None])
    return y[pos].astype(F32).reshape(T, TOP_K, D).sum(axis=1)


def _rope_tables(L):
    rows = L // GRID_W
    row = jnp.repeat(jnp.arange(rows, dtype=F32), GRID_W)
    col = jnp.tile(jnp.arange(GRID_W, dtype=F32), rows)
    half = HEAD_DIM // 2
    freqs = ROPE_THETA ** (-jnp.arange(0, half, 2, dtype=F32) / half)
    ang = jnp.concatenate([row[:, None] * freqs, col[:, None] * freqs], axis=-1)
    cos = jnp.repeat(jnp.cos(ang), 2, axis=-1)
    sin = jnp.repeat(jnp.sin(ang), 2, axis=-1)
    sign = jnp.tile(jnp.array([-1.0, 1.0], F32), HEAD_DIM // 2)
    reps = LANES // HEAD_DIM
    return jnp.tile(cos, (1, reps)), jnp.tile(sin * sign, (1, reps))


def kernel(x, norm1_g, w_in, conv_w, conv_b, filt_w1, filt_b1, filt_freq1, filt_w2, filt_b2, filt_freq2, filt_w3, filt_b3, hyena_d, q_norm_g, k_norm_g, w_hyena_up, w_attn_up, w_out, norm2_g, router_w, router_b, expert_w1, expert_b1, expert_w2, expert_b2):
    B, L, D = x.shape
    T = B * L
    depth = w_in.shape[0]
    hw = conv_w.shape[-1] // 3
    aw = N_Q_HEADS * HEAD_DIM
    kvw = N_KV_HEADS * HEAD_DIM
    widths = (3 * hw, aw, 2 * kvw, 2 * D)
    cosf, sinf = _rope_tables(L)
    for l in range(depth):
        xt = x.reshape(T, D)
        z, q, kv, gates = _inproj(xt, norm1_g[l], w_in[l].astype(BF16), widths)
        y_hy = _hyena(z.reshape(B, L, 3 * hw), conv_w[l], conv_b[l], filt_w1[l], filt_b1[l],
                      filt_freq1[l], filt_w2[l], filt_b2[l], filt_freq2[l], filt_w3[l], filt_b3[l],
                      hyena_d[l])
        gq = jnp.tile(q_norm_g[l].astype(F32), N_Q_HEADS)[None, :]
        gk = jnp.tile(k_norm_g[l].astype(F32), N_KV_HEADS)[None, :]
        qr, kr, vr = _qkrope(q, kv, gq, gk, cosf, sinf, L)
        y_at = _attention(qr.reshape(B, L, aw), kr.reshape(B, L, kvw), vr.reshape(B, L, kvw))
        h, hn, logits = _merge(y_hy.reshape(T, hw), y_at.reshape(T, aw), gates, xt,
                               w_hyena_up[l], w_attn_up[l], w_out[l], norm2_g[l],
                               router_w[l], router_b[l])
        mo = _moe(hn, logits[:, :N_EXPERTS], expert_w1[l], expert_b1[l], expert_w2[l], expert_b2[l])
        x = (h + mo).reshape(B, L, D)
    return x
```

```python
import functools
import math

import jax
import jax.numpy as jnp
from jax import lax
from jax.experimental import pallas as pl
from jax.experimental.pallas import tpu as pltpu

F32 = jnp.float32
BF16 = jnp.bfloat16

GRID_W = 64
HEAD_DIM = 64
N_Q_HEADS = 8
N_KV_HEADS = 2
Q_PER_KV = N_Q_HEADS // N_KV_HEADS
ROPE_THETA = 10000.0
HYENA_ORDER = 2
SHORT_CONV = 3
FILTER_EMB = 33
FAST_DECAY_PCT = 0.3
SLOW_DECAY_PCT = 1.5
DECAY_TARGET = 1e-2
N_EXPERTS = 32
TOP_K = 4
SWIGLU_LIMIT = 7.0
SWIGLU_ALPHA = 1.702
EPS = 1e-6

LANES = 128
SUBLANES = 8
VMEM_LIMIT = 56 * 1024 * 1024

FFT_N2 = LANES
FFT_PITCH = FFT_N2 + SUBLANES
FFT_K1_CHUNK_MAX = 16
FFT_UNROLL_OUTER = 16
FFT_UNROLL_INNER = True
MOE_TM = 512
LOGIT_PAD = LANES


def _cparams(sem):
    return pltpu.CompilerParams(dimension_semantics=sem, vmem_limit_bytes=VMEM_LIMIT)


def _split_bf16(x):
    hi = x.astype(BF16)
    lo = (x - hi.astype(F32)).astype(BF16)
    return hi, lo


def _inproj_body(x_ref, g_ref, w_ref, z_ref, q_ref, kv_ref, gate_ref, *, widths):
    x = x_ref[...]
    ms = jnp.mean(x * x, axis=-1, keepdims=True)
    u = (x * lax.rsqrt(ms + EPS) * g_ref[...]).astype(BF16)
    off = 0
    for ref, w in zip((z_ref, q_ref, kv_ref, gate_ref), widths):
        ref[...] = jnp.dot(u, w_ref[:, off:off + w], preferred_element_type=F32)
        off += w


def _inproj(xt, g, w_bf16, widths, tm=512):
    T, D = xt.shape
    n = w_bf16.shape[1]
    return pl.pallas_call(
        functools.partial(_inproj_body, widths=widths),
        grid=(T // tm,),
        in_specs=[
            pl.BlockSpec((tm, D), lambda i: (i, 0)),
            pl.BlockSpec((1, D), lambda i: (0, 0)),
            pl.BlockSpec((D, n), lambda i: (0, 0)),
        ],
        out_specs=[pl.BlockSpec((tm, w), lambda i: (i, 0)) for w in widths],
        out_shape=[jax.ShapeDtypeStruct((T, w), F32) for w in widths],
        compiler_params=_cparams(("parallel",)),
        name="inproj",
    )(xt, g.reshape(1, D), w_bf16)


def _head_norm_rope(x, gain, cosf, sinf, ones_blk):
    w = x.shape[-1]
    hi, lo = _split_bf16(x * x)
    ss = (jnp.dot(hi, ones_blk, preferred_element_type=F32)
          + jnp.dot(lo, ones_blk, preferred_element_type=F32))
    xn = x * lax.rsqrt(ss * (1.0 / HEAD_DIM) + EPS) * gain
    lane = lax.broadcasted_iota(jnp.int32, (x.shape[0], LANES), 1)
    cols = []
    for c in range(w // LANES):
        col = xn[:, c * LANES:(c + 1) * LANES]
        cols.append(jnp.where(lane % 2 == 0, pltpu.roll(col, LANES - 1, 1), pltpu.roll(col, 1, 1)))
    swapped = cols[0] if len(cols) == 1 else jnp.concatenate(cols, axis=1)
    return xn * cosf + swapped * sinf


def _qkrope_body(q_ref, kv_ref, gq_ref, gk_ref, cos_ref, sin_ref, oq_ref, ok_ref,
                 qo_ref, ko_ref, vo_ref):
    cosf = cos_ref[...]
    sinf = sin_ref[...]
    nq = q_ref.shape[-1] // LANES
    q = _head_norm_rope(q_ref[...], gq_ref[...], jnp.tile(cosf, (1, nq)), jnp.tile(sinf, (1, nq)),
                        oq_ref[...])
    qo_ref[...] = (q * (HEAD_DIM ** -0.5 * math.log2(math.e))).astype(BF16)
    kv = kv_ref[...]
    kw = kv.shape[-1] // 2
    k = _head_norm_rope(kv[:, :kw], gk_ref[...], cosf, sinf, ok_ref[...])
    ko_ref[...] = k.astype(BF16)
    v = kv[:, kw:]
    lane = lax.broadcasted_iota(jnp.int32, v.shape, 1)
    for h in range(N_KV_HEADS):
        vo_ref[h] = jnp.where((lane // HEAD_DIM) == h, v, 1.0).astype(BF16)


def _qkrope(q, kv, gq, gk, cosf, sinf, seq_len, tm=512):
    T, qw = q.shape
    kw = kv.shape[1] // 2
    assert kw == LANES and N_KV_HEADS * HEAD_DIM == LANES and N_KV_HEADS == 2
    nl = seq_len // tm

    def blk_ones(w):
        r = jnp.arange(w) // HEAD_DIM
        return (r[:, None] == r[None, :]).astype(BF16)

    return pl.pallas_call(
        _qkrope_body,
        grid=(T // tm,),
        in_specs=[
            pl.BlockSpec((tm, qw), lambda i: (i, 0)),
            pl.BlockSpec((tm, 2 * kw), lambda i: (i, 0)),
            pl.BlockSpec((1, qw), lambda i: (0, 0)),
            pl.BlockSpec((1, kw), lambda i: (0, 0)),
            pl.BlockSpec((tm, LANES), lambda i: (i % nl, 0)),
            pl.BlockSpec((tm, LANES), lambda i: (i % nl, 0)),
            pl.BlockSpec((qw, qw), lambda i: (0, 0)),
            pl.BlockSpec((kw, kw), lambda i: (0, 0)),
        ],
        out_specs=[
            pl.BlockSpec((tm, qw), lambda i: (i, 0)),
            pl.BlockSpec((tm, kw), lambda i: (i, 0)),
            pl.BlockSpec((N_KV_HEADS, tm, kw), lambda i: (0, i, 0)),
        ],
        out_shape=[
            jax.ShapeDtypeStruct((T, qw), BF16),
            jax.ShapeDtypeStruct((T, kw), BF16),
            jax.ShapeDtypeStruct((N_KV_HEADS, T, kw), BF16),
        ],
        compiler_params=_cparams(("parallel",)),
        name="qkrope",
    )(q, kv, gq, gk, cosf, sinf, blk_ones(qw), blk_ones(kw))


def _attn_body(q_ref, k_ref, v_ref, o_ref, qs_ref, m_ref, acc_ref, *, tk, nsplit):
    kvh = pl.program_id(1)
    tq = q_ref.shape[0]
    seq = k_ref.shape[0]
    rows = Q_PER_KV * tq
    lane = lax.broadcasted_iota(jnp.int32, (tq, LANES), 1)
    in_head = (lane // HEAD_DIM) == kvh

    for g in range(Q_PER_KV):
        col = q_ref[:, (g // 2) * LANES:(g // 2 + 1) * LANES].astype(F32)
        col = jnp.where((g % 2) == kvh, col, pltpu.roll(col, HEAD_DIM, 1))
        qs_ref[g * tq:(g + 1) * tq, :] = jnp.where(in_head, col, 0.0).astype(BF16)
    m_ref[...] = jnp.full(m_ref.shape, -jnp.inf, F32)
    acc_ref[...] = jnp.zeros(acc_ref.shape, F32)
    part = rows // nsplit

    def step(c, carry):
        r0 = pl.multiple_of(c * tk, tk)
        kc = k_ref[pl.ds(r0, tk), :]
        vc = v_ref[pl.ds(r0, tk), :]
        for h in range(nsplit):
            sl = slice(h * part, (h + 1) * part)
            s = lax.dot_general(qs_ref[sl, :], kc, (((1,), (1,)), ((), ())),
                                preferred_element_type=F32)
            m_prev = m_ref[sl, :]
            m_new = jnp.maximum(m_prev, jnp.max(s, axis=-1, keepdims=True))
            p = jnp.exp2(s - jnp.tile(m_new, (1, tk // LANES)))
            acc_ref[sl, :] = jnp.exp2(m_prev - m_new) * acc_ref[sl, :] + jnp.dot(
                p.astype(BF16), vc, preferred_element_type=F32)
            m_ref[sl, :] = m_new
        return carry

    lax.fori_loop(0, seq // tk, step, 0)

    acc = acc_ref[...]
    o = acc / pltpu.roll(acc, HEAD_DIM, 1)
    for c in range(Q_PER_KV // 2):
        even = o[(2 * c) * tq:(2 * c + 1) * tq, :]
        odd = o[(2 * c + 1) * tq:(2 * c + 2) * tq, :]
        even = jnp.where(kvh == 0, even, pltpu.roll(even, HEAD_DIM, 1))
        odd = jnp.where(kvh == 1, odd, pltpu.roll(odd, HEAD_DIM, 1))
        o_ref[:, c * LANES:(c + 1) * LANES] = jnp.where(lane < HEAD_DIM, even, odd)


def _attention(q, k, v_aug, tq=512, tk=2048, nsplit=2):
    B, L, qw = q.shape
    gw = Q_PER_KV * HEAD_DIM
    kw = k.shape[-1]
    rows = Q_PER_KV * tq
    return pl.pallas_call(
        functools.partial(_attn_body, tk=tk, nsplit=nsplit),
        grid=(B, N_KV_HEADS, L // tq),
        in_specs=[
            pl.BlockSpec((None, tq, gw), lambda b, h, i: (b, i, h)),
            pl.BlockSpec((None, L, kw), lambda b, h, i: (b, 0, 0)),
            pl.BlockSpec((None, None, L, kw), lambda b, h, i: (h, b, 0, 0)),
        ],
        out_specs=pl.BlockSpec((None, tq, gw), lambda b, h, i: (b, i, h)),
        out_shape=jax.ShapeDtypeStruct((B, L, qw), F32),
        scratch_shapes=[
            pltpu.VMEM((rows, kw), BF16),
            pltpu.VMEM((rows, LANES), F32),
            pltpu.VMEM((rows, kw), F32),
        ],
        compiler_params=_cparams(("parallel", "parallel", "parallel")),
        name="attention",
    )(q, k, v_aug)


def _hdot(a, b):
    ah, al = _split_bf16(a)
    bh, bl = _split_bf16(b)
    return (jnp.dot(ah, bh, preferred_element_type=F32)
            + jnp.dot(al, bh, preferred_element_type=F32)
            + jnp.dot(ah, bl, preferred_element_type=F32))


def _filter_body(feat_ref, t_ref, w1_ref, b1_ref, f1_ref, w2_ref, b2_ref, f2_ref, w3_ref, b3_ref,
                 delta_ref, bwd_ref, h_ref, sum_ref):
    i = pl.program_id(0)
    h = jnp.sin(f1_ref[...] * (_hdot(feat_ref[...], w1_ref[...]) + b1_ref[...]))
    h = jnp.sin(f2_ref[...] * (_hdot(h, w2_ref[...]) + b2_ref[...]))
    h = _hdot(h, w3_ref[...]) + b3_ref[...]
    t = t_ref[...]
    h = h * jnp.exp(-t * delta_ref[...])
    width = h_ref.shape[-1]
    for j in range(h_ref.shape[0]):
        h_ref[j] = h[:, j * width:(j + 1) * width]
    row = lax.broadcasted_iota(jnp.int32, h.shape, 0) + i * h.shape[0]
    a = jnp.where((row == 0) & (bwd_ref[...] > 0.5), 0.0, jnp.abs(h))
    part = jnp.sum(a, axis=0, keepdims=True)

    @pl.when(i == 0)
    def _():
        sum_ref[...] = jnp.zeros(sum_ref.shape, F32)

    sum_ref[...] += jnp.broadcast_to(part, sum_ref.shape)


def _hyena_filter(L, width, fw1, fb1, ff1, fw2, fb2, ff2, fw3, fb3, tl=512):
    bands = (FILTER_EMB - 1) // 2
    t = jnp.linspace(0.0, 1.0, L, dtype=F32)[:, None]
    w = 2.0 * math.pi * jnp.arange(L, dtype=F32)[:, None] / L
    fr = jnp.linspace(1e-4, bands - 1, bands, dtype=F32)[None, :]
    feats = jnp.concatenate([t, jnp.cos(w * fr), -jnp.sin(w * fr)], axis=-1)
    max_decay = math.log(DECAY_TARGET) / FAST_DECAY_PCT
    min_decay = math.log(DECAY_TARGET) / SLOW_DECAY_PCT
    deltas = jnp.abs(jnp.linspace(min_decay, max_decay, width, dtype=F32))
    ncol = fw3.shape[1]
    delta_cols = jnp.tile(deltas, ncol // width)[None, :]
    is_bwd = ((jnp.arange(ncol) // width) % 2).astype(F32)[None, :]
    emb = hid = LANES

    def pad2(a, r, c):
        a = a.astype(F32)
        return jnp.zeros((r, c), F32).at[:a.shape[0], :a.shape[1]].set(a)

    row = lambda a: pad2(a.reshape(1, -1), 1, hid)
    feats = pad2(feats, L, emb)
    fw1, fw2, fw3 = pad2(fw1, emb, hid), pad2(fw2, hid, hid), pad2(fw3, hid, ncol)
    fb3 = fb3.reshape(1, ncol).astype(F32)
    const = lambda shape: pl.BlockSpec(shape, lambda i: (0, 0))
    h, sums = pl.pallas_call(
        _filter_body,
        grid=(L // tl,),
        in_specs=[
            pl.BlockSpec((tl, emb), lambda i: (i, 0)),
            pl.BlockSpec((tl, 1), lambda i: (i, 0)),
            const((emb, hid)), const((1, hid)), const((1, hid)),
            const((hid, hid)), const((1, hid)), const((1, hid)),
            const((hid, ncol)), const((1, ncol)), const((1, ncol)), const((1, ncol)),
        ],
        out_specs=[pl.BlockSpec((ncol // width, tl, width), lambda i: (0, i, 0)),
                   pl.BlockSpec((SUBLANES, ncol), lambda i: (0, 0))],
        out_shape=[jax.ShapeDtypeStruct((ncol // width, L, width), F32),
                   jax.ShapeDtypeStruct((SUBLANES, ncol), F32)],
        compiler_params=_cparams(("arbitrary",)),
        name="hyena_filter",
    )(feats, t, fw1, row(fb1), row(ff1), fw2, row(fb2), row(ff2), fw3, fb3, delta_cols, is_bwd)
    return h, sums[0]


def _shortconv_body(z_ref, w_ref, b_ref, o_ref, *, chunk):
    L = z_ref.shape[0]
    w0, w1, w2 = w_ref[0:1, :], w_ref[1:2, :], w_ref[2:3, :]
    bias = b_ref[...]
    row = lax.broadcasted_iota(jnp.int32, (chunk, z_ref.shape[1]), 0)

    def step(c, carry):
        r0 = pl.multiple_of(c * chunk, chunk)
        cur = z_ref[pl.ds(r0, chunk), :]
        prev_row = z_ref[pl.ds(jnp.maximum(r0 - 1, 0), 1), :]
        next_row = z_ref[pl.ds(jnp.minimum(r0 + chunk, L - 1), 1), :]
        prev_row = jnp.where(c == 0, 0.0, prev_row)
        next_row = jnp.where(c == L // chunk - 1, 0.0, next_row)
        down = jnp.where(row == 0, prev_row, pltpu.roll(cur, 1, 0))
        up = jnp.where(row == chunk - 1, next_row, pltpu.roll(cur, chunk - 1, 0))
        o_ref[pl.ds(r0, chunk), :] = bias + down * w0 + cur * w1 + up * w2
        return carry

    lax.fori_loop(0, L // chunk, step, 0)


def _shortconv(z, conv_w, conv_b, chunk=256):
    B, L, C = z.shape
    return pl.pallas_call(
        functools.partial(_shortconv_body, chunk=chunk),
        grid=(B, C // LANES),
        in_specs=[
            pl.BlockSpec((None, L, LANES), lambda b, c: (b, 0, c)),
            pl.BlockSpec((SHORT_CONV, LANES), lambda b, c: (0, c)),
            pl.BlockSpec((1, LANES), lambda b, c: (0, c)),
        ],
        out_specs=pl.BlockSpec((None, L, LANES), lambda b, c: (b, 0, c)),
        out_shape=jax.ShapeDtypeStruct((B, L, C), F32),
        compiler_params=_cparams(("parallel", "parallel")),
        name="shortconv",
    )(z, conv_w, conv_b.reshape(1, C))


def _dft_tables(n1_len, n2_len):
    n = n1_len * n2_len

    def root(num, den):
        ang = (2.0 * math.pi / den) * (num % den).astype(F32)
        return jnp.cos(ang), -jnp.sin(ang)

    i1 = jnp.arange(n1_len, dtype=jnp.int32)
    i2 = jnp.arange(n2_len, dtype=jnp.int32)
    f1r, f1i = root(i1[:, None] * i1[None, :], n1_len)
    f2r, f2i = root(i2[:, None] * i2[None, :], n2_len)
    twr, twi = root(i1[:, None] * i2[None, :], n)
    return (f1r, f1i), (f2r, f2i), (twr, twi)


def _stacked_inner_dft(f2r, f2i, twr_row, twi_row):
    gr = f2r * twr_row - f2i * twi_row
    gi = f2r * twi_row + f2i * twr_row
    top = jnp.concatenate([gr, -gi], axis=1)
    bot = jnp.concatenate([gi, gr], axis=1)
    return jnp.concatenate([top, bot], axis=0)


def _fft_plan(n):
    n1_len = n // FFT_N2
    nk1 = n1_len // 2 + 1
    nk1_pad = -(-nk1 // SUBLANES) * SUBLANES
    chunk = max(c for c in range(1, FFT_K1_CHUNK_MAX + 1) if nk1 % c == 0)
    return n1_len, nk1, nk1_pad, chunk


def _outer_dft_to_scratch(load_rows, fa_ref, ar_ref, ai_ref, nk1_pad):
    fa = fa_ref[...]

    def step(n2, carry):
        a = jnp.dot(fa, load_rows(n2).astype(BF16), preferred_element_type=F32)
        ar_ref[pl.ds(n2, nk1_pad, stride=FFT_PITCH), :] = a[:nk1_pad]
        ai_ref[pl.ds(n2, nk1_pad, stride=FFT_PITCH), :] = a[nk1_pad:]
        return carry

    lax.fori_loop(0, FFT_N2, step, 0, unroll=FFT_UNROLL_OUTER)


def _spec_body(f_ref, b_ref, inv_ref, fa_ref, f2r_ref, f2i_ref, twr_ref, twi_ref, kr_ref, ki_ref,
               fr_ref, fi_ref, br_ref, bi_ref, *, n1_len, nk1_pad, chunk):
    kc = pl.program_id(2)
    half = n1_len // 2

    @pl.when(kc == 0)
    def _():
        inv = inv_ref[...]
        row = lax.broadcasted_iota(jnp.int32, (half, LANES), 0)
        _outer_dft_to_scratch(lambda n2: f_ref[pl.ds(n2, half, stride=FFT_N2), :] * inv,
                              fa_ref, fr_ref, fi_ref, nk1_pad)
        _outer_dft_to_scratch(
            lambda n2: jnp.where((row == 0) & (n2 == 0), 0.0,
                                 b_ref[pl.ds(n2, half, stride=FFT_N2), :] * inv),
            fa_ref, br_ref, bi_ref, nk1_pad)

    f2r = f2r_ref[...]
    f2i = f2i_ref[...]

    def step(t, carry):
        k1 = kc * chunk + t
        base = pl.multiple_of(k1 * FFT_PITCH, SUBLANES)
        mf = _stacked_inner_dft(f2r, f2i, twr_ref[pl.ds(k1, 1), :],
                                twi_ref[pl.ds(k1, 1), :]).astype(BF16)
        rows = pl.ds(base, FFT_N2)
        xf = jnp.dot(mf, jnp.concatenate([fr_ref[rows, :], fi_ref[rows, :]], axis=0).astype(BF16),
                     preferred_element_type=F32)
        xb = jnp.dot(mf, jnp.concatenate([br_ref[rows, :], bi_ref[rows, :]], axis=0).astype(BF16),
                     preferred_element_type=F32)
        o = pl.multiple_of(t * FFT_N2, FFT_N2)
        kr_ref[pl.ds(o, FFT_N2), :] = xf[:FFT_N2] + xb[:FFT_N2]
        ki_ref[pl.ds(o, FFT_N2), :] = xf[FFT_N2:] - xb[FFT_N2:]
        return carry

    lax.fori_loop(0, chunk, step, 0)


def _filter_spectrum(h, inv_norm, tables):
    O2, L, C = h.shape
    O = O2 // 2
    n1_len, nk1, nk1_pad, chunk = _fft_plan(2 * L)
    half = n1_len // 2
    (f1r, f1i), (f2r, f2i), (twr, twi) = tables
    fa = jnp.concatenate([f1r[:nk1_pad, :half], f1i[:nk1_pad, :half]], axis=0).astype(BF16)
    rows = chunk * FFT_N2
    const = lambda shape: pl.BlockSpec(shape, lambda o, c, k: (0, 0))
    out = jax.ShapeDtypeStruct((O, nk1 * FFT_N2, C), F32)
    scratch = pltpu.VMEM((nk1_pad * FFT_PITCH, LANES), F32)
    return pl.pallas_call(
        functools.partial(_spec_body, n1_len=n1_len, nk1_pad=nk1_pad, chunk=chunk),
        grid=(O, C // LANES, nk1 // chunk),
        in_specs=[
            pl.BlockSpec((None, L, LANES), lambda o, c, k: (2 * o, 0, c)),
            pl.BlockSpec((None, L, LANES), lambda o, c, k: (2 * o + 1, 0, c)),
            pl.BlockSpec((None, 1, LANES), lambda o, c, k: (o, 0, c)),
            const(fa.shape), const(f2r.shape), const(f2i.shape), const(twr.shape), const(twi.shape),
        ],
        out_specs=[pl.BlockSpec((None, rows, LANES), lambda o, c, k: (o, k, c)),
                   pl.BlockSpec((None, rows, LANES), lambda o, c, k: (o, k, c))],
        out_shape=[out, out],
        scratch_shapes=[scratch, scratch, scratch, scratch],
        compiler_params=_cparams(("parallel", "parallel", "arbitrary")),
        name="filter_spectrum",
    )(h, h, inv_norm, fa, f2r, f2i, twr, twi)


def _fftconv_body(y_ref, gate_ref, d_ref, kr_ref, ki_ref, fa_ref, fs_ref, f2r_ref, f2i_ref,
                  twr_ref, twi_ref, o_ref, xs_ref, ar_ref, ai_ref, *, n1_len, nk1_pad, chunk):
    kc = pl.program_id(2)
    half = n1_len // 2

    @pl.when(kc == 0)
    def _():
        def copy(n1, carry):
            src = pl.multiple_of(n1 * FFT_N2, FFT_N2)
            dst = pl.multiple_of(n1 * FFT_PITCH, SUBLANES)
            xs_ref[pl.ds(dst, FFT_N2), :] = y_ref[pl.ds(src, FFT_N2), :]
            return carry

        lax.fori_loop(0, half, copy, 0)
        _outer_dft_to_scratch(lambda n2: xs_ref[pl.ds(n2, half, stride=FFT_PITCH), :],
                              fa_ref, ar_ref, ai_ref, nk1_pad)

    f2r = f2r_ref[...]
    f2i = f2i_ref[...]

    def step(t, carry):
        k1 = kc * chunk + t
        base = pl.multiple_of(k1 * FFT_PITCH, SUBLANES)
        rhs = jnp.concatenate([ar_ref[pl.ds(base, FFT_N2), :], ai_ref[pl.ds(base, FFT_N2), :]],
                              axis=0).astype(BF16)
        mf = _stacked_inner_dft(f2r, f2i, twr_ref[pl.ds(k1, 1), :], twi_ref[pl.ds(k1, 1), :])
        x = jnp.dot(mf.astype(BF16), rhs, preferred_element_type=F32)
        xr, xi = x[:FFT_N2], x[FFT_N2:]
        o = pl.multiple_of(t * FFT_N2, FFT_N2)
        kr = kr_ref[pl.ds(o, FFT_N2), :]
        ki = ki_ref[pl.ds(o, FFT_N2), :]
        z = jnp.concatenate([xr * kr - xi * ki, xr * ki + xi * kr], axis=0).astype(BF16)
        b = jnp.dot(mf.T.astype(BF16), z, preferred_element_type=F32)
        ar_ref[pl.ds(base, FFT_N2), :] = b[:FFT_N2]
        ai_ref[pl.ds(base, FFT_N2), :] = b[FFT_N2:]
        return carry

    lax.fori_loop(0, chunk, step, 0, unroll=FFT_UNROLL_INNER)

    @pl.when(kc == pl.num_programs(2) - 1)
    def _():
        fs = fs_ref[...]

        def inv_outer(n2, carry):
            rhs = jnp.concatenate([ar_ref[pl.ds(n2, nk1_pad, stride=FFT_PITCH), :],
                                   ai_ref[pl.ds(n2, nk1_pad, stride=FFT_PITCH), :]],
                                  axis=0).astype(BF16)
            conv = jnp.dot(fs, rhs, preferred_element_type=F32)
            ar_ref[pl.ds(n2, half, stride=FFT_PITCH), :] = conv
            return carry

        lax.fori_loop(0, FFT_N2, inv_outer, 0, unroll=FFT_UNROLL_OUTER)
        d = d_ref[...]

        def finish(n1, carry):
            src = pl.multiple_of(n1 * FFT_PITCH, SUBLANES)
            dst = pl.multiple_of(n1 * FFT_N2, FFT_N2)
            y = y_ref[pl.ds(dst, FFT_N2), :]
            o_ref[pl.ds(dst, FFT_N2), :] = gate_ref[pl.ds(dst, FFT_N2), :] * (
                ar_ref[pl.ds(src, FFT_N2), :] + y * d)
            return carry

        lax.fori_loop(0, half, finish, 0)


def _fftconv_gate(y, y_off, gate, gate_off, d, kr, ki, tables):
    B, L, _ = y.shape
    C = d.shape[-1]
    yo, go = y_off // LANES, gate_off // LANES
    N = 2 * L
    n1_len, nk1, nk1_pad, chunk = _fft_plan(N)
    half = n1_len // 2
    (f1r, f1i), (f2r, f2i), (twr, twi) = tables
    fa = jnp.concatenate([f1r[:nk1_pad, :half], f1i[:nk1_pad, :half]], axis=0).astype(BF16)
    wts = jnp.concatenate([jnp.ones((1,), F32), jnp.full((nk1 - 2,), 2.0, F32), jnp.ones((1,), F32),
                           jnp.zeros((nk1_pad - nk1,), F32)]) * (1.0 / N)
    fs = jnp.concatenate([f1r[:half, :nk1_pad] * wts, f1i[:half, :nk1_pad] * wts],
                         axis=1).astype(BF16)
    rows = chunk * FFT_N2
    const = lambda shape: pl.BlockSpec(shape, lambda c, b, k: (0, 0))
    return pl.pallas_call(
        functools.partial(_fftconv_body, n1_len=n1_len, nk1_pad=nk1_pad, chunk=chunk),
        grid=(C // LANES, B, nk1 // chunk),
        in_specs=[
            pl.BlockSpec((None, L, LANES), lambda c, b, k: (b, 0, c + yo)),
            pl.BlockSpec((None, L, LANES), lambda c, b, k: (b, 0, c + go)),
            pl.BlockSpec((1, LANES), lambda c, b, k: (0, c)),
            pl.BlockSpec((rows, LANES), lambda c, b, k: (k, c)),
            pl.BlockSpec((rows, LANES), lambda c, b, k: (k, c)),
            const(fa.shape), const(fs.shape), const(f2r.shape), const(f2i.shape),
            const(twr.shape), const(twi.shape),
        ],
        out_specs=pl.BlockSpec((None, L, LANES), lambda c, b, k: (b, 0, c)),
        out_shape=jax.ShapeDtypeStruct((B, L, C), F32),
        scratch_shapes=[pltpu.VMEM((half * FFT_PITCH, LANES), F32),
                        pltpu.VMEM((nk1_pad * FFT_PITCH, LANES), F32),
                        pltpu.VMEM((nk1_pad * FFT_PITCH, LANES), F32)],
        compiler_params=_cparams(("parallel", "parallel", "arbitrary")),
        name="fftconv_gate",
    )(y, gate, d, kr, ki, fa, fs, f2r, f2i, twr, twi)


def _hyena(z, conv_w, conv_b, fw1, fb1, ff1, fw2, fb2, ff2, fw3, fb3, hyena_d):
    B, L, C3 = z.shape
    W = C3 // 3
    zc = _shortconv(z, conv_w, conv_b)
    h, sums = _hyena_filter(L, W, fw1, fb1, ff1, fw2, fb2, ff2, fw3, fb3)
    sums = sums.reshape(HYENA_ORDER, 2, W)
    inv_norm = (1.0 / (sums[:, 0] + sums[:, 1]))[:, None, :]
    tables = _dft_tables(2 * L // FFT_N2, FFT_N2)
    kr, ki = _filter_spectrum(h, inv_norm, tables)
    y = zc
    for o in range(HYENA_ORDER):
        y = _fftconv_gate(y, 0, zc, (o + 1) * W, hyena_d[o].reshape(1, W).astype(F32),
                          kr[o], ki[o], tables)
    return y


def _merge_body(yh_ref, ya_ref, g_ref, x_ref, whu_ref, wau_ref, wo_ref, n2_ref, rwh_ref, rwl_ref,
                rb_ref, h_ref, hn_ref, lg_ref):
    D = x_ref.shape[-1]
    up_h = jnp.dot(yh_ref[...].astype(BF16), whu_ref[...], preferred_element_type=F32)
    up_a = jnp.dot(ya_ref[...].astype(BF16), wau_ref[...], preferred_element_type=F32)
    merged = jax.nn.sigmoid(g_ref[:, :D]) * up_h + jax.nn.sigmoid(g_ref[:, D:]) * up_a
    h = x_ref[...] + jnp.dot(merged.astype(BF16), wo_ref[...], preferred_element_type=F32)
    h_ref[...] = h
    ms = jnp.mean(h * h, axis=-1, keepdims=True)
    hn = h * lax.rsqrt(ms + EPS) * n2_ref[...]
    hn_ref[...] = hn.astype(BF16)
    hh, hl = _split_bf16(hn)
    lg_ref[...] = (jnp.dot(hh, rwh_ref[...], preferred_element_type=F32)
                   + jnp.dot(hl, rwh_ref[...], preferred_element_type=F32)
                   + jnp.dot(hh, rwl_ref[...], preferred_element_type=F32)) + rb_ref[...]


def _merge(yh, ya, gates, xt, whu, wau, wo, n2g, rw, rb, tm=256):
    T, D = xt.shape
    W = yh.shape[1]
    E = rw.shape[1]
    rwp = jnp.zeros((D, LOGIT_PAD), F32).at[:, :E].set(rw)
    rwh, rwl = _split_bf16(rwp)
    rbp = jnp.zeros((1, LOGIT_PAD), F32).at[0, :E].set(rb)
    rowblk = lambda w: pl.BlockSpec((tm, w), lambda i: (i, 0))
    const = lambda shape: pl.BlockSpec(shape, lambda i: (0, 0))
    return pl.pallas_call(
        _merge_body,
        grid=(T // tm,),
        in_specs=[rowblk(W), rowblk(W), rowblk(2 * D), rowblk(D),
                  const((W, D)), const((W, D)), const((D, D)), const((1, D)),
                  const((D, LOGIT_PAD)), const((D, LOGIT_PAD)), const((1, LOGIT_PAD))],
        out_specs=[rowblk(D), rowblk(D), rowblk(LOGIT_PAD)],
        out_shape=[jax.ShapeDtypeStruct((T, D), F32), jax.ShapeDtypeStruct((T, D), BF16),
                   jax.ShapeDtypeStruct((T, LOGIT_PAD), F32)],
        compiler_params=_cparams(("parallel",)),
        name="merge",
    )(yh, ya, gates, xt, whu.astype(BF16), wau.astype(BF16), wo.astype(BF16), n2g.reshape(1, D),
      rwh, rwl, rbp)


def _moe_body(be_ref, nused_ref, x_ref, w1g_ref, w1l_ref, b1g_ref, b1l_ref, w2_ref, b2_ref, g_ref,
              o_ref):
    i = pl.program_id(0)

    @pl.when(i < nused_ref[0])
    def _():
        x = x_ref[...]
        glu = jnp.dot(x, w1g_ref[...], preferred_element_type=F32) + b1g_ref[...]
        lin = jnp.dot(x, w1l_ref[...], preferred_element_type=F32) + b1l_ref[...]
        glu = jnp.minimum(glu, SWIGLU_LIMIT)
        lin = jnp.clip(lin, -SWIGLU_LIMIT, SWIGLU_LIMIT)
        act = glu * jax.nn.sigmoid(SWIGLU_ALPHA * glu) * (lin + 1.0)
        y = jnp.dot(act.astype(BF16), w2_ref[...], preferred_element_type=F32) + b2_ref[...]
        o_ref[...] = (y * g_ref[...]).astype(o_ref.dtype)

    @pl.when(i >= nused_ref[0])
    def _():
        o_ref[...] = jnp.zeros(o_ref.shape, o_ref.dtype)


def _moe_experts(xg, block_expert, n_used, w1g, w1l, b1g, b1l, w2, b2, gpad):
    P, D = xg.shape
    dff = w2.shape[1]
    nb = P // MOE_TM
    wspec = lambda k, n: pl.BlockSpec((None, k, n), lambda i, be, nu: (be[i], 0, 0))
    grid_spec = pltpu.PrefetchScalarGridSpec(
        num_scalar_prefetch=2,
        grid=(nb,),
        in_specs=[
            pl.BlockSpec((MOE_TM, D), lambda i, be, nu: (i, 0)),
            wspec(D, dff), wspec(D, dff), wspec(1, dff), wspec(1, dff),
            wspec(dff, D), wspec(1, D),
            pl.BlockSpec((MOE_TM, 1), lambda i, be, nu: (i, 0)),
        ],
        out_specs=pl.BlockSpec((MOE_TM, D), lambda i, be, nu: (i, 0)),
    )
    return pl.pallas_call(
        _moe_body,
        grid_spec=grid_spec,
        out_shape=jax.ShapeDtypeStruct((P, D), BF16),
        compiler_params=_cparams(("arbitrary",)),
        name="moe_experts",
    )(block_expert, n_used, xg, w1g, w1l, b1g, b1l, w2, b2, gpad)


def _prep_w1_body(w_ref, g_ref, l_ref, t_ref):
    half = g_ref.shape[-1]
    wt = w_ref[...].T
    for c in range(t_ref.shape[0]):
        rows = slice(c * LANES, (c + 1) * LANES)
        t_ref[c] = wt[:, rows]
        g_ref[rows, :] = t_ref[c, pl.ds(0, half, stride=2), :].T.astype(BF16)
        l_ref[rows, :] = t_ref[c, pl.ds(1, half, stride=2), :].T.astype(BF16)


def _prep_w1(w1, tc=8 * LANES):
    E, D, F2 = w1.shape
    tc = min(tc, F2)
    out = jax.ShapeDtypeStruct((E, D, F2 // 2), BF16)
    return pl.pallas_call(
        _prep_w1_body,
        grid=(E, F2 // tc),
        in_specs=[pl.BlockSpec((None, D, tc), lambda e, j: (e, 0, j))],
        out_specs=[pl.BlockSpec((None, D, tc // 2), lambda e, j: (e, 0, j)),
                   pl.BlockSpec((None, D, tc // 2), lambda e, j: (e, 0, j))],
        out_shape=[out, out],
        scratch_shapes=[pltpu.VMEM((D // LANES, tc, LANES), F32)],
        compiler_params=_cparams(("parallel", "parallel")),
        name="prep_w1",
    )(w1)


def _moe(hn_bf16, logits, w1, b1, w2, b2):
    T, D = hn_bf16.shape
    E = w1.shape[0]
    top_val, top_idx = lax.top_k(logits, TOP_K)
    gates = jax.nn.softmax(top_val, axis=-1)
    TK = T * TOP_K
    e_flat = top_idx.reshape(TK).astype(jnp.int32)
    g_flat = gates.reshape(TK)
    order = jnp.argsort(e_flat).astype(jnp.int32)
    rank = jnp.argsort(order).astype(jnp.int32)
    counts = jnp.sum(jnp.arange(E, dtype=jnp.int32)[:, None] == e_flat[None, :], axis=1,
                     dtype=jnp.int32)
    starts = jnp.cumsum(counts) - counts
    padded = ((counts + MOE_TM - 1) // MOE_TM) * MOE_TM
    pad_ends = jnp.cumsum(padded)
    pad_starts = pad_ends - padded
    nb = (TK + E * (MOE_TM - 1) + MOE_TM - 1) // MOE_TM
    P = nb * MOE_TM
    block_start = jnp.arange(nb, dtype=jnp.int32) * MOE_TM
    block_expert = jnp.minimum(jnp.sum(block_start[:, None] >= pad_ends[None, :], axis=1),
                               E - 1).astype(jnp.int32)
    n_used = (pad_ends[-1] // MOE_TM).astype(jnp.int32).reshape(1)
    row_e = jnp.repeat(block_expert, MOE_TM)
    within = jnp.arange(P, dtype=jnp.int32) - pad_starts[row_e]
    valid = within < counts[row_e]
    sorted_idx = jnp.where(valid, starts[row_e] + within, 0)
    flat_idx = order[sorted_idx]
    src = jnp.where(valid, flat_idx // TOP_K, 0)
    gpad = jnp.where(valid, g_flat[flat_idx], 0.0)
    pos = pad_starts[e_flat] + rank - starts[e_flat]
    xg = hn_bf16[src]
    w1g, w1l = _prep_w1(w1)
    y = _moe_experts(xg, block_expert, n_used, w1g, w1l,
                     b1[:, None, 0::2].astype(F32), b1[:, None, 1::2].astype(F32),
                     w2.astype(BF16), b2[:, None, :].astype(F32), gpad[:, None])
    return y[pos.reshape(T, TOP_K).T].astype(F32).sum(axis=0)


def _rope_tables(L):
    rows = L // GRID_W
    row = jnp.repeat(jnp.arange(rows, dtype=F32), GRID_W)
    col = jnp.tile(jnp.arange(GRID_W, dtype=F32), rows)
    half = HEAD_DIM // 2
    freqs = ROPE_THETA ** (-jnp.arange(0, half, 2, dtype=F32) / half)
    ang = jnp.concatenate([row[:, None] * freqs, col[:, None] * freqs], axis=-1)
    cos = jnp.repeat(jnp.cos(ang), 2, axis=-1)
    sin = jnp.repeat(jnp.sin(ang), 2, axis=-1)
    sign = jnp.tile(jnp.array([-1.0, 1.0], F32), HEAD_DIM // 2)
    reps = LANES // HEAD_DIM
    return jnp.tile(cos, (1, reps)), jnp.tile(sin * sign, (1, reps))


def kernel(x, norm1_g, w_in, conv_w, conv_b, filt_w1, filt_b1, filt_freq1, filt_w2, filt_b2, filt_freq2, filt_w3, filt_b3, hyena_d, q_norm_g, k_norm_g, w_hyena_up, w_attn_up, w_out, norm2_g, router_w, router_b, expert_w1, expert_b1, expert_w2, expert_b2):
    B, L, D = x.shape
    T = B * L
    depth = w_in.shape[0]
    hw = conv_w.shape[-1] // 3
    aw = N_Q_HEADS * HEAD_DIM
    kvw = N_KV_HEADS * HEAD_DIM
    widths = (3 * hw, aw, 2 * kvw, 2 * D)
    cosf, sinf = _rope_tables(L)
    for l in range(depth):
        xt = x.reshape(T, D)
        z, q, kv, gates = _inproj(xt, norm1_g[l], w_in[l].astype(BF16), widths)
        y_hy = _hyena(z.reshape(B, L, 3 * hw), conv_w[l], conv_b[l], filt_w1[l], filt_b1[l],
                      filt_freq1[l], filt_w2[l], filt_b2[l], filt_freq2[l], filt_w3[l], filt_b3[l],
                      hyena_d[l])
        gq = jnp.tile(q_norm_g[l].astype(F32), N_Q_HEADS)[None, :]
        gk = jnp.tile(k_norm_g[l].astype(F32), N_KV_HEADS)[None, :]
        qr, kr, va = _qkrope(q, kv, gq, gk, cosf, sinf, L)
        y_at = _attention(qr.reshape(B, L, aw), kr.reshape(B, L, kvw),
                          va.reshape(N_KV_HEADS, B, L, kvw))
        h, hn, logits = _merge(y_hy.reshape(T, hw), y_at.reshape(T, aw), gates, xt,
                               w_hyena_up[l], w_attn_up[l], w_out[l], norm2_g[l],
                               router_w[l], router_b[l])
        mo = _moe(hn, logits[:, :N_EXPERTS], expert_w1[l], expert_b1[l], expert_w2[l], expert_b2[l])
        x = (h + mo).reshape(B, L, D)
    return x
```

```python
import functools
import math

import jax
import jax.numpy as jnp
from jax import lax
from jax.experimental import pallas as pl
from jax.experimental.pallas import tpu as pltpu

F32 = jnp.float32
BF16 = jnp.bfloat16

GRID_W = 64
HEAD_DIM = 64
N_Q_HEADS = 8
N_KV_HEADS = 2
Q_PER_KV = N_Q_HEADS // N_KV_HEADS
ROPE_THETA = 10000.0
HYENA_ORDER = 2
SHORT_CONV = 3
FILTER_EMB = 33
FAST_DECAY_PCT = 0.3
SLOW_DECAY_PCT = 1.5
DECAY_TARGET = 1e-2
N_EXPERTS = 32
TOP_K = 4
SWIGLU_LIMIT = 7.0
SWIGLU_ALPHA = 1.702
EPS = 1e-6

LANES = 128
SUBLANES = 8
VMEM_LIMIT = 56 * 1024 * 1024

FFT_N2 = LANES
FFT_PITCH = FFT_N2 + SUBLANES
FFT_K1_CHUNK_MAX = 16
FFT_UNROLL_OUTER = 16
FFT_UNROLL_INNER = True
MOE_TM = 512
LOGIT_PAD = LANES


def _cparams(sem):
    return pltpu.CompilerParams(dimension_semantics=sem, vmem_limit_bytes=VMEM_LIMIT)


def _split_bf16(x):
    hi = x.astype(BF16)
    lo = (x - hi.astype(F32)).astype(BF16)
    return hi, lo


def _inproj_body(x_ref, g_ref, w_ref, z_ref, q_ref, kv_ref, gate_ref, *, widths):
    x = x_ref[...]
    ms = jnp.mean(x * x, axis=-1, keepdims=True)
    u = (x * lax.rsqrt(ms + EPS) * g_ref[...]).astype(BF16)
    off = 0
    for ref, w in zip((z_ref, q_ref, kv_ref, gate_ref), widths):
        ref[...] = jnp.dot(u, w_ref[:, off:off + w], preferred_element_type=F32)
        off += w


def _inproj(xt, g, w_bf16, widths, tm=512):
    T, D = xt.shape
    n = w_bf16.shape[1]
    return pl.pallas_call(
        functools.partial(_inproj_body, widths=widths),
        grid=(T // tm,),
        in_specs=[
            pl.BlockSpec((tm, D), lambda i: (i, 0)),
            pl.BlockSpec((1, D), lambda i: (0, 0)),
            pl.BlockSpec((D, n), lambda i: (0, 0)),
        ],
        out_specs=[pl.BlockSpec((tm, w), lambda i: (i, 0)) for w in widths],
        out_shape=[jax.ShapeDtypeStruct((T, w), F32) for w in widths],
        compiler_params=_cparams(("parallel",)),
        name="inproj",
    )(xt, g.reshape(1, D), w_bf16)


def _head_norm_rope(x, gain, cosf, sinf, ones_blk):
    w = x.shape[-1]
    hi, lo = _split_bf16(x * x)
    ss = (jnp.dot(hi, ones_blk, preferred_element_type=F32)
          + jnp.dot(lo, ones_blk, preferred_element_type=F32))
    xn = x * lax.rsqrt(ss * (1.0 / HEAD_DIM) + EPS) * gain
    lane = lax.broadcasted_iota(jnp.int32, (x.shape[0], LANES), 1)
    cols = []
    for c in range(w // LANES):
        col = xn[:, c * LANES:(c + 1) * LANES]
        cols.append(jnp.where(lane % 2 == 0, pltpu.roll(col, LANES - 1, 1), pltpu.roll(col, 1, 1)))
    swapped = cols[0] if len(cols) == 1 else jnp.concatenate(cols, axis=1)
    return xn * cosf + swapped * sinf


def _qkrope_body(q_ref, kv_ref, gq_ref, gk_ref, cos_ref, sin_ref, oq_ref, ok_ref,
                 qo_ref, ko_ref, vo_ref):
    cosf = cos_ref[...]
    sinf = sin_ref[...]
    nq = q_ref.shape[-1] // LANES
    q = _head_norm_rope(q_ref[...], gq_ref[...], jnp.tile(cosf, (1, nq)), jnp.tile(sinf, (1, nq)),
                        oq_ref[...])
    qo_ref[...] = (q * (HEAD_DIM ** -0.5 * math.log2(math.e))).astype(BF16)
    kv = kv_ref[...]
    kw = kv.shape[-1] // 2
    k = _head_norm_rope(kv[:, :kw], gk_ref[...], cosf, sinf, ok_ref[...])
    ko_ref[...] = k.astype(BF16)
    v = kv[:, kw:]
    lane = lax.broadcasted_iota(jnp.int32, v.shape, 1)
    for h in range(N_KV_HEADS):
        vo_ref[h] = jnp.where((lane // HEAD_DIM) == h, v, 1.0).astype(BF16)


def _qkrope(q, kv, gq, gk, cosf, sinf, seq_len, tm=512):
    T, qw = q.shape
    kw = kv.shape[1] // 2
    assert kw == LANES and N_KV_HEADS * HEAD_DIM == LANES and N_KV_HEADS == 2
    nl = seq_len // tm

    def blk_ones(w):
        r = jnp.arange(w) // HEAD_DIM
        return (r[:, None] == r[None, :]).astype(BF16)

    return pl.pallas_call(
        _qkrope_body,
        grid=(T // tm,),
        in_specs=[
            pl.BlockSpec((tm, qw), lambda i: (i, 0)),
            pl.BlockSpec((tm, 2 * kw), lambda i: (i, 0)),
            pl.BlockSpec((1, qw), lambda i: (0, 0)),
            pl.BlockSpec((1, kw), lambda i: (0, 0)),
            pl.BlockSpec((tm, LANES), lambda i: (i % nl, 0)),
            pl.BlockSpec((tm, LANES), lambda i: (i % nl, 0)),
            pl.BlockSpec((qw, qw), lambda i: (0, 0)),
            pl.BlockSpec((kw, kw), lambda i: (0, 0)),
        ],
        out_specs=[
            pl.BlockSpec((tm, qw), lambda i: (i, 0)),
            pl.BlockSpec((tm, kw), lambda i: (i, 0)),
            pl.BlockSpec((N_KV_HEADS, tm, kw), lambda i: (0, i, 0)),
        ],
        out_shape=[
            jax.ShapeDtypeStruct((T, qw), BF16),
            jax.ShapeDtypeStruct((T, kw), BF16),
            jax.ShapeDtypeStruct((N_KV_HEADS, T, kw), BF16),
        ],
        compiler_params=_cparams(("parallel",)),
        name="qkrope",
    )(q, kv, gq, gk, cosf, sinf, blk_ones(qw), blk_ones(kw))


def _attn_body(q_ref, k_ref, v_ref, o_ref, qs_ref, m_ref, acc_ref, *, tk, nsplit):
    kvh = pl.program_id(1)
    tq = q_ref.shape[0]
    seq = k_ref.shape[0]
    rows = Q_PER_KV * tq
    lane = lax.broadcasted_iota(jnp.int32, (tq, LANES), 1)
    in_head = (lane // HEAD_DIM) == kvh

    for g in range(Q_PER_KV):
        col = q_ref[:, (g // 2) * LANES:(g // 2 + 1) * LANES].astype(F32)
        col = jnp.where((g % 2) == kvh, col, pltpu.roll(col, HEAD_DIM, 1))
        qs_ref[g * tq:(g + 1) * tq, :] = jnp.where(in_head, col, 0.0).astype(BF16)
    m_ref[...] = jnp.full(m_ref.shape, -jnp.inf, F32)
    acc_ref[...] = jnp.zeros(acc_ref.shape, F32)
    part = rows // nsplit

    def step(c, carry):
        r0 = pl.multiple_of(c * tk, tk)
        kc = k_ref[pl.ds(r0, tk), :]
        vc = v_ref[pl.ds(r0, tk), :]
        for h in range(nsplit):
            sl = slice(h * part, (h + 1) * part)
            s = lax.dot_general(qs_ref[sl, :], kc, (((1,), (1,)), ((), ())),
                                preferred_element_type=F32)
            m_prev = m_ref[sl, :]
            m_new = jnp.maximum(m_prev, jnp.max(s, axis=-1, keepdims=True))
            p = jnp.exp2(s - jnp.tile(m_new, (1, tk // LANES)))
            acc_ref[sl, :] = jnp.exp2(m_prev - m_new) * acc_ref[sl, :] + jnp.dot(
                p.astype(BF16), vc, preferred_element_type=F32)
            m_ref[sl, :] = m_new
        return carry

    lax.fori_loop(0, seq // tk, step, 0)

    acc = acc_ref[...]
    o = acc / pltpu.roll(acc, HEAD_DIM, 1)
    for c in range(Q_PER_KV // 2):
        even = o[(2 * c) * tq:(2 * c + 1) * tq, :]
        odd = o[(2 * c + 1) * tq:(2 * c + 2) * tq, :]
        even = jnp.where(kvh == 0, even, pltpu.roll(even, HEAD_DIM, 1))
        odd = jnp.where(kvh == 1, odd, pltpu.roll(odd, HEAD_DIM, 1))
        o_ref[:, c * LANES:(c + 1) * LANES] = jnp.where(lane < HEAD_DIM, even, odd)


def _attention(q, k, v_aug, tq=512, tk=2048, nsplit=2):
    B, L, qw = q.shape
    gw = Q_PER_KV * HEAD_DIM
    kw = k.shape[-1]
    rows = Q_PER_KV * tq
    return pl.pallas_call(
        functools.partial(_attn_body, tk=tk, nsplit=nsplit),
        grid=(B, N_KV_HEADS, L // tq),
        in_specs=[
            pl.BlockSpec((None, tq, gw), lambda b, h, i: (b, i, h)),
            pl.BlockSpec((None, L, kw), lambda b, h, i: (b, 0, 0)),
            pl.BlockSpec((None, None, L, kw), lambda b, h, i: (h, b, 0, 0)),
        ],
        out_specs=pl.BlockSpec((None, tq, gw), lambda b, h, i: (b, i, h)),
        out_shape=jax.ShapeDtypeStruct((B, L, qw), F32),
        scratch_shapes=[
            pltpu.VMEM((rows, kw), BF16),
            pltpu.VMEM((rows, LANES), F32),
            pltpu.VMEM((rows, kw), F32),
        ],
        compiler_params=_cparams(("parallel", "parallel", "parallel")),
        name="attention",
    )(q, k, v_aug)


def _hdot(a, b):
    ah, al = _split_bf16(a)
    bh, bl = _split_bf16(b)
    return (jnp.dot(ah, bh, preferred_element_type=F32)
            + jnp.dot(al, bh, preferred_element_type=F32)
            + jnp.dot(ah, bl, preferred_element_type=F32))


def _filter_body(feat_ref, t_ref, w1_ref, b1_ref, f1_ref, w2_ref, b2_ref, f2_ref, w3_ref, b3_ref,
                 delta_ref, bwd_ref, h_ref, sum_ref):
    i = pl.program_id(0)
    h = jnp.sin(f1_ref[...] * (_hdot(feat_ref[...], w1_ref[...]) + b1_ref[...]))
    h = jnp.sin(f2_ref[...] * (_hdot(h, w2_ref[...]) + b2_ref[...]))
    h = _hdot(h, w3_ref[...]) + b3_ref[...]
    t = t_ref[...]
    h = h * jnp.exp(-t * delta_ref[...])
    width = h_ref.shape[-1]
    for j in range(h_ref.shape[0]):
        h_ref[j] = h[:, j * width:(j + 1) * width]
    row = lax.broadcasted_iota(jnp.int32, h.shape, 0) + i * h.shape[0]
    a = jnp.where((row == 0) & (bwd_ref[...] > 0.5), 0.0, jnp.abs(h))
    part = jnp.sum(a, axis=0, keepdims=True)

    @pl.when(i == 0)
    def _():
        sum_ref[...] = jnp.zeros(sum_ref.shape, F32)

    sum_ref[...] += jnp.broadcast_to(part, sum_ref.shape)


def _hyena_filter(L, width, fw1, fb1, ff1, fw2, fb2, ff2, fw3, fb3, tl=512):
    bands = (FILTER_EMB - 1) // 2
    t = jnp.linspace(0.0, 1.0, L, dtype=F32)[:, None]
    w = 2.0 * math.pi * jnp.arange(L, dtype=F32)[:, None] / L
    fr = jnp.linspace(1e-4, bands - 1, bands, dtype=F32)[None, :]
    feats = jnp.concatenate([t, jnp.cos(w * fr), -jnp.sin(w * fr)], axis=-1)
    max_decay = math.log(DECAY_TARGET) / FAST_DECAY_PCT
    min_decay = math.log(DECAY_TARGET) / SLOW_DECAY_PCT
    deltas = jnp.abs(jnp.linspace(min_decay, max_decay, width, dtype=F32))
    ncol = fw3.shape[1]
    delta_cols = jnp.tile(deltas, ncol // width)[None, :]
    is_bwd = ((jnp.arange(ncol) // width) % 2).astype(F32)[None, :]
    emb = hid = LANES

    def pad2(a, r, c):
        a = a.astype(F32)
        return jnp.zeros((r, c), F32).at[:a.shape[0], :a.shape[1]].set(a)

    row = lambda a: pad2(a.reshape(1, -1), 1, hid)
    feats = pad2(feats, L, emb)
    fw1, fw2, fw3 = pad2(fw1, emb, hid), pad2(fw2, hid, hid), pad2(fw3, hid, ncol)
    fb3 = fb3.reshape(1, ncol).astype(F32)
    const = lambda shape: pl.BlockSpec(shape, lambda i: (0, 0))
    h, sums = pl.pallas_call(
        _filter_body,
        grid=(L // tl,),
        in_specs=[
            pl.BlockSpec((tl, emb), lambda i: (i, 0)),
            pl.BlockSpec((tl, 1), lambda i: (i, 0)),
            const((emb, hid)), const((1, hid)), const((1, hid)),
            const((hid, hid)), const((1, hid)), const((1, hid)),
            const((hid, ncol)), const((1, ncol)), const((1, ncol)), const((1, ncol)),
        ],
        out_specs=[pl.BlockSpec((ncol // width, tl, width), lambda i: (0, i, 0)),
                   pl.BlockSpec((SUBLANES, ncol), lambda i: (0, 0))],
        out_shape=[jax.ShapeDtypeStruct((ncol // width, L, width), F32),
                   jax.ShapeDtypeStruct((SUBLANES, ncol), F32)],
        compiler_params=_cparams(("arbitrary",)),
        name="hyena_filter",
    )(feats, t, fw1, row(fb1), row(ff1), fw2, row(fb2), row(ff2), fw3, fb3, delta_cols, is_bwd)
    return h, sums[0]


def _shortconv_body(z_ref, w_ref, b_ref, o_ref, *, chunk):
    L = z_ref.shape[0]
    w0, w1, w2 = w_ref[0:1, :], w_ref[1:2, :], w_ref[2:3, :]
    bias = b_ref[...]
    row = lax.broadcasted_iota(jnp.int32, (chunk, z_ref.shape[1]), 0)

    def step(c, carry):
        r0 = pl.multiple_of(c * chunk, chunk)
        cur = z_ref[pl.ds(r0, chunk), :]
        prev_row = z_ref[pl.ds(jnp.maximum(r0 - 1, 0), 1), :]
        next_row = z_ref[pl.ds(jnp.minimum(r0 + chunk, L - 1), 1), :]
        prev_row = jnp.where(c == 0, 0.0, prev_row)
        next_row = jnp.where(c == L // chunk - 1, 0.0, next_row)
        down = jnp.where(row == 0, prev_row, pltpu.roll(cur, 1, 0))
        up = jnp.where(row == chunk - 1, next_row, pltpu.roll(cur, chunk - 1, 0))
        o_ref[pl.ds(r0, chunk), :] = bias + down * w0 + cur * w1 + up * w2
        return carry

    lax.fori_loop(0, L // chunk, step, 0)


def _shortconv(z, conv_w, conv_b, chunk=256):
    B, L, C = z.shape
    return pl.pallas_call(
        functools.partial(_shortconv_body, chunk=chunk),
        grid=(B, C // LANES),
        in_specs=[
            pl.BlockSpec((None, L, LANES), lambda b, c: (b, 0, c)),
            pl.BlockSpec((SHORT_CONV, LANES), lambda b, c: (0, c)),
            pl.BlockSpec((1, LANES), lambda b, c: (0, c)),
        ],
        out_specs=pl.BlockSpec((None, L, LANES), lambda b, c: (b, 0, c)),
        out_shape=jax.ShapeDtypeStruct((B, L, C), F32),
        compiler_params=_cparams(("parallel", "parallel")),
        name="shortconv",
    )(z, conv_w, conv_b.reshape(1, C))


def _dft_tables(n1_len, n2_len):
    n = n1_len * n2_len

    def root(num, den):
        ang = (2.0 * math.pi / den) * (num % den).astype(F32)
        return jnp.cos(ang), -jnp.sin(ang)

    i1 = jnp.arange(n1_len, dtype=jnp.int32)
    i2 = jnp.arange(n2_len, dtype=jnp.int32)
    f1r, f1i = root(i1[:, None] * i1[None, :], n1_len)
    f2r, f2i = root(i2[:, None] * i2[None, :], n2_len)
    twr, twi = root(i1[:, None] * i2[None, :], n)
    return (f1r, f1i), (f2r, f2i), (twr, twi)


def _stacked_inner_dft(f2r, f2i, twr_row, twi_row):
    gr = f2r * twr_row - f2i * twi_row
    gi = f2r * twi_row + f2i * twr_row
    top = jnp.concatenate([gr, -gi], axis=1)
    bot = jnp.concatenate([gi, gr], axis=1)
    return jnp.concatenate([top, bot], axis=0)


def _fft_plan(n):
    n1_len = n // FFT_N2
    nk1 = n1_len // 2 + 1
    nk1_pad = -(-nk1 // SUBLANES) * SUBLANES
    chunk = max(c for c in range(1, FFT_K1_CHUNK_MAX + 1) if nk1 % c == 0)
    return n1_len, nk1, nk1_pad, chunk


def _outer_dft_to_scratch(load_rows, fa_ref, ar_ref, ai_ref, nk1_pad):
    fa = fa_ref[...]

    def step(n2, carry):
        a = jnp.dot(fa, load_rows(n2).astype(BF16), preferred_element_type=F32)
        ar_ref[pl.ds(n2, nk1_pad, stride=FFT_PITCH), :] = a[:nk1_pad]
        ai_ref[pl.ds(n2, nk1_pad, stride=FFT_PITCH), :] = a[nk1_pad:]
        return carry

    lax.fori_loop(0, FFT_N2, step, 0, unroll=FFT_UNROLL_OUTER)


def _spec_body(f_ref, b_ref, inv_ref, fa_ref, f2r_ref, f2i_ref, twr_ref, twi_ref, kr_ref, ki_ref,
               fr_ref, fi_ref, br_ref, bi_ref, *, n1_len, nk1_pad, chunk):
    kc = pl.program_id(2)
    half = n1_len // 2

    @pl.when(kc == 0)
    def _():
        inv = inv_ref[...]
        row = lax.broadcasted_iota(jnp.int32, (half, LANES), 0)
        _outer_dft_to_scratch(lambda n2: f_ref[pl.ds(n2, half, stride=FFT_N2), :] * inv,
                              fa_ref, fr_ref, fi_ref, nk1_pad)
        _outer_dft_to_scratch(
            lambda n2: jnp.where((row == 0) & (n2 == 0), 0.0,
                                 b_ref[pl.ds(n2, half, stride=FFT_N2), :] * inv),
            fa_ref, br_ref, bi_ref, nk1_pad)

    f2r = f2r_ref[...]
    f2i = f2i_ref[...]

    def step(t, carry):
        k1 = kc * chunk + t
        base = pl.multiple_of(k1 * FFT_PITCH, SUBLANES)
        mf = _stacked_inner_dft(f2r, f2i, twr_ref[pl.ds(k1, 1), :],
                                twi_ref[pl.ds(k1, 1), :]).astype(BF16)
        rows = pl.ds(base, FFT_N2)
        xf = jnp.dot(mf, jnp.concatenate([fr_ref[rows, :], fi_ref[rows, :]], axis=0).astype(BF16),
                     preferred_element_type=F32)
        xb = jnp.dot(mf, jnp.concatenate([br_ref[rows, :], bi_ref[rows, :]], axis=0).astype(BF16),
                     preferred_element_type=F32)
        o = pl.multiple_of(t * FFT_N2, FFT_N2)
        kr_ref[pl.ds(o, FFT_N2), :] = xf[:FFT_N2] + xb[:FFT_N2]
        ki_ref[pl.ds(o, FFT_N2), :] = xf[FFT_N2:] - xb[FFT_N2:]
        return carry

    lax.fori_loop(0, chunk, step, 0)


def _filter_spectrum(h, inv_norm, tables):
    O2, L, C = h.shape
    O = O2 // 2
    n1_len, nk1, nk1_pad, chunk = _fft_plan(2 * L)
    half = n1_len // 2
    (f1r, f1i), (f2r, f2i), (twr, twi) = tables
    fa = jnp.concatenate([f1r[:nk1_pad, :half], f1i[:nk1_pad, :half]], axis=0).astype(BF16)
    rows = chunk * FFT_N2
    const = lambda shape: pl.BlockSpec(shape, lambda o, c, k: (0, 0))
    out = jax.ShapeDtypeStruct((O, nk1 * FFT_N2, C), F32)
    scratch = pltpu.VMEM((nk1_pad * FFT_PITCH, LANES), F32)
    return pl.pallas_call(
        functools.partial(_spec_body, n1_len=n1_len, nk1_pad=nk1_pad, chunk=chunk),
        grid=(O, C // LANES, nk1 // chunk),
        in_specs=[
            pl.BlockSpec((None, L, LANES), lambda o, c, k: (2 * o, 0, c)),
            pl.BlockSpec((None, L, LANES), lambda o, c, k: (2 * o + 1, 0, c)),
            pl.BlockSpec((None, 1, LANES), lambda o, c, k: (o, 0, c)),
            const(fa.shape), const(f2r.shape), const(f2i.shape), const(twr.shape), const(twi.shape),
        ],
        out_specs=[pl.BlockSpec((None, rows, LANES), lambda o, c, k: (o, k, c)),
                   pl.BlockSpec((None, rows, LANES), lambda o, c, k: (o, k, c))],
        out_shape=[out, out],
        scratch_shapes=[scratch, scratch, scratch, scratch],
        compiler_params=_cparams(("parallel", "parallel", "arbitrary")),
        name="filter_spectrum",
    )(h, h, inv_norm, fa, f2r, f2i, twr, twi)


def _fftconv_body(y_ref, gate_ref, d_ref, kr_ref, ki_ref, fa_ref, fs_ref, f2r_ref, f2i_ref,
                  twr_ref, twi_ref, o_ref, xs_ref, ar_ref, ai_ref, *, n1_len, nk1_pad, chunk):
    kc = pl.program_id(2)
    half = n1_len // 2

    @pl.when(kc == 0)
    def _():
        def copy(n1, carry):
            src = pl.multiple_of(n1 * FFT_N2, FFT_N2)
            dst = pl.multiple_of(n1 * FFT_PITCH, SUBLANES)
            xs_ref[pl.ds(dst, FFT_N2), :] = y_ref[pl.ds(src, FFT_N2), :]
            return carry

        lax.fori_loop(0, half, copy, 0)
        _outer_dft_to_scratch(lambda n2: xs_ref[pl.ds(n2, half, stride=FFT_PITCH), :],
                              fa_ref, ar_ref, ai_ref, nk1_pad)

    f2r = f2r_ref[...]
    f2i = f2i_ref[...]

    def step(t, carry):
        k1 = kc * chunk + t
        base = pl.multiple_of(k1 * FFT_PITCH, SUBLANES)
        rhs = jnp.concatenate([ar_ref[pl.ds(base, FFT_N2), :], ai_ref[pl.ds(base, FFT_N2), :]],
                              axis=0).astype(BF16)
        mf = _stacked_inner_dft(f2r, f2i, twr_ref[pl.ds(k1, 1), :], twi_ref[pl.ds(k1, 1), :])
        x = jnp.dot(mf.astype(BF16), rhs, preferred_element_type=F32)
        xr, xi = x[:FFT_N2], x[FFT_N2:]
        o = pl.multiple_of(t * FFT_N2, FFT_N2)
        kr = kr_ref[pl.ds(o, FFT_N2), :]
        ki = ki_ref[pl.ds(o, FFT_N2), :]
        z = jnp.concatenate([xr * kr - xi * ki, xr * ki + xi * kr], axis=0).astype(BF16)
        b = jnp.dot(mf.T.astype(BF16), z, preferred_element_type=F32)
        ar_ref[pl.ds(base, FFT_N2), :] = b[:FFT_N2]
        ai_ref[pl.ds(base, FFT_N2), :] = b[FFT_N2:]
        return carry

    lax.fori_loop(0, chunk, step, 0, unroll=FFT_UNROLL_INNER)

    @pl.when(kc == pl.num_programs(2) - 1)
    def _():
        fs = fs_ref[...]

        def inv_outer(n2, carry):
            rhs = jnp.concatenate([ar_ref[pl.ds(n2, nk1_pad, stride=FFT_PITCH), :],
                                   ai_ref[pl.ds(n2, nk1_pad, stride=FFT_PITCH), :]],
                                  axis=0).astype(BF16)
            conv = jnp.dot(fs, rhs, preferred_element_type=F32)
            ar_ref[pl.ds(n2, half, stride=FFT_PITCH), :] = conv
            return carry

        lax.fori_loop(0, FFT_N2, inv_outer, 0, unroll=FFT_UNROLL_OUTER)
        d = d_ref[...]

        def finish(n1, carry):
            src = pl.multiple_of(n1 * FFT_PITCH, SUBLANES)
            dst = pl.multiple_of(n1 * FFT_N2, FFT_N2)
            y = y_ref[pl.ds(dst, FFT_N2), :]
            o_ref[pl.ds(dst, FFT_N2), :] = gate_ref[pl.ds(dst, FFT_N2), :] * (
                ar_ref[pl.ds(src, FFT_N2), :] + y * d)
            return carry

        lax.fori_loop(0, half, finish, 0)


def _fftconv_gate(y, y_off, gate, gate_off, d, kr, ki, tables):
    B, L, _ = y.shape
    C = d.shape[-1]
    yo, go = y_off // LANES, gate_off // LANES
    N = 2 * L
    n1_len, nk1, nk1_pad, chunk = _fft_plan(N)
    half = n1_len // 2
    (f1r, f1i), (f2r, f2i), (twr, twi) = tables
    fa = jnp.concatenate([f1r[:nk1_pad, :half], f1i[:nk1_pad, :half]], axis=0).astype(BF16)
    wts = jnp.concatenate([jnp.ones((1,), F32), jnp.full((nk1 - 2,), 2.0, F32), jnp.ones((1,), F32),
                           jnp.zeros((nk1_pad - nk1,), F32)]) * (1.0 / N)
    fs = jnp.concatenate([f1r[:half, :nk1_pad] * wts, f1i[:half, :nk1_pad] * wts],
                         axis=1).astype(BF16)
    rows = chunk * FFT_N2
    const = lambda shape: pl.BlockSpec(shape, lambda c, b, k: (0, 0))
    return pl.pallas_call(
        functools.partial(_fftconv_body, n1_len=n1_len, nk1_pad=nk1_pad, chunk=chunk),
        grid=(C // LANES, B, nk1 // chunk),
        in_specs=[
            pl.BlockSpec((None, L, LANES), lambda c, b, k: (b, 0, c + yo)),
            pl.BlockSpec((None, L, LANES), lambda c, b, k: (b, 0, c + go)),
            pl.BlockSpec((1, LANES), lambda c, b, k: (0, c)),
            pl.BlockSpec((rows, LANES), lambda c, b, k: (k, c)),
            pl.BlockSpec((rows, LANES), lambda c, b, k: (k, c)),
            const(fa.shape), const(fs.shape), const(f2r.shape), const(f2i.shape),
            const(twr.shape), const(twi.shape),
        ],
        out_specs=pl.BlockSpec((None, L, LANES), lambda c, b, k: (b, 0, c)),
        out_shape=jax.ShapeDtypeStruct((B, L, C), F32),
        scratch_shapes=[pltpu.VMEM((half * FFT_PITCH, LANES), F32),
                        pltpu.VMEM((nk1_pad * FFT_PITCH, LANES), F32),
                        pltpu.VMEM((nk1_pad * FFT_PITCH, LANES), F32)],
        compiler_params=_cparams(("parallel", "parallel", "arbitrary")),
        name="fftconv_gate",
    )(y, gate, d, kr, ki, fa, fs, f2r, f2i, twr, twi)


def _hyena(z, conv_w, conv_b, fw1, fb1, ff1, fw2, fb2, ff2, fw3, fb3, hyena_d):
    B, L, C3 = z.shape
    W = C3 // 3
    zc = _shortconv(z, conv_w, conv_b)
    h, sums = _hyena_filter(L, W, fw1, fb1, ff1, fw2, fb2, ff2, fw3, fb3)
    sums = sums.reshape(HYENA_ORDER, 2, W)
    inv_norm = (1.0 / (sums[:, 0] + sums[:, 1]))[:, None, :]
    tables = _dft_tables(2 * L // FFT_N2, FFT_N2)
    kr, ki = _filter_spectrum(h, inv_norm, tables)
    y = zc
    for o in range(HYENA_ORDER):
        y = _fftconv_gate(y, 0, zc, (o + 1) * W, hyena_d[o].reshape(1, W).astype(F32),
                          kr[o], ki[o], tables)
    return y


def _merge_body(yh_ref, ya_ref, g_ref, x_ref, whu_ref, wau_ref, wo_ref, n2_ref, rwh_ref, rwl_ref,
                rb_ref, h_ref, hn_ref, lg_ref):
    D = x_ref.shape[-1]
    up_h = jnp.dot(yh_ref[...].astype(BF16), whu_ref[...], preferred_element_type=F32)
    up_a = jnp.dot(ya_ref[...].astype(BF16), wau_ref[...], preferred_element_type=F32)
    merged = jax.nn.sigmoid(g_ref[:, :D]) * up_h + jax.nn.sigmoid(g_ref[:, D:]) * up_a
    h = x_ref[...] + jnp.dot(merged.astype(BF16), wo_ref[...], preferred_element_type=F32)
    h_ref[...] = h
    ms = jnp.mean(h * h, axis=-1, keepdims=True)
    hn = h * lax.rsqrt(ms + EPS) * n2_ref[...]
    hn_ref[...] = hn.astype(BF16)
    hh, hl = _split_bf16(hn)
    lg_ref[...] = (jnp.dot(hh, rwh_ref[...], preferred_element_type=F32)
                   + jnp.dot(hl, rwh_ref[...], preferred_element_type=F32)
                   + jnp.dot(hh, rwl_ref[...], preferred_element_type=F32)) + rb_ref[...]


def _merge(yh, ya, gates, xt, whu, wau, wo, n2g, rw, rb, tm=256):
    T, D = xt.shape
    W = yh.shape[1]
    E = rw.shape[1]
    rwp = jnp.zeros((D, LOGIT_PAD), F32).at[:, :E].set(rw)
    rwh, rwl = _split_bf16(rwp)
    rbp = jnp.zeros((1, LOGIT_PAD), F32).at[0, :E].set(rb)
    rowblk = lambda w: pl.BlockSpec((tm, w), lambda i: (i, 0))
    const = lambda shape: pl.BlockSpec(shape, lambda i: (0, 0))
    return pl.pallas_call(
        _merge_body,
        grid=(T // tm,),
        in_specs=[rowblk(W), rowblk(W), rowblk(2 * D), rowblk(D),
                  const((W, D)), const((W, D)), const((D, D)), const((1, D)),
                  const((D, LOGIT_PAD)), const((D, LOGIT_PAD)), const((1, LOGIT_PAD))],
        out_specs=[rowblk(D), rowblk(D), rowblk(LOGIT_PAD)],
        out_shape=[jax.ShapeDtypeStruct((T, D), F32), jax.ShapeDtypeStruct((T, D), BF16),
                   jax.ShapeDtypeStruct((T, LOGIT_PAD), F32)],
        compiler_params=_cparams(("parallel",)),
        name="merge",
    )(yh, ya, gates, xt, whu.astype(BF16), wau.astype(BF16), wo.astype(BF16), n2g.reshape(1, D),
      rwh, rwl, rbp)


def _moe_body(be_ref, nused_ref, x_ref, w1g_ref, w1l_ref, b1g_ref, b1l_ref, w2_ref, b2_ref, o_ref):
    i = pl.program_id(0)

    @pl.when(i < nused_ref[0])
    def _():
        x = x_ref[...]
        glu = jnp.dot(x, w1g_ref[...], preferred_element_type=F32) + b1g_ref[...]
        lin = jnp.dot(x, w1l_ref[...], preferred_element_type=F32) + b1l_ref[...]
        glu = jnp.minimum(glu, SWIGLU_LIMIT)
        lin = jnp.clip(lin, -SWIGLU_LIMIT, SWIGLU_LIMIT)
        act = glu * jax.nn.sigmoid(SWIGLU_ALPHA * glu) * (lin + 1.0)
        y = jnp.dot(act.astype(BF16), w2_ref[...].astype(BF16),
                    preferred_element_type=F32) + b2_ref[...]
        o_ref[...] = y.astype(o_ref.dtype)

    @pl.when(i >= nused_ref[0])
    def _():
        o_ref[...] = jnp.zeros(o_ref.shape, o_ref.dtype)


def _moe_experts(xg, block_expert, n_used, w1g, w1l, b1g, b1l, w2, b2):
    P, D = xg.shape
    dff = w2.shape[1]
    nb = P // MOE_TM
    wspec = lambda k, n: pl.BlockSpec((None, k, n), lambda i, be, nu: (be[i], 0, 0))
    grid_spec = pltpu.PrefetchScalarGridSpec(
        num_scalar_prefetch=2,
        grid=(nb,),
        in_specs=[
            pl.BlockSpec((MOE_TM, D), lambda i, be, nu: (i, 0)),
            wspec(D, dff), wspec(D, dff), wspec(1, dff), wspec(1, dff),
            wspec(dff, D), wspec(1, D),
        ],
        out_specs=pl.BlockSpec((MOE_TM, D), lambda i, be, nu: (i, 0)),
    )
    return pl.pallas_call(
        _moe_body,
        grid_spec=grid_spec,
        out_shape=jax.ShapeDtypeStruct((P, D), BF16),
        compiler_params=_cparams(("arbitrary",)),
        name="moe_experts",
    )(block_expert, n_used, xg, w1g, w1l, b1g, b1l, w2, b2)


def _prep_w1_body(w_ref, g_ref, l_ref, t_ref):
    half = g_ref.shape[-1]
    wt = w_ref[...].T
    for c in range(t_ref.shape[0]):
        rows = slice(c * LANES, (c + 1) * LANES)
        t_ref[c] = wt[:, rows]
        g_ref[rows, :] = t_ref[c, pl.ds(0, half, stride=2), :].T.astype(BF16)
        l_ref[rows, :] = t_ref[c, pl.ds(1, half, stride=2), :].T.astype(BF16)


def _prep_w1(w1, tc=8 * LANES):
    E, D, F2 = w1.shape
    tc = min(tc, F2)
    out = jax.ShapeDtypeStruct((E, D, F2 // 2), BF16)
    return pl.pallas_call(
        _prep_w1_body,
        grid=(E, F2 // tc),
        in_specs=[pl.BlockSpec((None, D, tc), lambda e, j: (e, 0, j))],
        out_specs=[pl.BlockSpec((None, D, tc // 2), lambda e, j: (e, 0, j)),
                   pl.BlockSpec((None, D, tc // 2), lambda e, j: (e, 0, j))],
        out_shape=[out, out],
        scratch_shapes=[pltpu.VMEM((D // LANES, tc, LANES), F32)],
        compiler_params=_cparams(("parallel", "parallel")),
        name="prep_w1",
    )(w1)


def _combine_body(h_ref, y_ref, g_ref, o_ref):
    acc = h_ref[...]
    g = g_ref[...]
    for k in range(y_ref.shape[0]):
        acc = acc + g[:, k:k + 1] * y_ref[k].astype(F32)
    o_ref[...] = acc


def _combine(h, yk, gates, tm=512):
    T, D = h.shape
    K = yk.shape[0]
    return pl.pallas_call(
        _combine_body,
        grid=(T // tm,),
        in_specs=[pl.BlockSpec((tm, D), lambda i: (i, 0)),
                  pl.BlockSpec((K, tm, D), lambda i: (0, i, 0)),
                  pl.BlockSpec((tm, K), lambda i: (i, 0))],
        out_specs=pl.BlockSpec((tm, D), lambda i: (i, 0)),
        out_shape=jax.ShapeDtypeStruct((T, D), F32),
        compiler_params=_cparams(("parallel",)),
        name="moe_combine",
    )(h, yk, gates)


def _lookup(table, idx):
    n = table.shape[0]
    hit = idx[None, :] == jnp.arange(n, dtype=idx.dtype)[:, None]
    return jnp.sum(jnp.where(hit, table[:, None], 0), axis=0)


def _moe(h, hn_bf16, logits, w1, b1, w2, b2):
    T, D = hn_bf16.shape
    E = w1.shape[0]
    top_val, top_idx = lax.top_k(logits, TOP_K)
    gates = jax.nn.softmax(top_val, axis=-1)
    TK = T * TOP_K
    e_flat = top_idx.reshape(TK).astype(jnp.int32)
    order = jnp.argsort(e_flat).astype(jnp.int32)
    rank = jnp.argsort(order).astype(jnp.int32)
    counts = jnp.sum(jnp.arange(E, dtype=jnp.int32)[:, None] == e_flat[None, :], axis=1,
                     dtype=jnp.int32)
    starts = jnp.cumsum(counts) - counts
    padded = ((counts + MOE_TM - 1) // MOE_TM) * MOE_TM
    pad_ends = jnp.cumsum(padded)
    pad_starts = pad_ends - padded
    nb = (TK + E * (MOE_TM - 1) + MOE_TM - 1) // MOE_TM
    block_start = jnp.arange(nb, dtype=jnp.int32) * MOE_TM
    block_expert = jnp.minimum(jnp.sum(block_start[:, None] >= pad_ends[None, :], axis=1),
                               E - 1).astype(jnp.int32)
    n_used = (pad_ends[-1] // MOE_TM).astype(jnp.int32).reshape(1)
    blk_first = block_start - pad_starts[block_expert]
    within = blk_first[:, None] + jnp.arange(MOE_TM, dtype=jnp.int32)[None, :]
    valid = within < counts[block_expert][:, None]
    sorted_idx = jnp.where(valid, starts[block_expert][:, None] + within, 0).reshape(nb * MOE_TM)
    src = jnp.where(valid.reshape(nb * MOE_TM), order[sorted_idx] // TOP_K, 0)
    pos = _lookup(pad_starts - starts, e_flat) + rank
    w1g, w1l = _prep_w1(w1)
    y = _moe_experts(hn_bf16[src], block_expert, n_used, w1g, w1l,
                     b1[:, None, 0::2].astype(F32), b1[:, None, 1::2].astype(F32),
                     w2, b2[:, None, :].astype(F32))
    return _combine(h, y[pos.reshape(T, TOP_K).T], gates)


def _rope_tables(L):
    rows = L // GRID_W
    row = jnp.repeat(jnp.arange(rows, dtype=F32), GRID_W)
    col = jnp.tile(jnp.arange(GRID_W, dtype=F32), rows)
    half = HEAD_DIM // 2
    freqs = ROPE_THETA ** (-jnp.arange(0, half, 2, dtype=F32) / half)
    ang = jnp.concatenate([row[:, None] * freqs, col[:, None] * freqs], axis=-1)
    cos = jnp.repeat(jnp.cos(ang), 2, axis=-1)
    sin = jnp.repeat(jnp.sin(ang), 2, axis=-1)
    sign = jnp.tile(jnp.array([-1.0, 1.0], F32), HEAD_DIM // 2)
    reps = LANES // HEAD_DIM
    return jnp.tile(cos, (1, reps)), jnp.tile(sin * sign, (1, reps))


def kernel(x, norm1_g, w_in, conv_w, conv_b, filt_w1, filt_b1, filt_freq1, filt_w2, filt_b2, filt_freq2, filt_w3, filt_b3, hyena_d, q_norm_g, k_norm_g, w_hyena_up, w_attn_up, w_out, norm2_g, router_w, router_b, expert_w1, expert_b1, expert_w2, expert_b2):
    B, L, D = x.shape
    T = B * L
    depth = w_in.shape[0]
    hw = conv_w.shape[-1] // 3
    aw = N_Q_HEADS * HEAD_DIM
    kvw = N_KV_HEADS * HEAD_DIM
    widths = (3 * hw, aw, 2 * kvw, 2 * D)
    cosf, sinf = _rope_tables(L)
    for l in range(depth):
        xt = x.reshape(T, D)
        z, q, kv, gates = _inproj(xt, norm1_g[l], w_in[l].astype(BF16), widths)
        y_hy = _hyena(z.reshape(B, L, 3 * hw), conv_w[l], conv_b[l], filt_w1[l], filt_b1[l],
                      filt_freq1[l], filt_w2[l], filt_b2[l], filt_freq2[l], filt_w3[l], filt_b3[l],
                      hyena_d[l])
        gq = jnp.tile(q_norm_g[l].astype(F32), N_Q_HEADS)[None, :]
        gk = jnp.tile(k_norm_g[l].astype(F32), N_KV_HEADS)[None, :]
        qr, kr, va = _qkrope(q, kv, gq, gk, cosf, sinf, L)
        y_at = _attention(qr.reshape(B, L, aw), kr.reshape(B, L, kvw),
                          va.reshape(N_KV_HEADS, B, L, kvw))
        h, hn, logits = _merge(y_hy.reshape(T, hw), y_at.reshape(T, aw), gates, xt,
                               w_hyena_up[l], w_attn_up[l], w_out[l], norm2_g[l],
                               router_w[l], router_b[l])
        x = _moe(h, hn, logits[:, :N_EXPERTS], expert_w1[l], expert_b1[l], expert_w2[l],
                 expert_b2[l]).reshape(B, L, D)
    return x
```

```python
import functools
import math

import jax
import jax.numpy as jnp
from jax import lax
from jax.experimental import pallas as pl
from jax.experimental.pallas import tpu as pltpu

F32 = jnp.float32
BF16 = jnp.bfloat16

GRID_W = 64
HEAD_DIM = 64
N_Q_HEADS = 8
N_KV_HEADS = 2
Q_PER_KV = N_Q_HEADS // N_KV_HEADS
ROPE_THETA = 10000.0
HYENA_ORDER = 2
SHORT_CONV = 3
FILTER_EMB = 33
FAST_DECAY_PCT = 0.3
SLOW_DECAY_PCT = 1.5
DECAY_TARGET = 1e-2
N_EXPERTS = 32
TOP_K = 4
SWIGLU_LIMIT = 7.0
SWIGLU_ALPHA = 1.702
EPS = 1e-6

LANES = 128
SUBLANES = 8
VMEM_LIMIT = 56 * 1024 * 1024

FFT_N2 = LANES
FFT_PITCH = FFT_N2 + SUBLANES
FFT_K1_CHUNK_MAX = 16
FFT_UNROLL_OUTER = 16
FFT_UNROLL_INNER = True
MOE_TM = 512
LOGIT_PAD = LANES


def _cparams(sem):
    return pltpu.CompilerParams(dimension_semantics=sem, vmem_limit_bytes=VMEM_LIMIT)


def _split_bf16(x):
    hi = x.astype(BF16)
    lo = (x - hi.astype(F32)).astype(BF16)
    return hi, lo


def _inproj_body(x_ref, g_ref, w_ref, z_ref, q_ref, kv_ref, gate_ref, *, widths):
    x = x_ref[...]
    ms = jnp.mean(x * x, axis=-1, keepdims=True)
    u = (x * lax.rsqrt(ms + EPS) * g_ref[...]).astype(BF16)
    off = 0
    for ref, w in zip((z_ref, q_ref, kv_ref, gate_ref), widths):
        ref[...] = jnp.dot(u, w_ref[:, off:off + w], preferred_element_type=F32).astype(ref.dtype)
        off += w


def _inproj(xt, g, w_bf16, widths, dtypes, tm=512):
    T, D = xt.shape
    n = w_bf16.shape[1]
    return pl.pallas_call(
        functools.partial(_inproj_body, widths=widths),
        grid=(T // tm,),
        in_specs=[
            pl.BlockSpec((tm, D), lambda i: (i, 0)),
            pl.BlockSpec((1, D), lambda i: (0, 0)),
            pl.BlockSpec((D, n), lambda i: (0, 0)),
        ],
        out_specs=[pl.BlockSpec((tm, w), lambda i: (i, 0)) for w in widths],
        out_shape=[jax.ShapeDtypeStruct((T, w), dt) for w, dt in zip(widths, dtypes)],
        compiler_params=_cparams(("parallel",)),
        name="inproj",
    )(xt, g.reshape(1, D), w_bf16)


def _head_norm_rope(x, gain, cosf, sinf, ones_blk):
    w = x.shape[-1]
    hi, lo = _split_bf16(x * x)
    ss = (jnp.dot(hi, ones_blk, preferred_element_type=F32)
          + jnp.dot(lo, ones_blk, preferred_element_type=F32))
    xn = x * lax.rsqrt(ss * (1.0 / HEAD_DIM) + EPS) * gain
    lane = lax.broadcasted_iota(jnp.int32, (x.shape[0], LANES), 1)
    cols = []
    for c in range(w // LANES):
        col = xn[:, c * LANES:(c + 1) * LANES]
        cols.append(jnp.where(lane % 2 == 0, pltpu.roll(col, LANES - 1, 1), pltpu.roll(col, 1, 1)))
    swapped = cols[0] if len(cols) == 1 else jnp.concatenate(cols, axis=1)
    return xn * cosf + swapped * sinf


def _qkrope_body(q_ref, kv_ref, gq_ref, gk_ref, cos_ref, sin_ref, oq_ref, ok_ref,
                 qo_ref, ko_ref, vo_ref):
    cosf = cos_ref[...]
    sinf = sin_ref[...]
    nq = q_ref.shape[-1] // LANES
    q = _head_norm_rope(q_ref[...].astype(F32), gq_ref[...], jnp.tile(cosf, (1, nq)),
                        jnp.tile(sinf, (1, nq)), oq_ref[...])
    qo_ref[...] = (q * (HEAD_DIM ** -0.5 * math.log2(math.e))).astype(BF16)
    kv = kv_ref[...].astype(F32)
    kw = kv.shape[-1] // 2
    k = _head_norm_rope(kv[:, :kw], gk_ref[...], cosf, sinf, ok_ref[...])
    ko_ref[...] = k.astype(BF16)
    v = kv[:, kw:]
    lane = lax.broadcasted_iota(jnp.int32, v.shape, 1)
    for h in range(N_KV_HEADS):
        vo_ref[h] = jnp.where((lane // HEAD_DIM) == h, v, 1.0).astype(BF16)


def _qkrope(q, kv, gq, gk, cosf, sinf, seq_len, tm=512):
    T, qw = q.shape
    kw = kv.shape[1] // 2
    assert kw == LANES and N_KV_HEADS * HEAD_DIM == LANES and N_KV_HEADS == 2
    nl = seq_len // tm

    def blk_ones(w):
        r = jnp.arange(w) // HEAD_DIM
        return (r[:, None] == r[None, :]).astype(BF16)

    return pl.pallas_call(
        _qkrope_body,
        grid=(T // tm,),
        in_specs=[
            pl.BlockSpec((tm, qw), lambda i: (i, 0)),
            pl.BlockSpec((tm, 2 * kw), lambda i: (i, 0)),
            pl.BlockSpec((1, qw), lambda i: (0, 0)),
            pl.BlockSpec((1, kw), lambda i: (0, 0)),
            pl.BlockSpec((tm, LANES), lambda i: (i % nl, 0)),
            pl.BlockSpec((tm, LANES), lambda i: (i % nl, 0)),
            pl.BlockSpec((qw, qw), lambda i: (0, 0)),
            pl.BlockSpec((kw, kw), lambda i: (0, 0)),
        ],
        out_specs=[
            pl.BlockSpec((tm, qw), lambda i: (i, 0)),
            pl.BlockSpec((tm, kw), lambda i: (i, 0)),
            pl.BlockSpec((N_KV_HEADS, tm, kw), lambda i: (0, i, 0)),
        ],
        out_shape=[
            jax.ShapeDtypeStruct((T, qw), BF16),
            jax.ShapeDtypeStruct((T, kw), BF16),
            jax.ShapeDtypeStruct((N_KV_HEADS, T, kw), BF16),
        ],
        compiler_params=_cparams(("parallel",)),
        name="qkrope",
    )(q, kv, gq, gk, cosf, sinf, blk_ones(qw), blk_ones(kw))


def _attn_body(q_ref, k_ref, v_ref, o_ref, qs_ref, m_ref, acc_ref, *, tk, nsplit):
    kvh = pl.program_id(1)
    tq = q_ref.shape[0]
    seq = k_ref.shape[0]
    rows = Q_PER_KV * tq
    lane = lax.broadcasted_iota(jnp.int32, (tq, LANES), 1)
    in_head = (lane // HEAD_DIM) == kvh

    for g in range(Q_PER_KV):
        col = q_ref[:, (g // 2) * LANES:(g // 2 + 1) * LANES].astype(F32)
        col = jnp.where((g % 2) == kvh, col, pltpu.roll(col, HEAD_DIM, 1))
        qs_ref[g * tq:(g + 1) * tq, :] = jnp.where(in_head, col, 0.0).astype(BF16)
    m_ref[...] = jnp.full(m_ref.shape, -jnp.inf, F32)
    acc_ref[...] = jnp.zeros(acc_ref.shape, F32)
    part = rows // nsplit

    def step(c, carry):
        r0 = pl.multiple_of(c * tk, tk)
        kc = k_ref[pl.ds(r0, tk), :]
        vc = v_ref[pl.ds(r0, tk), :]
        for h in range(nsplit):
            sl = slice(h * part, (h + 1) * part)
            s = lax.dot_general(qs_ref[sl, :], kc, (((1,), (1,)), ((), ())),
                                preferred_element_type=F32)
            m_prev = m_ref[sl, :]
            m_new = jnp.maximum(m_prev, jnp.max(s, axis=-1, keepdims=True))
            p = jnp.exp2(s - jnp.tile(m_new, (1, tk // LANES)))
            acc_ref[sl, :] = jnp.exp2(m_prev - m_new) * acc_ref[sl, :] + jnp.dot(
                p.astype(BF16), vc, preferred_element_type=F32)
            m_ref[sl, :] = m_new
        return carry

    lax.fori_loop(0, seq // tk, step, 0)

    acc = acc_ref[...]
    o = acc / pltpu.roll(acc, HEAD_DIM, 1)
    for c in range(Q_PER_KV // 2):
        even = o[(2 * c) * tq:(2 * c + 1) * tq, :]
        odd = o[(2 * c + 1) * tq:(2 * c + 2) * tq, :]
        even = jnp.where(kvh == 0, even, pltpu.roll(even, HEAD_DIM, 1))
        odd = jnp.where(kvh == 1, odd, pltpu.roll(odd, HEAD_DIM, 1))
        o_ref[:, c * LANES:(c + 1) * LANES] = jnp.where(lane < HEAD_DIM, even, odd).astype(o_ref.dtype)


def _attention(q, k, v_aug, tq=512, tk=2048, nsplit=2):
    B, L, qw = q.shape
    gw = Q_PER_KV * HEAD_DIM
    kw = k.shape[-1]
    rows = Q_PER_KV * tq
    return pl.pallas_call(
        functools.partial(_attn_body, tk=tk, nsplit=nsplit),
        grid=(B, N_KV_HEADS, L // tq),
        in_specs=[
            pl.BlockSpec((None, tq, gw), lambda b, h, i: (b, i, h)),
            pl.BlockSpec((None, L, kw), lambda b, h, i: (b, 0, 0)),
            pl.BlockSpec((None, None, L, kw), lambda b, h, i: (h, b, 0, 0)),
        ],
        out_specs=pl.BlockSpec((None, tq, gw), lambda b, h, i: (b, i, h)),
        out_shape=jax.ShapeDtypeStruct((B, L, qw), BF16),
        scratch_shapes=[
            pltpu.VMEM((rows, kw), BF16),
            pltpu.VMEM((rows, LANES), F32),
            pltpu.VMEM((rows, kw), F32),
        ],
        compiler_params=_cparams(("parallel", "parallel", "parallel")),
        name="attention",
    )(q, k, v_aug)


def _hdot(a, b):
    ah, al = _split_bf16(a)
    bh, bl = _split_bf16(b)
    return (jnp.dot(ah, bh, preferred_element_type=F32)
            + jnp.dot(al, bh, preferred_element_type=F32)
            + jnp.dot(ah, bl, preferred_element_type=F32))


def _filter_body(feat_ref, t_ref, w1_ref, b1_ref, f1_ref, w2_ref, b2_ref, f2_ref, w3_ref, b3_ref,
                 delta_ref, bwd_ref, h_ref, sum_ref):
    i = pl.program_id(0)
    h = jnp.sin(f1_ref[...] * (_hdot(feat_ref[...], w1_ref[...]) + b1_ref[...]))
    h = jnp.sin(f2_ref[...] * (_hdot(h, w2_ref[...]) + b2_ref[...]))
    h = _hdot(h, w3_ref[...]) + b3_ref[...]
    t = t_ref[...]
    h = h * jnp.exp(-t * delta_ref[...])
    width = h_ref.shape[-1]
    for j in range(h_ref.shape[0]):
        h_ref[j] = h[:, j * width:(j + 1) * width]
    row = lax.broadcasted_iota(jnp.int32, h.shape, 0) + i * h.shape[0]
    a = jnp.where((row == 0) & (bwd_ref[...] > 0.5), 0.0, jnp.abs(h))
    part = jnp.sum(a, axis=0, keepdims=True)

    @pl.when(i == 0)
    def _():
        sum_ref[...] = jnp.zeros(sum_ref.shape, F32)

    sum_ref[...] += jnp.broadcast_to(part, sum_ref.shape)


def _hyena_filter(L, width, fw1, fb1, ff1, fw2, fb2, ff2, fw3, fb3, tl=512):
    bands = (FILTER_EMB - 1) // 2
    t = jnp.linspace(0.0, 1.0, L, dtype=F32)[:, None]
    w = 2.0 * math.pi * jnp.arange(L, dtype=F32)[:, None] / L
    fr = jnp.linspace(1e-4, bands - 1, bands, dtype=F32)[None, :]
    feats = jnp.concatenate([t, jnp.cos(w * fr), -jnp.sin(w * fr)], axis=-1)
    max_decay = math.log(DECAY_TARGET) / FAST_DECAY_PCT
    min_decay = math.log(DECAY_TARGET) / SLOW_DECAY_PCT
    deltas = jnp.abs(jnp.linspace(min_decay, max_decay, width, dtype=F32))
    ncol = fw3.shape[1]
    delta_cols = jnp.tile(deltas, ncol // width)[None, :]
    is_bwd = ((jnp.arange(ncol) // width) % 2).astype(F32)[None, :]
    emb = hid = LANES

    def pad2(a, r, c):
        a = a.astype(F32)
        return jnp.zeros((r, c), F32).at[:a.shape[0], :a.shape[1]].set(a)

    row = lambda a: pad2(a.reshape(1, -1), 1, hid)
    feats = pad2(feats, L, emb)
    fw1, fw2, fw3 = pad2(fw1, emb, hid), pad2(fw2, hid, hid), pad2(fw3, hid, ncol)
    fb3 = fb3.reshape(1, ncol).astype(F32)
    const = lambda shape: pl.BlockSpec(shape, lambda i: (0, 0))
    h, sums = pl.pallas_call(
        _filter_body,
        grid=(L // tl,),
        in_specs=[
            pl.BlockSpec((tl, emb), lambda i: (i, 0)),
            pl.BlockSpec((tl, 1), lambda i: (i, 0)),
            const((emb, hid)), const((1, hid)), const((1, hid)),
            const((hid, hid)), const((1, hid)), const((1, hid)),
            const((hid, ncol)), const((1, ncol)), const((1, ncol)), const((1, ncol)),
        ],
        out_specs=[pl.BlockSpec((ncol // width, tl, width), lambda i: (0, i, 0)),
                   pl.BlockSpec((SUBLANES, ncol), lambda i: (0, 0))],
        out_shape=[jax.ShapeDtypeStruct((ncol // width, L, width), F32),
                   jax.ShapeDtypeStruct((SUBLANES, ncol), F32)],
        compiler_params=_cparams(("arbitrary",)),
        name="hyena_filter",
    )(feats, t, fw1, row(fb1), row(ff1), fw2, row(fb2), row(ff2), fw3, fb3, delta_cols, is_bwd)
    return h, sums[0]


def _shortconv_body(z_ref, w_ref, b_ref, o_ref, *, chunk):
    L = z_ref.shape[0]
    w0, w1, w2 = w_ref[0:1, :], w_ref[1:2, :], w_ref[2:3, :]
    bias = b_ref[...]
    row = lax.broadcasted_iota(jnp.int32, (chunk, z_ref.shape[1]), 0)

    def step(c, carry):
        r0 = pl.multiple_of(c * chunk, chunk)
        cur = z_ref[pl.ds(r0, chunk), :]
        prev_row = z_ref[pl.ds(jnp.maximum(r0 - 1, 0), 1), :]
        next_row = z_ref[pl.ds(jnp.minimum(r0 + chunk, L - 1), 1), :]
        prev_row = jnp.where(c == 0, 0.0, prev_row)
        next_row = jnp.where(c == L // chunk - 1, 0.0, next_row)
        down = jnp.where(row == 0, prev_row, pltpu.roll(cur, 1, 0))
        up = jnp.where(row == chunk - 1, next_row, pltpu.roll(cur, chunk - 1, 0))
        o_ref[pl.ds(r0, chunk), :] = bias + down * w0 + cur * w1 + up * w2
        return carry

    lax.fori_loop(0, L // chunk, step, 0)


def _shortconv(z, conv_w, conv_b, chunk=256):
    B, L, C = z.shape
    return pl.pallas_call(
        functools.partial(_shortconv_body, chunk=chunk),
        grid=(B, C // LANES),
        in_specs=[
            pl.BlockSpec((None, L, LANES), lambda b, c: (b, 0, c)),
            pl.BlockSpec((SHORT_CONV, LANES), lambda b, c: (0, c)),
            pl.BlockSpec((1, LANES), lambda b, c: (0, c)),
        ],
        out_specs=pl.BlockSpec((None, L, LANES), lambda b, c: (b, 0, c)),
        out_shape=jax.ShapeDtypeStruct((B, L, C), F32),
        compiler_params=_cparams(("parallel", "parallel")),
        name="shortconv",
    )(z, conv_w, conv_b.reshape(1, C))


def _dft_tables(n1_len, n2_len):
    n = n1_len * n2_len

    def root(num, den):
        ang = (2.0 * math.pi / den) * (num % den).astype(F32)
        return jnp.cos(ang), -jnp.sin(ang)

    i1 = jnp.arange(n1_len, dtype=jnp.int32)
    i2 = jnp.arange(n2_len, dtype=jnp.int32)
    f1r, f1i = root(i1[:, None] * i1[None, :], n1_len)
    f2r, f2i = root(i2[:, None] * i2[None, :], n2_len)
    twr, twi = root(i1[:, None] * i2[None, :], n)
    return (f1r, f1i), (f2r, f2i), (twr, twi)


def _stacked_inner_dft(f2r, f2i, twr_row, twi_row):
    gr = f2r * twr_row - f2i * twi_row
    gi = f2r * twi_row + f2i * twr_row
    top = jnp.concatenate([gr, -gi], axis=1)
    bot = jnp.concatenate([gi, gr], axis=1)
    return jnp.concatenate([top, bot], axis=0)


def _fft_plan(n):
    n1_len = n // FFT_N2
    nk1 = n1_len // 2 + 1
    nk1_pad = -(-nk1 // SUBLANES) * SUBLANES
    chunk = max(c for c in range(1, FFT_K1_CHUNK_MAX + 1) if nk1 % c == 0)
    return n1_len, nk1, nk1_pad, chunk


def _outer_dft_to_scratch(load_rows, fa_ref, ar_ref, ai_ref, nk1_pad):
    fa = fa_ref[...]

    def step(n2, carry):
        a = jnp.dot(fa, load_rows(n2).astype(BF16), preferred_element_type=F32)
        ar_ref[pl.ds(n2, nk1_pad, stride=FFT_PITCH), :] = a[:nk1_pad]
        ai_ref[pl.ds(n2, nk1_pad, stride=FFT_PITCH), :] = a[nk1_pad:]
        return carry

    lax.fori_loop(0, FFT_N2, step, 0, unroll=FFT_UNROLL_OUTER)


def _spec_body(f_ref, b_ref, inv_ref, fa_ref, f2r_ref, f2i_ref, twr_ref, twi_ref, kr_ref, ki_ref,
               fr_ref, fi_ref, br_ref, bi_ref, *, n1_len, nk1_pad, chunk):
    kc = pl.program_id(2)
    half = n1_len // 2

    @pl.when(kc == 0)
    def _():
        inv = inv_ref[...]
        row = lax.broadcasted_iota(jnp.int32, (half, LANES), 0)
        _outer_dft_to_scratch(lambda n2: f_ref[pl.ds(n2, half, stride=FFT_N2), :] * inv,
                              fa_ref, fr_ref, fi_ref, nk1_pad)
        _outer_dft_to_scratch(
            lambda n2: jnp.where((row == 0) & (n2 == 0), 0.0,
                                 b_ref[pl.ds(n2, half, stride=FFT_N2), :] * inv),
            fa_ref, br_ref, bi_ref, nk1_pad)

    f2r = f2r_ref[...]
    f2i = f2i_ref[...]

    def step(t, carry):
        k1 = kc * chunk + t
        base = pl.multiple_of(k1 * FFT_PITCH, SUBLANES)
        mf = _stacked_inner_dft(f2r, f2i, twr_ref[pl.ds(k1, 1), :],
                                twi_ref[pl.ds(k1, 1), :]).astype(BF16)
        rows = pl.ds(base, FFT_N2)
        xf = jnp.dot(mf, jnp.concatenate([fr_ref[rows, :], fi_ref[rows, :]], axis=0).astype(BF16),
                     preferred_element_type=F32)
        xb = jnp.dot(mf, jnp.concatenate([br_ref[rows, :], bi_ref[rows, :]], axis=0).astype(BF16),
                     preferred_element_type=F32)
        o = pl.multiple_of(t * FFT_N2, FFT_N2)
        kr_ref[pl.ds(o, FFT_N2), :] = xf[:FFT_N2] + xb[:FFT_N2]
        ki_ref[pl.ds(o, FFT_N2), :] = xf[FFT_N2:] - xb[FFT_N2:]
        return carry

    lax.fori_loop(0, chunk, step, 0)


def _filter_spectrum(h, inv_norm, tables):
    O2, L, C = h.shape
    O = O2 // 2
    n1_len, nk1, nk1_pad, chunk = _fft_plan(2 * L)
    half = n1_len // 2
    (f1r, f1i), (f2r, f2i), (twr, twi) = tables
    fa = jnp.concatenate([f1r[:nk1_pad, :half], f1i[:nk1_pad, :half]], axis=0).astype(BF16)
    rows = chunk * FFT_N2
    const = lambda shape: pl.BlockSpec(shape, lambda o, c, k: (0, 0))
    out = jax.ShapeDtypeStruct((O, nk1 * FFT_N2, C), F32)
    scratch = pltpu.VMEM((nk1_pad * FFT_PITCH, LANES), F32)
    return pl.pallas_call(
        functools.partial(_spec_body, n1_len=n1_len, nk1_pad=nk1_pad, chunk=chunk),
        grid=(O, C // LANES, nk1 // chunk),
        in_specs=[
            pl.BlockSpec((None, L, LANES), lambda o, c, k: (2 * o, 0, c)),
            pl.BlockSpec((None, L, LANES), lambda o, c, k: (2 * o + 1, 0, c)),
            pl.BlockSpec((None, 1, LANES), lambda o, c, k: (o, 0, c)),
            const(fa.shape), const(f2r.shape), const(f2i.shape), const(twr.shape), const(twi.shape),
        ],
        out_specs=[pl.BlockSpec((None, rows, LANES), lambda o, c, k: (o, k, c)),
                   pl.BlockSpec((None, rows, LANES), lambda o, c, k: (o, k, c))],
        out_shape=[out, out],
        scratch_shapes=[scratch, scratch, scratch, scratch],
        compiler_params=_cparams(("parallel", "parallel", "arbitrary")),
        name="filter_spectrum",
    )(h, h, inv_norm, fa, f2r, f2i, twr, twi)


def _fftconv_body(y_ref, gate_ref, d_ref, kr_ref, ki_ref, fa_ref, fs_ref, f2r_ref, f2i_ref,
                  twr_ref, twi_ref, o_ref, xs_ref, ar_ref, ai_ref, *, n1_len, nk1_pad, chunk):
    kc = pl.program_id(2)
    half = n1_len // 2

    @pl.when(kc == 0)
    def _():
        def copy(n1, carry):
            src = pl.multiple_of(n1 * FFT_N2, FFT_N2)
            dst = pl.multiple_of(n1 * FFT_PITCH, SUBLANES)
            xs_ref[pl.ds(dst, FFT_N2), :] = y_ref[pl.ds(src, FFT_N2), :]
            return carry

        lax.fori_loop(0, half, copy, 0)
        _outer_dft_to_scratch(lambda n2: xs_ref[pl.ds(n2, half, stride=FFT_PITCH), :],
                              fa_ref, ar_ref, ai_ref, nk1_pad)

    f2r = f2r_ref[...]
    f2i = f2i_ref[...]

    def step(t, carry):
        k1 = kc * chunk + t
        base = pl.multiple_of(k1 * FFT_PITCH, SUBLANES)
        rhs = jnp.concatenate([ar_ref[pl.ds(base, FFT_N2), :], ai_ref[pl.ds(base, FFT_N2), :]],
                              axis=0).astype(BF16)
        mf = _stacked_inner_dft(f2r, f2i, twr_ref[pl.ds(k1, 1), :], twi_ref[pl.ds(k1, 1), :])
        x = jnp.dot(mf.astype(BF16), rhs, preferred_element_type=F32)
        xr, xi = x[:FFT_N2], x[FFT_N2:]
        o = pl.multiple_of(t * FFT_N2, FFT_N2)
        kr = kr_ref[pl.ds(o, FFT_N2), :]
        ki = ki_ref[pl.ds(o, FFT_N2), :]
        z = jnp.concatenate([xr * kr - xi * ki, xr * ki + xi * kr], axis=0).astype(BF16)
        b = jnp.dot(mf.T.astype(BF16), z, preferred_element_type=F32)
        ar_ref[pl.ds(base, FFT_N2), :] = b[:FFT_N2]
        ai_ref[pl.ds(base, FFT_N2), :] = b[FFT_N2:]
        return carry

    lax.fori_loop(0, chunk, step, 0, unroll=FFT_UNROLL_INNER)

    @pl.when(kc == pl.num_programs(2) - 1)
    def _():
        fs = fs_ref[...]

        def inv_outer(n2, carry):
            rhs = jnp.concatenate([ar_ref[pl.ds(n2, nk1_pad, stride=FFT_PITCH), :],
                                   ai_ref[pl.ds(n2, nk1_pad, stride=FFT_PITCH), :]],
                                  axis=0).astype(BF16)
            conv = jnp.dot(fs, rhs, preferred_element_type=F32)
            ar_ref[pl.ds(n2, half, stride=FFT_PITCH), :] = conv
            return carry

        lax.fori_loop(0, FFT_N2, inv_outer, 0, unroll=FFT_UNROLL_OUTER)
        d = d_ref[...]

        def finish(n1, carry):
            src = pl.multiple_of(n1 * FFT_PITCH, SUBLANES)
            dst = pl.multiple_of(n1 * FFT_N2, FFT_N2)
            y = y_ref[pl.ds(dst, FFT_N2), :]
            o_ref[pl.ds(dst, FFT_N2), :] = (gate_ref[pl.ds(dst, FFT_N2), :] * (
                ar_ref[pl.ds(src, FFT_N2), :] + y * d)).astype(o_ref.dtype)
            return carry

        lax.fori_loop(0, half, finish, 0)


def _fftconv_gate(y, y_off, gate, gate_off, d, kr, ki, tables, out_dtype):
    B, L, _ = y.shape
    C = d.shape[-1]
    yo, go = y_off // LANES, gate_off // LANES
    N = 2 * L
    n1_len, nk1, nk1_pad, chunk = _fft_plan(N)
    half = n1_len // 2
    (f1r, f1i), (f2r, f2i), (twr, twi) = tables
    fa = jnp.concatenate([f1r[:nk1_pad, :half], f1i[:nk1_pad, :half]], axis=0).astype(BF16)
    wts = jnp.concatenate([jnp.ones((1,), F32), jnp.full((nk1 - 2,), 2.0, F32), jnp.ones((1,), F32),
                           jnp.zeros((nk1_pad - nk1,), F32)]) * (1.0 / N)
    fs = jnp.concatenate([f1r[:half, :nk1_pad] * wts, f1i[:half, :nk1_pad] * wts],
                         axis=1).astype(BF16)
    rows = chunk * FFT_N2
    const = lambda shape: pl.BlockSpec(shape, lambda c, b, k: (0, 0))
    return pl.pallas_call(
        functools.partial(_fftconv_body, n1_len=n1_len, nk1_pad=nk1_pad, chunk=chunk),
        grid=(C // LANES, B, nk1 // chunk),
        in_specs=[
            pl.BlockSpec((None, L, LANES), lambda c, b, k: (b, 0, c + yo)),
            pl.BlockSpec((None, L, LANES), lambda c, b, k: (b, 0, c + go)),
            pl.BlockSpec((1, LANES), lambda c, b, k: (0, c)),
            pl.BlockSpec((rows, LANES), lambda c, b, k: (k, c)),
            pl.BlockSpec((rows, LANES), lambda c, b, k: (k, c)),
            const(fa.shape), const(fs.shape), const(f2r.shape), const(f2i.shape),
            const(twr.shape), const(twi.shape),
        ],
        out_specs=pl.BlockSpec((None, L, LANES), lambda c, b, k: (b, 0, c)),
        out_shape=jax.ShapeDtypeStruct((B, L, C), out_dtype),
        scratch_shapes=[pltpu.VMEM((half * FFT_PITCH, LANES), F32),
                        pltpu.VMEM((nk1_pad * FFT_PITCH, LANES), F32),
                        pltpu.VMEM((nk1_pad * FFT_PITCH, LANES), F32)],
        compiler_params=_cparams(("parallel", "parallel", "arbitrary")),
        name="fftconv_gate",
    )(y, gate, d, kr, ki, fa, fs, f2r, f2i, twr, twi)


def _hyena(z, conv_w, conv_b, fw1, fb1, ff1, fw2, fb2, ff2, fw3, fb3, hyena_d):
    B, L, C3 = z.shape
    W = C3 // 3
    zc = _shortconv(z, conv_w, conv_b)
    h, sums = _hyena_filter(L, W, fw1, fb1, ff1, fw2, fb2, ff2, fw3, fb3)
    sums = sums.reshape(HYENA_ORDER, 2, W)
    inv_norm = (1.0 / (sums[:, 0] + sums[:, 1]))[:, None, :]
    tables = _dft_tables(2 * L // FFT_N2, FFT_N2)
    kr, ki = _filter_spectrum(h, inv_norm, tables)
    y = zc
    for o in range(HYENA_ORDER):
        y = _fftconv_gate(y, 0, zc, (o + 1) * W, hyena_d[o].reshape(1, W).astype(F32),
                          kr[o], ki[o], tables, BF16 if o == HYENA_ORDER - 1 else F32)
    return y


def _merge_body(yh_ref, ya_ref, g_ref, x_ref, whu_ref, wau_ref, wo_ref, n2_ref, rwh_ref, rwl_ref,
                rb_ref, h_ref, hn_ref, lg_ref):
    D = x_ref.shape[-1]
    up_h = jnp.dot(yh_ref[...], whu_ref[...], preferred_element_type=F32)
    up_a = jnp.dot(ya_ref[...], wau_ref[...], preferred_element_type=F32)
    g = g_ref[...].astype(F32)
    merged = jax.nn.sigmoid(g[:, :D]) * up_h + jax.nn.sigmoid(g[:, D:]) * up_a
    h = x_ref[...] + jnp.dot(merged.astype(BF16), wo_ref[...], preferred_element_type=F32)
    h_ref[...] = h
    ms = jnp.mean(h * h, axis=-1, keepdims=True)
    hn = h * lax.rsqrt(ms + EPS) * n2_ref[...]
    hn_ref[...] = hn.astype(BF16)
    hh, hl = _split_bf16(hn)
    lg_ref[...] = (jnp.dot(hh, rwh_ref[...], preferred_element_type=F32)
                   + jnp.dot(hl, rwh_ref[...], preferred_element_type=F32)
                   + jnp.dot(hh, rwl_ref[...], preferred_element_type=F32)) + rb_ref[...]


def _merge(yh, ya, gates, xt, whu, wau, wo, n2g, rw, rb, tm=256):
    T, D = xt.shape
    W = yh.shape[1]
    E = rw.shape[1]
    rwp = jnp.zeros((D, LOGIT_PAD), F32).at[:, :E].set(rw)
    rwh, rwl = _split_bf16(rwp)
    rbp = jnp.zeros((1, LOGIT_PAD), F32).at[0, :E].set(rb)
    rowblk = lambda w: pl.BlockSpec((tm, w), lambda i: (i, 0))
    const = lambda shape: pl.BlockSpec(shape, lambda i: (0, 0))
    return pl.pallas_call(
        _merge_body,
        grid=(T // tm,),
        in_specs=[rowblk(W), rowblk(W), rowblk(2 * D), rowblk(D),
                  const((W, D)), const((W, D)), const((D, D)), const((1, D)),
                  const((D, LOGIT_PAD)), const((D, LOGIT_PAD)), const((1, LOGIT_PAD))],
        out_specs=[rowblk(D), rowblk(D), rowblk(LOGIT_PAD)],
        out_shape=[jax.ShapeDtypeStruct((T, D), F32), jax.ShapeDtypeStruct((T, D), BF16),
                   jax.ShapeDtypeStruct((T, LOGIT_PAD), F32)],
        compiler_params=_cparams(("parallel",)),
        name="merge",
    )(yh, ya, gates, xt, whu.astype(BF16), wau.astype(BF16), wo.astype(BF16), n2g.reshape(1, D),
      rwh, rwl, rbp)


def _moe_body(be_ref, nused_ref, x_ref, w1g_ref, w1l_ref, b1g_ref, b1l_ref, w2_ref, b2_ref, o_ref):
    i = pl.program_id(0)

    @pl.when(i < nused_ref[0])
    def _():
        x = x_ref[...]
        glu = jnp.dot(x, w1g_ref[...], preferred_element_type=F32) + b1g_ref[...]
        lin = jnp.dot(x, w1l_ref[...], preferred_element_type=F32) + b1l_ref[...]
        glu = jnp.minimum(glu, SWIGLU_LIMIT)
        lin = jnp.clip(lin, -SWIGLU_LIMIT, SWIGLU_LIMIT)
        act = glu * jax.nn.sigmoid(SWIGLU_ALPHA * glu) * (lin + 1.0)
        y = jnp.dot(act.astype(BF16), w2_ref[...].astype(BF16),
                    preferred_element_type=F32) + b2_ref[...]
        o_ref[...] = y.astype(o_ref.dtype)

    @pl.when(i >= nused_ref[0])
    def _():
        o_ref[...] = jnp.zeros(o_ref.shape, o_ref.dtype)


def _moe_experts(xg, block_expert, n_used, w1g, w1l, b1g, b1l, w2, b2):
    P, D = xg.shape
    dff = w2.shape[1]
    nb = P // MOE_TM
    wspec = lambda k, n: pl.BlockSpec((None, k, n), lambda i, be, nu: (be[i], 0, 0))
    grid_spec = pltpu.PrefetchScalarGridSpec(
        num_scalar_prefetch=2,
        grid=(nb,),
        in_specs=[
            pl.BlockSpec((MOE_TM, D), lambda i, be, nu: (i, 0)),
            wspec(D, dff), wspec(D, dff), wspec(1, dff), wspec(1, dff),
            wspec(dff, D), wspec(1, D),
        ],
        out_specs=pl.BlockSpec((MOE_TM, D), lambda i, be, nu: (i, 0)),
    )
    return pl.pallas_call(
        _moe_body,
        grid_spec=grid_spec,
        out_shape=jax.ShapeDtypeStruct((P, D), BF16),
        compiler_params=_cparams(("arbitrary",)),
        name="moe_experts",
    )(block_expert, n_used, xg, w1g, w1l, b1g, b1l, w2, b2)


def _prep_w1_body(w_ref, g_ref, l_ref, t_ref):
    half = g_ref.shape[-1]
    wt = w_ref[...].T
    for c in range(t_ref.shape[0]):
        rows = slice(c * LANES, (c + 1) * LANES)
        t_ref[c] = wt[:, rows]
        g_ref[rows, :] = t_ref[c, pl.ds(0, half, stride=2), :].T.astype(BF16)
        l_ref[rows, :] = t_ref[c, pl.ds(1, half, stride=2), :].T.astype(BF16)


def _prep_w1(w1, tr=2 * LANES):
    E, D, F2 = w1.shape
    out = jax.ShapeDtypeStruct((E, D, F2 // 2), BF16)
    return pl.pallas_call(
        _prep_w1_body,
        grid=(E, D // tr),
        in_specs=[pl.BlockSpec((None, tr, F2), lambda e, j: (e, j, 0))],
        out_specs=[pl.BlockSpec((None, tr, F2 // 2), lambda e, j: (e, j, 0)),
                   pl.BlockSpec((None, tr, F2 // 2), lambda e, j: (e, j, 0))],
        out_shape=[out, out],
        scratch_shapes=[pltpu.VMEM((tr // LANES, F2, LANES), F32)],
        compiler_params=_cparams(("parallel", "parallel")),
        name="prep_w1",
    )(w1)


def _combine_body(h_ref, y_ref, g_ref, o_ref):
    acc = h_ref[...]
    g = g_ref[...]
    for k in range(y_ref.shape[0]):
        acc = acc + g[:, k:k + 1] * y_ref[k].astype(F32)
    o_ref[...] = acc


def _combine(h, yk, gates, tm=512):
    T, D = h.shape
    K = yk.shape[0]
    return pl.pallas_call(
        _combine_body,
        grid=(T // tm,),
        in_specs=[pl.BlockSpec((tm, D), lambda i: (i, 0)),
                  pl.BlockSpec((K, tm, D), lambda i: (0, i, 0)),
                  pl.BlockSpec((tm, K), lambda i: (i, 0))],
        out_specs=pl.BlockSpec((tm, D), lambda i: (i, 0)),
        out_shape=jax.ShapeDtypeStruct((T, D), F32),
        compiler_params=_cparams(("parallel",)),
        name="moe_combine",
    )(h, yk, gates)


def _lookup(table, idx):
    n = table.shape[0]
    hit = idx[None, :] == jnp.arange(n, dtype=idx.dtype)[:, None]
    return jnp.sum(jnp.where(hit, table[:, None], 0), axis=0)


def _moe(h, hn_bf16, logits, w1, b1, w2, b2):
    T, D = hn_bf16.shape
    E = w1.shape[0]
    top_val, top_idx = lax.top_k(logits, TOP_K)
    gates = jax.nn.softmax(top_val, axis=-1)
    TK = T * TOP_K
    e_flat = top_idx.reshape(TK).astype(jnp.int32)
    order = jnp.argsort(e_flat).astype(jnp.int32)
    rank = jnp.argsort(order).astype(jnp.int32)
    counts = jnp.sum(jnp.arange(E, dtype=jnp.int32)[:, None] == e_flat[None, :], axis=1,
                     dtype=jnp.int32)
    starts = jnp.cumsum(counts) - counts
    padded = ((counts + MOE_TM - 1) // MOE_TM) * MOE_TM
    pad_ends = jnp.cumsum(padded)
    pad_starts = pad_ends - padded
    nb = (TK + E * (MOE_TM - 1) + MOE_TM - 1) // MOE_TM
    block_start = jnp.arange(nb, dtype=jnp.int32) * MOE_TM
    block_expert = jnp.minimum(jnp.sum(block_start[:, None] >= pad_ends[None, :], axis=1),
                               E - 1).astype(jnp.int32)
    n_used = (pad_ends[-1] // MOE_TM).astype(jnp.int32).reshape(1)
    blk_first = block_start - pad_starts[block_expert]
    within = blk_first[:, None] + jnp.arange(MOE_TM, dtype=jnp.int32)[None, :]
    valid = within < counts[block_expert][:, None]
    sorted_idx = jnp.where(valid, starts[block_expert][:, None] + within, 0).reshape(nb * MOE_TM)
    filler = jnp.arange(nb * MOE_TM, dtype=jnp.int32) % T
    src = jnp.where(valid.reshape(nb * MOE_TM), order[sorted_idx] // TOP_K, filler)
    pos = _lookup(pad_starts - starts, e_flat) + rank
    w1g, w1l = _prep_w1(w1)
    y = _moe_experts(hn_bf16[src], block_expert, n_used, w1g, w1l,
                     b1[:, None, 0::2].astype(F32), b1[:, None, 1::2].astype(F32),
                     w2, b2[:, None, :].astype(F32))
    return _combine(h, y[pos.reshape(T, TOP_K).T], gates)


def _rope_tables(L):
    rows = L // GRID_W
    row = jnp.repeat(jnp.arange(rows, dtype=F32), GRID_W)
    col = jnp.tile(jnp.arange(GRID_W, dtype=F32), rows)
    half = HEAD_DIM // 2
    freqs = ROPE_THETA ** (-jnp.arange(0, half, 2, dtype=F32) / half)
    ang = jnp.concatenate([row[:, None] * freqs, col[:, None] * freqs], axis=-1)
    cos = jnp.repeat(jnp.cos(ang), 2, axis=-1)
    sin = jnp.repeat(jnp.sin(ang), 2, axis=-1)
    sign = jnp.tile(jnp.array([-1.0, 1.0], F32), HEAD_DIM // 2)
    reps = LANES // HEAD_DIM
    return jnp.tile(cos, (1, reps)), jnp.tile(sin * sign, (1, reps))


def kernel(x, norm1_g, w_in, conv_w, conv_b, filt_w1, filt_b1, filt_freq1, filt_w2, filt_b2, filt_freq2, filt_w3, filt_b3, hyena_d, q_norm_g, k_norm_g, w_hyena_up, w_attn_up, w_out, norm2_g, router_w, router_b, expert_w1, expert_b1, expert_w2, expert_b2):
    B, L, D = x.shape
    T = B * L
    depth = w_in.shape[0]
    hw = conv_w.shape[-1] // 3
    aw = N_Q_HEADS * HEAD_DIM
    kvw = N_KV_HEADS * HEAD_DIM
    widths = (3 * hw, aw, 2 * kvw, 2 * D)
    cosf, sinf = _rope_tables(L)
    for l in range(depth):
        xt = x.reshape(T, D)
        z, q, kv, gates = _inproj(xt, norm1_g[l], w_in[l].astype(BF16), widths,
                                  (F32, BF16, BF16, BF16))
        y_hy = _hyena(z.reshape(B, L, 3 * hw), conv_w[l], conv_b[l], filt_w1[l], filt_b1[l],
                      filt_freq1[l], filt_w2[l], filt_b2[l], filt_freq2[l], filt_w3[l], filt_b3[l],
                      hyena_d[l])
        gq = jnp.tile(q_norm_g[l].astype(F32), N_Q_HEADS)[None, :]
        gk = jnp.tile(k_norm_g[l].astype(F32), N_KV_HEADS)[None, :]
        qr, kr, va = _qkrope(q, kv, gq, gk, cosf, sinf, L)
        y_at = _attention(qr.reshape(B, L, aw), kr.reshape(B, L, kvw),
                          va.reshape(N_KV_HEADS, B, L, kvw))
        h, hn, logits = _merge(y_hy.reshape(T, hw), y_at.reshape(T, aw), gates, xt,
                               w_hyena_up[l], w_attn_up[l], w_out[l], norm2_g[l],
                               router_w[l], router_b[l])
        x = _moe(h, hn, logits[:, :N_EXPERTS], expert_w1[l], expert_b1[l], expert_w2[l],
                 expert_b2[l]).reshape(B, L, D)
    return x
```

```python
import functools
import math

import jax
import jax.numpy as jnp
from jax import lax
from jax.experimental import pallas as pl
from jax.experimental.pallas import tpu as pltpu

F32 = jnp.float32
BF16 = jnp.bfloat16

GRID_W = 64
HEAD_DIM = 64
N_Q_HEADS = 8
N_KV_HEADS = 2
Q_PER_KV = N_Q_HEADS // N_KV_HEADS
ROPE_THETA = 10000.0
HYENA_ORDER = 2
SHORT_CONV = 3
FILTER_EMB = 33
FAST_DECAY_PCT = 0.3
SLOW_DECAY_PCT = 1.5
DECAY_TARGET = 1e-2
N_EXPERTS = 32
TOP_K = 4
SWIGLU_LIMIT = 7.0
SWIGLU_ALPHA = 1.702
EPS = 1e-6

LANES = 128
SUBLANES = 8
VMEM_LIMIT = 56 * 1024 * 1024

FFT_N2 = LANES
FFT_PITCH = FFT_N2 + SUBLANES
FFT_K1_CHUNK_MAX = 16
FFT_UNROLL_OUTER = 16
FFT_UNROLL_INNER = True
MOE_TM = 512
LOGIT_PAD = LANES


def _cparams(sem):
    return pltpu.CompilerParams(dimension_semantics=sem, vmem_limit_bytes=VMEM_LIMIT)


def _split_bf16(x):
    hi = x.astype(BF16)
    lo = (x - hi.astype(F32)).astype(BF16)
    return hi, lo


def _inproj_body(x_ref, g_ref, w_ref, z_ref, q_ref, kv_ref, gate_ref, *, widths):
    x = x_ref[...]
    ms = jnp.mean(x * x, axis=-1, keepdims=True)
    u = (x * lax.rsqrt(ms + EPS) * g_ref[...]).astype(BF16)
    off = 0
    for ref, w in zip((z_ref, q_ref, kv_ref, gate_ref), widths):
        ref[...] = jnp.dot(u, w_ref[:, off:off + w], preferred_element_type=F32).astype(ref.dtype)
        off += w


def _inproj(xt, g, w_bf16, widths, dtypes, tm=512):
    T, D = xt.shape
    n = w_bf16.shape[1]
    return pl.pallas_call(
        functools.partial(_inproj_body, widths=widths),
        grid=(T // tm,),
        in_specs=[
            pl.BlockSpec((tm, D), lambda i: (i, 0)),
            pl.BlockSpec((1, D), lambda i: (0, 0)),
            pl.BlockSpec((D, n), lambda i: (0, 0)),
        ],
        out_specs=[pl.BlockSpec((tm, w), lambda i: (i, 0)) for w in widths],
        out_shape=[jax.ShapeDtypeStruct((T, w), dt) for w, dt in zip(widths, dtypes)],
        compiler_params=_cparams(("parallel",)),
        name="inproj",
    )(xt, g.reshape(1, D), w_bf16)


def _head_norm_rope(x, gain, cosf, sinf, ones_blk):
    w = x.shape[-1]
    hi, lo = _split_bf16(x * x)
    ss = (jnp.dot(hi, ones_blk, preferred_element_type=F32)
          + jnp.dot(lo, ones_blk, preferred_element_type=F32))
    xn = x * lax.rsqrt(ss * (1.0 / HEAD_DIM) + EPS) * gain
    lane = lax.broadcasted_iota(jnp.int32, (x.shape[0], LANES), 1)
    cols = []
    for c in range(w // LANES):
        col = xn[:, c * LANES:(c + 1) * LANES]
        cols.append(jnp.where(lane % 2 == 0, pltpu.roll(col, LANES - 1, 1), pltpu.roll(col, 1, 1)))
    swapped = cols[0] if len(cols) == 1 else jnp.concatenate(cols, axis=1)
    return xn * cosf + swapped * sinf


def _qkrope_body(q_ref, kv_ref, gq_ref, gk_ref, cos_ref, sin_ref, oq_ref, ok_ref,
                 qo_ref, ko_ref, vo_ref):
    cosf = cos_ref[...]
    sinf = sin_ref[...]
    nq = q_ref.shape[-1] // LANES
    q = _head_norm_rope(q_ref[...].astype(F32), gq_ref[...], jnp.tile(cosf, (1, nq)),
                        jnp.tile(sinf, (1, nq)), oq_ref[...])
    qo_ref[...] = (q * (HEAD_DIM ** -0.5 * math.log2(math.e))).astype(BF16)
    kv = kv_ref[...].astype(F32)
    kw = kv.shape[-1] // 2
    k = _head_norm_rope(kv[:, :kw], gk_ref[...], cosf, sinf, ok_ref[...])
    ko_ref[...] = k.astype(BF16)
    v = kv[:, kw:]
    lane = lax.broadcasted_iota(jnp.int32, v.shape, 1)
    for h in range(N_KV_HEADS):
        vo_ref[h] = jnp.where((lane // HEAD_DIM) == h, v, 1.0).astype(BF16)


def _qkrope(q, kv, gq, gk, cosf, sinf, seq_len, tm=512):
    T, qw = q.shape
    kw = kv.shape[1] // 2
    assert kw == LANES and N_KV_HEADS * HEAD_DIM == LANES and N_KV_HEADS == 2
    nl = seq_len // tm

    def blk_ones(w):
        r = jnp.arange(w) // HEAD_DIM
        return (r[:, None] == r[None, :]).astype(BF16)

    return pl.pallas_call(
        _qkrope_body,
        grid=(T // tm,),
        in_specs=[
            pl.BlockSpec((tm, qw), lambda i: (i, 0)),
            pl.BlockSpec((tm, 2 * kw), lambda i: (i, 0)),
            pl.BlockSpec((1, qw), lambda i: (0, 0)),
            pl.BlockSpec((1, kw), lambda i: (0, 0)),
            pl.BlockSpec((tm, LANES), lambda i: (i % nl, 0)),
            pl.BlockSpec((tm, LANES), lambda i: (i % nl, 0)),
            pl.BlockSpec((qw, qw), lambda i: (0, 0)),
            pl.BlockSpec((kw, kw), lambda i: (0, 0)),
        ],
        out_specs=[
            pl.BlockSpec((tm, qw), lambda i: (i, 0)),
            pl.BlockSpec((tm, kw), lambda i: (i, 0)),
            pl.BlockSpec((N_KV_HEADS, tm, kw), lambda i: (0, i, 0)),
        ],
        out_shape=[
            jax.ShapeDtypeStruct((T, qw), BF16),
            jax.ShapeDtypeStruct((T, kw), BF16),
            jax.ShapeDtypeStruct((N_KV_HEADS, T, kw), BF16),
        ],
        compiler_params=_cparams(("parallel",)),
        name="qkrope",
    )(q, kv, gq, gk, cosf, sinf, blk_ones(qw), blk_ones(kw))


def _attn_body(q_ref, k_ref, v_ref, o_ref, qs_ref, m_ref, acc_ref, *, tk, nsplit):
    kvh = pl.program_id(1)
    tq = q_ref.shape[0]
    seq = k_ref.shape[0]
    rows = Q_PER_KV * tq
    lane = lax.broadcasted_iota(jnp.int32, (tq, LANES), 1)
    in_head = (lane // HEAD_DIM) == kvh

    for g in range(Q_PER_KV):
        col = q_ref[:, (g // 2) * LANES:(g // 2 + 1) * LANES].astype(F32)
        col = jnp.where((g % 2) == kvh, col, pltpu.roll(col, HEAD_DIM, 1))
        qs_ref[g * tq:(g + 1) * tq, :] = jnp.where(in_head, col, 0.0).astype(BF16)
    m_ref[...] = jnp.full(m_ref.shape, -jnp.inf, F32)
    acc_ref[...] = jnp.zeros(acc_ref.shape, F32)
    part = rows // nsplit

    def step(c, carry):
        r0 = pl.multiple_of(c * tk, tk)
        kc = k_ref[pl.ds(r0, tk), :]
        vc = v_ref[pl.ds(r0, tk), :]
        for h in range(nsplit):
            sl = slice(h * part, (h + 1) * part)
            s = lax.dot_general(qs_ref[sl, :], kc, (((1,), (1,)), ((), ())),
                                preferred_element_type=F32)
            m_prev = m_ref[sl, :]
            m_new = jnp.maximum(m_prev, jnp.max(s, axis=-1, keepdims=True))
            p = jnp.exp2(s - jnp.tile(m_new, (1, tk // LANES)))
            acc_ref[sl, :] = jnp.exp2(m_prev - m_new) * acc_ref[sl, :] + jnp.dot(
                p.astype(BF16), vc, preferred_element_type=F32)
            m_ref[sl, :] = m_new
        return carry

    lax.fori_loop(0, seq // tk, step, 0)

    acc = acc_ref[...]
    o = acc / pltpu.roll(acc, HEAD_DIM, 1)
    for c in range(Q_PER_KV // 2):
        even = o[(2 * c) * tq:(2 * c + 1) * tq, :]
        odd = o[(2 * c + 1) * tq:(2 * c + 2) * tq, :]
        even = jnp.where(kvh == 0, even, pltpu.roll(even, HEAD_DIM, 1))
        odd = jnp.where(kvh == 1, odd, pltpu.roll(odd, HEAD_DIM, 1))
        o_ref[:, c * LANES:(c + 1) * LANES] = jnp.where(lane < HEAD_DIM, even, odd).astype(o_ref.dtype)


def _attention(q, k, v_aug, tq=512, tk=2048, nsplit=2):
    B, L, qw = q.shape
    gw = Q_PER_KV * HEAD_DIM
    kw = k.shape[-1]
    rows = Q_PER_KV * tq
    return pl.pallas_call(
        functools.partial(_attn_body, tk=tk, nsplit=nsplit),
        grid=(B, N_KV_HEADS, L // tq),
        in_specs=[
            pl.BlockSpec((None, tq, gw), lambda b, h, i: (b, i, h)),
            pl.BlockSpec((None, L, kw), lambda b, h, i: (b, 0, 0)),
            pl.BlockSpec((None, None, L, kw), lambda b, h, i: (h, b, 0, 0)),
        ],
        out_specs=pl.BlockSpec((None, tq, gw), lambda b, h, i: (b, i, h)),
        out_shape=jax.ShapeDtypeStruct((B, L, qw), BF16),
        scratch_shapes=[
            pltpu.VMEM((rows, kw), BF16),
            pltpu.VMEM((rows, LANES), F32),
            pltpu.VMEM((rows, kw), F32),
        ],
        compiler_params=_cparams(("parallel", "parallel", "parallel")),
        name="attention",
    )(q, k, v_aug)


def _hdot(a, b):
    ah, al = _split_bf16(a)
    bh, bl = _split_bf16(b)
    return (jnp.dot(ah, bh, preferred_element_type=F32)
            + jnp.dot(al, bh, preferred_element_type=F32)
            + jnp.dot(ah, bl, preferred_element_type=F32))


def _filter_body(feat_ref, t_ref, w1_ref, b1_ref, f1_ref, w2_ref, b2_ref, f2_ref, w3_ref, b3_ref,
                 delta_ref, bwd_ref, h_ref, sum_ref):
    i = pl.program_id(0)
    h = jnp.sin(f1_ref[...] * (_hdot(feat_ref[...], w1_ref[...]) + b1_ref[...]))
    h = jnp.sin(f2_ref[...] * (_hdot(h, w2_ref[...]) + b2_ref[...]))
    h = _hdot(h, w3_ref[...]) + b3_ref[...]
    t = t_ref[...]
    h = h * jnp.exp(-t * delta_ref[...])
    width = h_ref.shape[-1]
    for j in range(h_ref.shape[0]):
        h_ref[j] = h[:, j * width:(j + 1) * width]
    row = lax.broadcasted_iota(jnp.int32, h.shape, 0) + i * h.shape[0]
    a = jnp.where((row == 0) & (bwd_ref[...] > 0.5), 0.0, jnp.abs(h))
    part = jnp.sum(a, axis=0, keepdims=True)

    @pl.when(i == 0)
    def _():
        sum_ref[...] = jnp.zeros(sum_ref.shape, F32)

    sum_ref[...] += jnp.broadcast_to(part, sum_ref.shape)


def _hyena_filter(L, width, fw1, fb1, ff1, fw2, fb2, ff2, fw3, fb3, tl=512):
    bands = (FILTER_EMB - 1) // 2
    t = jnp.linspace(0.0, 1.0, L, dtype=F32)[:, None]
    w = 2.0 * math.pi * jnp.arange(L, dtype=F32)[:, None] / L
    fr = jnp.linspace(1e-4, bands - 1, bands, dtype=F32)[None, :]
    feats = jnp.concatenate([t, jnp.cos(w * fr), -jnp.sin(w * fr)], axis=-1)
    max_decay = math.log(DECAY_TARGET) / FAST_DECAY_PCT
    min_decay = math.log(DECAY_TARGET) / SLOW_DECAY_PCT
    deltas = jnp.abs(jnp.linspace(min_decay, max_decay, width, dtype=F32))
    ncol = fw3.shape[1]
    delta_cols = jnp.tile(deltas, ncol // width)[None, :]
    is_bwd = ((jnp.arange(ncol) // width) % 2).astype(F32)[None, :]
    emb = hid = LANES

    def pad2(a, r, c):
        a = a.astype(F32)
        return jnp.zeros((r, c), F32).at[:a.shape[0], :a.shape[1]].set(a)

    row = lambda a: pad2(a.reshape(1, -1), 1, hid)
    feats = pad2(feats, L, emb)
    fw1, fw2, fw3 = pad2(fw1, emb, hid), pad2(fw2, hid, hid), pad2(fw3, hid, ncol)
    fb3 = fb3.reshape(1, ncol).astype(F32)
    const = lambda shape: pl.BlockSpec(shape, lambda i: (0, 0))
    h, sums = pl.pallas_call(
        _filter_body,
        grid=(L // tl,),
        in_specs=[
            pl.BlockSpec((tl, emb), lambda i: (i, 0)),
            pl.BlockSpec((tl, 1), lambda i: (i, 0)),
            const((emb, hid)), const((1, hid)), const((1, hid)),
            const((hid, hid)), const((1, hid)), const((1, hid)),
            const((hid, ncol)), const((1, ncol)), const((1, ncol)), const((1, ncol)),
        ],
        out_specs=[pl.BlockSpec((ncol // width, tl, width), lambda i: (0, i, 0)),
                   pl.BlockSpec((SUBLANES, ncol), lambda i: (0, 0))],
        out_shape=[jax.ShapeDtypeStruct((ncol // width, L, width), F32),
                   jax.ShapeDtypeStruct((SUBLANES, ncol), F32)],
        compiler_params=_cparams(("arbitrary",)),
        name="hyena_filter",
    )(feats, t, fw1, row(fb1), row(ff1), fw2, row(fb2), row(ff2), fw3, fb3, delta_cols, is_bwd)
    return h, sums[0]


def _shortconv_rows(z_ref, w_ref, b_ref, c, chunk):
    L = z_ref.shape[0]
    row = lax.broadcasted_iota(jnp.int32, (chunk, z_ref.shape[1]), 0)
    r0 = pl.multiple_of(c * chunk, chunk)
    cur = z_ref[pl.ds(r0, chunk), :]
    prev_row = z_ref[pl.ds(jnp.maximum(r0 - 1, 0), 1), :]
    next_row = z_ref[pl.ds(jnp.minimum(r0 + chunk, L - 1), 1), :]
    prev_row = jnp.where(c == 0, 0.0, prev_row)
    next_row = jnp.where(c == L // chunk - 1, 0.0, next_row)
    down = jnp.where(row == 0, prev_row, pltpu.roll(cur, 1, 0))
    up = jnp.where(row == chunk - 1, next_row, pltpu.roll(cur, chunk - 1, 0))
    return b_ref[...] + down * w_ref[0:1, :] + cur * w_ref[1:2, :] + up * w_ref[2:3, :]


def _dft_tables(n1_len, n2_len):
    n = n1_len * n2_len

    def root(num, den):
        ang = (2.0 * math.pi / den) * (num % den).astype(F32)
        return jnp.cos(ang), -jnp.sin(ang)

    i1 = jnp.arange(n1_len, dtype=jnp.int32)
    i2 = jnp.arange(n2_len, dtype=jnp.int32)
    f1r, f1i = root(i1[:, None] * i1[None, :], n1_len)
    f2r, f2i = root(i2[:, None] * i2[None, :], n2_len)
    twr, twi = root(i1[:, None] * i2[None, :], n)
    return (f1r, f1i), (f2r, f2i), (twr, twi)


def _stacked_inner_dft(f2r, f2i, twr_row, twi_row):
    gr = f2r * twr_row - f2i * twi_row
    gi = f2r * twi_row + f2i * twr_row
    top = jnp.concatenate([gr, -gi], axis=1)
    bot = jnp.concatenate([gi, gr], axis=1)
    return jnp.concatenate([top, bot], axis=0)


def _fft_plan(n):
    n1_len = n // FFT_N2
    nk1 = n1_len // 2 + 1
    nk1_pad = -(-nk1 // SUBLANES) * SUBLANES
    chunk = max(c for c in range(1, FFT_K1_CHUNK_MAX + 1) if nk1 % c == 0)
    return n1_len, nk1, nk1_pad, chunk


def _outer_dft_to_scratch(load_rows, fa_ref, ar_ref, ai_ref, nk1_pad):
    fa = fa_ref[...]

    def step(n2, carry):
        a = jnp.dot(fa, load_rows(n2).astype(BF16), preferred_element_type=F32)
        ar_ref[pl.ds(n2, nk1_pad, stride=FFT_PITCH), :] = a[:nk1_pad]
        ai_ref[pl.ds(n2, nk1_pad, stride=FFT_PITCH), :] = a[nk1_pad:]
        return carry

    lax.fori_loop(0, FFT_N2, step, 0, unroll=FFT_UNROLL_OUTER)


def _spec_body(f_ref, b_ref, inv_ref, fa_ref, f2r_ref, f2i_ref, twr_ref, twi_ref, kr_ref, ki_ref,
               fr_ref, fi_ref, br_ref, bi_ref, *, n1_len, nk1_pad, chunk):
    kc = pl.program_id(2)
    half = n1_len // 2

    @pl.when(kc == 0)
    def _():
        inv = inv_ref[...]
        row = lax.broadcasted_iota(jnp.int32, (half, LANES), 0)
        _outer_dft_to_scratch(lambda n2: f_ref[pl.ds(n2, half, stride=FFT_N2), :] * inv,
                              fa_ref, fr_ref, fi_ref, nk1_pad)
        _outer_dft_to_scratch(
            lambda n2: jnp.where((row == 0) & (n2 == 0), 0.0,
                                 b_ref[pl.ds(n2, half, stride=FFT_N2), :] * inv),
            fa_ref, br_ref, bi_ref, nk1_pad)

    f2r = f2r_ref[...]
    f2i = f2i_ref[...]

    def step(t, carry):
        k1 = kc * chunk + t
        base = pl.multiple_of(k1 * FFT_PITCH, SUBLANES)
        mf = _stacked_inner_dft(f2r, f2i, twr_ref[pl.ds(k1, 1), :],
                                twi_ref[pl.ds(k1, 1), :]).astype(BF16)
        rows = pl.ds(base, FFT_N2)
        xf = jnp.dot(mf, jnp.concatenate([fr_ref[rows, :], fi_ref[rows, :]], axis=0).astype(BF16),
                     preferred_element_type=F32)
        xb = jnp.dot(mf, jnp.concatenate([br_ref[rows, :], bi_ref[rows, :]], axis=0).astype(BF16),
                     preferred_element_type=F32)
        o = pl.multiple_of(t * FFT_N2, FFT_N2)
        kr_ref[pl.ds(o, FFT_N2), :] = xf[:FFT_N2] + xb[:FFT_N2]
        ki_ref[pl.ds(o, FFT_N2), :] = xf[FFT_N2:] - xb[FFT_N2:]
        return carry

    lax.fori_loop(0, chunk, step, 0)


def _filter_spectrum(h, inv_norm, tables):
    O2, L, C = h.shape
    O = O2 // 2
    n1_len, nk1, nk1_pad, chunk = _fft_plan(2 * L)
    half = n1_len // 2
    (f1r, f1i), (f2r, f2i), (twr, twi) = tables
    fa = jnp.concatenate([f1r[:nk1_pad, :half], f1i[:nk1_pad, :half]], axis=0).astype(BF16)
    rows = chunk * FFT_N2
    const = lambda shape: pl.BlockSpec(shape, lambda o, c, k: (0, 0))
    out = jax.ShapeDtypeStruct((O, nk1 * FFT_N2, C), F32)
    scratch = pltpu.VMEM((nk1_pad * FFT_PITCH, LANES), F32)
    return pl.pallas_call(
        functools.partial(_spec_body, n1_len=n1_len, nk1_pad=nk1_pad, chunk=chunk),
        grid=(O, C // LANES, nk1 // chunk),
        in_specs=[
            pl.BlockSpec((None, L, LANES), lambda o, c, k: (2 * o, 0, c)),
            pl.BlockSpec((None, L, LANES), lambda o, c, k: (2 * o + 1, 0, c)),
            pl.BlockSpec((None, 1, LANES), lambda o, c, k: (o, 0, c)),
            const(fa.shape), const(f2r.shape), const(f2i.shape), const(twr.shape), const(twi.shape),
        ],
        out_specs=[pl.BlockSpec((None, rows, LANES), lambda o, c, k: (o, k, c)),
                   pl.BlockSpec((None, rows, LANES), lambda o, c, k: (o, k, c))],
        out_shape=[out, out],
        scratch_shapes=[scratch, scratch, scratch, scratch],
        compiler_params=_cparams(("parallel", "parallel", "arbitrary")),
        name="filter_spectrum",
    )(h, h, inv_norm, fa, f2r, f2i, twr, twi)


def _fftconv_body(y_ref, gate_ref, wy_ref, by_ref, wg_ref, bg_ref, d_ref, kr_ref, ki_ref, fa_ref,
                  fs_ref, f2r_ref, f2i_ref, twr_ref, twi_ref, o_ref, xs_ref, ar_ref, ai_ref, *,
                  n1_len, nk1_pad, chunk, conv_y):
    kc = pl.program_id(2)
    half = n1_len // 2

    @pl.when(kc == 0)
    def _():
        def copy(n1, carry):
            src = pl.multiple_of(n1 * FFT_N2, FFT_N2)
            dst = pl.multiple_of(n1 * FFT_PITCH, SUBLANES)
            if conv_y:
                xs_ref[pl.ds(dst, FFT_N2), :] = _shortconv_rows(y_ref, wy_ref, by_ref, n1, FFT_N2)
            else:
                xs_ref[pl.ds(dst, FFT_N2), :] = y_ref[pl.ds(src, FFT_N2), :]
            return carry

        lax.fori_loop(0, half, copy, 0)
        _outer_dft_to_scratch(lambda n2: xs_ref[pl.ds(n2, half, stride=FFT_PITCH), :],
                              fa_ref, ar_ref, ai_ref, nk1_pad)

    f2r = f2r_ref[...]
    f2i = f2i_ref[...]

    def step(t, carry):
        k1 = kc * chunk + t
        base = pl.multiple_of(k1 * FFT_PITCH, SUBLANES)
        rhs = jnp.concatenate([ar_ref[pl.ds(base, FFT_N2), :], ai_ref[pl.ds(base, FFT_N2), :]],
                              axis=0).astype(BF16)
        mf = _stacked_inner_dft(f2r, f2i, twr_ref[pl.ds(k1, 1), :], twi_ref[pl.ds(k1, 1), :])
        x = jnp.dot(mf.astype(BF16), rhs, preferred_element_type=F32)
        xr, xi = x[:FFT_N2], x[FFT_N2:]
        o = pl.multiple_of(t * FFT_N2, FFT_N2)
        kr = kr_ref[pl.ds(o, FFT_N2), :]
        ki = ki_ref[pl.ds(o, FFT_N2), :]
        z = jnp.concatenate([xr * kr - xi * ki, xr * ki + xi * kr], axis=0).astype(BF16)
        b = jnp.dot(mf.T.astype(BF16), z, preferred_element_type=F32)
        ar_ref[pl.ds(base, FFT_N2), :] = b[:FFT_N2]
        ai_ref[pl.ds(base, FFT_N2), :] = b[FFT_N2:]
        return carry

    lax.fori_loop(0, chunk, step, 0, unroll=FFT_UNROLL_INNER)

    @pl.when(kc == pl.num_programs(2) - 1)
    def _():
        fs = fs_ref[...]

        def inv_outer(n2, carry):
            rhs = jnp.concatenate([ar_ref[pl.ds(n2, nk1_pad, stride=FFT_PITCH), :],
                                   ai_ref[pl.ds(n2, nk1_pad, stride=FFT_PITCH), :]],
                                  axis=0).astype(BF16)
            conv = jnp.dot(fs, rhs, preferred_element_type=F32)
            ar_ref[pl.ds(n2, half, stride=FFT_PITCH), :] = conv
            return carry

        lax.fori_loop(0, FFT_N2, inv_outer, 0, unroll=FFT_UNROLL_OUTER)
        d = d_ref[...]

        def finish(n1, carry):
            src = pl.multiple_of(n1 * FFT_PITCH, SUBLANES)
            dst = pl.multiple_of(n1 * FFT_N2, FFT_N2)
            y = xs_ref[pl.ds(src, FFT_N2), :]
            gate = _shortconv_rows(gate_ref, wg_ref, bg_ref, n1, FFT_N2)
            o_ref[pl.ds(dst, FFT_N2), :] = (gate * (
                ar_ref[pl.ds(src, FFT_N2), :] + y * d)).astype(o_ref.dtype)
            return carry

        lax.fori_loop(0, half, finish, 0)


def _fftconv_gate(y, y_off, conv_y, z, gate_off, conv_w, conv_b, d, kr, ki, tables, out_dtype):
    B, L, _ = y.shape
    C = d.shape[-1]
    yo, go = y_off // LANES, gate_off // LANES
    wo = yo if conv_y else go
    N = 2 * L
    n1_len, nk1, nk1_pad, chunk = _fft_plan(N)
    half = n1_len // 2
    (f1r, f1i), (f2r, f2i), (twr, twi) = tables
    fa = jnp.concatenate([f1r[:nk1_pad, :half], f1i[:nk1_pad, :half]], axis=0).astype(BF16)
    wts = jnp.concatenate([jnp.ones((1,), F32), jnp.full((nk1 - 2,), 2.0, F32), jnp.ones((1,), F32),
                           jnp.zeros((nk1_pad - nk1,), F32)]) * (1.0 / N)
    fs = jnp.concatenate([f1r[:half, :nk1_pad] * wts, f1i[:half, :nk1_pad] * wts],
                         axis=1).astype(BF16)
    rows = chunk * FFT_N2
    const = lambda shape: pl.BlockSpec(shape, lambda c, b, k: (0, 0))
    return pl.pallas_call(
        functools.partial(_fftconv_body, n1_len=n1_len, nk1_pad=nk1_pad, chunk=chunk,
                          conv_y=conv_y),
        grid=(C // LANES, B, nk1 // chunk),
        in_specs=[
            pl.BlockSpec((None, L, LANES), lambda c, b, k: (b, 0, c + yo)),
            pl.BlockSpec((None, L, LANES), lambda c, b, k: (b, 0, c + go)),
            pl.BlockSpec((SHORT_CONV, LANES), lambda c, b, k: (0, c + wo)),
            pl.BlockSpec((1, LANES), lambda c, b, k: (0, c + wo)),
            pl.BlockSpec((SHORT_CONV, LANES), lambda c, b, k: (0, c + go)),
            pl.BlockSpec((1, LANES), lambda c, b, k: (0, c + go)),
            pl.BlockSpec((1, LANES), lambda c, b, k: (0, c)),
            pl.BlockSpec((rows, LANES), lambda c, b, k: (k, c)),
            pl.BlockSpec((rows, LANES), lambda c, b, k: (k, c)),
            const(fa.shape), const(fs.shape), const(f2r.shape), const(f2i.shape),
            const(twr.shape), const(twi.shape),
        ],
        out_specs=pl.BlockSpec((None, L, LANES), lambda c, b, k: (b, 0, c)),
        out_shape=jax.ShapeDtypeStruct((B, L, C), out_dtype),
        scratch_shapes=[pltpu.VMEM((half * FFT_PITCH, LANES), F32),
                        pltpu.VMEM((nk1_pad * FFT_PITCH, LANES), F32),
                        pltpu.VMEM((nk1_pad * FFT_PITCH, LANES), F32)],
        compiler_params=_cparams(("parallel", "parallel", "arbitrary")),
        name="fftconv_gate",
    )(y, z, conv_w, conv_b, conv_w, conv_b, d, kr, ki, fa, fs, f2r, f2i, twr, twi)


def _hyena(z, conv_w, conv_b, fw1, fb1, ff1, fw2, fb2, ff2, fw3, fb3, hyena_d):
    B, L, C3 = z.shape
    W = C3 // 3
    h, sums = _hyena_filter(L, W, fw1, fb1, ff1, fw2, fb2, ff2, fw3, fb3)
    sums = sums.reshape(HYENA_ORDER, 2, W)
    inv_norm = (1.0 / (sums[:, 0] + sums[:, 1]))[:, None, :]
    tables = _dft_tables(2 * L // FFT_N2, FFT_N2)
    kr, ki = _filter_spectrum(h, inv_norm, tables)
    conv_b = conv_b.reshape(1, C3)
    y = z
    for o in range(HYENA_ORDER):
        y = _fftconv_gate(y, 0, o == 0, z, (o + 1) * W, conv_w, conv_b,
                          hyena_d[o].reshape(1, W).astype(F32), kr[o], ki[o], tables,
                          BF16 if o == HYENA_ORDER - 1 else F32)
    return y


def _merge_body(yh_ref, ya_ref, g_ref, x_ref, whu_ref, wau_ref, wo_ref, n2_ref, rwh_ref, rwl_ref,
                rb_ref, h_ref, hn_ref, lg_ref):
    D = x_ref.shape[-1]
    up_h = jnp.dot(yh_ref[...], whu_ref[...], preferred_element_type=F32)
    up_a = jnp.dot(ya_ref[...], wau_ref[...], preferred_element_type=F32)
    g = g_ref[...].astype(F32)
    merged = jax.nn.sigmoid(g[:, :D]) * up_h + jax.nn.sigmoid(g[:, D:]) * up_a
    h = x_ref[...] + jnp.dot(merged.astype(BF16), wo_ref[...], preferred_element_type=F32)
    h_ref[...] = h
    ms = jnp.mean(h * h, axis=-1, keepdims=True)
    hn = h * lax.rsqrt(ms + EPS) * n2_ref[...]
    hn_ref[...] = hn.astype(BF16)
    hh, hl = _split_bf16(hn)
    lg_ref[...] = (jnp.dot(hh, rwh_ref[...], preferred_element_type=F32)
                   + jnp.dot(hl, rwh_ref[...], preferred_element_type=F32)
                   + jnp.dot(hh, rwl_ref[...], preferred_element_type=F32)) + rb_ref[...]


def _merge(yh, ya, gates, xt, whu, wau, wo, n2g, rw, rb, tm=256):
    T, D = xt.shape
    W = yh.shape[1]
    E = rw.shape[1]
    rwp = jnp.zeros((D, LOGIT_PAD), F32).at[:, :E].set(rw)
    rwh, rwl = _split_bf16(rwp)
    rbp = jnp.zeros((1, LOGIT_PAD), F32).at[0, :E].set(rb)
    rowblk = lambda w: pl.BlockSpec((tm, w), lambda i: (i, 0))
    const = lambda shape: pl.BlockSpec(shape, lambda i: (0, 0))
    return pl.pallas_call(
        _merge_body,
        grid=(T // tm,),
        in_specs=[rowblk(W), rowblk(W), rowblk(2 * D), rowblk(D),
                  const((W, D)), const((W, D)), const((D, D)), const((1, D)),
                  const((D, LOGIT_PAD)), const((D, LOGIT_PAD)), const((1, LOGIT_PAD))],
        out_specs=[rowblk(D), rowblk(D), rowblk(LOGIT_PAD)],
        out_shape=[jax.ShapeDtypeStruct((T, D), F32), jax.ShapeDtypeStruct((T, D), BF16),
                   jax.ShapeDtypeStruct((T, LOGIT_PAD), F32)],
        compiler_params=_cparams(("parallel",)),
        name="merge",
    )(yh, ya, gates, xt, whu.astype(BF16), wau.astype(BF16), wo.astype(BF16), n2g.reshape(1, D),
      rwh, rwl, rbp)


def _moe_body(be_ref, nused_ref, x_ref, w1g_ref, w1l_ref, b1g_ref, b1l_ref, w2_ref, b2_ref, o_ref):
    i = pl.program_id(0)

    @pl.when(i < nused_ref[0])
    def _():
        x = x_ref[...]
        nt = (((1,), (1,)), ((), ()))
        glu = lax.dot_general(x, w1g_ref[...], nt, preferred_element_type=F32) + b1g_ref[...]
        lin = lax.dot_general(x, w1l_ref[...], nt, preferred_element_type=F32) + b1l_ref[...]
        glu = jnp.minimum(glu, SWIGLU_LIMIT)
        lin = jnp.clip(lin, -SWIGLU_LIMIT, SWIGLU_LIMIT)
        act = glu * jax.nn.sigmoid(SWIGLU_ALPHA * glu) * (lin + 1.0)
        y = jnp.dot(act.astype(BF16), w2_ref[...].astype(BF16),
                    preferred_element_type=F32) + b2_ref[...]
        o_ref[...] = y.astype(o_ref.dtype)

    @pl.when(i >= nused_ref[0])
    def _():
        o_ref[...] = jnp.zeros(o_ref.shape, o_ref.dtype)


def _moe_experts(xg, block_expert, n_used, w1g, w1l, b1g, b1l, w2, b2):
    P, D = xg.shape
    dff = w2.shape[1]
    nb = P // MOE_TM
    wspec = lambda k, n: pl.BlockSpec((None, k, n), lambda i, be, nu: (be[i], 0, 0))
    grid_spec = pltpu.PrefetchScalarGridSpec(
        num_scalar_prefetch=2,
        grid=(nb,),
        in_specs=[
            pl.BlockSpec((MOE_TM, D), lambda i, be, nu: (i, 0)),
            wspec(dff, D), wspec(dff, D), wspec(1, dff), wspec(1, dff),
            wspec(dff, D), wspec(1, D),
        ],
        out_specs=pl.BlockSpec((MOE_TM, D), lambda i, be, nu: (i, 0)),
    )
    return pl.pallas_call(
        _moe_body,
        grid_spec=grid_spec,
        out_shape=jax.ShapeDtypeStruct((P, D), BF16),
        compiler_params=_cparams(("arbitrary",)),
        name="moe_experts",
    )(block_expert, n_used, xg, w1g, w1l, b1g, b1l, w2, b2)


def _prep_w1_body(w_ref, g_ref, l_ref, t_ref):
    half = g_ref.shape[0]
    wt = w_ref[...].T
    for c in range(t_ref.shape[0]):
        cols = slice(c * LANES, (c + 1) * LANES)
        t_ref[c] = wt[:, cols]
        g_ref[:, cols] = t_ref[c, pl.ds(0, half, stride=2), :].astype(BF16)
        l_ref[:, cols] = t_ref[c, pl.ds(1, half, stride=2), :].astype(BF16)


def _prep_w1(w1, tc=8 * LANES):
    E, D, F2 = w1.shape
    tc = min(tc, F2)
    out = jax.ShapeDtypeStruct((E, F2 // 2, D), BF16)
    return pl.pallas_call(
        _prep_w1_body,
        grid=(E, F2 // tc),
        in_specs=[pl.BlockSpec((None, D, tc), lambda e, j: (e, 0, j))],
        out_specs=[pl.BlockSpec((None, tc // 2, D), lambda e, j: (e, j, 0)),
                   pl.BlockSpec((None, tc // 2, D), lambda e, j: (e, j, 0))],
        out_shape=[out, out],
        scratch_shapes=[pltpu.VMEM((D // LANES, tc, LANES), F32)],
        compiler_params=_cparams(("parallel", "parallel")),
        name="prep_w1",
    )(w1)


def _combine_body(h_ref, y_ref, g_ref, o_ref):
    acc = h_ref[...]
    g = g_ref[...]
    for k in range(y_ref.shape[0]):
        acc = acc + g[:, k:k + 1] * y_ref[k].astype(F32)
    o_ref[...] = acc


def _combine(h, yk, gates, tm=512):
    T, D = h.shape
    K = yk.shape[0]
    return pl.pallas_call(
        _combine_body,
        grid=(T // tm,),
        in_specs=[pl.BlockSpec((tm, D), lambda i: (i, 0)),
                  pl.BlockSpec((K, tm, D), lambda i: (0, i, 0)),
                  pl.BlockSpec((tm, K), lambda i: (i, 0))],
        out_specs=pl.BlockSpec((tm, D), lambda i: (i, 0)),
        out_shape=jax.ShapeDtypeStruct((T, D), F32),
        compiler_params=_cparams(("parallel",)),
        name="moe_combine",
    )(h, yk, gates)


def _lookup(table, idx):
    n = table.shape[0]
    hit = idx[None, :] == jnp.arange(n, dtype=idx.dtype)[:, None]
    return jnp.sum(jnp.where(hit, table[:, None], 0), axis=0)


def _moe(h, hn_bf16, logits, w1, b1, w2, b2):
    T, D = hn_bf16.shape
    E = w1.shape[0]
    top_val, top_idx = lax.top_k(logits, TOP_K)
    gates = jax.nn.softmax(top_val, axis=-1)
    TK = T * TOP_K
    e_flat = top_idx.reshape(TK).astype(jnp.int32)
    order = jnp.argsort(e_flat).astype(jnp.int32)
    rank = jnp.argsort(order).astype(jnp.int32)
    counts = jnp.sum(jnp.arange(E, dtype=jnp.int32)[:, None] == e_flat[None, :], axis=1,
                     dtype=jnp.int32)
    starts = jnp.cumsum(counts) - counts
    padded = ((counts + MOE_TM - 1) // MOE_TM) * MOE_TM
    pad_ends = jnp.cumsum(padded)
    pad_starts = pad_ends - padded
    nb = (TK + E * (MOE_TM - 1) + MOE_TM - 1) // MOE_TM
    block_start = jnp.arange(nb, dtype=jnp.int32) * MOE_TM
    block_expert = jnp.minimum(jnp.sum(block_start[:, None] >= pad_ends[None, :], axis=1),
                               E - 1).astype(jnp.int32)
    n_used = (pad_ends[-1] // MOE_TM).astype(jnp.int32).reshape(1)
    blk_first = block_start - pad_starts[block_expert]
    within = blk_first[:, None] + jnp.arange(MOE_TM, dtype=jnp.int32)[None, :]
    valid = within < counts[block_expert][:, None]
    sorted_idx = jnp.where(valid, starts[block_expert][:, None] + within, 0).reshape(nb * MOE_TM)
    filler = jnp.arange(nb * MOE_TM, dtype=jnp.int32) % T
    src = jnp.where(valid.reshape(nb * MOE_TM), order[sorted_idx] // TOP_K, filler)
    pos = _lookup(pad_starts - starts, e_flat) + rank
    w1g, w1l = _prep_w1(w1)
    y = _moe_experts(hn_bf16[src], block_expert, n_used, w1g, w1l,
                     b1[:, None, 0::2].astype(F32), b1[:, None, 1::2].astype(F32),
                     w2, b2[:, None, :].astype(F32))
    return _combine(h, y[pos.reshape(T, TOP_K).T], gates)


def _rope_tables(L):
    rows = L // GRID_W
    row = jnp.repeat(jnp.arange(rows, dtype=F32), GRID_W)
    col = jnp.tile(jnp.arange(GRID_W, dtype=F32), rows)
    half = HEAD_DIM // 2
    freqs = ROPE_THETA ** (-jnp.arange(0, half, 2, dtype=F32) / half)
    ang = jnp.concatenate([row[:, None] * freqs, col[:, None] * freqs], axis=-1)
    cos = jnp.repeat(jnp.cos(ang), 2, axis=-1)
    sin = jnp.repeat(jnp.sin(ang), 2, axis=-1)
    sign = jnp.tile(jnp.array([-1.0, 1.0], F32), HEAD_DIM // 2)
    reps = LANES // HEAD_DIM
    return jnp.tile(cos, (1, reps)), jnp.tile(sin * sign, (1, reps))


def kernel(x, norm1_g, w_in, conv_w, conv_b, filt_w1, filt_b1, filt_freq1, filt_w2, filt_b2, filt_freq2, filt_w3, filt_b3, hyena_d, q_norm_g, k_norm_g, w_hyena_up, w_attn_up, w_out, norm2_g, router_w, router_b, expert_w1, expert_b1, expert_w2, expert_b2):
    B, L, D = x.shape
    T = B * L
    depth = w_in.shape[0]
    hw = conv_w.shape[-1] // 3
    aw = N_Q_HEADS * HEAD_DIM
    kvw = N_KV_HEADS * HEAD_DIM
    widths = (3 * hw, aw, 2 * kvw, 2 * D)
    cosf, sinf = _rope_tables(L)
    for l in range(depth):
        xt = x.reshape(T, D)
        z, q, kv, gates = _inproj(xt, norm1_g[l], w_in[l].astype(BF16), widths,
                                  (F32, BF16, BF16, BF16))
        y_hy = _hyena(z.reshape(B, L, 3 * hw), conv_w[l], conv_b[l], filt_w1[l], filt_b1[l],
                      filt_freq1[l], filt_w2[l], filt_b2[l], filt_freq2[l], filt_w3[l], filt_b3[l],
                      hyena_d[l])
        gq = jnp.tile(q_norm_g[l].astype(F32), N_Q_HEADS)[None, :]
        gk = jnp.tile(k_norm_g[l].astype(F32), N_KV_HEADS)[None, :]
        qr, kr, va = _qkrope(q, kv, gq, gk, cosf, sinf, L)
        y_at = _attention(qr.reshape(B, L, aw), kr.reshape(B, L, kvw),
                          va.reshape(N_KV_HEADS, B, L, kvw))
        h, hn, logits = _merge(y_hy.reshape(T, hw), y_at.reshape(T, aw), gates, xt,
                               w_hyena_up[l], w_attn_up[l], w_out[l], norm2_g[l],
                               router_w[l], router_b[l])
        x = _moe(h, hn, logits[:, :N_EXPERTS], expert_w1[l], expert_b1[l], expert_w2[l],
                 expert_b2[l]).reshape(B, L, D)
    return x
```

```python
import functools
import math

import jax
import jax.numpy as jnp
from jax import lax
from jax.experimental import pallas as pl
from jax.experimental.pallas import tpu as pltpu

F32 = jnp.float32
BF16 = jnp.bfloat16

GRID_W = 64
HEAD_DIM = 64
N_Q_HEADS = 8
N_KV_HEADS = 2
Q_PER_KV = N_Q_HEADS // N_KV_HEADS
ROPE_THETA = 10000.0
HYENA_ORDER = 2
SHORT_CONV = 3
FILTER_EMB = 33
FAST_DECAY_PCT = 0.3
SLOW_DECAY_PCT = 1.5
DECAY_TARGET = 1e-2
N_EXPERTS = 32
TOP_K = 4
SWIGLU_LIMIT = 7.0
SWIGLU_ALPHA = 1.702
EPS = 1e-6

LANES = 128
SUBLANES = 8
VMEM_LIMIT = 56 * 1024 * 1024

FFT_N2 = LANES
FFT_PITCH = FFT_N2 + SUBLANES
FFT_K1_CHUNK_MAX = 16
FFT_UNROLL_OUTER = 16
FFT_UNROLL_INNER = True
MOE_TM = 512
LOGIT_PAD = LANES


def _cparams(sem):
    return pltpu.CompilerParams(dimension_semantics=sem, vmem_limit_bytes=VMEM_LIMIT)


def _split_bf16(x):
    hi = x.astype(BF16)
    lo = (x - hi.astype(F32)).astype(BF16)
    return hi, lo


def _inproj_body(x_ref, g_ref, w_ref, z_ref, q_ref, kv_ref, gate_ref, *, widths):
    x = x_ref[...]
    ms = jnp.mean(x * x, axis=-1, keepdims=True)
    u = (x * lax.rsqrt(ms + EPS) * g_ref[...]).astype(BF16)
    off = 0
    for ref, w in zip((z_ref, q_ref, kv_ref, gate_ref), widths):
        ref[...] = jnp.dot(u, w_ref[:, off:off + w], preferred_element_type=F32).astype(ref.dtype)
        off += w


def _inproj(xt, g, w_bf16, widths, dtypes, tm=512):
    T, D = xt.shape
    n = w_bf16.shape[1]
    return pl.pallas_call(
        functools.partial(_inproj_body, widths=widths),
        grid=(T // tm,),
        in_specs=[
            pl.BlockSpec((tm, D), lambda i: (i, 0)),
            pl.BlockSpec((1, D), lambda i: (0, 0)),
            pl.BlockSpec((D, n), lambda i: (0, 0)),
        ],
        out_specs=[pl.BlockSpec((tm, w), lambda i: (i, 0)) for w in widths],
        out_shape=[jax.ShapeDtypeStruct((T, w), dt) for w, dt in zip(widths, dtypes)],
        compiler_params=_cparams(("parallel",)),
        name="inproj",
    )(xt, g.reshape(1, D), w_bf16)


def _head_norm_rope(x, gain, cosf, sinf, ones_blk):
    w = x.shape[-1]
    hi, lo = _split_bf16(x * x)
    ss = (jnp.dot(hi, ones_blk, preferred_element_type=F32)
          + jnp.dot(lo, ones_blk, preferred_element_type=F32))
    xn = x * lax.rsqrt(ss * (1.0 / HEAD_DIM) + EPS) * gain
    lane = lax.broadcasted_iota(jnp.int32, (x.shape[0], LANES), 1)
    cols = []
    for c in range(w // LANES):
        col = xn[:, c * LANES:(c + 1) * LANES]
        cols.append(jnp.where(lane % 2 == 0, pltpu.roll(col, LANES - 1, 1), pltpu.roll(col, 1, 1)))
    swapped = cols[0] if len(cols) == 1 else jnp.concatenate(cols, axis=1)
    return xn * cosf + swapped * sinf


def _qkrope_body(q_ref, kv_ref, gq_ref, gk_ref, cos_ref, sin_ref, oq_ref, ok_ref,
                 qo_ref, ko_ref, vo_ref):
    cosf = cos_ref[...]
    sinf = sin_ref[...]
    nq = q_ref.shape[-1] // LANES
    q = _head_norm_rope(q_ref[...].astype(F32), gq_ref[...], jnp.tile(cosf, (1, nq)),
                        jnp.tile(sinf, (1, nq)), oq_ref[...])
    qo_ref[...] = (q * (HEAD_DIM ** -0.5 * math.log2(math.e))).astype(BF16)
    kv = kv_ref[...].astype(F32)
    kw = kv.shape[-1] // 2
    k = _head_norm_rope(kv[:, :kw], gk_ref[...], cosf, sinf, ok_ref[...])
    ko_ref[...] = k.T.astype(BF16)
    v = kv[:, kw:]
    lane = lax.broadcasted_iota(jnp.int32, v.shape, 1)
    for h in range(N_KV_HEADS):
        vo_ref[h] = jnp.where((lane // HEAD_DIM) == h, v, 1.0).astype(BF16)


def _qkrope(q, kv, gq, gk, cosf, sinf, seq_len, tm=512):
    T, qw = q.shape
    kw = kv.shape[1] // 2
    assert kw == LANES and N_KV_HEADS * HEAD_DIM == LANES and N_KV_HEADS == 2
    nl = seq_len // tm

    def blk_ones(w):
        r = jnp.arange(w) // HEAD_DIM
        return (r[:, None] == r[None, :]).astype(BF16)

    return pl.pallas_call(
        _qkrope_body,
        grid=(T // tm,),
        in_specs=[
            pl.BlockSpec((tm, qw), lambda i: (i, 0)),
            pl.BlockSpec((tm, 2 * kw), lambda i: (i, 0)),
            pl.BlockSpec((1, qw), lambda i: (0, 0)),
            pl.BlockSpec((1, kw), lambda i: (0, 0)),
            pl.BlockSpec((tm, LANES), lambda i: (i % nl, 0)),
            pl.BlockSpec((tm, LANES), lambda i: (i % nl, 0)),
            pl.BlockSpec((qw, qw), lambda i: (0, 0)),
            pl.BlockSpec((kw, kw), lambda i: (0, 0)),
        ],
        out_specs=[
            pl.BlockSpec((tm, qw), lambda i: (i, 0)),
            pl.BlockSpec((kw, tm), lambda i: (0, i)),
            pl.BlockSpec((N_KV_HEADS, tm, kw), lambda i: (0, i, 0)),
        ],
        out_shape=[
            jax.ShapeDtypeStruct((T, qw), BF16),
            jax.ShapeDtypeStruct((kw, T), BF16),
            jax.ShapeDtypeStruct((N_KV_HEADS, T, kw), BF16),
        ],
        compiler_params=_cparams(("parallel",)),
        name="qkrope",
    )(q, kv, gq, gk, cosf, sinf, blk_ones(qw), blk_ones(kw))


def _attn_body(q_ref, k_ref, v_ref, o_ref, qs_ref, m_ref, acc_ref, *, tk, nsplit):
    kvh = pl.program_id(1)
    tq = q_ref.shape[0]
    seq = k_ref.shape[1]
    rows = Q_PER_KV * tq
    lane = lax.broadcasted_iota(jnp.int32, (tq, LANES), 1)
    in_head = (lane // HEAD_DIM) == kvh

    for g in range(Q_PER_KV):
        col = q_ref[:, (g // 2) * LANES:(g // 2 + 1) * LANES].astype(F32)
        col = jnp.where((g % 2) == kvh, col, pltpu.roll(col, HEAD_DIM, 1))
        qs_ref[g * tq:(g + 1) * tq, :] = jnp.where(in_head, col, 0.0).astype(BF16)
    m_ref[...] = jnp.full(m_ref.shape, -jnp.inf, F32)
    acc_ref[...] = jnp.zeros(acc_ref.shape, F32)
    part = rows // nsplit

    def step(c, carry):
        r0 = pl.multiple_of(c * tk, tk)
        kc = k_ref[:, pl.ds(r0, tk)]
        vc = v_ref[pl.ds(r0, tk), :]
        for h in range(nsplit):
            sl = slice(h * part, (h + 1) * part)
            s = jnp.dot(qs_ref[sl, :], kc, preferred_element_type=F32)
            m_prev = m_ref[sl, :]
            m_new = jnp.maximum(m_prev, jnp.max(s, axis=-1, keepdims=True))
            p = jnp.exp2(s - jnp.tile(m_new, (1, tk // LANES)))
            acc_ref[sl, :] = jnp.exp2(m_prev - m_new) * acc_ref[sl, :] + jnp.dot(
                p.astype(BF16), vc, preferred_element_type=F32)
            m_ref[sl, :] = m_new
        return carry

    lax.fori_loop(0, seq // tk, step, 0)

    acc = acc_ref[...]
    o = acc / pltpu.roll(acc, HEAD_DIM, 1)
    for c in range(Q_PER_KV // 2):
        even = o[(2 * c) * tq:(2 * c + 1) * tq, :]
        odd = o[(2 * c + 1) * tq:(2 * c + 2) * tq, :]
        even = jnp.where(kvh == 0, even, pltpu.roll(even, HEAD_DIM, 1))
        odd = jnp.where(kvh == 1, odd, pltpu.roll(odd, HEAD_DIM, 1))
        o_ref[:, c * LANES:(c + 1) * LANES] = jnp.where(lane < HEAD_DIM, even, odd).astype(o_ref.dtype)


def _attention(q, kt, v_aug, tq=512, tk=2048, nsplit=2):
    B, L, qw = q.shape
    gw = Q_PER_KV * HEAD_DIM
    kw = kt.shape[0]
    rows = Q_PER_KV * tq
    return pl.pallas_call(
        functools.partial(_attn_body, tk=tk, nsplit=nsplit),
        grid=(B, N_KV_HEADS, L // tq),
        in_specs=[
            pl.BlockSpec((None, tq, gw), lambda b, h, i: (b, i, h)),
            pl.BlockSpec((kw, L), lambda b, h, i: (0, b)),
            pl.BlockSpec((None, None, L, kw), lambda b, h, i: (h, b, 0, 0)),
        ],
        out_specs=pl.BlockSpec((None, tq, gw), lambda b, h, i: (b, i, h)),
        out_shape=jax.ShapeDtypeStruct((B, L, qw), BF16),
        scratch_shapes=[
            pltpu.VMEM((rows, kw), BF16),
            pltpu.VMEM((rows, LANES), F32),
            pltpu.VMEM((rows, kw), F32),
        ],
        compiler_params=_cparams(("parallel", "parallel", "parallel")),
        name="attention",
    )(q, kt, v_aug)


def _hdot(a, b):
    ah, al = _split_bf16(a)
    bh, bl = _split_bf16(b)
    return (jnp.dot(ah, bh, preferred_element_type=F32)
            + jnp.dot(al, bh, preferred_element_type=F32)
            + jnp.dot(ah, bl, preferred_element_type=F32))


def _filter_body(feat_ref, t_ref, w1_ref, b1_ref, f1_ref, w2_ref, b2_ref, f2_ref, w3_ref, b3_ref,
                 delta_ref, bwd_ref, h_ref, sum_ref):
    i = pl.program_id(0)
    h = jnp.sin(f1_ref[...] * (_hdot(feat_ref[...], w1_ref[...]) + b1_ref[...]))
    h = jnp.sin(f2_ref[...] * (_hdot(h, w2_ref[...]) + b2_ref[...]))
    h = _hdot(h, w3_ref[...]) + b3_ref[...]
    t = t_ref[...]
    h = h * jnp.exp(-t * delta_ref[...])
    width = h_ref.shape[-1]
    for j in range(h_ref.shape[0]):
        h_ref[j] = h[:, j * width:(j + 1) * width]
    row = lax.broadcasted_iota(jnp.int32, h.shape, 0) + i * h.shape[0]
    a = jnp.where((row == 0) & (bwd_ref[...] > 0.5), 0.0, jnp.abs(h))
    part = jnp.sum(a, axis=0, keepdims=True)

    @pl.when(i == 0)
    def _():
        sum_ref[...] = jnp.zeros(sum_ref.shape, F32)

    sum_ref[...] += jnp.broadcast_to(part, sum_ref.shape)


def _hyena_filter(L, width, fw1, fb1, ff1, fw2, fb2, ff2, fw3, fb3, tl=512):
    bands = (FILTER_EMB - 1) // 2
    t = jnp.linspace(0.0, 1.0, L, dtype=F32)[:, None]
    w = 2.0 * math.pi * jnp.arange(L, dtype=F32)[:, None] / L
    fr = jnp.linspace(1e-4, bands - 1, bands, dtype=F32)[None, :]
    feats = jnp.concatenate([t, jnp.cos(w * fr), -jnp.sin(w * fr)], axis=-1)
    max_decay = math.log(DECAY_TARGET) / FAST_DECAY_PCT
    min_decay = math.log(DECAY_TARGET) / SLOW_DECAY_PCT
    deltas = jnp.abs(jnp.linspace(min_decay, max_decay, width, dtype=F32))
    ncol = fw3.shape[1]
    delta_cols = jnp.tile(deltas, ncol // width)[None, :]
    is_bwd = ((jnp.arange(ncol) // width) % 2).astype(F32)[None, :]
    emb = hid = LANES

    def pad2(a, r, c):
        a = a.astype(F32)
        return jnp.zeros((r, c), F32).at[:a.shape[0], :a.shape[1]].set(a)

    row = lambda a: pad2(a.reshape(1, -1), 1, hid)
    feats = pad2(feats, L, emb)
    fw1, fw2, fw3 = pad2(fw1, emb, hid), pad2(fw2, hid, hid), pad2(fw3, hid, ncol)
    fb3 = fb3.reshape(1, ncol).astype(F32)
    const = lambda shape: pl.BlockSpec(shape, lambda i: (0, 0))
    h, sums = pl.pallas_call(
        _filter_body,
        grid=(L // tl,),
        in_specs=[
            pl.BlockSpec((tl, emb), lambda i: (i, 0)),
            pl.BlockSpec((tl, 1), lambda i: (i, 0)),
            const((emb, hid)), const((1, hid)), const((1, hid)),
            const((hid, hid)), const((1, hid)), const((1, hid)),
            const((hid, ncol)), const((1, ncol)), const((1, ncol)), const((1, ncol)),
        ],
        out_specs=[pl.BlockSpec((ncol // width, tl, width), lambda i: (0, i, 0)),
                   pl.BlockSpec((SUBLANES, ncol), lambda i: (0, 0))],
        out_shape=[jax.ShapeDtypeStruct((ncol // width, L, width), F32),
                   jax.ShapeDtypeStruct((SUBLANES, ncol), F32)],
        compiler_params=_cparams(("arbitrary",)),
        name="hyena_filter",
    )(feats, t, fw1, row(fb1), row(ff1), fw2, row(fb2), row(ff2), fw3, fb3, delta_cols, is_bwd)
    return h, sums[0]


def _shortconv_rows(z_ref, w_ref, b_ref, c, chunk):
    L = z_ref.shape[0]
    row = lax.broadcasted_iota(jnp.int32, (chunk, z_ref.shape[1]), 0)
    r0 = pl.multiple_of(c * chunk, chunk)
    cur = z_ref[pl.ds(r0, chunk), :]
    prev_row = z_ref[pl.ds(jnp.maximum(r0 - 1, 0), 1), :]
    next_row = z_ref[pl.ds(jnp.minimum(r0 + chunk, L - 1), 1), :]
    prev_row = jnp.where(c == 0, 0.0, prev_row)
    next_row = jnp.where(c == L // chunk - 1, 0.0, next_row)
    down = jnp.where(row == 0, prev_row, pltpu.roll(cur, 1, 0))
    up = jnp.where(row == chunk - 1, next_row, pltpu.roll(cur, chunk - 1, 0))
    return b_ref[...] + down * w_ref[0:1, :] + cur * w_ref[1:2, :] + up * w_ref[2:3, :]


def _dft_tables(n1_len, n2_len):
    n = n1_len * n2_len

    def root(num, den):
        ang = (2.0 * math.pi / den) * (num % den).astype(F32)
        return jnp.cos(ang), -jnp.sin(ang)

    i1 = jnp.arange(n1_len, dtype=jnp.int32)
    i2 = jnp.arange(n2_len, dtype=jnp.int32)
    f1r, f1i = root(i1[:, None] * i1[None, :], n1_len)
    f2r, f2i = root(i2[:, None] * i2[None, :], n2_len)
    twr, twi = root(i1[:, None] * i2[None, :], n)
    return (f1r, f1i), (f2r, f2i), (twr, twi)


def _stacked_inner_dft(f2r, f2i, twr_row, twi_row):
    gr = f2r * twr_row - f2i * twi_row
    gi = f2r * twi_row + f2i * twr_row
    top = jnp.concatenate([gr, -gi], axis=1)
    bot = jnp.concatenate([gi, gr], axis=1)
    return jnp.concatenate([top, bot], axis=0)


def _fft_plan(n):
    n1_len = n // FFT_N2
    nk1 = n1_len // 2 + 1
    nk1_pad = -(-nk1 // SUBLANES) * SUBLANES
    chunk = max(c for c in range(1, FFT_K1_CHUNK_MAX + 1) if nk1 % c == 0)
    return n1_len, nk1, nk1_pad, chunk


def _outer_dft_to_scratch(load_rows, fa_ref, ar_ref, ai_ref, nk1_pad):
    fa = fa_ref[...]

    def step(n2, carry):
        a = jnp.dot(fa, load_rows(n2).astype(BF16), preferred_element_type=F32)
        ar_ref[pl.ds(n2, nk1_pad, stride=FFT_PITCH), :] = a[:nk1_pad]
        ai_ref[pl.ds(n2, nk1_pad, stride=FFT_PITCH), :] = a[nk1_pad:]
        return carry

    lax.fori_loop(0, FFT_N2, step, 0, unroll=FFT_UNROLL_OUTER)


def _spec_body(f_ref, b_ref, inv_ref, fa_ref, f2r_ref, f2i_ref, twr_ref, twi_ref, kr_ref, ki_ref,
               fr_ref, fi_ref, br_ref, bi_ref, *, n1_len, nk1_pad, chunk):
    kc = pl.program_id(2)
    half = n1_len // 2

    @pl.when(kc == 0)
    def _():
        inv = inv_ref[...]
        row = lax.broadcasted_iota(jnp.int32, (half, LANES), 0)
        _outer_dft_to_scratch(lambda n2: f_ref[pl.ds(n2, half, stride=FFT_N2), :] * inv,
                              fa_ref, fr_ref, fi_ref, nk1_pad)
        _outer_dft_to_scratch(
            lambda n2: jnp.where((row == 0) & (n2 == 0), 0.0,
                                 b_ref[pl.ds(n2, half, stride=FFT_N2), :] * inv),
            fa_ref, br_ref, bi_ref, nk1_pad)

    f2r = f2r_ref[...]
    f2i = f2i_ref[...]

    def step(t, carry):
        k1 = kc * chunk + t
        base = pl.multiple_of(k1 * FFT_PITCH, SUBLANES)
        mf = _stacked_inner_dft(f2r, f2i, twr_ref[pl.ds(k1, 1), :],
                                twi_ref[pl.ds(k1, 1), :]).astype(BF16)
        rows = pl.ds(base, FFT_N2)
        xf = jnp.dot(mf, jnp.concatenate([fr_ref[rows, :], fi_ref[rows, :]], axis=0).astype(BF16),
                     preferred_element_type=F32)
        xb = jnp.dot(mf, jnp.concatenate([br_ref[rows, :], bi_ref[rows, :]], axis=0).astype(BF16),
                     preferred_element_type=F32)
        o = pl.multiple_of(t * FFT_N2, FFT_N2)
        kr_ref[pl.ds(o, FFT_N2), :] = xf[:FFT_N2] + xb[:FFT_N2]
        ki_ref[pl.ds(o, FFT_N2), :] = xf[FFT_N2:] - xb[FFT_N2:]
        return carry

    lax.fori_loop(0, chunk, step, 0, unroll=FFT_UNROLL_INNER)


def _filter_spectrum(h, inv_norm, tables):
    O2, L, C = h.shape
    O = O2 // 2
    n1_len, nk1, nk1_pad, chunk = _fft_plan(2 * L)
    half = n1_len // 2
    (f1r, f1i), (f2r, f2i), (twr, twi) = tables
    fa = jnp.concatenate([f1r[:nk1_pad, :half], f1i[:nk1_pad, :half]], axis=0).astype(BF16)
    rows = chunk * FFT_N2
    const = lambda shape: pl.BlockSpec(shape, lambda o, c, k: (0, 0))
    out = jax.ShapeDtypeStruct((O, nk1 * FFT_N2, C), F32)
    scratch = pltpu.VMEM((nk1_pad * FFT_PITCH, LANES), F32)
    return pl.pallas_call(
        functools.partial(_spec_body, n1_len=n1_len, nk1_pad=nk1_pad, chunk=chunk),
        grid=(O, C // LANES, nk1 // chunk),
        in_specs=[
            pl.BlockSpec((None, L, LANES), lambda o, c, k: (2 * o, 0, c)),
            pl.BlockSpec((None, L, LANES), lambda o, c, k: (2 * o + 1, 0, c)),
            pl.BlockSpec((None, 1, LANES), lambda o, c, k: (o, 0, c)),
            const(fa.shape), const(f2r.shape), const(f2i.shape), const(twr.shape), const(twi.shape),
        ],
        out_specs=[pl.BlockSpec((None, rows, LANES), lambda o, c, k: (o, k, c)),
                   pl.BlockSpec((None, rows, LANES), lambda o, c, k: (o, k, c))],
        out_shape=[out, out],
        scratch_shapes=[scratch, scratch, scratch, scratch],
        compiler_params=_cparams(("parallel", "parallel", "arbitrary")),
        name="filter_spectrum",
    )(h, h, inv_norm, fa, f2r, f2i, twr, twi)


def _fftconv_body(y_ref, gate_ref, wy_ref, by_ref, wg_ref, bg_ref, d_ref, kr_ref, ki_ref, fa_ref,
                  fs_ref, f2r_ref, f2i_ref, twr_ref, twi_ref, o_ref, xs_ref, ar_ref, ai_ref, *,
                  n1_len, nk1_pad, chunk, conv_y):
    kc = pl.program_id(2)
    half = n1_len // 2

    @pl.when(kc == 0)
    def _():
        def copy(n1, carry):
            src = pl.multiple_of(n1 * FFT_N2, FFT_N2)
            dst = pl.multiple_of(n1 * FFT_PITCH, SUBLANES)
            if conv_y:
                xs_ref[pl.ds(dst, FFT_N2), :] = _shortconv_rows(y_ref, wy_ref, by_ref, n1, FFT_N2)
            else:
                xs_ref[pl.ds(dst, FFT_N2), :] = y_ref[pl.ds(src, FFT_N2), :]
            return carry

        lax.fori_loop(0, half, copy, 0)
        _outer_dft_to_scratch(lambda n2: xs_ref[pl.ds(n2, half, stride=FFT_PITCH), :],
                              fa_ref, ar_ref, ai_ref, nk1_pad)

    f2r = f2r_ref[...]
    f2i = f2i_ref[...]

    def step(t, carry):
        k1 = kc * chunk + t
        base = pl.multiple_of(k1 * FFT_PITCH, SUBLANES)
        rhs = jnp.concatenate([ar_ref[pl.ds(base, FFT_N2), :], ai_ref[pl.ds(base, FFT_N2), :]],
                              axis=0).astype(BF16)
        mf = _stacked_inner_dft(f2r, f2i, twr_ref[pl.ds(k1, 1), :], twi_ref[pl.ds(k1, 1), :])
        x = jnp.dot(mf.astype(BF16), rhs, preferred_element_type=F32)
        xr, xi = x[:FFT_N2], x[FFT_N2:]
        o = pl.multiple_of(t * FFT_N2, FFT_N2)
        kr = kr_ref[pl.ds(o, FFT_N2), :]
        ki = ki_ref[pl.ds(o, FFT_N2), :]
        z = jnp.concatenate([xr * kr - xi * ki, xr * ki + xi * kr], axis=0).astype(BF16)
        b = jnp.dot(mf.T.astype(BF16), z, preferred_element_type=F32)
        ar_ref[pl.ds(base, FFT_N2), :] = b[:FFT_N2]
        ai_ref[pl.ds(base, FFT_N2), :] = b[FFT_N2:]
        return carry

    lax.fori_loop(0, chunk, step, 0, unroll=FFT_UNROLL_INNER)

    @pl.when(kc == pl.num_programs(2) - 1)
    def _():
        fs = fs_ref[...]

        def inv_outer(n2, carry):
            rhs = jnp.concatenate([ar_ref[pl.ds(n2, nk1_pad, stride=FFT_PITCH), :],
                                   ai_ref[pl.ds(n2, nk1_pad, stride=FFT_PITCH), :]],
                                  axis=0).astype(BF16)
            conv = jnp.dot(fs, rhs, preferred_element_type=F32)
            ar_ref[pl.ds(n2, half, stride=FFT_PITCH), :] = conv
            return carry

        lax.fori_loop(0, FFT_N2, inv_outer, 0, unroll=FFT_UNROLL_OUTER)
        d = d_ref[...]

        def finish(n1, carry):
            src = pl.multiple_of(n1 * FFT_PITCH, SUBLANES)
            dst = pl.multiple_of(n1 * FFT_N2, FFT_N2)
            y = xs_ref[pl.ds(src, FFT_N2), :]
            gate = _shortconv_rows(gate_ref, wg_ref, bg_ref, n1, FFT_N2)
            o_ref[pl.ds(dst, FFT_N2), :] = (gate * (
                ar_ref[pl.ds(src, FFT_N2), :] + y * d)).astype(o_ref.dtype)
            return carry

        lax.fori_loop(0, half, finish, 0)


def _fftconv_gate(y, y_off, conv_y, z, gate_off, conv_w, conv_b, d, kr, ki, tables, out_dtype):
    B, L, _ = y.shape
    C = d.shape[-1]
    yo, go = y_off // LANES, gate_off // LANES
    wo = yo if conv_y else go
    N = 2 * L
    n1_len, nk1, nk1_pad, chunk = _fft_plan(N)
    half = n1_len // 2
    (f1r, f1i), (f2r, f2i), (twr, twi) = tables
    fa = jnp.concatenate([f1r[:nk1_pad, :half], f1i[:nk1_pad, :half]], axis=0).astype(BF16)
    wts = jnp.concatenate([jnp.ones((1,), F32), jnp.full((nk1 - 2,), 2.0, F32), jnp.ones((1,), F32),
                           jnp.zeros((nk1_pad - nk1,), F32)]) * (1.0 / N)
    fs = jnp.concatenate([f1r[:half, :nk1_pad] * wts, f1i[:half, :nk1_pad] * wts],
                         axis=1).astype(BF16)
    rows = chunk * FFT_N2
    const = lambda shape: pl.BlockSpec(shape, lambda c, b, k: (0, 0))
    return pl.pallas_call(
        functools.partial(_fftconv_body, n1_len=n1_len, nk1_pad=nk1_pad, chunk=chunk,
                          conv_y=conv_y),
        grid=(C // LANES, B, nk1 // chunk),
        in_specs=[
            pl.BlockSpec((None, L, LANES), lambda c, b, k: (b, 0, c + yo)),
            pl.BlockSpec((None, L, LANES), lambda c, b, k: (b, 0, c + go)),
            pl.BlockSpec((SHORT_CONV, LANES), lambda c, b, k: (0, c + wo)),
            pl.BlockSpec((1, LANES), lambda c, b, k: (0, c + wo)),
            pl.BlockSpec((SHORT_CONV, LANES), lambda c, b, k: (0, c + go)),
            pl.BlockSpec((1, LANES), lambda c, b, k: (0, c + go)),
            pl.BlockSpec((1, LANES), lambda c, b, k: (0, c)),
            pl.BlockSpec((rows, LANES), lambda c, b, k: (k, c)),
            pl.BlockSpec((rows, LANES), lambda c, b, k: (k, c)),
            const(fa.shape), const(fs.shape), const(f2r.shape), const(f2i.shape),
            const(twr.shape), const(twi.shape),
        ],
        out_specs=pl.BlockSpec((None, L, LANES), lambda c, b, k: (b, 0, c)),
        out_shape=jax.ShapeDtypeStruct((B, L, C), out_dtype),
        scratch_shapes=[pltpu.VMEM((half * FFT_PITCH, LANES), F32),
                        pltpu.VMEM((nk1_pad * FFT_PITCH, LANES), F32),
                        pltpu.VMEM((nk1_pad * FFT_PITCH, LANES), F32)],
        compiler_params=_cparams(("parallel", "parallel", "arbitrary")),
        name="fftconv_gate",
    )(y, z, conv_w, conv_b, conv_w, conv_b, d, kr, ki, fa, fs, f2r, f2i, twr, twi)


def _hyena(z, conv_w, conv_b, fw1, fb1, ff1, fw2, fb2, ff2, fw3, fb3, hyena_d):
    B, L, C3 = z.shape
    W = C3 // 3
    h, sums = _hyena_filter(L, W, fw1, fb1, ff1, fw2, fb2, ff2, fw3, fb3)
    sums = sums.reshape(HYENA_ORDER, 2, W)
    inv_norm = (1.0 / (sums[:, 0] + sums[:, 1]))[:, None, :]
    tables = _dft_tables(2 * L // FFT_N2, FFT_N2)
    kr, ki = _filter_spectrum(h, inv_norm, tables)
    conv_b = conv_b.reshape(1, C3)
    y = z
    for o in range(HYENA_ORDER):
        y = _fftconv_gate(y, 0, o == 0, z, (o + 1) * W, conv_w, conv_b,
                          hyena_d[o].reshape(1, W).astype(F32), kr[o], ki[o], tables,
                          BF16 if o == HYENA_ORDER - 1 else F32)
    return y


def _merge_body(yh_ref, ya_ref, g_ref, x_ref, whu_ref, wau_ref, wo_ref, n2_ref, rwh_ref, rwl_ref,
                rb_ref, h_ref, hn_ref, lg_ref):
    D = x_ref.shape[-1]
    up_h = jnp.dot(yh_ref[...], whu_ref[...], preferred_element_type=F32)
    up_a = jnp.dot(ya_ref[...], wau_ref[...], preferred_element_type=F32)
    g = g_ref[...].astype(F32)
    merged = jax.nn.sigmoid(g[:, :D]) * up_h + jax.nn.sigmoid(g[:, D:]) * up_a
    h = x_ref[...] + jnp.dot(merged.astype(BF16), wo_ref[...], preferred_element_type=F32)
    h_ref[...] = h
    ms = jnp.mean(h * h, axis=-1, keepdims=True)
    hn = h * lax.rsqrt(ms + EPS) * n2_ref[...]
    hn_ref[...] = hn.astype(BF16)
    hh, hl = _split_bf16(hn)
    lg_ref[...] = (jnp.dot(hh, rwh_ref[...], preferred_element_type=F32)
                   + jnp.dot(hl, rwh_ref[...], preferred_element_type=F32)
                   + jnp.dot(hh, rwl_ref[...], preferred_element_type=F32)) + rb_ref[...]


def _merge(yh, ya, gates, xt, whu, wau, wo, n2g, rw, rb, tm=256):
    T, D = xt.shape
    W = yh.shape[1]
    E = rw.shape[1]
    rwp = jnp.zeros((D, LOGIT_PAD), F32).at[:, :E].set(rw)
    rwh, rwl = _split_bf16(rwp)
    rbp = jnp.zeros((1, LOGIT_PAD), F32).at[0, :E].set(rb)
    rowblk = lambda w: pl.BlockSpec((tm, w), lambda i: (i, 0))
    const = lambda shape: pl.BlockSpec(shape, lambda i: (0, 0))
    return pl.pallas_call(
        _merge_body,
        grid=(T // tm,),
        in_specs=[rowblk(W), rowblk(W), rowblk(2 * D), rowblk(D),
                  const((W, D)), const((W, D)), const((D, D)), const((1, D)),
                  const((D, LOGIT_PAD)), const((D, LOGIT_PAD)), const((1, LOGIT_PAD))],
        out_specs=[rowblk(D), rowblk(D), rowblk(LOGIT_PAD)],
        out_shape=[jax.ShapeDtypeStruct((T, D), F32), jax.ShapeDtypeStruct((T, D), BF16),
                   jax.ShapeDtypeStruct((T, LOGIT_PAD), F32)],
        compiler_params=_cparams(("parallel",)),
        name="merge",
    )(yh, ya, gates, xt, whu.astype(BF16), wau.astype(BF16), wo.astype(BF16), n2g.reshape(1, D),
      rwh, rwl, rbp)


def _moe_body(be_ref, nused_ref, x_ref, w1g_ref, w1l_ref, b1g_ref, b1l_ref, w2_ref, b2_ref, o_ref):
    i = pl.program_id(0)

    @pl.when(i < nused_ref[0])
    def _():
        x = x_ref[...]
        nt = (((1,), (1,)), ((), ()))
        glu = lax.dot_general(x, w1g_ref[...], nt, preferred_element_type=F32) + b1g_ref[...]
        lin = lax.dot_general(x, w1l_ref[...], nt, preferred_element_type=F32) + b1l_ref[...]
        glu = jnp.minimum(glu, SWIGLU_LIMIT)
        lin = jnp.clip(lin, -SWIGLU_LIMIT, SWIGLU_LIMIT)
        act = glu * jax.nn.sigmoid(SWIGLU_ALPHA * glu) * (lin + 1.0)
        y = jnp.dot(act.astype(BF16), w2_ref[...].astype(BF16),
                    preferred_element_type=F32) + b2_ref[...]
        o_ref[...] = y.astype(o_ref.dtype)

    @pl.when(i >= nused_ref[0])
    def _():
        o_ref[...] = jnp.zeros(o_ref.shape, o_ref.dtype)


def _moe_experts(xg, block_expert, n_used, w1g, w1l, b1g, b1l, w2, b2):
    P, D = xg.shape
    dff = w2.shape[1]
    nb = P // MOE_TM
    wspec = lambda k, n: pl.BlockSpec((None, k, n), lambda i, be, nu: (be[i], 0, 0))
    grid_spec = pltpu.PrefetchScalarGridSpec(
        num_scalar_prefetch=2,
        grid=(nb,),
        in_specs=[
            pl.BlockSpec((MOE_TM, D), lambda i, be, nu: (i, 0)),
            wspec(dff, D), wspec(dff, D), wspec(1, dff), wspec(1, dff),
            wspec(dff, D), wspec(1, D),
        ],
        out_specs=pl.BlockSpec((MOE_TM, D), lambda i, be, nu: (i, 0)),
    )
    return pl.pallas_call(
        _moe_body,
        grid_spec=grid_spec,
        out_shape=jax.ShapeDtypeStruct((P, D), BF16),
        compiler_params=_cparams(("arbitrary",)),
        name="moe_experts",
    )(block_expert, n_used, xg, w1g, w1l, b1g, b1l, w2, b2)


def _prep_w1_body(w_ref, g_ref, l_ref, t_ref):
    half = g_ref.shape[0]
    wt = w_ref[...].T
    for c in range(t_ref.shape[0]):
        cols = slice(c * LANES, (c + 1) * LANES)
        t_ref[c] = wt[:, cols]
        g_ref[:, cols] = t_ref[c, pl.ds(0, half, stride=2), :].astype(BF16)
        l_ref[:, cols] = t_ref[c, pl.ds(1, half, stride=2), :].astype(BF16)


def _prep_w1(w1, tc=8 * LANES):
    E, D, F2 = w1.shape
    tc = min(tc, F2)
    out = jax.ShapeDtypeStruct((E, F2 // 2, D), BF16)
    return pl.pallas_call(
        _prep_w1_body,
        grid=(E, F2 // tc),
        in_specs=[pl.BlockSpec((None, D, tc), lambda e, j: (e, 0, j))],
        out_specs=[pl.BlockSpec((None, tc // 2, D), lambda e, j: (e, j, 0)),
                   pl.BlockSpec((None, tc // 2, D), lambda e, j: (e, j, 0))],
        out_shape=[out, out],
        scratch_shapes=[pltpu.VMEM((D // LANES, tc, LANES), F32)],
        compiler_params=_cparams(("parallel", "parallel")),
        name="prep_w1",
    )(w1)


def _combine_body(h_ref, y_ref, g_ref, o_ref):
    acc = h_ref[...]
    g = g_ref[...]
    for k in range(y_ref.shape[0]):
        acc = acc + g[:, k:k + 1] * y_ref[k].astype(F32)
    o_ref[...] = acc


def _combine(h, yk, gates, tm=512):
    T, D = h.shape
    K = yk.shape[0]
    return pl.pallas_call(
        _combine_body,
        grid=(T // tm,),
        in_specs=[pl.BlockSpec((tm, D), lambda i: (i, 0)),
                  pl.BlockSpec((K, tm, D), lambda i: (0, i, 0)),
                  pl.BlockSpec((tm, K), lambda i: (i, 0))],
        out_specs=pl.BlockSpec((tm, D), lambda i: (i, 0)),
        out_shape=jax.ShapeDtypeStruct((T, D), F32),
        compiler_params=_cparams(("parallel",)),
        name="moe_combine",
    )(h, yk, gates)


def _lookup(table, idx):
    n = table.shape[0]
    hit = idx[None, :] == jnp.arange(n, dtype=idx.dtype)[:, None]
    return jnp.sum(jnp.where(hit, table[:, None], 0), axis=0)


def _moe(h, hn_bf16, logits, w1, b1, w2, b2):
    T, D = hn_bf16.shape
    E = w1.shape[0]
    top_val, top_idx = lax.top_k(logits, TOP_K)
    gates = jax.nn.softmax(top_val, axis=-1)
    TK = T * TOP_K
    e_flat = top_idx.reshape(TK).astype(jnp.int32)
    order = jnp.argsort(e_flat).astype(jnp.int32)
    rank = jnp.argsort(order).astype(jnp.int32)
    counts = jnp.sum(jnp.arange(E, dtype=jnp.int32)[:, None] == e_flat[None, :], axis=1,
                     dtype=jnp.int32)
    starts = jnp.cumsum(counts) - counts
    padded = ((counts + MOE_TM - 1) // MOE_TM) * MOE_TM
    pad_ends = jnp.cumsum(padded)
    pad_starts = pad_ends - padded
    nb = (TK + E * (MOE_TM - 1) + MOE_TM - 1) // MOE_TM
    block_start = jnp.arange(nb, dtype=jnp.int32) * MOE_TM
    block_expert = jnp.minimum(jnp.sum(block_start[:, None] >= pad_ends[None, :], axis=1),
                               E - 1).astype(jnp.int32)
    n_used = (pad_ends[-1] // MOE_TM).astype(jnp.int32).reshape(1)
    blk_first = block_start - pad_starts[block_expert]
    within = blk_first[:, None] + jnp.arange(MOE_TM, dtype=jnp.int32)[None, :]
    valid = within < counts[block_expert][:, None]
    sorted_idx = jnp.where(valid, starts[block_expert][:, None] + within, 0).reshape(nb * MOE_TM)
    filler = jnp.arange(nb * MOE_TM, dtype=jnp.int32) % T
    src = jnp.where(valid.reshape(nb * MOE_TM), order[sorted_idx] // TOP_K, filler)
    pos = _lookup(pad_starts - starts, e_flat) + rank
    w1g, w1l = _prep_w1(w1)
    y = _moe_experts(hn_bf16[src], block_expert, n_used, w1g, w1l,
                     b1[:, None, 0::2].astype(F32), b1[:, None, 1::2].astype(F32),
                     w2, b2[:, None, :].astype(F32))
    return _combine(h, y[pos.reshape(T, TOP_K).T], gates)


def _rope_tables(L):
    rows = L // GRID_W
    row = jnp.repeat(jnp.arange(rows, dtype=F32), GRID_W)
    col = jnp.tile(jnp.arange(GRID_W, dtype=F32), rows)
    half = HEAD_DIM // 2
    freqs = ROPE_THETA ** (-jnp.arange(0, half, 2, dtype=F32) / half)
    ang = jnp.concatenate([row[:, None] * freqs, col[:, None] * freqs], axis=-1)
    cos = jnp.repeat(jnp.cos(ang), 2, axis=-1)
    sin = jnp.repeat(jnp.sin(ang), 2, axis=-1)
    sign = jnp.tile(jnp.array([-1.0, 1.0], F32), HEAD_DIM // 2)
    reps = LANES // HEAD_DIM
    return jnp.tile(cos, (1, reps)), jnp.tile(sin * sign, (1, reps))


def kernel(x, norm1_g, w_in, conv_w, conv_b, filt_w1, filt_b1, filt_freq1, filt_w2, filt_b2, filt_freq2, filt_w3, filt_b3, hyena_d, q_norm_g, k_norm_g, w_hyena_up, w_attn_up, w_out, norm2_g, router_w, router_b, expert_w1, expert_b1, expert_w2, expert_b2):
    B, L, D = x.shape
    T = B * L
    depth = w_in.shape[0]
    hw = conv_w.shape[-1] // 3
    aw = N_Q_HEADS * HEAD_DIM
    kvw = N_KV_HEADS * HEAD_DIM
    widths = (3 * hw, aw, 2 * kvw, 2 * D)
    cosf, sinf = _rope_tables(L)
    for l in range(depth):
        xt = x.reshape(T, D)
        z, q, kv, gates = _inproj(xt, norm1_g[l], w_in[l].astype(BF16), widths,
                                  (F32, BF16, BF16, BF16))
        y_hy = _hyena(z.reshape(B, L, 3 * hw), conv_w[l], conv_b[l], filt_w1[l], filt_b1[l],
                      filt_freq1[l], filt_w2[l], filt_b2[l], filt_freq2[l], filt_w3[l], filt_b3[l],
                      hyena_d[l])
        gq = jnp.tile(q_norm_g[l].astype(F32), N_Q_HEADS)[None, :]
        gk = jnp.tile(k_norm_g[l].astype(F32), N_KV_HEADS)[None, :]
        qr, kr, va = _qkrope(q, kv, gq, gk, cosf, sinf, L)
        y_at = _attention(qr.reshape(B, L, aw), kr, va.reshape(N_KV_HEADS, B, L, kvw))
        h, hn, logits = _merge(y_hy.reshape(T, hw), y_at.reshape(T, aw), gates, xt,
                               w_hyena_up[l], w_attn_up[l], w_out[l], norm2_g[l],
                               router_w[l], router_b[l])
        x = _moe(h, hn, logits[:, :N_EXPERTS], expert_w1[l], expert_b1[l], expert_w2[l],
                 expert_b2[l]).reshape(B, L, D)
    return x
```

```python
import functools
import math

import jax
import jax.numpy as jnp
from jax import lax
from jax.experimental import pallas as pl
from jax.experimental.pallas import tpu as pltpu

F32 = jnp.float32
BF16 = jnp.bfloat16

GRID_W = 64
HEAD_DIM = 64
N_Q_HEADS = 8
N_KV_HEADS = 2
Q_PER_KV = N_Q_HEADS // N_KV_HEADS
ROPE_THETA = 10000.0
HYENA_ORDER = 2
SHORT_CONV = 3
FILTER_EMB = 33
FAST_DECAY_PCT = 0.3
SLOW_DECAY_PCT = 1.5
DECAY_TARGET = 1e-2
N_EXPERTS = 32
TOP_K = 4
SWIGLU_LIMIT = 7.0
SWIGLU_ALPHA = 1.702
EPS = 1e-6

LANES = 128
SUBLANES = 8
VMEM_LIMIT = 56 * 1024 * 1024

FFT_N2 = LANES
FFT_PITCH = FFT_N2 + SUBLANES
FFT_K1_CHUNK_MAX = 16
FFT_UNROLL_OUTER = 16
FFT_UNROLL_INNER = True
MOE_TM = 512
COMBINE_CHUNKS = 4
LOGIT_PAD = LANES


def _cparams(sem):
    return pltpu.CompilerParams(dimension_semantics=sem, vmem_limit_bytes=VMEM_LIMIT)


def _split_bf16(x):
    hi = x.astype(BF16)
    lo = (x - hi.astype(F32)).astype(BF16)
    return hi, lo


def _inproj_body(x_ref, g_ref, w_ref, z_ref, q_ref, kv_ref, gate_ref, *, widths):
    x = x_ref[...]
    ms = jnp.mean(x * x, axis=-1, keepdims=True)
    u = (x * lax.rsqrt(ms + EPS) * g_ref[...]).astype(BF16)
    off = 0
    for ref, w in zip((z_ref, q_ref, kv_ref, gate_ref), widths):
        ref[...] = jnp.dot(u, w_ref[:, off:off + w], preferred_element_type=F32).astype(ref.dtype)
        off += w


def _inproj(xt, g, w_bf16, widths, dtypes, tm=1024):
    T, D = xt.shape
    n = w_bf16.shape[1]
    return pl.pallas_call(
        functools.partial(_inproj_body, widths=widths),
        grid=(T // tm,),
        in_specs=[
            pl.BlockSpec((tm, D), lambda i: (i, 0)),
            pl.BlockSpec((1, D), lambda i: (0, 0)),
            pl.BlockSpec((D, n), lambda i: (0, 0)),
        ],
        out_specs=[pl.BlockSpec((tm, w), lambda i: (i, 0)) for w in widths],
        out_shape=[jax.ShapeDtypeStruct((T, w), dt) for w, dt in zip(widths, dtypes)],
        compiler_params=_cparams(("parallel",)),
        name="inproj",
    )(xt, g.reshape(1, D), w_bf16)


def _head_norm_rope(x, gain, cosf, sinf, ones_blk):
    w = x.shape[-1]
    hi, lo = _split_bf16(x * x)
    ss = (jnp.dot(hi, ones_blk, preferred_element_type=F32)
          + jnp.dot(lo, ones_blk, preferred_element_type=F32))
    xn = x * lax.rsqrt(ss * (1.0 / HEAD_DIM) + EPS) * gain
    lane = lax.broadcasted_iota(jnp.int32, (x.shape[0], LANES), 1)
    cols = []
    for c in range(w // LANES):
        col = xn[:, c * LANES:(c + 1) * LANES]
        cols.append(jnp.where(lane % 2 == 0, pltpu.roll(col, LANES - 1, 1), pltpu.roll(col, 1, 1)))
    swapped = cols[0] if len(cols) == 1 else jnp.concatenate(cols, axis=1)
    return xn * cosf + swapped * sinf


def _qkrope_body(q_ref, kv_ref, gq_ref, gk_ref, cos_ref, sin_ref, oq_ref, ok_ref,
                 qo_ref, ko_ref, vo_ref):
    cosf = cos_ref[...]
    sinf = sin_ref[...]
    nq = q_ref.shape[-1] // LANES
    q = _head_norm_rope(q_ref[...].astype(F32), gq_ref[...], jnp.tile(cosf, (1, nq)),
                        jnp.tile(sinf, (1, nq)), oq_ref[...])
    qo_ref[...] = (q * (HEAD_DIM ** -0.5 * math.log2(math.e))).astype(BF16)
    kv = kv_ref[...].astype(F32)
    kw = kv.shape[-1] // 2
    k = _head_norm_rope(kv[:, :kw], gk_ref[...], cosf, sinf, ok_ref[...])
    ko_ref[...] = k.T.astype(BF16)
    v = kv[:, kw:]
    lane = lax.broadcasted_iota(jnp.int32, v.shape, 1)
    for h in range(N_KV_HEADS):
        vo_ref[h] = jnp.where((lane // HEAD_DIM) == h, v, 1.0).astype(BF16)


def _qkrope(q, kv, gq, gk, cosf, sinf, seq_len, tm=512):
    T, qw = q.shape
    kw = kv.shape[1] // 2
    assert kw == LANES and N_KV_HEADS * HEAD_DIM == LANES and N_KV_HEADS == 2
    nl = seq_len // tm

    def blk_ones(w):
        r = jnp.arange(w) // HEAD_DIM
        return (r[:, None] == r[None, :]).astype(BF16)

    return pl.pallas_call(
        _qkrope_body,
        grid=(T // tm,),
        in_specs=[
            pl.BlockSpec((tm, qw), lambda i: (i, 0)),
            pl.BlockSpec((tm, 2 * kw), lambda i: (i, 0)),
            pl.BlockSpec((1, qw), lambda i: (0, 0)),
            pl.BlockSpec((1, kw), lambda i: (0, 0)),
            pl.BlockSpec((tm, LANES), lambda i: (i % nl, 0)),
            pl.BlockSpec((tm, LANES), lambda i: (i % nl, 0)),
            pl.BlockSpec((qw, qw), lambda i: (0, 0)),
            pl.BlockSpec((kw, kw), lambda i: (0, 0)),
        ],
        out_specs=[
            pl.BlockSpec((tm, qw), lambda i: (i, 0)),
            pl.BlockSpec((kw, tm), lambda i: (0, i)),
            pl.BlockSpec((N_KV_HEADS, tm, kw), lambda i: (0, i, 0)),
        ],
        out_shape=[
            jax.ShapeDtypeStruct((T, qw), BF16),
            jax.ShapeDtypeStruct((kw, T), BF16),
            jax.ShapeDtypeStruct((N_KV_HEADS, T, kw), BF16),
        ],
        compiler_params=_cparams(("parallel",)),
        name="qkrope",
    )(q, kv, gq, gk, cosf, sinf, blk_ones(qw), blk_ones(kw))


def _attn_body(q_ref, k_ref, v_ref, o_ref, qs_ref, m_ref, acc_ref, *, tk, nsplit):
    kvh = pl.program_id(1)
    tq = q_ref.shape[0]
    seq = k_ref.shape[1]
    rows = Q_PER_KV * tq
    lane = lax.broadcasted_iota(jnp.int32, (tq, LANES), 1)
    in_head = (lane // HEAD_DIM) == kvh

    for g in range(Q_PER_KV):
        col = q_ref[:, (g // 2) * LANES:(g // 2 + 1) * LANES].astype(F32)
        col = jnp.where((g % 2) == kvh, col, pltpu.roll(col, HEAD_DIM, 1))
        qs_ref[g * tq:(g + 1) * tq, :] = jnp.where(in_head, col, 0.0).astype(BF16)
    m_ref[...] = jnp.full(m_ref.shape, -jnp.inf, F32)
    acc_ref[...] = jnp.zeros(acc_ref.shape, F32)
    part = rows // nsplit

    def step(c, carry):
        r0 = pl.multiple_of(c * tk, tk)
        kc = k_ref[:, pl.ds(r0, tk)]
        vc = v_ref[pl.ds(r0, tk), :]
        for h in range(nsplit):
            sl = slice(h * part, (h + 1) * part)
            s = jnp.dot(qs_ref[sl, :], kc, preferred_element_type=F32)
            m_prev = m_ref[sl, :]
            m_new = jnp.maximum(m_prev, jnp.max(s, axis=-1, keepdims=True))
            p = jnp.exp2(s - jnp.tile(m_new, (1, tk // LANES)))
            acc_ref[sl, :] = jnp.exp2(m_prev - m_new) * acc_ref[sl, :] + jnp.dot(
                p.astype(BF16), vc, preferred_element_type=F32)
            m_ref[sl, :] = m_new
        return carry

    lax.fori_loop(0, seq // tk, step, 0)

    acc = acc_ref[...]
    o = acc / pltpu.roll(acc, HEAD_DIM, 1)
    for c in range(Q_PER_KV // 2):
        even = o[(2 * c) * tq:(2 * c + 1) * tq, :]
        odd = o[(2 * c + 1) * tq:(2 * c + 2) * tq, :]
        even = jnp.where(kvh == 0, even, pltpu.roll(even, HEAD_DIM, 1))
        odd = jnp.where(kvh == 1, odd, pltpu.roll(odd, HEAD_DIM, 1))
        o_ref[:, c * LANES:(c + 1) * LANES] = jnp.where(lane < HEAD_DIM, even, odd).astype(o_ref.dtype)


def _attention(q, kt, v_aug, tq=512, tk=2048, nsplit=2):
    B, L, qw = q.shape
    gw = Q_PER_KV * HEAD_DIM
    kw = kt.shape[0]
    rows = Q_PER_KV * tq
    return pl.pallas_call(
        functools.partial(_attn_body, tk=tk, nsplit=nsplit),
        grid=(B, N_KV_HEADS, L // tq),
        in_specs=[
            pl.BlockSpec((None, tq, gw), lambda b, h, i: (b, i, h)),
            pl.BlockSpec((kw, L), lambda b, h, i: (0, b)),
            pl.BlockSpec((None, None, L, kw), lambda b, h, i: (h, b, 0, 0)),
        ],
        out_specs=pl.BlockSpec((None, tq, gw), lambda b, h, i: (b, i, h)),
        out_shape=jax.ShapeDtypeStruct((B, L, qw), BF16),
        scratch_shapes=[
            pltpu.VMEM((rows, kw), BF16),
            pltpu.VMEM((rows, LANES), F32),
            pltpu.VMEM((rows, kw), F32),
        ],
        compiler_params=_cparams(("parallel", "parallel", "parallel")),
        name="attention",
    )(q, kt, v_aug)


def _hdot(a, b):
    ah, al = _split_bf16(a)
    bh, bl = _split_bf16(b)
    return (jnp.dot(ah, bh, preferred_element_type=F32)
            + jnp.dot(al, bh, preferred_element_type=F32)
            + jnp.dot(ah, bl, preferred_element_type=F32))


def _filter_body(feat_ref, t_ref, w1_ref, b1_ref, f1_ref, w2_ref, b2_ref, f2_ref, w3_ref, b3_ref,
                 delta_ref, bwd_ref, h_ref, sum_ref):
    i = pl.program_id(0)
    h = jnp.sin(f1_ref[...] * (_hdot(feat_ref[...], w1_ref[...]) + b1_ref[...]))
    h = jnp.sin(f2_ref[...] * (_hdot(h, w2_ref[...]) + b2_ref[...]))
    h = _hdot(h, w3_ref[...]) + b3_ref[...]
    t = t_ref[...]
    h = h * jnp.exp(-t * delta_ref[...])
    width = h_ref.shape[-1]
    for j in range(h_ref.shape[0]):
        h_ref[j] = h[:, j * width:(j + 1) * width]
    row = lax.broadcasted_iota(jnp.int32, h.shape, 0) + i * h.shape[0]
    a = jnp.where((row == 0) & (bwd_ref[...] > 0.5), 0.0, jnp.abs(h))
    part = jnp.sum(a, axis=0, keepdims=True)

    @pl.when(i == 0)
    def _():
        sum_ref[...] = jnp.zeros(sum_ref.shape, F32)

    sum_ref[...] += jnp.broadcast_to(part, sum_ref.shape)


def _hyena_filter(L, width, fw1, fb1, ff1, fw2, fb2, ff2, fw3, fb3, tl=512):
    bands = (FILTER_EMB - 1) // 2
    t = jnp.linspace(0.0, 1.0, L, dtype=F32)[:, None]
    w = 2.0 * math.pi * jnp.arange(L, dtype=F32)[:, None] / L
    fr = jnp.linspace(1e-4, bands - 1, bands, dtype=F32)[None, :]
    feats = jnp.concatenate([t, jnp.cos(w * fr), -jnp.sin(w * fr)], axis=-1)
    max_decay = math.log(DECAY_TARGET) / FAST_DECAY_PCT
    min_decay = math.log(DECAY_TARGET) / SLOW_DECAY_PCT
    deltas = jnp.abs(jnp.linspace(min_decay, max_decay, width, dtype=F32))
    ncol = fw3.shape[1]
    delta_cols = jnp.tile(deltas, ncol // width)[None, :]
    is_bwd = ((jnp.arange(ncol) // width) % 2).astype(F32)[None, :]
    emb = hid = LANES

    def pad2(a, r, c):
        a = a.astype(F32)
        return jnp.zeros((r, c), F32).at[:a.shape[0], :a.shape[1]].set(a)

    row = lambda a: pad2(a.reshape(1, -1), 1, hid)
    feats = pad2(feats, L, emb)
    fw1, fw2, fw3 = pad2(fw1, emb, hid), pad2(fw2, hid, hid), pad2(fw3, hid, ncol)
    fb3 = fb3.reshape(1, ncol).astype(F32)
    const = lambda shape: pl.BlockSpec(shape, lambda i: (0, 0))
    h, sums = pl.pallas_call(
        _filter_body,
        grid=(L // tl,),
        in_specs=[
            pl.BlockSpec((tl, emb), lambda i: (i, 0)),
            pl.BlockSpec((tl, 1), lambda i: (i, 0)),
            const((emb, hid)), const((1, hid)), const((1, hid)),
            const((hid, hid)), const((1, hid)), const((1, hid)),
            const((hid, ncol)), const((1, ncol)), const((1, ncol)), const((1, ncol)),
        ],
        out_specs=[pl.BlockSpec((ncol // width, tl, width), lambda i: (0, i, 0)),
                   pl.BlockSpec((SUBLANES, ncol), lambda i: (0, 0))],
        out_shape=[jax.ShapeDtypeStruct((ncol // width, L, width), F32),
                   jax.ShapeDtypeStruct((SUBLANES, ncol), F32)],
        compiler_params=_cparams(("arbitrary",)),
        name="hyena_filter",
    )(feats, t, fw1, row(fb1), row(ff1), fw2, row(fb2), row(ff2), fw3, fb3, delta_cols, is_bwd)
    return h, sums[0]


def _shortconv_rows(z_ref, w_ref, b_ref, c, chunk):
    L = z_ref.shape[0]
    row = lax.broadcasted_iota(jnp.int32, (chunk, z_ref.shape[1]), 0)
    r0 = pl.multiple_of(c * chunk, chunk)
    cur = z_ref[pl.ds(r0, chunk), :]
    prev_row = z_ref[pl.ds(jnp.maximum(r0 - 1, 0), 1), :]
    next_row = z_ref[pl.ds(jnp.minimum(r0 + chunk, L - 1), 1), :]
    prev_row = jnp.where(c == 0, 0.0, prev_row)
    next_row = jnp.where(c == L // chunk - 1, 0.0, next_row)
    down = jnp.where(row == 0, prev_row, pltpu.roll(cur, 1, 0))
    up = jnp.where(row == chunk - 1, next_row, pltpu.roll(cur, chunk - 1, 0))
    return b_ref[...] + down * w_ref[0:1, :] + cur * w_ref[1:2, :] + up * w_ref[2:3, :]


def _dft_tables(n1_len, n2_len):
    n = n1_len * n2_len

    def root(num, den):
        ang = (2.0 * math.pi / den) * (num % den).astype(F32)
        return jnp.cos(ang), -jnp.sin(ang)

    i1 = jnp.arange(n1_len, dtype=jnp.int32)
    i2 = jnp.arange(n2_len, dtype=jnp.int32)
    f1r, f1i = root(i1[:, None] * i1[None, :], n1_len)
    f2r, f2i = root(i2[:, None] * i2[None, :], n2_len)
    twr, twi = root(i1[:, None] * i2[None, :], n)
    return (f1r, f1i), (f2r, f2i), (twr, twi)


def _stacked_inner_dft(f2r, f2i, twr_row, twi_row):
    gr = f2r * twr_row - f2i * twi_row
    gi = f2r * twi_row + f2i * twr_row
    top = jnp.concatenate([gr, -gi], axis=1)
    bot = jnp.concatenate([gi, gr], axis=1)
    return jnp.concatenate([top, bot], axis=0)


def _fft_plan(n):
    n1_len = n // FFT_N2
    nk1 = n1_len // 2 + 1
    nk1_pad = -(-nk1 // SUBLANES) * SUBLANES
    chunk = max(c for c in range(1, FFT_K1_CHUNK_MAX + 1) if nk1 % c == 0)
    return n1_len, nk1, nk1_pad, chunk


def _outer_dft_to_scratch(load_rows, fa_ref, ar_ref, ai_ref, nk1_pad):
    fa = fa_ref[...]

    def step(n2, carry):
        a = jnp.dot(fa, load_rows(n2).astype(BF16), preferred_element_type=F32)
        ar_ref[pl.ds(n2, nk1_pad, stride=FFT_PITCH), :] = a[:nk1_pad]
        ai_ref[pl.ds(n2, nk1_pad, stride=FFT_PITCH), :] = a[nk1_pad:]
        return carry

    lax.fori_loop(0, FFT_N2, step, 0, unroll=FFT_UNROLL_OUTER)


def _spec_body(f_ref, b_ref, inv_ref, fa_ref, f2r_ref, f2i_ref, twr_ref, twi_ref, kr_ref, ki_ref,
               fr_ref, fi_ref, br_ref, bi_ref, *, n1_len, nk1_pad, chunk):
    kc = pl.program_id(2)
    half = n1_len // 2

    @pl.when(kc == 0)
    def _():
        inv = inv_ref[...]
        row = lax.broadcasted_iota(jnp.int32, (half, LANES), 0)
        _outer_dft_to_scratch(lambda n2: f_ref[pl.ds(n2, half, stride=FFT_N2), :] * inv,
                              fa_ref, fr_ref, fi_ref, nk1_pad)
        _outer_dft_to_scratch(
            lambda n2: jnp.where((row == 0) & (n2 == 0), 0.0,
                                 b_ref[pl.ds(n2, half, stride=FFT_N2), :] * inv),
            fa_ref, br_ref, bi_ref, nk1_pad)

    f2r = f2r_ref[...]
    f2i = f2i_ref[...]

    def step(t, carry):
        k1 = kc * chunk + t
        base = pl.multiple_of(k1 * FFT_PITCH, SUBLANES)
        mf = _stacked_inner_dft(f2r, f2i, twr_ref[pl.ds(k1, 1), :],
                                twi_ref[pl.ds(k1, 1), :]).astype(BF16)
        rows = pl.ds(base, FFT_N2)
        xf = jnp.dot(mf, jnp.concatenate([fr_ref[rows, :], fi_ref[rows, :]], axis=0).astype(BF16),
                     preferred_element_type=F32)
        xb = jnp.dot(mf, jnp.concatenate([br_ref[rows, :], bi_ref[rows, :]], axis=0).astype(BF16),
                     preferred_element_type=F32)
        o = pl.multiple_of(t * FFT_N2, FFT_N2)
        kr_ref[pl.ds(o, FFT_N2), :] = xf[:FFT_N2] + xb[:FFT_N2]
        ki_ref[pl.ds(o, FFT_N2), :] = xf[FFT_N2:] - xb[FFT_N2:]
        return carry

    lax.fori_loop(0, chunk, step, 0, unroll=FFT_UNROLL_INNER)


def _filter_spectrum(h, inv_norm, tables):
    O2, L, C = h.shape
    O = O2 // 2
    n1_len, nk1, nk1_pad, chunk = _fft_plan(2 * L)
    half = n1_len // 2
    (f1r, f1i), (f2r, f2i), (twr, twi) = tables
    fa = jnp.concatenate([f1r[:nk1_pad, :half], f1i[:nk1_pad, :half]], axis=0).astype(BF16)
    rows = chunk * FFT_N2
    const = lambda shape: pl.BlockSpec(shape, lambda o, c, k: (0, 0))
    out = jax.ShapeDtypeStruct((O, nk1 * FFT_N2, C), F32)
    scratch = pltpu.VMEM((nk1_pad * FFT_PITCH, LANES), F32)
    return pl.pallas_call(
        functools.partial(_spec_body, n1_len=n1_len, nk1_pad=nk1_pad, chunk=chunk),
        grid=(O, C // LANES, nk1 // chunk),
        in_specs=[
            pl.BlockSpec((None, L, LANES), lambda o, c, k: (2 * o, 0, c)),
            pl.BlockSpec((None, L, LANES), lambda o, c, k: (2 * o + 1, 0, c)),
            pl.BlockSpec((None, 1, LANES), lambda o, c, k: (o, 0, c)),
            const(fa.shape), const(f2r.shape), const(f2i.shape), const(twr.shape), const(twi.shape),
        ],
        out_specs=[pl.BlockSpec((None, rows, LANES), lambda o, c, k: (o, k, c)),
                   pl.BlockSpec((None, rows, LANES), lambda o, c, k: (o, k, c))],
        out_shape=[out, out],
        scratch_shapes=[scratch, scratch, scratch, scratch],
        compiler_params=_cparams(("parallel", "parallel", "arbitrary")),
        name="filter_spectrum",
    )(h, h, inv_norm, fa, f2r, f2i, twr, twi)


def _fftconv_body(y_ref, gate_ref, wy_ref, by_ref, wg_ref, bg_ref, d_ref, kr_ref, ki_ref, fa_ref,
                  fs_ref, f2r_ref, f2i_ref, twr_ref, twi_ref, o_ref, xs_ref, ar_ref, ai_ref, *,
                  n1_len, nk1_pad, chunk, conv_y):
    kc = pl.program_id(2)
    half = n1_len // 2

    @pl.when(kc == 0)
    def _():
        def copy(n1, carry):
            src = pl.multiple_of(n1 * FFT_N2, FFT_N2)
            dst = pl.multiple_of(n1 * FFT_PITCH, SUBLANES)
            if conv_y:
                xs_ref[pl.ds(dst, FFT_N2), :] = _shortconv_rows(y_ref, wy_ref, by_ref, n1, FFT_N2)
            else:
                xs_ref[pl.ds(dst, FFT_N2), :] = y_ref[pl.ds(src, FFT_N2), :]
            return carry

        lax.fori_loop(0, half, copy, 0)
        _outer_dft_to_scratch(lambda n2: xs_ref[pl.ds(n2, half, stride=FFT_PITCH), :],
                              fa_ref, ar_ref, ai_ref, nk1_pad)

    f2r = f2r_ref[...]
    f2i = f2i_ref[...]

    def step(t, carry):
        k1 = kc * chunk + t
        base = pl.multiple_of(k1 * FFT_PITCH, SUBLANES)
        rhs = jnp.concatenate([ar_ref[pl.ds(base, FFT_N2), :], ai_ref[pl.ds(base, FFT_N2), :]],
                              axis=0).astype(BF16)
        mf = _stacked_inner_dft(f2r, f2i, twr_ref[pl.ds(k1, 1), :], twi_ref[pl.ds(k1, 1), :])
        x = jnp.dot(mf.astype(BF16), rhs, preferred_element_type=F32)
        xr, xi = x[:FFT_N2], x[FFT_N2:]
        o = pl.multiple_of(t * FFT_N2, FFT_N2)
        kr = kr_ref[pl.ds(o, FFT_N2), :]
        ki = ki_ref[pl.ds(o, FFT_N2), :]
        z = jnp.concatenate([xr * kr - xi * ki, xr * ki + xi * kr], axis=0).astype(BF16)
        b = jnp.dot(mf.T.astype(BF16), z, preferred_element_type=F32)
        ar_ref[pl.ds(base, FFT_N2), :] = b[:FFT_N2]
        ai_ref[pl.ds(base, FFT_N2), :] = b[FFT_N2:]
        return carry

    lax.fori_loop(0, chunk, step, 0, unroll=FFT_UNROLL_INNER)

    @pl.when(kc == pl.num_programs(2) - 1)
    def _():
        fs = fs_ref[...]

        def inv_outer(n2, carry):
            rhs = jnp.concatenate([ar_ref[pl.ds(n2, nk1_pad, stride=FFT_PITCH), :],
                                   ai_ref[pl.ds(n2, nk1_pad, stride=FFT_PITCH), :]],
                                  axis=0).astype(BF16)
            conv = jnp.dot(fs, rhs, preferred_element_type=F32)
            ar_ref[pl.ds(n2, half, stride=FFT_PITCH), :] = conv
            return carry

        lax.fori_loop(0, FFT_N2, inv_outer, 0, unroll=FFT_UNROLL_OUTER)
        d = d_ref[...]

        def finish(n1, carry):
            src = pl.multiple_of(n1 * FFT_PITCH, SUBLANES)
            dst = pl.multiple_of(n1 * FFT_N2, FFT_N2)
            y = xs_ref[pl.ds(src, FFT_N2), :]
            gate = _shortconv_rows(gate_ref, wg_ref, bg_ref, n1, FFT_N2)
            o_ref[pl.ds(dst, FFT_N2), :] = (gate * (
                ar_ref[pl.ds(src, FFT_N2), :] + y * d)).astype(o_ref.dtype)
            return carry

        lax.fori_loop(0, half, finish, 0)


def _fftconv_gate(y, y_off, conv_y, z, gate_off, conv_w, conv_b, d, kr, ki, tables, out_dtype):
    B, L, _ = y.shape
    C = d.shape[-1]
    yo, go = y_off // LANES, gate_off // LANES
    wo = yo if conv_y else go
    N = 2 * L
    n1_len, nk1, nk1_pad, chunk = _fft_plan(N)
    half = n1_len // 2
    (f1r, f1i), (f2r, f2i), (twr, twi) = tables
    fa = jnp.concatenate([f1r[:nk1_pad, :half], f1i[:nk1_pad, :half]], axis=0).astype(BF16)
    wts = jnp.concatenate([jnp.ones((1,), F32), jnp.full((nk1 - 2,), 2.0, F32), jnp.ones((1,), F32),
                           jnp.zeros((nk1_pad - nk1,), F32)]) * (1.0 / N)
    fs = jnp.concatenate([f1r[:half, :nk1_pad] * wts, f1i[:half, :nk1_pad] * wts],
                         axis=1).astype(BF16)
    rows = chunk * FFT_N2
    const = lambda shape: pl.BlockSpec(shape, lambda c, b, k: (0, 0))
    return pl.pallas_call(
        functools.partial(_fftconv_body, n1_len=n1_len, nk1_pad=nk1_pad, chunk=chunk,
                          conv_y=conv_y),
        grid=(C // LANES, B, nk1 // chunk),
        in_specs=[
            pl.BlockSpec((None, L, LANES), lambda c, b, k: (b, 0, c + yo)),
            pl.BlockSpec((None, L, LANES), lambda c, b, k: (b, 0, c + go)),
            pl.BlockSpec((SHORT_CONV, LANES), lambda c, b, k: (0, c + wo)),
            pl.BlockSpec((1, LANES), lambda c, b, k: (0, c + wo)),
            pl.BlockSpec((SHORT_CONV, LANES), lambda c, b, k: (0, c + go)),
            pl.BlockSpec((1, LANES), lambda c, b, k: (0, c + go)),
            pl.BlockSpec((1, LANES), lambda c, b, k: (0, c)),
            pl.BlockSpec((rows, LANES), lambda c, b, k: (k, c)),
            pl.BlockSpec((rows, LANES), lambda c, b, k: (k, c)),
            const(fa.shape), const(fs.shape), const(f2r.shape), const(f2i.shape),
            const(twr.shape), const(twi.shape),
        ],
        out_specs=pl.BlockSpec((None, L, LANES), lambda c, b, k: (b, 0, c)),
        out_shape=jax.ShapeDtypeStruct((B, L, C), out_dtype),
        scratch_shapes=[pltpu.VMEM((half * FFT_PITCH, LANES), F32),
                        pltpu.VMEM((nk1_pad * FFT_PITCH, LANES), F32),
                        pltpu.VMEM((nk1_pad * FFT_PITCH, LANES), F32)],
        compiler_params=_cparams(("parallel", "parallel", "arbitrary")),
        name="fftconv_gate",
    )(y, z, conv_w, conv_b, conv_w, conv_b, d, kr, ki, fa, fs, f2r, f2i, twr, twi)


def _hyena(z, conv_w, conv_b, fw1, fb1, ff1, fw2, fb2, ff2, fw3, fb3, hyena_d):
    B, L, C3 = z.shape
    W = C3 // 3
    h, sums = _hyena_filter(L, W, fw1, fb1, ff1, fw2, fb2, ff2, fw3, fb3)
    sums = sums.reshape(HYENA_ORDER, 2, W)
    inv_norm = (1.0 / (sums[:, 0] + sums[:, 1]))[:, None, :]
    tables = _dft_tables(2 * L // FFT_N2, FFT_N2)
    kr, ki = _filter_spectrum(h, inv_norm, tables)
    conv_b = conv_b.reshape(1, C3)
    y = z
    for o in range(HYENA_ORDER):
        y = _fftconv_gate(y, 0, o == 0, z, (o + 1) * W, conv_w, conv_b,
                          hyena_d[o].reshape(1, W).astype(F32), kr[o], ki[o], tables,
                          BF16 if o == HYENA_ORDER - 1 else F32)
    return y


def _merge_body(yh_ref, ya_ref, g_ref, x_ref, whu_ref, wau_ref, wo_ref, n2_ref, rwh_ref, rwl_ref,
                rb_ref, h_ref, hn_ref, lg_ref):
    D = x_ref.shape[-1]
    up_h = jnp.dot(yh_ref[...], whu_ref[...], preferred_element_type=F32)
    up_a = jnp.dot(ya_ref[...], wau_ref[...], preferred_element_type=F32)
    g = g_ref[...].astype(F32)
    merged = jax.nn.sigmoid(g[:, :D]) * up_h + jax.nn.sigmoid(g[:, D:]) * up_a
    h = x_ref[...] + jnp.dot(merged.astype(BF16), wo_ref[...], preferred_element_type=F32)
    h_ref[...] = h
    ms = jnp.mean(h * h, axis=-1, keepdims=True)
    hn = h * lax.rsqrt(ms + EPS) * n2_ref[...]
    hn_ref[...] = hn.astype(BF16)
    hh, hl = _split_bf16(hn)
    lg_ref[...] = (jnp.dot(hh, rwh_ref[...], preferred_element_type=F32)
                   + jnp.dot(hl, rwh_ref[...], preferred_element_type=F32)
                   + jnp.dot(hh, rwl_ref[...], preferred_element_type=F32)) + rb_ref[...]


def _merge(yh, ya, gates, xt, whu, wau, wo, n2g, rw, rb, tm=1024):
    T, D = xt.shape
    W = yh.shape[1]
    E = rw.shape[1]
    rwp = jnp.zeros((D, LOGIT_PAD), F32).at[:, :E].set(rw)
    rwh, rwl = _split_bf16(rwp)
    rbp = jnp.zeros((1, LOGIT_PAD), F32).at[0, :E].set(rb)
    rowblk = lambda w: pl.BlockSpec((tm, w), lambda i: (i, 0))
    const = lambda shape: pl.BlockSpec(shape, lambda i: (0, 0))
    return pl.pallas_call(
        _merge_body,
        grid=(T // tm,),
        in_specs=[rowblk(W), rowblk(W), rowblk(2 * D), rowblk(D),
                  const((W, D)), const((W, D)), const((D, D)), const((1, D)),
                  const((D, LOGIT_PAD)), const((D, LOGIT_PAD)), const((1, LOGIT_PAD))],
        out_specs=[rowblk(D), rowblk(D), rowblk(LOGIT_PAD)],
        out_shape=[jax.ShapeDtypeStruct((T, D), F32), jax.ShapeDtypeStruct((T, D), BF16),
                   jax.ShapeDtypeStruct((T, LOGIT_PAD), F32)],
        compiler_params=_cparams(("parallel",)),
        name="merge",
    )(yh, ya, gates, xt, whu.astype(BF16), wau.astype(BF16), wo.astype(BF16), n2g.reshape(1, D),
      rwh, rwl, rbp)


def _moe_body(be_ref, nused_ref, x_ref, w1g_ref, w1l_ref, b1g_ref, b1l_ref, w2_ref, b2_ref, o_ref):
    i = pl.program_id(0)

    @pl.when(i < nused_ref[0])
    def _():
        x = x_ref[...]
        nt = (((1,), (1,)), ((), ()))
        glu = lax.dot_general(x, w1g_ref[...], nt, preferred_element_type=F32) + b1g_ref[...]
        lin = lax.dot_general(x, w1l_ref[...], nt, preferred_element_type=F32) + b1l_ref[...]
        glu = jnp.minimum(glu, SWIGLU_LIMIT)
        lin = jnp.clip(lin, -SWIGLU_LIMIT, SWIGLU_LIMIT)
        act = glu * jax.nn.sigmoid(SWIGLU_ALPHA * glu) * (lin + 1.0)
        y = jnp.dot(act.astype(BF16), w2_ref[...].astype(BF16),
                    preferred_element_type=F32) + b2_ref[...]
        o_ref[...] = y.astype(o_ref.dtype)

    @pl.when(i >= nused_ref[0])
    def _():
        o_ref[...] = jnp.zeros(o_ref.shape, o_ref.dtype)


def _moe_experts(xg, block_expert, n_used, w1g, w1l, b1g, b1l, w2, b2):
    P, D = xg.shape
    dff = w2.shape[1]
    nb = P // MOE_TM
    wspec = lambda k, n: pl.BlockSpec((None, k, n), lambda i, be, nu: (be[i], 0, 0))
    grid_spec = pltpu.PrefetchScalarGridSpec(
        num_scalar_prefetch=2,
        grid=(nb,),
        in_specs=[
            pl.BlockSpec((MOE_TM, D), lambda i, be, nu: (i, 0)),
            wspec(dff, D), wspec(dff, D), wspec(1, dff), wspec(1, dff),
            wspec(dff, D), wspec(1, D),
        ],
        out_specs=pl.BlockSpec((MOE_TM, D), lambda i, be, nu: (i, 0)),
    )
    return pl.pallas_call(
        _moe_body,
        grid_spec=grid_spec,
        out_shape=jax.ShapeDtypeStruct((P, D), BF16),
        compiler_params=_cparams(("arbitrary",)),
        name="moe_experts",
    )(block_expert, n_used, xg, w1g, w1l, b1g, b1l, w2, b2)


def _prep_w1_body(w_ref, g_ref, l_ref, t_ref):
    half = g_ref.shape[0]
    wt = w_ref[...].T
    for c in range(t_ref.shape[0]):
        cols = slice(c * LANES, (c + 1) * LANES)
        t_ref[c] = wt[:, cols]
        g_ref[:, cols] = t_ref[c, pl.ds(0, half, stride=2), :].astype(BF16)
        l_ref[:, cols] = t_ref[c, pl.ds(1, half, stride=2), :].astype(BF16)


def _prep_w1(w1, tc=16 * LANES):
    E, D, F2 = w1.shape
    tc = min(tc, F2)
    out = jax.ShapeDtypeStruct((E, F2 // 2, D), BF16)
    return pl.pallas_call(
        _prep_w1_body,
        grid=(E, F2 // tc),
        in_specs=[pl.BlockSpec((None, D, tc), lambda e, j: (e, 0, j))],
        out_specs=[pl.BlockSpec((None, tc // 2, D), lambda e, j: (e, j, 0)),
                   pl.BlockSpec((None, tc // 2, D), lambda e, j: (e, j, 0))],
        out_shape=[out, out],
        scratch_shapes=[pltpu.VMEM((D // LANES, tc, LANES), F32)],
        compiler_params=_cparams(("parallel", "parallel")),
        name="prep_w1",
    )(w1)


def _combine_body(h_ref, y_ref, g_ref, o_ref):
    acc = h_ref[...]
    g = g_ref[...]
    for k in range(y_ref.shape[0]):
        acc = acc + g[:, k:k + 1] * y_ref[k].astype(F32)
    o_ref[...] = acc


def _combine(h, yk, gates, chunk, tm=512):
    T, D = h.shape
    K, tc, _ = yk.shape
    first = chunk * (tc // tm)
    return pl.pallas_call(
        _combine_body,
        grid=(tc // tm,),
        in_specs=[pl.BlockSpec((tm, D), lambda i: (first + i, 0)),
                  pl.BlockSpec((K, tm, D), lambda i: (0, i, 0)),
                  pl.BlockSpec((tm, K), lambda i: (first + i, 0))],
        out_specs=pl.BlockSpec((tm, D), lambda i: (first + i, 0)),
        out_shape=jax.ShapeDtypeStruct((T, D), F32),
        input_output_aliases={0: 0},
        compiler_params=_cparams(("parallel",)),
        name="moe_combine",
    )(h, yk, gates)


def _lookup(table, idx):
    n = table.shape[0]
    hit = idx[None, :] == jnp.arange(n, dtype=idx.dtype)[:, None]
    return jnp.sum(jnp.where(hit, table[:, None], 0), axis=0)


def _moe(h, hn_bf16, logits, w1, b1, w2, b2):
    T, D = hn_bf16.shape
    E = w1.shape[0]
    top_val, top_idx = lax.top_k(logits, TOP_K)
    gates = jax.nn.softmax(top_val, axis=-1)
    TK = T * TOP_K
    e_flat = top_idx.reshape(TK).astype(jnp.int32)
    order = jnp.argsort(e_flat).astype(jnp.int32)
    rank = jnp.argsort(order).astype(jnp.int32)
    counts = jnp.sum(jnp.arange(E, dtype=jnp.int32)[:, None] == e_flat[None, :], axis=1,
                     dtype=jnp.int32)
    starts = jnp.cumsum(counts) - counts
    padded = ((counts + MOE_TM - 1) // MOE_TM) * MOE_TM
    pad_ends = jnp.cumsum(padded)
    pad_starts = pad_ends - padded
    nb = (TK + E * (MOE_TM - 1) + MOE_TM - 1) // MOE_TM
    block_start = jnp.arange(nb, dtype=jnp.int32) * MOE_TM
    block_expert = jnp.minimum(jnp.sum(block_start[:, None] >= pad_ends[None, :], axis=1),
                               E - 1).astype(jnp.int32)
    n_used = (pad_ends[-1] // MOE_TM).astype(jnp.int32).reshape(1)
    blk_first = block_start - pad_starts[block_expert]
    within = blk_first[:, None] + jnp.arange(MOE_TM, dtype=jnp.int32)[None, :]
    valid = within < counts[block_expert][:, None]
    sorted_idx = jnp.where(valid, starts[block_expert][:, None] + within, 0).reshape(nb * MOE_TM)
    filler = jnp.arange(nb * MOE_TM, dtype=jnp.int32) % T
    src = jnp.where(valid.reshape(nb * MOE_TM), order[sorted_idx] // TOP_K, filler)
    pos = _lookup(pad_starts - starts, e_flat) + rank
    w1g, w1l = _prep_w1(w1)
    y = _moe_experts(hn_bf16[src], block_expert, n_used, w1g, w1l,
                     b1[:, None, 0::2].astype(F32), b1[:, None, 1::2].astype(F32),
                     w2, b2[:, None, :].astype(F32))
    pos_kt = pos.reshape(T, TOP_K).T
    tc = T // COMBINE_CHUNKS
    for i in range(COMBINE_CHUNKS):
        h = _combine(h, y[pos_kt[:, i * tc:(i + 1) * tc]], gates, i)
    return h


def _rope_tables(L):
    rows = L // GRID_W
    row = jnp.repeat(jnp.arange(rows, dtype=F32), GRID_W)
    col = jnp.tile(jnp.arange(GRID_W, dtype=F32), rows)
    half = HEAD_DIM // 2
    freqs = ROPE_THETA ** (-jnp.arange(0, half, 2, dtype=F32) / half)
    ang = jnp.concatenate([row[:, None] * freqs, col[:, None] * freqs], axis=-1)
    cos = jnp.repeat(jnp.cos(ang), 2, axis=-1)
    sin = jnp.repeat(jnp.sin(ang), 2, axis=-1)
    sign = jnp.tile(jnp.array([-1.0, 1.0], F32), HEAD_DIM // 2)
    reps = LANES // HEAD_DIM
    return jnp.tile(cos, (1, reps)), jnp.tile(sin * sign, (1, reps))


def kernel(x, norm1_g, w_in, conv_w, conv_b, filt_w1, filt_b1, filt_freq1, filt_w2, filt_b2, filt_freq2, filt_w3, filt_b3, hyena_d, q_norm_g, k_norm_g, w_hyena_up, w_attn_up, w_out, norm2_g, router_w, router_b, expert_w1, expert_b1, expert_w2, expert_b2):
    B, L, D = x.shape
    T = B * L
    depth = w_in.shape[0]
    hw = conv_w.shape[-1] // 3
    aw = N_Q_HEADS * HEAD_DIM
    kvw = N_KV_HEADS * HEAD_DIM
    widths = (3 * hw, aw, 2 * kvw, 2 * D)
    cosf, sinf = _rope_tables(L)
    for l in range(depth):
        xt = x.reshape(T, D)
        z, q, kv, gates = _inproj(xt, norm1_g[l], w_in[l].astype(BF16), widths,
                                  (F32, BF16, BF16, BF16))
        y_hy = _hyena(z.reshape(B, L, 3 * hw), conv_w[l], conv_b[l], filt_w1[l], filt_b1[l],
                      filt_freq1[l], filt_w2[l], filt_b2[l], filt_freq2[l], filt_w3[l], filt_b3[l],
                      hyena_d[l])
        gq = jnp.tile(q_norm_g[l].astype(F32), N_Q_HEADS)[None, :]
        gk = jnp.tile(k_norm_g[l].astype(F32), N_KV_HEADS)[None, :]
        qr, kr, va = _qkrope(q, kv, gq, gk, cosf, sinf, L)
        y_at = _attention(qr.reshape(B, L, aw), kr, va.reshape(N_KV_HEADS, B, L, kvw))
        h, hn, logits = _merge(y_hy.reshape(T, hw), y_at.reshape(T, aw), gates, xt,
                               w_hyena_up[l], w_attn_up[l], w_out[l], norm2_g[l],
                               router_w[l], router_b[l])
        x = _moe(h, hn, logits[:, :N_EXPERTS], expert_w1[l], expert_b1[l], expert_w2[l],
                 expert_b2[l]).reshape(B, L, D)
    return x
```

```python
import functools
import math

import jax
import jax.numpy as jnp
from jax import lax
from jax.experimental import pallas as pl
from jax.experimental.pallas import tpu as pltpu

F32 = jnp.float32
BF16 = jnp.bfloat16

GRID_W = 64
HEAD_DIM = 64
N_Q_HEADS = 8
N_KV_HEADS = 2
Q_PER_KV = N_Q_HEADS // N_KV_HEADS
ROPE_THETA = 10000.0
HYENA_ORDER = 2
SHORT_CONV = 3
FILTER_EMB = 33
FAST_DECAY_PCT = 0.3
SLOW_DECAY_PCT = 1.5
DECAY_TARGET = 1e-2
N_EXPERTS = 32
TOP_K = 4
SWIGLU_LIMIT = 7.0
SWIGLU_ALPHA = 1.702
EPS = 1e-6

LANES = 128
SUBLANES = 8
VMEM_LIMIT = 56 * 1024 * 1024

FFT_N2 = LANES
FFT_PITCH = FFT_N2 + SUBLANES
FFT_K1_CHUNK_MAX = 16
FFT_UNROLL_OUTER = 16
FFT_UNROLL_INNER = True
MOE_TM = 512
LOGIT_PAD = LANES


def _cparams(sem):
    return pltpu.CompilerParams(dimension_semantics=sem, vmem_limit_bytes=VMEM_LIMIT)


def _split_bf16(x):
    hi = x.astype(BF16)
    lo = (x - hi.astype(F32)).astype(BF16)
    return hi, lo


def _inproj_body(x_ref, g_ref, w_ref, z_ref, q_ref, kv_ref, gate_ref, *, widths):
    x = x_ref[...]
    ms = jnp.mean(x * x, axis=-1, keepdims=True)
    u = (x * lax.rsqrt(ms + EPS) * g_ref[...]).astype(BF16)
    off = 0
    for ref, w in zip((z_ref, q_ref, kv_ref, gate_ref), widths):
        ref[...] = jnp.dot(u, w_ref[:, off:off + w], preferred_element_type=F32).astype(ref.dtype)
        off += w


def _inproj(xt, g, w_bf16, widths, dtypes, tm=1024):
    T, D = xt.shape
    n = w_bf16.shape[1]
    return pl.pallas_call(
        functools.partial(_inproj_body, widths=widths),
        grid=(T // tm,),
        in_specs=[
            pl.BlockSpec((tm, D), lambda i: (i, 0)),
            pl.BlockSpec((1, D), lambda i: (0, 0)),
            pl.BlockSpec((D, n), lambda i: (0, 0)),
        ],
        out_specs=[pl.BlockSpec((tm, w), lambda i: (i, 0)) for w in widths],
        out_shape=[jax.ShapeDtypeStruct((T, w), dt) for w, dt in zip(widths, dtypes)],
        compiler_params=_cparams(("parallel",)),
        name="inproj",
    )(xt, g.reshape(1, D), w_bf16)


def _head_norm_rope(x, gain, cosf, sinf, ones_blk):
    w = x.shape[-1]
    hi, lo = _split_bf16(x * x)
    ss = (jnp.dot(hi, ones_blk, preferred_element_type=F32)
          + jnp.dot(lo, ones_blk, preferred_element_type=F32))
    xn = x * lax.rsqrt(ss * (1.0 / HEAD_DIM) + EPS) * gain
    lane = lax.broadcasted_iota(jnp.int32, (x.shape[0], LANES), 1)
    cols = []
    for c in range(w // LANES):
        col = xn[:, c * LANES:(c + 1) * LANES]
        cols.append(jnp.where(lane % 2 == 0, pltpu.roll(col, LANES - 1, 1), pltpu.roll(col, 1, 1)))
    swapped = cols[0] if len(cols) == 1 else jnp.concatenate(cols, axis=1)
    return xn * cosf + swapped * sinf


def _qkrope_body(q_ref, kv_ref, gq_ref, gk_ref, cos_ref, sin_ref, oq_ref, ok_ref,
                 qo_ref, ko_ref, vo_ref):
    cosf = cos_ref[...]
    sinf = sin_ref[...]
    nq = q_ref.shape[-1] // LANES
    q = _head_norm_rope(q_ref[...].astype(F32), gq_ref[...], jnp.tile(cosf, (1, nq)),
                        jnp.tile(sinf, (1, nq)), oq_ref[...])
    qo_ref[...] = (q * (HEAD_DIM ** -0.5 * math.log2(math.e))).astype(BF16)
    kv = kv_ref[...].astype(F32)
    kw = kv.shape[-1] // 2
    k = _head_norm_rope(kv[:, :kw], gk_ref[...], cosf, sinf, ok_ref[...])
    ko_ref[...] = k.T.astype(BF16)
    v = kv[:, kw:]
    lane = lax.broadcasted_iota(jnp.int32, v.shape, 1)
    for h in range(N_KV_HEADS):
        vo_ref[h] = jnp.where((lane // HEAD_DIM) == h, v, 1.0).astype(BF16)


def _qkrope(q, kv, gq, gk, cosf, sinf, seq_len, tm=512):
    T, qw = q.shape
    kw = kv.shape[1] // 2
    assert kw == LANES and N_KV_HEADS * HEAD_DIM == LANES and N_KV_HEADS == 2
    nl = seq_len // tm

    def blk_ones(w):
        r = jnp.arange(w) // HEAD_DIM
        return (r[:, None] == r[None, :]).astype(BF16)

    return pl.pallas_call(
        _qkrope_body,
        grid=(T // tm,),
        in_specs=[
            pl.BlockSpec((tm, qw), lambda i: (i, 0)),
            pl.BlockSpec((tm, 2 * kw), lambda i: (i, 0)),
            pl.BlockSpec((1, qw), lambda i: (0, 0)),
            pl.BlockSpec((1, kw), lambda i: (0, 0)),
            pl.BlockSpec((tm, LANES), lambda i: (i % nl, 0)),
            pl.BlockSpec((tm, LANES), lambda i: (i % nl, 0)),
            pl.BlockSpec((qw, qw), lambda i: (0, 0)),
            pl.BlockSpec((kw, kw), lambda i: (0, 0)),
        ],
        out_specs=[
            pl.BlockSpec((tm, qw), lambda i: (i, 0)),
            pl.BlockSpec((kw, tm), lambda i: (0, i)),
            pl.BlockSpec((N_KV_HEADS, tm, kw), lambda i: (0, i, 0)),
        ],
        out_shape=[
            jax.ShapeDtypeStruct((T, qw), BF16),
            jax.ShapeDtypeStruct((kw, T), BF16),
            jax.ShapeDtypeStruct((N_KV_HEADS, T, kw), BF16),
        ],
        compiler_params=_cparams(("parallel",)),
        name="qkrope",
    )(q, kv, gq, gk, cosf, sinf, blk_ones(qw), blk_ones(kw))


def _attn_body(q_ref, k_ref, v_ref, o_ref, qs_ref, m_ref, acc_ref, *, tk, nsplit):
    kvh = pl.program_id(1)
    tq = q_ref.shape[0]
    seq = k_ref.shape[1]
    rows = Q_PER_KV * tq
    lane = lax.broadcasted_iota(jnp.int32, (tq, LANES), 1)
    in_head = (lane // HEAD_DIM) == kvh

    for g in range(Q_PER_KV):
        col = q_ref[:, (g // 2) * LANES:(g // 2 + 1) * LANES].astype(F32)
        col = jnp.where((g % 2) == kvh, col, pltpu.roll(col, HEAD_DIM, 1))
        qs_ref[g * tq:(g + 1) * tq, :] = jnp.where(in_head, col, 0.0).astype(BF16)
    m_ref[...] = jnp.full(m_ref.shape, -jnp.inf, F32)
    acc_ref[...] = jnp.zeros(acc_ref.shape, F32)
    part = rows // nsplit

    def step(c, carry):
        r0 = pl.multiple_of(c * tk, tk)
        kc = k_ref[:, pl.ds(r0, tk)]
        vc = v_ref[pl.ds(r0, tk), :]
        for h in range(nsplit):
            sl = slice(h * part, (h + 1) * part)
            s = jnp.dot(qs_ref[sl, :], kc, preferred_element_type=F32)
            m_prev = m_ref[sl, :]
            m_new = jnp.maximum(m_prev, jnp.max(s, axis=-1, keepdims=True))
            p = jnp.exp2(s - jnp.tile(m_new, (1, tk // LANES)))
            acc_ref[sl, :] = jnp.exp2(m_prev - m_new) * acc_ref[sl, :] + jnp.dot(
                p.astype(BF16), vc, preferred_element_type=F32)
            m_ref[sl, :] = m_new
        return carry

    lax.fori_loop(0, seq // tk, step, 0)

    acc = acc_ref[...]
    o = acc / pltpu.roll(acc, HEAD_DIM, 1)
    for c in range(Q_PER_KV // 2):
        even = o[(2 * c) * tq:(2 * c + 1) * tq, :]
        odd = o[(2 * c + 1) * tq:(2 * c + 2) * tq, :]
        even = jnp.where(kvh == 0, even, pltpu.roll(even, HEAD_DIM, 1))
        odd = jnp.where(kvh == 1, odd, pltpu.roll(odd, HEAD_DIM, 1))
        o_ref[:, c * LANES:(c + 1) * LANES] = jnp.where(lane < HEAD_DIM, even, odd).astype(o_ref.dtype)


def _attention(q, kt, v_aug, tq=512, tk=2048, nsplit=2):
    B, L, qw = q.shape
    gw = Q_PER_KV * HEAD_DIM
    kw = kt.shape[0]
    rows = Q_PER_KV * tq
    return pl.pallas_call(
        functools.partial(_attn_body, tk=tk, nsplit=nsplit),
        grid=(B, N_KV_HEADS, L // tq),
        in_specs=[
            pl.BlockSpec((None, tq, gw), lambda b, h, i: (b, i, h)),
            pl.BlockSpec((kw, L), lambda b, h, i: (0, b)),
            pl.BlockSpec((None, None, L, kw), lambda b, h, i: (h, b, 0, 0)),
        ],
        out_specs=pl.BlockSpec((None, tq, gw), lambda b, h, i: (b, i, h)),
        out_shape=jax.ShapeDtypeStruct((B, L, qw), BF16),
        scratch_shapes=[
            pltpu.VMEM((rows, kw), BF16),
            pltpu.VMEM((rows, LANES), F32),
            pltpu.VMEM((rows, kw), F32),
        ],
        compiler_params=_cparams(("parallel", "parallel", "parallel")),
        name="attention",
    )(q, kt, v_aug)


def _hdot(a, b):
    ah, al = _split_bf16(a)
    bh, bl = _split_bf16(b)
    return (jnp.dot(ah, bh, preferred_element_type=F32)
            + jnp.dot(al, bh, preferred_element_type=F32)
            + jnp.dot(ah, bl, preferred_element_type=F32))


def _filter_body(feat_ref, t_ref, w1_ref, b1_ref, f1_ref, w2_ref, b2_ref, f2_ref, w3_ref, b3_ref,
                 delta_ref, bwd_ref, h_ref, sum_ref):
    i = pl.program_id(0)
    h = jnp.sin(f1_ref[...] * (_hdot(feat_ref[...], w1_ref[...]) + b1_ref[...]))
    h = jnp.sin(f2_ref[...] * (_hdot(h, w2_ref[...]) + b2_ref[...]))
    h = _hdot(h, w3_ref[...]) + b3_ref[...]
    t = t_ref[...]
    h = h * jnp.exp(-t * delta_ref[...])
    width = h_ref.shape[-1]
    for j in range(h_ref.shape[0]):
        h_ref[j] = h[:, j * width:(j + 1) * width]
    row = lax.broadcasted_iota(jnp.int32, h.shape, 0) + i * h.shape[0]
    a = jnp.where((row == 0) & (bwd_ref[...] > 0.5), 0.0, jnp.abs(h))
    part = jnp.sum(a, axis=0, keepdims=True)

    @pl.when(i == 0)
    def _():
        sum_ref[...] = jnp.zeros(sum_ref.shape, F32)

    sum_ref[...] += jnp.broadcast_to(part, sum_ref.shape)


def _hyena_filter(L, width, fw1, fb1, ff1, fw2, fb2, ff2, fw3, fb3, tl=512):
    bands = (FILTER_EMB - 1) // 2
    t = jnp.linspace(0.0, 1.0, L, dtype=F32)[:, None]
    w = 2.0 * math.pi * jnp.arange(L, dtype=F32)[:, None] / L
    fr = jnp.linspace(1e-4, bands - 1, bands, dtype=F32)[None, :]
    feats = jnp.concatenate([t, jnp.cos(w * fr), -jnp.sin(w * fr)], axis=-1)
    max_decay = math.log(DECAY_TARGET) / FAST_DECAY_PCT
    min_decay = math.log(DECAY_TARGET) / SLOW_DECAY_PCT
    deltas = jnp.abs(jnp.linspace(min_decay, max_decay, width, dtype=F32))
    ncol = fw3.shape[1]
    delta_cols = jnp.tile(deltas, ncol // width)[None, :]
    is_bwd = ((jnp.arange(ncol) // width) % 2).astype(F32)[None, :]
    emb = hid = LANES

    def pad2(a, r, c):
        a = a.astype(F32)
        return jnp.zeros((r, c), F32).at[:a.shape[0], :a.shape[1]].set(a)

    row = lambda a: pad2(a.reshape(1, -1), 1, hid)
    feats = pad2(feats, L, emb)
    fw1, fw2, fw3 = pad2(fw1, emb, hid), pad2(fw2, hid, hid), pad2(fw3, hid, ncol)
    fb3 = fb3.reshape(1, ncol).astype(F32)
    const = lambda shape: pl.BlockSpec(shape, lambda i: (0, 0))
    h, sums = pl.pallas_call(
        _filter_body,
        grid=(L // tl,),
        in_specs=[
            pl.BlockSpec((tl, emb), lambda i: (i, 0)),
            pl.BlockSpec((tl, 1), lambda i: (i, 0)),
            const((emb, hid)), const((1, hid)), const((1, hid)),
            const((hid, hid)), const((1, hid)), const((1, hid)),
            const((hid, ncol)), const((1, ncol)), const((1, ncol)), const((1, ncol)),
        ],
        out_specs=[pl.BlockSpec((ncol // width, tl, width), lambda i: (0, i, 0)),
                   pl.BlockSpec((SUBLANES, ncol), lambda i: (0, 0))],
        out_shape=[jax.ShapeDtypeStruct((ncol // width, L, width), F32),
                   jax.ShapeDtypeStruct((SUBLANES, ncol), F32)],
        compiler_params=_cparams(("arbitrary",)),
        name="hyena_filter",
    )(feats, t, fw1, row(fb1), row(ff1), fw2, row(fb2), row(ff2), fw3, fb3, delta_cols, is_bwd)
    return h, sums[0]


def _shortconv_rows(z_ref, w_ref, b_ref, c, chunk):
    L = z_ref.shape[0]
    row = lax.broadcasted_iota(jnp.int32, (chunk, z_ref.shape[1]), 0)
    r0 = pl.multiple_of(c * chunk, chunk)
    cur = z_ref[pl.ds(r0, chunk), :]
    prev_row = z_ref[pl.ds(jnp.maximum(r0 - 1, 0), 1), :]
    next_row = z_ref[pl.ds(jnp.minimum(r0 + chunk, L - 1), 1), :]
    prev_row = jnp.where(c == 0, 0.0, prev_row)
    next_row = jnp.where(c == L // chunk - 1, 0.0, next_row)
    down = jnp.where(row == 0, prev_row, pltpu.roll(cur, 1, 0))
    up = jnp.where(row == chunk - 1, next_row, pltpu.roll(cur, chunk - 1, 0))
    return b_ref[...] + down * w_ref[0:1, :] + cur * w_ref[1:2, :] + up * w_ref[2:3, :]


def _dft_tables(n1_len, n2_len):
    n = n1_len * n2_len

    def root(num, den):
        ang = (2.0 * math.pi / den) * (num % den).astype(F32)
        return jnp.cos(ang), -jnp.sin(ang)

    i1 = jnp.arange(n1_len, dtype=jnp.int32)
    i2 = jnp.arange(n2_len, dtype=jnp.int32)
    f1r, f1i = root(i1[:, None] * i1[None, :], n1_len)
    f2r, f2i = root(i2[:, None] * i2[None, :], n2_len)
    twr, twi = root(i1[:, None] * i2[None, :], n)
    return (f1r, f1i), (f2r, f2i), (twr, twi)


def _stacked_inner_dft(f2r, f2i, twr_row, twi_row):
    gr = f2r * twr_row - f2i * twi_row
    gi = f2r * twi_row + f2i * twr_row
    top = jnp.concatenate([gr, -gi], axis=1)
    bot = jnp.concatenate([gi, gr], axis=1)
    return jnp.concatenate([top, bot], axis=0)


def _fft_plan(n):
    n1_len = n // FFT_N2
    nk1 = n1_len // 2 + 1
    nk1_pad = -(-nk1 // SUBLANES) * SUBLANES
    chunk = max(c for c in range(1, FFT_K1_CHUNK_MAX + 1) if nk1 % c == 0)
    return n1_len, nk1, nk1_pad, chunk


def _outer_dft_to_scratch(load_rows, fa_ref, ar_ref, ai_ref, nk1_pad):
    fa = fa_ref[...]

    def step(n2, carry):
        a = jnp.dot(fa, load_rows(n2).astype(BF16), preferred_element_type=F32)
        ar_ref[pl.ds(n2, nk1_pad, stride=FFT_PITCH), :] = a[:nk1_pad]
        ai_ref[pl.ds(n2, nk1_pad, stride=FFT_PITCH), :] = a[nk1_pad:]
        return carry

    lax.fori_loop(0, FFT_N2, step, 0, unroll=FFT_UNROLL_OUTER)


def _spec_body(f_ref, b_ref, inv_ref, fa_ref, f2r_ref, f2i_ref, twr_ref, twi_ref, kr_ref, ki_ref,
               fr_ref, fi_ref, br_ref, bi_ref, *, n1_len, nk1_pad, chunk):
    kc = pl.program_id(2)
    half = n1_len // 2

    @pl.when(kc == 0)
    def _():
        inv = inv_ref[...]
        row = lax.broadcasted_iota(jnp.int32, (half, LANES), 0)
        _outer_dft_to_scratch(lambda n2: f_ref[pl.ds(n2, half, stride=FFT_N2), :] * inv,
                              fa_ref, fr_ref, fi_ref, nk1_pad)
        _outer_dft_to_scratch(
            lambda n2: jnp.where((row == 0) & (n2 == 0), 0.0,
                                 b_ref[pl.ds(n2, half, stride=FFT_N2), :] * inv),
            fa_ref, br_ref, bi_ref, nk1_pad)

    f2r = f2r_ref[...]
    f2i = f2i_ref[...]

    def step(t, carry):
        k1 = kc * chunk + t
        base = pl.multiple_of(k1 * FFT_PITCH, SUBLANES)
        mf = _stacked_inner_dft(f2r, f2i, twr_ref[pl.ds(k1, 1), :],
                                twi_ref[pl.ds(k1, 1), :]).astype(BF16)
        rows = pl.ds(base, FFT_N2)
        xf = jnp.dot(mf, jnp.concatenate([fr_ref[rows, :], fi_ref[rows, :]], axis=0).astype(BF16),
                     preferred_element_type=F32)
        xb = jnp.dot(mf, jnp.concatenate([br_ref[rows, :], bi_ref[rows, :]], axis=0).astype(BF16),
                     preferred_element_type=F32)
        o = pl.multiple_of(t * FFT_N2, FFT_N2)
        kr_ref[pl.ds(o, FFT_N2), :] = xf[:FFT_N2] + xb[:FFT_N2]
        ki_ref[pl.ds(o, FFT_N2), :] = xf[FFT_N2:] - xb[FFT_N2:]
        return carry

    lax.fori_loop(0, chunk, step, 0, unroll=FFT_UNROLL_INNER)


def _filter_spectrum(h, inv_norm, tables):
    O2, L, C = h.shape
    O = O2 // 2
    n1_len, nk1, nk1_pad, chunk = _fft_plan(2 * L)
    half = n1_len // 2
    (f1r, f1i), (f2r, f2i), (twr, twi) = tables
    fa = jnp.concatenate([f1r[:nk1_pad, :half], f1i[:nk1_pad, :half]], axis=0).astype(BF16)
    rows = chunk * FFT_N2
    const = lambda shape: pl.BlockSpec(shape, lambda o, c, k: (0, 0))
    out = jax.ShapeDtypeStruct((O, nk1 * FFT_N2, C), F32)
    scratch = pltpu.VMEM((nk1_pad * FFT_PITCH, LANES), F32)
    return pl.pallas_call(
        functools.partial(_spec_body, n1_len=n1_len, nk1_pad=nk1_pad, chunk=chunk),
        grid=(O, C // LANES, nk1 // chunk),
        in_specs=[
            pl.BlockSpec((None, L, LANES), lambda o, c, k: (2 * o, 0, c)),
            pl.BlockSpec((None, L, LANES), lambda o, c, k: (2 * o + 1, 0, c)),
            pl.BlockSpec((None, 1, LANES), lambda o, c, k: (o, 0, c)),
            const(fa.shape), const(f2r.shape), const(f2i.shape), const(twr.shape), const(twi.shape),
        ],
        out_specs=[pl.BlockSpec((None, rows, LANES), lambda o, c, k: (o, k, c)),
                   pl.BlockSpec((None, rows, LANES), lambda o, c, k: (o, k, c))],
        out_shape=[out, out],
        scratch_shapes=[scratch, scratch, scratch, scratch],
        compiler_params=_cparams(("parallel", "parallel", "arbitrary")),
        name="filter_spectrum",
    )(h, h, inv_norm, fa, f2r, f2i, twr, twi)


def _fftconv_body(y_ref, gate_ref, wy_ref, by_ref, wg_ref, bg_ref, d_ref, kr_ref, ki_ref, fa_ref,
                  fs_ref, f2r_ref, f2i_ref, twr_ref, twi_ref, o_ref, xs_ref, ar_ref, ai_ref, *,
                  n1_len, nk1_pad, chunk, conv_y):
    kc = pl.program_id(2)
    half = n1_len // 2

    @pl.when(kc == 0)
    def _():
        def copy(n1, carry):
            src = pl.multiple_of(n1 * FFT_N2, FFT_N2)
            dst = pl.multiple_of(n1 * FFT_PITCH, SUBLANES)
            if conv_y:
                xs_ref[pl.ds(dst, FFT_N2), :] = _shortconv_rows(y_ref, wy_ref, by_ref, n1, FFT_N2)
            else:
                xs_ref[pl.ds(dst, FFT_N2), :] = y_ref[pl.ds(src, FFT_N2), :]
            return carry

        lax.fori_loop(0, half, copy, 0)
        _outer_dft_to_scratch(lambda n2: xs_ref[pl.ds(n2, half, stride=FFT_PITCH), :],
                              fa_ref, ar_ref, ai_ref, nk1_pad)

    f2r = f2r_ref[...]
    f2i = f2i_ref[...]

    def step(t, carry):
        k1 = kc * chunk + t
        base = pl.multiple_of(k1 * FFT_PITCH, SUBLANES)
        rhs = jnp.concatenate([ar_ref[pl.ds(base, FFT_N2), :], ai_ref[pl.ds(base, FFT_N2), :]],
                              axis=0).astype(BF16)
        mf = _stacked_inner_dft(f2r, f2i, twr_ref[pl.ds(k1, 1), :], twi_ref[pl.ds(k1, 1), :])
        x = jnp.dot(mf.astype(BF16), rhs, preferred_element_type=F32)
        xr, xi = x[:FFT_N2], x[FFT_N2:]
        o = pl.multiple_of(t * FFT_N2, FFT_N2)
        kr = kr_ref[pl.ds(o, FFT_N2), :]
        ki = ki_ref[pl.ds(o, FFT_N2), :]
        z = jnp.concatenate([xr * kr - xi * ki, xr * ki + xi * kr], axis=0).astype(BF16)
        b = jnp.dot(mf.T.astype(BF16), z, preferred_element_type=F32)
        ar_ref[pl.ds(base, FFT_N2), :] = b[:FFT_N2]
        ai_ref[pl.ds(base, FFT_N2), :] = b[FFT_N2:]
        return carry

    lax.fori_loop(0, chunk, step, 0, unroll=FFT_UNROLL_INNER)

    @pl.when(kc == pl.num_programs(2) - 1)
    def _():
        fs = fs_ref[...]

        def inv_outer(n2, carry):
            rhs = jnp.concatenate([ar_ref[pl.ds(n2, nk1_pad, stride=FFT_PITCH), :],
                                   ai_ref[pl.ds(n2, nk1_pad, stride=FFT_PITCH), :]],
                                  axis=0).astype(BF16)
            conv = jnp.dot(fs, rhs, preferred_element_type=F32)
            ar_ref[pl.ds(n2, half, stride=FFT_PITCH), :] = conv
            return carry

        lax.fori_loop(0, FFT_N2, inv_outer, 0, unroll=FFT_UNROLL_OUTER)
        d = d_ref[...]

        def finish(n1, carry):
            src = pl.multiple_of(n1 * FFT_PITCH, SUBLANES)
            dst = pl.multiple_of(n1 * FFT_N2, FFT_N2)
            y = xs_ref[pl.ds(src, FFT_N2), :]
            gate = _shortconv_rows(gate_ref, wg_ref, bg_ref, n1, FFT_N2)
            o_ref[pl.ds(dst, FFT_N2), :] = (gate * (
                ar_ref[pl.ds(src, FFT_N2), :] + y * d)).astype(o_ref.dtype)
            return carry

        lax.fori_loop(0, half, finish, 0)


def _fftconv_gate(y, y_off, conv_y, z, gate_off, conv_w, conv_b, d, kr, ki, tables, out_dtype):
    B, L, _ = y.shape
    C = d.shape[-1]
    yo, go = y_off // LANES, gate_off // LANES
    wo = yo if conv_y else go
    N = 2 * L
    n1_len, nk1, nk1_pad, chunk = _fft_plan(N)
    half = n1_len // 2
    (f1r, f1i), (f2r, f2i), (twr, twi) = tables
    fa = jnp.concatenate([f1r[:nk1_pad, :half], f1i[:nk1_pad, :half]], axis=0).astype(BF16)
    wts = jnp.concatenate([jnp.ones((1,), F32), jnp.full((nk1 - 2,), 2.0, F32), jnp.ones((1,), F32),
                           jnp.zeros((nk1_pad - nk1,), F32)]) * (1.0 / N)
    fs = jnp.concatenate([f1r[:half, :nk1_pad] * wts, f1i[:half, :nk1_pad] * wts],
                         axis=1).astype(BF16)
    rows = chunk * FFT_N2
    const = lambda shape: pl.BlockSpec(shape, lambda c, b, k: (0, 0))
    return pl.pallas_call(
        functools.partial(_fftconv_body, n1_len=n1_len, nk1_pad=nk1_pad, chunk=chunk,
                          conv_y=conv_y),
        grid=(C // LANES, B, nk1 // chunk),
        in_specs=[
            pl.BlockSpec((None, L, LANES), lambda c, b, k: (b, 0, c + yo)),
            pl.BlockSpec((None, L, LANES), lambda c, b, k: (b, 0, c + go)),
            pl.BlockSpec((SHORT_CONV, LANES), lambda c, b, k: (0, c + wo)),
            pl.BlockSpec((1, LANES), lambda c, b, k: (0, c + wo)),
            pl.BlockSpec((SHORT_CONV, LANES), lambda c, b, k: (0, c + go)),
            pl.BlockSpec((1, LANES), lambda c, b, k: (0, c + go)),
            pl.BlockSpec((1, LANES), lambda c, b, k: (0, c)),
            pl.BlockSpec((rows, LANES), lambda c, b, k: (k, c)),
            pl.BlockSpec((rows, LANES), lambda c, b, k: (k, c)),
            const(fa.shape), const(fs.shape), const(f2r.shape), const(f2i.shape),
            const(twr.shape), const(twi.shape),
        ],
        out_specs=pl.BlockSpec((None, L, LANES), lambda c, b, k: (b, 0, c)),
        out_shape=jax.ShapeDtypeStruct((B, L, C), out_dtype),
        scratch_shapes=[pltpu.VMEM((half * FFT_PITCH, LANES), F32),
                        pltpu.VMEM((nk1_pad * FFT_PITCH, LANES), F32),
                        pltpu.VMEM((nk1_pad * FFT_PITCH, LANES), F32)],
        compiler_params=_cparams(("parallel", "parallel", "arbitrary")),
        name="fftconv_gate",
    )(y, z, conv_w, conv_b, conv_w, conv_b, d, kr, ki, fa, fs, f2r, f2i, twr, twi)


def _hyena(z, conv_w, conv_b, fw1, fb1, ff1, fw2, fb2, ff2, fw3, fb3, hyena_d):
    B, L, C3 = z.shape
    W = C3 // 3
    h, sums = _hyena_filter(L, W, fw1, fb1, ff1, fw2, fb2, ff2, fw3, fb3)
    sums = sums.reshape(HYENA_ORDER, 2, W)
    inv_norm = (1.0 / (sums[:, 0] + sums[:, 1]))[:, None, :]
    tables = _dft_tables(2 * L // FFT_N2, FFT_N2)
    kr, ki = _filter_spectrum(h, inv_norm, tables)
    conv_b = conv_b.reshape(1, C3)
    y = z
    for o in range(HYENA_ORDER):
        y = _fftconv_gate(y, 0, o == 0, z, (o + 1) * W, conv_w, conv_b,
                          hyena_d[o].reshape(1, W).astype(F32), kr[o], ki[o], tables,
                          BF16 if o == HYENA_ORDER - 1 else F32)
    return y


def _merge_body(yh_ref, ya_ref, g_ref, x_ref, whu_ref, wau_ref, wo_ref, n2_ref, rwh_ref, rwl_ref,
                rb_ref, h_ref, hn_ref, lg_ref):
    D = x_ref.shape[-1]
    up_h = jnp.dot(yh_ref[...], whu_ref[...], preferred_element_type=F32)
    up_a = jnp.dot(ya_ref[...], wau_ref[...], preferred_element_type=F32)
    g = g_ref[...].astype(F32)
    merged = jax.nn.sigmoid(g[:, :D]) * up_h + jax.nn.sigmoid(g[:, D:]) * up_a
    h = x_ref[...] + jnp.dot(merged.astype(BF16), wo_ref[...], preferred_element_type=F32)
    h_ref[...] = h
    ms = jnp.mean(h * h, axis=-1, keepdims=True)
    hn = h * lax.rsqrt(ms + EPS) * n2_ref[...]
    hn_ref[...] = hn.astype(BF16)
    hh, hl = _split_bf16(hn)
    lg_ref[...] = (jnp.dot(hh, rwh_ref[...], preferred_element_type=F32)
                   + jnp.dot(hl, rwh_ref[...], preferred_element_type=F32)
                   + jnp.dot(hh, rwl_ref[...], preferred_element_type=F32)) + rb_ref[...]


def _merge(yh, ya, gates, xt, whu, wau, wo, n2g, rw, rb, tm=1024):
    T, D = xt.shape
    W = yh.shape[1]
    E = rw.shape[1]
    rwp = jnp.zeros((D, LOGIT_PAD), F32).at[:, :E].set(rw)
    rwh, rwl = _split_bf16(rwp)
    rbp = jnp.zeros((1, LOGIT_PAD), F32).at[0, :E].set(rb)
    rowblk = lambda w: pl.BlockSpec((tm, w), lambda i: (i, 0))
    const = lambda shape: pl.BlockSpec(shape, lambda i: (0, 0))
    return pl.pallas_call(
        _merge_body,
        grid=(T // tm,),
        in_specs=[rowblk(W), rowblk(W), rowblk(2 * D), rowblk(D),
                  const((W, D)), const((W, D)), const((D, D)), const((1, D)),
                  const((D, LOGIT_PAD)), const((D, LOGIT_PAD)), const((1, LOGIT_PAD))],
        out_specs=[rowblk(D), rowblk(D), rowblk(LOGIT_PAD)],
        out_shape=[jax.ShapeDtypeStruct((T, D), F32), jax.ShapeDtypeStruct((T, D), BF16),
                   jax.ShapeDtypeStruct((T, LOGIT_PAD), F32)],
        compiler_params=_cparams(("parallel",)),
        name="merge",
    )(yh, ya, gates, xt, whu.astype(BF16), wau.astype(BF16), wo.astype(BF16), n2g.reshape(1, D),
      rwh, rwl, rbp)


def _moe_body(be_ref, nused_ref, x_ref, w1g_ref, w1l_ref, b1g_ref, b1l_ref, w2_ref, b2_ref, o_ref):
    i = pl.program_id(0)

    @pl.when(i < nused_ref[0])
    def _():
        x = x_ref[...]
        nt = (((1,), (1,)), ((), ()))
        glu = lax.dot_general(x, w1g_ref[...], nt, preferred_element_type=F32) + b1g_ref[...]
        lin = lax.dot_general(x, w1l_ref[...], nt, preferred_element_type=F32) + b1l_ref[...]
        glu = jnp.minimum(glu, SWIGLU_LIMIT)
        lin = jnp.clip(lin, -SWIGLU_LIMIT, SWIGLU_LIMIT)
        act = glu * jax.nn.sigmoid(SWIGLU_ALPHA * glu) * (lin + 1.0)
        y = jnp.dot(act.astype(BF16), w2_ref[...].astype(BF16),
                    preferred_element_type=F32) + b2_ref[...]
        o_ref[...] = y.astype(o_ref.dtype)

    @pl.when(i >= nused_ref[0])
    def _():
        o_ref[...] = jnp.zeros(o_ref.shape, o_ref.dtype)


def _moe_experts(xg, block_expert, n_used, w1g, w1l, b1g, b1l, w2, b2):
    P, D = xg.shape
    dff = w2.shape[1]
    nb = P // MOE_TM
    wspec = lambda k, n: pl.BlockSpec((None, k, n), lambda i, be, nu: (be[i], 0, 0))
    grid_spec = pltpu.PrefetchScalarGridSpec(
        num_scalar_prefetch=2,
        grid=(nb,),
        in_specs=[
            pl.BlockSpec((MOE_TM, D), lambda i, be, nu: (i, 0)),
            wspec(dff, D), wspec(dff, D), wspec(1, dff), wspec(1, dff),
            wspec(dff, D), wspec(1, D),
        ],
        out_specs=pl.BlockSpec((MOE_TM, D), lambda i, be, nu: (i, 0)),
    )
    return pl.pallas_call(
        _moe_body,
        grid_spec=grid_spec,
        out_shape=jax.ShapeDtypeStruct((P, D), BF16),
        compiler_params=_cparams(("arbitrary",)),
        name="moe_experts",
    )(block_expert, n_used, xg, w1g, w1l, b1g, b1l, w2, b2)


def _prep_w1_body(w_ref, g_ref, l_ref, t_ref):
    half = g_ref.shape[0]
    wt = w_ref[...].T
    for c in range(t_ref.shape[0]):
        cols = slice(c * LANES, (c + 1) * LANES)
        t_ref[c] = wt[:, cols]
        g_ref[:, cols] = t_ref[c, pl.ds(0, half, stride=2), :].astype(BF16)
        l_ref[:, cols] = t_ref[c, pl.ds(1, half, stride=2), :].astype(BF16)


def _prep_w1(w1, tc=8 * LANES):
    E, D, F2 = w1.shape
    tc = min(tc, F2)
    out = jax.ShapeDtypeStruct((E, F2 // 2, D), BF16)
    return pl.pallas_call(
        _prep_w1_body,
        grid=(E, F2 // tc),
        in_specs=[pl.BlockSpec((None, D, tc), lambda e, j: (e, 0, j))],
        out_specs=[pl.BlockSpec((None, tc // 2, D), lambda e, j: (e, j, 0)),
                   pl.BlockSpec((None, tc // 2, D), lambda e, j: (e, j, 0))],
        out_shape=[out, out],
        scratch_shapes=[pltpu.VMEM((D // LANES, tc, LANES), F32)],
        compiler_params=_cparams(("parallel", "parallel")),
        name="prep_w1",
    )(w1)


def _combine_body(h_ref, y_ref, g_ref, o_ref):
    acc = h_ref[...]
    g = g_ref[...]
    for k in range(y_ref.shape[0]):
        acc = acc + g[:, k:k + 1] * y_ref[k].astype(F32)
    o_ref[...] = acc


def _combine(h, yk, gates, tm=512):
    T, D = h.shape
    K = yk.shape[0]
    return pl.pallas_call(
        _combine_body,
        grid=(T // tm,),
        in_specs=[pl.BlockSpec((tm, D), lambda i: (i, 0)),
                  pl.BlockSpec((K, tm, D), lambda i: (0, i, 0)),
                  pl.BlockSpec((tm, K), lambda i: (i, 0))],
        out_specs=pl.BlockSpec((tm, D), lambda i: (i, 0)),
        out_shape=jax.ShapeDtypeStruct((T, D), F32),
        compiler_params=_cparams(("parallel",)),
        name="moe_combine",
    )(h, yk, gates)


def _lookup(table, idx):
    n = table.shape[0]
    hit = idx[None, :] == jnp.arange(n, dtype=idx.dtype)[:, None]
    return jnp.sum(jnp.where(hit, table[:, None], 0), axis=0)


def _moe(h, hn_bf16, logits, w1, b1, w2, b2):
    T, D = hn_bf16.shape
    E = w1.shape[0]
    top_val, top_idx = lax.top_k(logits, TOP_K)
    gates = jax.nn.softmax(top_val, axis=-1)
    TK = T * TOP_K
    e_flat = top_idx.reshape(TK).astype(jnp.int32)
    order = jnp.argsort(e_flat).astype(jnp.int32)
    rank = jnp.argsort(order).astype(jnp.int32)
    counts = jnp.sum(jnp.arange(E, dtype=jnp.int32)[:, None] == e_flat[None, :], axis=1,
                     dtype=jnp.int32)
    starts = jnp.cumsum(counts) - counts
    padded = ((counts + MOE_TM - 1) // MOE_TM) * MOE_TM
    pad_ends = jnp.cumsum(padded)
    pad_starts = pad_ends - padded
    nb = (TK + E * (MOE_TM - 1) + MOE_TM - 1) // MOE_TM
    block_start = jnp.arange(nb, dtype=jnp.int32) * MOE_TM
    block_expert = jnp.minimum(jnp.sum(block_start[:, None] >= pad_ends[None, :], axis=1),
                               E - 1).astype(jnp.int32)
    n_used = (pad_ends[-1] // MOE_TM).astype(jnp.int32).reshape(1)
    blk_first = block_start - pad_starts[block_expert]
    within = blk_first[:, None] + jnp.arange(MOE_TM, dtype=jnp.int32)[None, :]
    valid = within < counts[block_expert][:, None]
    sorted_idx = jnp.where(valid, starts[block_expert][:, None] + within, 0).reshape(nb * MOE_TM)
    filler = jnp.arange(nb * MOE_TM, dtype=jnp.int32) % T
    src = jnp.where(valid.reshape(nb * MOE_TM), order[sorted_idx] // TOP_K, filler)
    pos = _lookup(pad_starts - starts, e_flat) + rank
    w1g, w1l = _prep_w1(w1)
    y = _moe_experts(hn_bf16[src], block_expert, n_used, w1g, w1l,
                     b1[:, None, 0::2].astype(F32), b1[:, None, 1::2].astype(F32),
                     w2, b2[:, None, :].astype(F32))
    return _combine(h, y[pos.reshape(T, TOP_K).T], gates)


def _rope_tables(L):
    rows = L // GRID_W
    row = jnp.repeat(jnp.arange(rows, dtype=F32), GRID_W)
    col = jnp.tile(jnp.arange(GRID_W, dtype=F32), rows)
    half = HEAD_DIM // 2
    freqs = ROPE_THETA ** (-jnp.arange(0, half, 2, dtype=F32) / half)
    ang = jnp.concatenate([row[:, None] * freqs, col[:, None] * freqs], axis=-1)
    cos = jnp.repeat(jnp.cos(ang), 2, axis=-1)
    sin = jnp.repeat(jnp.sin(ang), 2, axis=-1)
    sign = jnp.tile(jnp.array([-1.0, 1.0], F32), HEAD_DIM // 2)
    reps = LANES // HEAD_DIM
    return jnp.tile(cos, (1, reps)), jnp.tile(sin * sign, (1, reps))


def kernel(x, norm1_g, w_in, conv_w, conv_b, filt_w1, filt_b1, filt_freq1, filt_w2, filt_b2, filt_freq2, filt_w3, filt_b3, hyena_d, q_norm_g, k_norm_g, w_hyena_up, w_attn_up, w_out, norm2_g, router_w, router_b, expert_w1, expert_b1, expert_w2, expert_b2):
    B, L, D = x.shape
    T = B * L
    depth = w_in.shape[0]
    hw = conv_w.shape[-1] // 3
    aw = N_Q_HEADS * HEAD_DIM
    kvw = N_KV_HEADS * HEAD_DIM
    widths = (3 * hw, aw, 2 * kvw, 2 * D)
    cosf, sinf = _rope_tables(L)
    for l in range(depth):
        xt = x.reshape(T, D)
        z, q, kv, gates = _inproj(xt, norm1_g[l], w_in[l].astype(BF16), widths,
                                  (F32, BF16, BF16, BF16))
        y_hy = _hyena(z.reshape(B, L, 3 * hw), conv_w[l], conv_b[l], filt_w1[l], filt_b1[l],
                      filt_freq1[l], filt_w2[l], filt_b2[l], filt_freq2[l], filt_w3[l], filt_b3[l],
                      hyena_d[l])
        gq = jnp.tile(q_norm_g[l].astype(F32), N_Q_HEADS)[None, :]
        gk = jnp.tile(k_norm_g[l].astype(F32), N_KV_HEADS)[None, :]
        qr, kr, va = _qkrope(q, kv, gq, gk, cosf, sinf, L)
        y_at = _attention(qr.reshape(B, L, aw), kr, va.reshape(N_KV_HEADS, B, L, kvw))
        h, hn, logits = _merge(y_hy.reshape(T, hw), y_at.reshape(T, aw), gates, xt,
                               w_hyena_up[l], w_attn_up[l], w_out[l], norm2_g[l],
                               router_w[l], router_b[l])
        x = _moe(h, hn, logits[:, :N_EXPERTS], expert_w1[l], expert_b1[l], expert_w2[l],
                 expert_b2[l]).reshape(B, L, D)
    return x
```

```python
import functools
import math

import jax
import jax.numpy as jnp
from jax import lax
from jax.experimental import pallas as pl
from jax.experimental.pallas import tpu as pltpu

F32 = jnp.float32
BF16 = jnp.bfloat16

GRID_W = 64
HEAD_DIM = 64
N_Q_HEADS = 8
N_KV_HEADS = 2
Q_PER_KV = N_Q_HEADS // N_KV_HEADS
ROPE_THETA = 10000.0
HYENA_ORDER = 2
SHORT_CONV = 3
FILTER_EMB = 33
FAST_DECAY_PCT = 0.3
SLOW_DECAY_PCT = 1.5
DECAY_TARGET = 1e-2
N_EXPERTS = 32
TOP_K = 4
SWIGLU_LIMIT = 7.0
SWIGLU_ALPHA = 1.702
EPS = 1e-6

LANES = 128
SUBLANES = 8
VMEM_LIMIT = 56 * 1024 * 1024

FFT_N2 = LANES
FFT_PITCH = FFT_N2 + SUBLANES
FFT_K1_CHUNK_MAX = 16
FFT_UNROLL_OUTER = 16
FFT_UNROLL_INNER = True
MOE_TM = 512
LOGIT_PAD = LANES


def _cparams(sem):
    return pltpu.CompilerParams(dimension_semantics=sem, vmem_limit_bytes=VMEM_LIMIT)


def _split_bf16(x):
    hi = x.astype(BF16)
    lo = (x - hi.astype(F32)).astype(BF16)
    return hi, lo


def _inproj_body(x_ref, g_ref, w_ref, z_ref, q_ref, kv_ref, gate_ref, *, widths):
    x = x_ref[...]
    ms = jnp.mean(x * x, axis=-1, keepdims=True)
    u = (x * lax.rsqrt(ms + EPS) * g_ref[...]).astype(BF16)
    off = 0
    for ref, w in zip((z_ref, q_ref, kv_ref, gate_ref), widths):
        ref[...] = jnp.dot(u, w_ref[:, off:off + w], preferred_element_type=F32).astype(ref.dtype)
        off += w


def _inproj(xt, g, w_bf16, widths, dtypes, tm=1024):
    T, D = xt.shape
    n = w_bf16.shape[1]
    return pl.pallas_call(
        functools.partial(_inproj_body, widths=widths),
        grid=(T // tm,),
        in_specs=[
            pl.BlockSpec((tm, D), lambda i: (i, 0)),
            pl.BlockSpec((1, D), lambda i: (0, 0)),
            pl.BlockSpec((D, n), lambda i: (0, 0)),
        ],
        out_specs=[pl.BlockSpec((tm, w), lambda i: (i, 0)) for w in widths],
        out_shape=[jax.ShapeDtypeStruct((T, w), dt) for w, dt in zip(widths, dtypes)],
        compiler_params=_cparams(("parallel",)),
        name="inproj",
    )(xt, g.reshape(1, D), w_bf16)


def _head_norm_rope(x, gain, cosf, sinf, ones_blk):
    w = x.shape[-1]
    hi, lo = _split_bf16(x * x)
    ss = (jnp.dot(hi, ones_blk, preferred_element_type=F32)
          + jnp.dot(lo, ones_blk, preferred_element_type=F32))
    xn = x * lax.rsqrt(ss * (1.0 / HEAD_DIM) + EPS) * gain
    lane = lax.broadcasted_iota(jnp.int32, (x.shape[0], LANES), 1)
    cols = []
    for c in range(w // LANES):
        col = xn[:, c * LANES:(c + 1) * LANES]
        cols.append(jnp.where(lane % 2 == 0, pltpu.roll(col, LANES - 1, 1), pltpu.roll(col, 1, 1)))
    swapped = cols[0] if len(cols) == 1 else jnp.concatenate(cols, axis=1)
    return xn * cosf + swapped * sinf


def _qkrope_body(q_ref, kv_ref, gq_ref, gk_ref, cos_ref, sin_ref, oq_ref, ok_ref,
                 qo_ref, ko_ref, vo_ref):
    cosf = cos_ref[...]
    sinf = sin_ref[...]
    nq = q_ref.shape[-1] // LANES
    q = _head_norm_rope(q_ref[...].astype(F32), gq_ref[...], jnp.tile(cosf, (1, nq)),
                        jnp.tile(sinf, (1, nq)), oq_ref[...])
    qo_ref[...] = (q * (HEAD_DIM ** -0.5 * math.log2(math.e))).astype(BF16)
    kv = kv_ref[...].astype(F32)
    kw = kv.shape[-1] // 2
    k = _head_norm_rope(kv[:, :kw], gk_ref[...], cosf, sinf, ok_ref[...])
    ko_ref[...] = k.T.astype(BF16)
    v = kv[:, kw:]
    lane = lax.broadcasted_iota(jnp.int32, v.shape, 1)
    for h in range(N_KV_HEADS):
        vo_ref[h] = jnp.where((lane // HEAD_DIM) == h, v, 1.0).astype(BF16)


def _qkrope(q, kv, gq, gk, cosf, sinf, seq_len, tm=512):
    T, qw = q.shape
    kw = kv.shape[1] // 2
    assert kw == LANES and N_KV_HEADS * HEAD_DIM == LANES and N_KV_HEADS == 2
    nl = seq_len // tm

    def blk_ones(w):
        r = jnp.arange(w) // HEAD_DIM
        return (r[:, None] == r[None, :]).astype(BF16)

    return pl.pallas_call(
        _qkrope_body,
        grid=(T // tm,),
        in_specs=[
            pl.BlockSpec((tm, qw), lambda i: (i, 0)),
            pl.BlockSpec((tm, 2 * kw), lambda i: (i, 0)),
            pl.BlockSpec((1, qw), lambda i: (0, 0)),
            pl.BlockSpec((1, kw), lambda i: (0, 0)),
            pl.BlockSpec((tm, LANES), lambda i: (i % nl, 0)),
            pl.BlockSpec((tm, LANES), lambda i: (i % nl, 0)),
            pl.BlockSpec((qw, qw), lambda i: (0, 0)),
            pl.BlockSpec((kw, kw), lambda i: (0, 0)),
        ],
        out_specs=[
            pl.BlockSpec((tm, qw), lambda i: (i, 0)),
            pl.BlockSpec((kw, tm), lambda i: (0, i)),
            pl.BlockSpec((N_KV_HEADS, tm, kw), lambda i: (0, i, 0)),
        ],
        out_shape=[
            jax.ShapeDtypeStruct((T, qw), BF16),
            jax.ShapeDtypeStruct((kw, T), BF16),
            jax.ShapeDtypeStruct((N_KV_HEADS, T, kw), BF16),
        ],
        compiler_params=_cparams(("parallel",)),
        name="qkrope",
    )(q, kv, gq, gk, cosf, sinf, blk_ones(qw), blk_ones(kw))


def _attn_body(q_ref, k_ref, v_ref, o_ref, qs_ref, m_ref, acc_ref, *, tk, nsplit):
    kvh = pl.program_id(1)
    tq = q_ref.shape[0]
    seq = k_ref.shape[1]
    rows = Q_PER_KV * tq
    lane = lax.broadcasted_iota(jnp.int32, (tq, LANES), 1)
    in_head = (lane // HEAD_DIM) == kvh

    for g in range(Q_PER_KV):
        col = q_ref[:, (g // 2) * LANES:(g // 2 + 1) * LANES].astype(F32)
        col = jnp.where((g % 2) == kvh, col, pltpu.roll(col, HEAD_DIM, 1))
        qs_ref[g * tq:(g + 1) * tq, :] = jnp.where(in_head, col, 0.0).astype(BF16)
    m_ref[...] = jnp.full(m_ref.shape, -jnp.inf, F32)
    acc_ref[...] = jnp.zeros(acc_ref.shape, F32)
    part = rows // nsplit

    def step(c, carry):
        r0 = pl.multiple_of(c * tk, tk)
        kc = k_ref[:, pl.ds(r0, tk)]
        vc = v_ref[pl.ds(r0, tk), :]
        for h in range(nsplit):
            sl = slice(h * part, (h + 1) * part)
            s = jnp.dot(qs_ref[sl, :], kc, preferred_element_type=F32)
            m_prev = m_ref[sl, :]
            m_new = jnp.maximum(m_prev, jnp.max(s, axis=-1, keepdims=True))
            p = jnp.exp2(s - jnp.tile(m_new, (1, tk // LANES)))
            acc_ref[sl, :] = jnp.exp2(m_prev - m_new) * acc_ref[sl, :] + jnp.dot(
                p.astype(BF16), vc, preferred_element_type=F32)
            m_ref[sl, :] = m_new
        return carry

    lax.fori_loop(0, seq // tk, step, 0)

    acc = acc_ref[...]
    o = acc / pltpu.roll(acc, HEAD_DIM, 1)
    for c in range(Q_PER_KV // 2):
        even = o[(2 * c) * tq:(2 * c + 1) * tq, :]
        odd = o[(2 * c + 1) * tq:(2 * c + 2) * tq, :]
        even = jnp.where(kvh == 0, even, pltpu.roll(even, HEAD_DIM, 1))
        odd = jnp.where(kvh == 1, odd, pltpu.roll(odd, HEAD_DIM, 1))
        o_ref[:, c * LANES:(c + 1) * LANES] = jnp.where(lane < HEAD_DIM, even, odd).astype(o_ref.dtype)


def _attention(q, kt, v_aug, tq=512, tk=2048, nsplit=2):
    B, L, qw = q.shape
    gw = Q_PER_KV * HEAD_DIM
    kw = kt.shape[0]
    rows = Q_PER_KV * tq
    return pl.pallas_call(
        functools.partial(_attn_body, tk=tk, nsplit=nsplit),
        grid=(B, N_KV_HEADS, L // tq),
        in_specs=[
            pl.BlockSpec((None, tq, gw), lambda b, h, i: (b, i, h)),
            pl.BlockSpec((kw, L), lambda b, h, i: (0, b)),
            pl.BlockSpec((None, None, L, kw), lambda b, h, i: (h, b, 0, 0)),
        ],
        out_specs=pl.BlockSpec((None, tq, gw), lambda b, h, i: (b, i, h)),
        out_shape=jax.ShapeDtypeStruct((B, L, qw), BF16),
        scratch_shapes=[
            pltpu.VMEM((rows, kw), BF16),
            pltpu.VMEM((rows, LANES), F32),
            pltpu.VMEM((rows, kw), F32),
        ],
        compiler_params=_cparams(("parallel", "parallel", "parallel")),
        name="attention",
    )(q, kt, v_aug)


def _hdot(a, b):
    ah, al = _split_bf16(a)
    bh, bl = _split_bf16(b)
    return (jnp.dot(ah, bh, preferred_element_type=F32)
            + jnp.dot(al, bh, preferred_element_type=F32)
            + jnp.dot(ah, bl, preferred_element_type=F32))


def _filter_body(feat_ref, t_ref, w1_ref, b1_ref, f1_ref, w2_ref, b2_ref, f2_ref, w3_ref, b3_ref,
                 delta_ref, bwd_ref, h_ref, sum_ref):
    i = pl.program_id(0)
    h = jnp.sin(f1_ref[...] * (_hdot(feat_ref[...], w1_ref[...]) + b1_ref[...]))
    h = jnp.sin(f2_ref[...] * (_hdot(h, w2_ref[...]) + b2_ref[...]))
    h = _hdot(h, w3_ref[...]) + b3_ref[...]
    t = t_ref[...]
    h = h * jnp.exp(-t * delta_ref[...])
    width = h_ref.shape[-1]
    for j in range(h_ref.shape[0]):
        h_ref[j] = h[:, j * width:(j + 1) * width]
    row = lax.broadcasted_iota(jnp.int32, h.shape, 0) + i * h.shape[0]
    a = jnp.where((row == 0) & (bwd_ref[...] > 0.5), 0.0, jnp.abs(h))
    part = jnp.sum(a, axis=0, keepdims=True)

    @pl.when(i == 0)
    def _():
        sum_ref[...] = jnp.zeros(sum_ref.shape, F32)

    sum_ref[...] += jnp.broadcast_to(part, sum_ref.shape)


def _hyena_filter(L, width, fw1, fb1, ff1, fw2, fb2, ff2, fw3, fb3, tl=512):
    bands = (FILTER_EMB - 1) // 2
    t = jnp.linspace(0.0, 1.0, L, dtype=F32)[:, None]
    w = 2.0 * math.pi * jnp.arange(L, dtype=F32)[:, None] / L
    fr = jnp.linspace(1e-4, bands - 1, bands, dtype=F32)[None, :]
    feats = jnp.concatenate([t, jnp.cos(w * fr), -jnp.sin(w * fr)], axis=-1)
    max_decay = math.log(DECAY_TARGET) / FAST_DECAY_PCT
    min_decay = math.log(DECAY_TARGET) / SLOW_DECAY_PCT
    deltas = jnp.abs(jnp.linspace(min_decay, max_decay, width, dtype=F32))
    ncol = fw3.shape[1]
    delta_cols = jnp.tile(deltas, ncol // width)[None, :]
    is_bwd = ((jnp.arange(ncol) // width) % 2).astype(F32)[None, :]
    emb = hid = LANES

    def pad2(a, r, c):
        a = a.astype(F32)
        return jnp.zeros((r, c), F32).at[:a.shape[0], :a.shape[1]].set(a)

    row = lambda a: pad2(a.reshape(1, -1), 1, hid)
    feats = pad2(feats, L, emb)
    fw1, fw2, fw3 = pad2(fw1, emb, hid), pad2(fw2, hid, hid), pad2(fw3, hid, ncol)
    fb3 = fb3.reshape(1, ncol).astype(F32)
    const = lambda shape: pl.BlockSpec(shape, lambda i: (0, 0))
    h, sums = pl.pallas_call(
        _filter_body,
        grid=(L // tl,),
        in_specs=[
            pl.BlockSpec((tl, emb), lambda i: (i, 0)),
            pl.BlockSpec((tl, 1), lambda i: (i, 0)),
            const((emb, hid)), const((1, hid)), const((1, hid)),
            const((hid, hid)), const((1, hid)), const((1, hid)),
            const((hid, ncol)), const((1, ncol)), const((1, ncol)), const((1, ncol)),
        ],
        out_specs=[pl.BlockSpec((ncol // width, tl, width), lambda i: (0, i, 0)),
                   pl.BlockSpec((SUBLANES, ncol), lambda i: (0, 0))],
        out_shape=[jax.ShapeDtypeStruct((ncol // width, L, width), F32),
                   jax.ShapeDtypeStruct((SUBLANES, ncol), F32)],
        compiler_params=_cparams(("arbitrary",)),
        name="hyena_filter",
    )(feats, t, fw1, row(fb1), row(ff1), fw2, row(fb2), row(ff2), fw3, fb3, delta_cols, is_bwd)
    return h, sums[0]


def _shortconv_rows(z_ref, w_ref, b_ref, c, chunk):
    L = z_ref.shape[0]
    row = lax.broadcasted_iota(jnp.int32, (chunk, z_ref.shape[1]), 0)
    r0 = pl.multiple_of(c * chunk, chunk)
    cur = z_ref[pl.ds(r0, chunk), :]
    prev_row = z_ref[pl.ds(jnp.maximum(r0 - 1, 0), 1), :]
    next_row = z_ref[pl.ds(jnp.minimum(r0 + chunk, L - 1), 1), :]
    prev_row = jnp.where(c == 0, 0.0, prev_row)
    next_row = jnp.where(c == L // chunk - 1, 0.0, next_row)
    down = jnp.where(row == 0, prev_row, pltpu.roll(cur, 1, 0))
    up = jnp.where(row == chunk - 1, next_row, pltpu.roll(cur, chunk - 1, 0))
    return b_ref[...] + down * w_ref[0:1, :] + cur * w_ref[1:2, :] + up * w_ref[2:3, :]


def _dft_tables(n1_len, n2_len):
    n = n1_len * n2_len

    def root(num, den):
        ang = (2.0 * math.pi / den) * (num % den).astype(F32)
        return jnp.cos(ang), -jnp.sin(ang)

    i1 = jnp.arange(n1_len, dtype=jnp.int32)
    i2 = jnp.arange(n2_len, dtype=jnp.int32)
    f1r, f1i = root(i1[:, None] * i1[None, :], n1_len)
    f2r, f2i = root(i2[:, None] * i2[None, :], n2_len)
    twr, twi = root(i1[:, None] * i2[None, :], n)
    return (f1r, f1i), (f2r, f2i), (twr, twi)


def _stacked_inner_dft(f2r, f2i, twr_row, twi_row):
    gr = f2r * twr_row - f2i * twi_row
    gi = f2r * twi_row + f2i * twr_row
    top = jnp.concatenate([gr, -gi], axis=1)
    bot = jnp.concatenate([gi, gr], axis=1)
    return jnp.concatenate([top, bot], axis=0)


def _fft_plan(n):
    n1_len = n // FFT_N2
    nk1 = n1_len // 2 + 1
    nk1_pad = -(-nk1 // SUBLANES) * SUBLANES
    chunk = max(c for c in range(1, FFT_K1_CHUNK_MAX + 1) if nk1 % c == 0)
    return n1_len, nk1, nk1_pad, chunk


def _outer_dft_to_scratch(load_rows, fa_ref, ar_ref, ai_ref, nk1_pad):
    fa = fa_ref[...]

    def step(n2, carry):
        a = jnp.dot(fa, load_rows(n2).astype(BF16), preferred_element_type=F32)
        ar_ref[pl.ds(n2, nk1_pad, stride=FFT_PITCH), :] = a[:nk1_pad]
        ai_ref[pl.ds(n2, nk1_pad, stride=FFT_PITCH), :] = a[nk1_pad:]
        return carry

    lax.fori_loop(0, FFT_N2, step, 0, unroll=FFT_UNROLL_OUTER)


def _spec_body(f_ref, b_ref, inv_ref, fa_ref, f2r_ref, f2i_ref, twr_ref, twi_ref, kr_ref, ki_ref,
               fr_ref, fi_ref, br_ref, bi_ref, *, n1_len, nk1_pad, chunk):
    kc = pl.program_id(2)
    half = n1_len // 2

    @pl.when(kc == 0)
    def _():
        inv = inv_ref[...]
        row = lax.broadcasted_iota(jnp.int32, (half, LANES), 0)
        _outer_dft_to_scratch(lambda n2: f_ref[pl.ds(n2, half, stride=FFT_N2), :] * inv,
                              fa_ref, fr_ref, fi_ref, nk1_pad)
        _outer_dft_to_scratch(
            lambda n2: jnp.where((row == 0) & (n2 == 0), 0.0,
                                 b_ref[pl.ds(n2, half, stride=FFT_N2), :] * inv),
            fa_ref, br_ref, bi_ref, nk1_pad)

    f2r = f2r_ref[...]
    f2i = f2i_ref[...]

    def step(t, carry):
        k1 = kc * chunk + t
        base = pl.multiple_of(k1 * FFT_PITCH, SUBLANES)
        mf = _stacked_inner_dft(f2r, f2i, twr_ref[pl.ds(k1, 1), :],
                                twi_ref[pl.ds(k1, 1), :]).astype(BF16)
        rows = pl.ds(base, FFT_N2)
        xf = jnp.dot(mf, jnp.concatenate([fr_ref[rows, :], fi_ref[rows, :]], axis=0).astype(BF16),
                     preferred_element_type=F32)
        xb = jnp.dot(mf, jnp.concatenate([br_ref[rows, :], bi_ref[rows, :]], axis=0).astype(BF16),
                     preferred_element_type=F32)
        o = pl.multiple_of(t * FFT_N2, FFT_N2)
        kr_ref[pl.ds(o, FFT_N2), :] = xf[:FFT_N2] + xb[:FFT_N2]
        ki_ref[pl.ds(o, FFT_N2), :] = xf[FFT_N2:] - xb[FFT_N2:]
        return carry

    lax.fori_loop(0, chunk, step, 0, unroll=FFT_UNROLL_INNER)


def _filter_spectrum(h, inv_norm, tables):
    O2, L, C = h.shape
    O = O2 // 2
    n1_len, nk1, nk1_pad, chunk = _fft_plan(2 * L)
    half = n1_len // 2
    (f1r, f1i), (f2r, f2i), (twr, twi) = tables
    fa = jnp.concatenate([f1r[:nk1_pad, :half], f1i[:nk1_pad, :half]], axis=0).astype(BF16)
    rows = chunk * FFT_N2
    const = lambda shape: pl.BlockSpec(shape, lambda o, c, k: (0, 0))
    out = jax.ShapeDtypeStruct((O, nk1 * FFT_N2, C), F32)
    scratch = pltpu.VMEM((nk1_pad * FFT_PITCH, LANES), F32)
    return pl.pallas_call(
        functools.partial(_spec_body, n1_len=n1_len, nk1_pad=nk1_pad, chunk=chunk),
        grid=(O, C // LANES, nk1 // chunk),
        in_specs=[
            pl.BlockSpec((None, L, LANES), lambda o, c, k: (2 * o, 0, c)),
            pl.BlockSpec((None, L, LANES), lambda o, c, k: (2 * o + 1, 0, c)),
            pl.BlockSpec((None, 1, LANES), lambda o, c, k: (o, 0, c)),
            const(fa.shape), const(f2r.shape), const(f2i.shape), const(twr.shape), const(twi.shape),
        ],
        out_specs=[pl.BlockSpec((None, rows, LANES), lambda o, c, k: (o, k, c)),
                   pl.BlockSpec((None, rows, LANES), lambda o, c, k: (o, k, c))],
        out_shape=[out, out],
        scratch_shapes=[scratch, scratch, scratch, scratch],
        compiler_params=_cparams(("parallel", "parallel", "arbitrary")),
        name="filter_spectrum",
    )(h, h, inv_norm, fa, f2r, f2i, twr, twi)


def _fftconv_body(y_ref, gate_ref, wy_ref, by_ref, wg_ref, bg_ref, d_ref, kr_ref, ki_ref, fa_ref,
                  fs_ref, f2r_ref, f2i_ref, twr_ref, twi_ref, o_ref, xs_ref, ar_ref, ai_ref, *,
                  n1_len, nk1_pad, chunk, conv_y):
    kc = pl.program_id(2)
    half = n1_len // 2

    @pl.when(kc == 0)
    def _():
        def copy(n1, carry):
            src = pl.multiple_of(n1 * FFT_N2, FFT_N2)
            dst = pl.multiple_of(n1 * FFT_PITCH, SUBLANES)
            if conv_y:
                xs_ref[pl.ds(dst, FFT_N2), :] = _shortconv_rows(y_ref, wy_ref, by_ref, n1, FFT_N2)
            else:
                xs_ref[pl.ds(dst, FFT_N2), :] = y_ref[pl.ds(src, FFT_N2), :]
            return carry

        lax.fori_loop(0, half, copy, 0)
        _outer_dft_to_scratch(lambda n2: xs_ref[pl.ds(n2, half, stride=FFT_PITCH), :],
                              fa_ref, ar_ref, ai_ref, nk1_pad)

    f2r = f2r_ref[...]
    f2i = f2i_ref[...]

    def step(t, carry):
        k1 = kc * chunk + t
        base = pl.multiple_of(k1 * FFT_PITCH, SUBLANES)
        rhs = jnp.concatenate([ar_ref[pl.ds(base, FFT_N2), :], ai_ref[pl.ds(base, FFT_N2), :]],
                              axis=0).astype(BF16)
        mf = _stacked_inner_dft(f2r, f2i, twr_ref[pl.ds(k1, 1), :], twi_ref[pl.ds(k1, 1), :])
        x = jnp.dot(mf.astype(BF16), rhs, preferred_element_type=F32)
        xr, xi = x[:FFT_N2], x[FFT_N2:]
        o = pl.multiple_of(t * FFT_N2, FFT_N2)
        kr = kr_ref[pl.ds(o, FFT_N2), :]
        ki = ki_ref[pl.ds(o, FFT_N2), :]
        z = jnp.concatenate([xr * kr - xi * ki, xr * ki + xi * kr], axis=0).astype(BF16)
        b = jnp.dot(mf.T.astype(BF16), z, preferred_element_type=F32)
        ar_ref[pl.ds(base, FFT_N2), :] = b[:FFT_N2]
        ai_ref[pl.ds(base, FFT_N2), :] = b[FFT_N2:]
        return carry

    lax.fori_loop(0, chunk, step, 0, unroll=FFT_UNROLL_INNER)

    @pl.when(kc == pl.num_programs(2) - 1)
    def _():
        fs = fs_ref[...]

        def inv_outer(n2, carry):
            rhs = jnp.concatenate([ar_ref[pl.ds(n2, nk1_pad, stride=FFT_PITCH), :],
                                   ai_ref[pl.ds(n2, nk1_pad, stride=FFT_PITCH), :]],
                                  axis=0).astype(BF16)
            conv = jnp.dot(fs, rhs, preferred_element_type=F32)
            ar_ref[pl.ds(n2, half, stride=FFT_PITCH), :] = conv
            return carry

        lax.fori_loop(0, FFT_N2, inv_outer, 0, unroll=FFT_UNROLL_OUTER)
        d = d_ref[...]

        def finish(n1, carry):
            src = pl.multiple_of(n1 * FFT_PITCH, SUBLANES)
            dst = pl.multiple_of(n1 * FFT_N2, FFT_N2)
            y = xs_ref[pl.ds(src, FFT_N2), :]
            gate = _shortconv_rows(gate_ref, wg_ref, bg_ref, n1, FFT_N2)
            o_ref[pl.ds(dst, FFT_N2), :] = (gate * (
                ar_ref[pl.ds(src, FFT_N2), :] + y * d)).astype(o_ref.dtype)
            return carry

        lax.fori_loop(0, half, finish, 0)


def _fftconv_gate(y, y_off, conv_y, z, gate_off, conv_w, conv_b, d, kr, ki, tables, out_dtype):
    B, L, _ = y.shape
    C = d.shape[-1]
    yo, go = y_off // LANES, gate_off // LANES
    wo = yo if conv_y else go
    N = 2 * L
    n1_len, nk1, nk1_pad, chunk = _fft_plan(N)
    half = n1_len // 2
    (f1r, f1i), (f2r, f2i), (twr, twi) = tables
    fa = jnp.concatenate([f1r[:nk1_pad, :half], f1i[:nk1_pad, :half]], axis=0).astype(BF16)
    wts = jnp.concatenate([jnp.ones((1,), F32), jnp.full((nk1 - 2,), 2.0, F32), jnp.ones((1,), F32),
                           jnp.zeros((nk1_pad - nk1,), F32)]) * (1.0 / N)
    fs = jnp.concatenate([f1r[:half, :nk1_pad] * wts, f1i[:half, :nk1_pad] * wts],
                         axis=1).astype(BF16)
    rows = chunk * FFT_N2
    const = lambda shape: pl.BlockSpec(shape, lambda c, b, k: (0, 0))
    return pl.pallas_call(
        functools.partial(_fftconv_body, n1_len=n1_len, nk1_pad=nk1_pad, chunk=chunk,
                          conv_y=conv_y),
        grid=(C // LANES, B, nk1 // chunk),
        in_specs=[
            pl.BlockSpec((None, L, LANES), lambda c, b, k: (b, 0, c + yo)),
            pl.BlockSpec((None, L, LANES), lambda c, b, k: (b, 0, c + go)),
            pl.BlockSpec((SHORT_CONV, LANES), lambda c, b, k: (0, c + wo)),
            pl.BlockSpec((1, LANES), lambda c, b, k: (0, c + wo)),
            pl.BlockSpec((SHORT_CONV, LANES), lambda c, b, k: (0, c + go)),
            pl.BlockSpec((1, LANES), lambda c, b, k: (0, c + go)),
            pl.BlockSpec((1, LANES), lambda c, b, k: (0, c)),
            pl.BlockSpec((rows, LANES), lambda c, b, k: (k, c)),
            pl.BlockSpec((rows, LANES), lambda c, b, k: (k, c)),
            const(fa.shape), const(fs.shape), const(f2r.shape), const(f2i.shape),
            const(twr.shape), const(twi.shape),
        ],
        out_specs=pl.BlockSpec((None, L, LANES), lambda c, b, k: (b, 0, c)),
        out_shape=jax.ShapeDtypeStruct((B, L, C), out_dtype),
        scratch_shapes=[pltpu.VMEM((half * FFT_PITCH, LANES), F32),
                        pltpu.VMEM((nk1_pad * FFT_PITCH, LANES), F32),
                        pltpu.VMEM((nk1_pad * FFT_PITCH, LANES), F32)],
        compiler_params=_cparams(("parallel", "parallel", "arbitrary")),
        name="fftconv_gate",
    )(y, z, conv_w, conv_b, conv_w, conv_b, d, kr, ki, fa, fs, f2r, f2i, twr, twi)


def _hyena(z, conv_w, conv_b, fw1, fb1, ff1, fw2, fb2, ff2, fw3, fb3, hyena_d):
    B, L, C3 = z.shape
    W = C3 // 3
    h, sums = _hyena_filter(L, W, fw1, fb1, ff1, fw2, fb2, ff2, fw3, fb3)
    sums = sums.reshape(HYENA_ORDER, 2, W)
    inv_norm = (1.0 / (sums[:, 0] + sums[:, 1]))[:, None, :]
    tables = _dft_tables(2 * L // FFT_N2, FFT_N2)
    kr, ki = _filter_spectrum(h, inv_norm, tables)
    conv_b = conv_b.reshape(1, C3)
    y = z
    for o in range(HYENA_ORDER):
        y = _fftconv_gate(y, 0, o == 0, z, (o + 1) * W, conv_w, conv_b,
                          hyena_d[o].reshape(1, W).astype(F32), kr[o], ki[o], tables,
                          BF16 if o == HYENA_ORDER - 1 else F32)
    return y


def _route(logits, tri, carry, n_experts):
    lane = lax.broadcasted_iota(jnp.int32, logits.shape, 1)
    lane_f = lane.astype(F32)
    work = jnp.where(lane < n_experts, logits, -jnp.inf)
    vals, ids = [], []
    onehot = jnp.zeros(logits.shape, F32)
    for _ in range(TOP_K):
        top = jnp.max(work, axis=-1, keepdims=True)
        idx = jnp.min(jnp.where(work == top, lane_f, float(LANES)), axis=-1, keepdims=True)
        sel = lane_f == idx
        vals.append(top)
        ids.append(idx)
        onehot = jnp.where(sel, 1.0, onehot)
        work = jnp.where(sel, -jnp.inf, work)
    exps = [jnp.exp(v - vals[0]) for v in vals]
    denom = functools.reduce(lambda a, b: a + b, exps)
    before = jnp.dot(tri, onehot.astype(BF16), preferred_element_type=F32) + carry
    record = jnp.zeros(logits.shape, F32)
    for k in range(TOP_K):
        rank = jnp.sum(jnp.where(lane_f == ids[k], before, 0.0), axis=-1, keepdims=True)
        record = jnp.where(lane == k, ids[k], record)
        record = jnp.where(lane == TOP_K + k, rank, record)
        record = jnp.where(lane == 2 * TOP_K + k, exps[k] / denom, record)
    return record, carry + jnp.sum(onehot, axis=0, keepdims=True)


def _merge_body(yh_ref, ya_ref, g_ref, x_ref, whu_ref, wau_ref, wo_ref, n2_ref, rwh_ref, rwl_ref,
                rb_ref, tri_ref, h_ref, hn_ref, route_ref, count_ref, *, n_experts):
    D = x_ref.shape[-1]

    @pl.when(pl.program_id(0) == 0)
    def _():
        count_ref[...] = jnp.zeros(count_ref.shape, F32)

    up_h = jnp.dot(yh_ref[...], whu_ref[...], preferred_element_type=F32)
    up_a = jnp.dot(ya_ref[...], wau_ref[...], preferred_element_type=F32)
    g = g_ref[...].astype(F32)
    merged = jax.nn.sigmoid(g[:, :D]) * up_h + jax.nn.sigmoid(g[:, D:]) * up_a
    h = x_ref[...] + jnp.dot(merged.astype(BF16), wo_ref[...], preferred_element_type=F32)
    h_ref[...] = h
    ms = jnp.mean(h * h, axis=-1, keepdims=True)
    hn = h * lax.rsqrt(ms + EPS) * n2_ref[...]
    hn_ref[...] = hn.astype(BF16)
    hh, hl = _split_bf16(hn)
    logits = (jnp.dot(hh, rwh_ref[...], preferred_element_type=F32)
              + jnp.dot(hl, rwh_ref[...], preferred_element_type=F32)
              + jnp.dot(hh, rwl_ref[...], preferred_element_type=F32)) + rb_ref[...]
    record, count = _route(logits, tri_ref[...], count_ref[0:1, :], n_experts)
    route_ref[...] = record
    count_ref[...] = jnp.broadcast_to(count, count_ref.shape)


def _merge(yh, ya, gates, xt, whu, wau, wo, n2g, rw, rb, tm=1024):
    T, D = xt.shape
    W = yh.shape[1]
    E = rw.shape[1]
    assert E <= LOGIT_PAD and 3 * TOP_K <= LOGIT_PAD
    rwp = jnp.zeros((D, LOGIT_PAD), F32).at[:, :E].set(rw)
    rwh, rwl = _split_bf16(rwp)
    rbp = jnp.zeros((1, LOGIT_PAD), F32).at[0, :E].set(rb)
    tri = jnp.tri(tm, k=-1, dtype=BF16)
    rowblk = lambda w: pl.BlockSpec((tm, w), lambda i: (i, 0))
    const = lambda shape: pl.BlockSpec(shape, lambda i: (0, 0))
    return pl.pallas_call(
        functools.partial(_merge_body, n_experts=E),
        grid=(T // tm,),
        in_specs=[rowblk(W), rowblk(W), rowblk(2 * D), rowblk(D),
                  const((W, D)), const((W, D)), const((D, D)), const((1, D)),
                  const((D, LOGIT_PAD)), const((D, LOGIT_PAD)), const((1, LOGIT_PAD)),
                  const((tm, tm))],
        out_specs=[rowblk(D), rowblk(D), rowblk(LOGIT_PAD), const((SUBLANES, LOGIT_PAD))],
        out_shape=[jax.ShapeDtypeStruct((T, D), F32), jax.ShapeDtypeStruct((T, D), BF16),
                   jax.ShapeDtypeStruct((T, LOGIT_PAD), F32),
                   jax.ShapeDtypeStruct((SUBLANES, LOGIT_PAD), F32)],
        compiler_params=_cparams(("arbitrary",)),
        name="merge",
    )(yh, ya, gates, xt, whu.astype(BF16), wau.astype(BF16), wo.astype(BF16), n2g.reshape(1, D),
      rwh, rwl, rbp, tri)


def _moe_body(be_ref, nused_ref, x_ref, w1g_ref, w1l_ref, b1g_ref, b1l_ref, w2_ref, b2_ref, o_ref):
    i = pl.program_id(0)

    @pl.when(i < nused_ref[0])
    def _():
        x = x_ref[...]
        nt = (((1,), (1,)), ((), ()))
        glu = lax.dot_general(x, w1g_ref[...], nt, preferred_element_type=F32) + b1g_ref[...]
        lin = lax.dot_general(x, w1l_ref[...], nt, preferred_element_type=F32) + b1l_ref[...]
        glu = jnp.minimum(glu, SWIGLU_LIMIT)
        lin = jnp.clip(lin, -SWIGLU_LIMIT, SWIGLU_LIMIT)
        act = glu * jax.nn.sigmoid(SWIGLU_ALPHA * glu) * (lin + 1.0)
        y = jnp.dot(act.astype(BF16), w2_ref[...].astype(BF16),
                    preferred_element_type=F32) + b2_ref[...]
        o_ref[...] = y.astype(o_ref.dtype)

    @pl.when(i >= nused_ref[0])
    def _():
        o_ref[...] = jnp.zeros(o_ref.shape, o_ref.dtype)


def _moe_experts(xg, block_expert, n_used, w1g, w1l, b1g, b1l, w2, b2):
    P, D = xg.shape
    dff = w2.shape[1]
    nb = P // MOE_TM
    wspec = lambda k, n: pl.BlockSpec((None, k, n), lambda i, be, nu: (be[i], 0, 0))
    grid_spec = pltpu.PrefetchScalarGridSpec(
        num_scalar_prefetch=2,
        grid=(nb,),
        in_specs=[
            pl.BlockSpec((MOE_TM, D), lambda i, be, nu: (i, 0)),
            wspec(dff, D), wspec(dff, D), wspec(1, dff), wspec(1, dff),
            wspec(dff, D), wspec(1, D),
        ],
        out_specs=pl.BlockSpec((MOE_TM, D), lambda i, be, nu: (i, 0)),
    )
    return pl.pallas_call(
        _moe_body,
        grid_spec=grid_spec,
        out_shape=jax.ShapeDtypeStruct((P, D), BF16),
        compiler_params=_cparams(("arbitrary",)),
        name="moe_experts",
    )(block_expert, n_used, xg, w1g, w1l, b1g, b1l, w2, b2)


def _prep_w1_body(w_ref, g_ref, l_ref, t_ref):
    half = g_ref.shape[0]
    wt = w_ref[...].T
    for c in range(t_ref.shape[0]):
        cols = slice(c * LANES, (c + 1) * LANES)
        t_ref[c] = wt[:, cols]
        g_ref[:, cols] = t_ref[c, pl.ds(0, half, stride=2), :].astype(BF16)
        l_ref[:, cols] = t_ref[c, pl.ds(1, half, stride=2), :].astype(BF16)


def _prep_w1(w1, tc=8 * LANES):
    E, D, F2 = w1.shape
    tc = min(tc, F2)
    out = jax.ShapeDtypeStruct((E, F2 // 2, D), BF16)
    return pl.pallas_call(
        _prep_w1_body,
        grid=(E, F2 // tc),
        in_specs=[pl.BlockSpec((None, D, tc), lambda e, j: (e, 0, j))],
        out_specs=[pl.BlockSpec((None, tc // 2, D), lambda e, j: (e, j, 0)),
                   pl.BlockSpec((None, tc // 2, D), lambda e, j: (e, j, 0))],
        out_shape=[out, out],
        scratch_shapes=[pltpu.VMEM((D // LANES, tc, LANES), F32)],
        compiler_params=_cparams(("parallel", "parallel")),
        name="prep_w1",
    )(w1)


def _combine_body(h_ref, y_ref, g_ref, o_ref):
    acc = h_ref[...]
    g = g_ref[...]
    for k in range(y_ref.shape[0]):
        acc = acc + g[:, k:k + 1] * y_ref[k].astype(F32)
    o_ref[...] = acc


def _combine(h, yk, gates, tm=512):
    T, D = h.shape
    K = yk.shape[0]
    return pl.pallas_call(
        _combine_body,
        grid=(T // tm,),
        in_specs=[pl.BlockSpec((tm, D), lambda i: (i, 0)),
                  pl.BlockSpec((K, tm, D), lambda i: (0, i, 0)),
                  pl.BlockSpec((tm, K), lambda i: (i, 0))],
        out_specs=pl.BlockSpec((tm, D), lambda i: (i, 0)),
        out_shape=jax.ShapeDtypeStruct((T, D), F32),
        compiler_params=_cparams(("parallel",)),
        name="moe_combine",
    )(h, yk, gates)


def _lookup(table, idx):
    n = table.shape[0]
    hit = idx[None, :] == jnp.arange(n, dtype=idx.dtype)[:, None]
    return jnp.sum(jnp.where(hit, table[:, None], 0), axis=0)


def _moe(h, hn_bf16, route, count, w1, b1, w2, b2):
    T, D = hn_bf16.shape
    E = w1.shape[0]
    TK = T * TOP_K
    e_flat = route[:, :TOP_K].astype(jnp.int32).reshape(TK)
    in_expert = route[:, TOP_K:2 * TOP_K].astype(jnp.int32).reshape(TK)
    gates = route[:, 2 * TOP_K:3 * TOP_K]
    counts = count[0, :E].astype(jnp.int32)
    order = jnp.argsort(e_flat).astype(jnp.int32)
    starts = jnp.cumsum(counts) - counts
    padded = ((counts + MOE_TM - 1) // MOE_TM) * MOE_TM
    pad_ends = jnp.cumsum(padded)
    pad_starts = pad_ends - padded
    nb = (TK + E * (MOE_TM - 1) + MOE_TM - 1) // MOE_TM
    block_start = jnp.arange(nb, dtype=jnp.int32) * MOE_TM
    block_expert = jnp.minimum(jnp.sum(block_start[:, None] >= pad_ends[None, :], axis=1),
                               E - 1).astype(jnp.int32)
    n_used = (pad_ends[-1] // MOE_TM).astype(jnp.int32).reshape(1)
    blk_first = block_start - pad_starts[block_expert]
    within = blk_first[:, None] + jnp.arange(MOE_TM, dtype=jnp.int32)[None, :]
    valid = within < counts[block_expert][:, None]
    sorted_idx = jnp.where(valid, starts[block_expert][:, None] + within, 0).reshape(nb * MOE_TM)
    filler = jnp.arange(nb * MOE_TM, dtype=jnp.int32) % T
    src = jnp.where(valid.reshape(nb * MOE_TM), order[sorted_idx] // TOP_K, filler)
    pos = _lookup(pad_starts, e_flat) + in_expert
    w1g, w1l = _prep_w1(w1)
    y = _moe_experts(hn_bf16[src], block_expert, n_used, w1g, w1l,
                     b1[:, None, 0::2].astype(F32), b1[:, None, 1::2].astype(F32),
                     w2, b2[:, None, :].astype(F32))
    return _combine(h, y[pos.reshape(T, TOP_K).T], gates)


def _rope_tables(L):
    rows = L // GRID_W
    row = jnp.repeat(jnp.arange(rows, dtype=F32), GRID_W)
    col = jnp.tile(jnp.arange(GRID_W, dtype=F32), rows)
    half = HEAD_DIM // 2
    freqs = ROPE_THETA ** (-jnp.arange(0, half, 2, dtype=F32) / half)
    ang = jnp.concatenate([row[:, None] * freqs, col[:, None] * freqs], axis=-1)
    cos = jnp.repeat(jnp.cos(ang), 2, axis=-1)
    sin = jnp.repeat(jnp.sin(ang), 2, axis=-1)
    sign = jnp.tile(jnp.array([-1.0, 1.0], F32), HEAD_DIM // 2)
    reps = LANES // HEAD_DIM
    return jnp.tile(cos, (1, reps)), jnp.tile(sin * sign, (1, reps))


def kernel(x, norm1_g, w_in, conv_w, conv_b, filt_w1, filt_b1, filt_freq1, filt_w2, filt_b2, filt_freq2, filt_w3, filt_b3, hyena_d, q_norm_g, k_norm_g, w_hyena_up, w_attn_up, w_out, norm2_g, router_w, router_b, expert_w1, expert_b1, expert_w2, expert_b2):
    B, L, D = x.shape
    T = B * L
    depth = w_in.shape[0]
    hw = conv_w.shape[-1] // 3
    aw = N_Q_HEADS * HEAD_DIM
    kvw = N_KV_HEADS * HEAD_DIM
    widths = (3 * hw, aw, 2 * kvw, 2 * D)
    cosf, sinf = _rope_tables(L)
    for l in range(depth):
        xt = x.reshape(T, D)
        z, q, kv, gates = _inproj(xt, norm1_g[l], w_in[l].astype(BF16), widths,
                                  (F32, BF16, BF16, BF16))
        y_hy = _hyena(z.reshape(B, L, 3 * hw), conv_w[l], conv_b[l], filt_w1[l], filt_b1[l],
                      filt_freq1[l], filt_w2[l], filt_b2[l], filt_freq2[l], filt_w3[l], filt_b3[l],
                      hyena_d[l])
        gq = jnp.tile(q_norm_g[l].astype(F32), N_Q_HEADS)[None, :]
        gk = jnp.tile(k_norm_g[l].astype(F32), N_KV_HEADS)[None, :]
        qr, kr, va = _qkrope(q, kv, gq, gk, cosf, sinf, L)
        y_at = _attention(qr.reshape(B, L, aw), kr, va.reshape(N_KV_HEADS, B, L, kvw))
        h, hn, route, count = _merge(y_hy.reshape(T, hw), y_at.reshape(T, aw), gates, xt,
                                     w_hyena_up[l], w_attn_up[l], w_out[l], norm2_g[l],
                                     router_w[l], router_b[l])
        x = _moe(h, hn, route, count, expert_w1[l], expert_b1[l], expert_w2[l],
                 expert_b2[l]).reshape(B, L, D)
    return x
```

```python
import functools
import math

import jax
import jax.numpy as jnp
from jax import lax
from jax.experimental import pallas as pl
from jax.experimental.pallas import tpu as pltpu

F32 = jnp.float32
BF16 = jnp.bfloat16

GRID_W = 64
HEAD_DIM = 64
N_Q_HEADS = 8
N_KV_HEADS = 2
Q_PER_KV = N_Q_HEADS // N_KV_HEADS
ROPE_THETA = 10000.0
HYENA_ORDER = 2
SHORT_CONV = 3
FILTER_EMB = 33
FAST_DECAY_PCT = 0.3
SLOW_DECAY_PCT = 1.5
DECAY_TARGET = 1e-2
N_EXPERTS = 32
TOP_K = 4
SWIGLU_LIMIT = 7.0
SWIGLU_ALPHA = 1.702
EPS = 1e-6

LANES = 128
SUBLANES = 8
VMEM_LIMIT = 56 * 1024 * 1024

FFT_N2 = LANES
FFT_PITCH = FFT_N2 + SUBLANES
FFT_K1_CHUNK_MAX = 16
FFT_UNROLL_OUTER = 16
FFT_UNROLL_INNER = True
MOE_TM = 512
LOGIT_PAD = LANES


def _cparams(sem):
    return pltpu.CompilerParams(dimension_semantics=sem, vmem_limit_bytes=VMEM_LIMIT)


def _split_bf16(x):
    hi = x.astype(BF16)
    lo = (x - hi.astype(F32)).astype(BF16)
    return hi, lo


def _inproj_body(x_ref, g_ref, w_ref, gq_ref, gk_ref, cos_ref, sin_ref, oq_ref, ok_ref,
                 z_ref, qo_ref, ko_ref, vo_ref, gate_ref, *, widths):
    x = x_ref[...]
    ms = jnp.mean(x * x, axis=-1, keepdims=True)
    u = (x * lax.rsqrt(ms + EPS) * g_ref[...]).astype(BF16)
    offs = [sum(widths[:i]) for i in range(len(widths) + 1)]
    proj = lambda i: jnp.dot(u, w_ref[:, offs[i]:offs[i + 1]], preferred_element_type=F32)
    z_ref[...] = proj(0)
    gate_ref[...] = proj(3).astype(gate_ref.dtype)
    _qk_epilogue(proj(1), proj(2), gq_ref[...], gk_ref[...], cos_ref[...], sin_ref[...],
                 oq_ref[...], ok_ref[...], qo_ref, ko_ref, vo_ref)


def _inproj(xt, g, w_bf16, widths, gq, gk, cosf, sinf, seq_len, tm=512):
    T, D = xt.shape
    n = w_bf16.shape[1]
    zw, qw, kvw, gw = widths
    kw = kvw // 2
    assert kw == LANES and N_KV_HEADS * HEAD_DIM == LANES and N_KV_HEADS == 2
    nl = seq_len // tm

    def blk_ones(w):
        r = jnp.arange(w) // HEAD_DIM
        return (r[:, None] == r[None, :]).astype(BF16)

    const = lambda shape: pl.BlockSpec(shape, lambda i: (0, 0))
    return pl.pallas_call(
        functools.partial(_inproj_body, widths=widths),
        grid=(T // tm,),
        in_specs=[
            pl.BlockSpec((tm, D), lambda i: (i, 0)),
            const((1, D)), const((D, n)), const((1, qw)), const((1, kw)),
            pl.BlockSpec((tm, LANES), lambda i: (i % nl, 0)),
            pl.BlockSpec((tm, LANES), lambda i: (i % nl, 0)),
            const((qw, qw)), const((kw, kw)),
        ],
        out_specs=[
            pl.BlockSpec((tm, zw), lambda i: (i, 0)),
            pl.BlockSpec((tm, qw), lambda i: (i, 0)),
            pl.BlockSpec((kw, tm), lambda i: (0, i)),
            pl.BlockSpec((N_KV_HEADS, tm, kw), lambda i: (0, i, 0)),
            pl.BlockSpec((tm, gw), lambda i: (i, 0)),
        ],
        out_shape=[
            jax.ShapeDtypeStruct((T, zw), F32),
            jax.ShapeDtypeStruct((T, qw), BF16),
            jax.ShapeDtypeStruct((kw, T), BF16),
            jax.ShapeDtypeStruct((N_KV_HEADS, T, kw), BF16),
            jax.ShapeDtypeStruct((T, gw), BF16),
        ],
        compiler_params=_cparams(("parallel",)),
        name="inproj",
    )(xt, g.reshape(1, D), w_bf16, gq, gk, cosf, sinf, blk_ones(qw), blk_ones(kw))


def _head_norm_rope(x, gain, cosf, sinf, ones_blk):
    w = x.shape[-1]
    hi, lo = _split_bf16(x * x)
    ss = (jnp.dot(hi, ones_blk, preferred_element_type=F32)
          + jnp.dot(lo, ones_blk, preferred_element_type=F32))
    xn = x * lax.rsqrt(ss * (1.0 / HEAD_DIM) + EPS) * gain
    lane = lax.broadcasted_iota(jnp.int32, (x.shape[0], LANES), 1)
    cols = []
    for c in range(w // LANES):
        col = xn[:, c * LANES:(c + 1) * LANES]
        cols.append(jnp.where(lane % 2 == 0, pltpu.roll(col, LANES - 1, 1), pltpu.roll(col, 1, 1)))
    swapped = cols[0] if len(cols) == 1 else jnp.concatenate(cols, axis=1)
    return xn * cosf + swapped * sinf


def _qk_epilogue(q, kv, gq, gk, cosf, sinf, ones_q, ones_k, qo_ref, ko_ref, vo_ref):
    nq = q.shape[-1] // LANES
    q = _head_norm_rope(q, gq, jnp.tile(cosf, (1, nq)), jnp.tile(sinf, (1, nq)), ones_q)
    qo_ref[...] = (q * (HEAD_DIM ** -0.5 * math.log2(math.e))).astype(BF16)
    kw = kv.shape[-1] // 2
    k = _head_norm_rope(kv[:, :kw], gk, cosf, sinf, ones_k)
    ko_ref[...] = k.T.astype(BF16)
    v = kv[:, kw:]
    lane = lax.broadcasted_iota(jnp.int32, v.shape, 1)
    for h in range(N_KV_HEADS):
        vo_ref[h] = jnp.where((lane // HEAD_DIM) == h, v, 1.0).astype(BF16)


def _attn_body(q_ref, k_ref, v_ref, o_ref, qs_ref, m_ref, acc_ref, *, tk, nsplit):
    kvh = pl.program_id(1)
    tq = q_ref.shape[0]
    seq = k_ref.shape[1]
    rows = Q_PER_KV * tq
    lane = lax.broadcasted_iota(jnp.int32, (tq, LANES), 1)
    in_head = (lane // HEAD_DIM) == kvh

    for g in range(Q_PER_KV):
        col = q_ref[:, (g // 2) * LANES:(g // 2 + 1) * LANES].astype(F32)
        col = jnp.where((g % 2) == kvh, col, pltpu.roll(col, HEAD_DIM, 1))
        qs_ref[g * tq:(g + 1) * tq, :] = jnp.where(in_head, col, 0.0).astype(BF16)
    m_ref[...] = jnp.full(m_ref.shape, -jnp.inf, F32)
    acc_ref[...] = jnp.zeros(acc_ref.shape, F32)
    part = rows // nsplit

    def step(c, carry):
        r0 = pl.multiple_of(c * tk, tk)
        kc = k_ref[:, pl.ds(r0, tk)]
        vc = v_ref[pl.ds(r0, tk), :]
        for h in range(nsplit):
            sl = slice(h * part, (h + 1) * part)
            s = jnp.dot(qs_ref[sl, :], kc, preferred_element_type=F32)
            m_prev = m_ref[sl, :]
            m_new = jnp.maximum(m_prev, jnp.max(s, axis=-1, keepdims=True))
            p = jnp.exp2(s - jnp.tile(m_new, (1, tk // LANES)))
            acc_ref[sl, :] = jnp.exp2(m_prev - m_new) * acc_ref[sl, :] + jnp.dot(
                p.astype(BF16), vc, preferred_element_type=F32)
            m_ref[sl, :] = m_new
        return carry

    lax.fori_loop(0, seq // tk, step, 0)

    acc = acc_ref[...]
    o = acc / pltpu.roll(acc, HEAD_DIM, 1)
    for c in range(Q_PER_KV // 2):
        even = o[(2 * c) * tq:(2 * c + 1) * tq, :]
        odd = o[(2 * c + 1) * tq:(2 * c + 2) * tq, :]
        even = jnp.where(kvh == 0, even, pltpu.roll(even, HEAD_DIM, 1))
        odd = jnp.where(kvh == 1, odd, pltpu.roll(odd, HEAD_DIM, 1))
        o_ref[:, c * LANES:(c + 1) * LANES] = jnp.where(lane < HEAD_DIM, even, odd).astype(o_ref.dtype)


def _attention(q, kt, v_aug, tq=512, tk=2048, nsplit=2):
    B, L, qw = q.shape
    gw = Q_PER_KV * HEAD_DIM
    kw = kt.shape[0]
    rows = Q_PER_KV * tq
    return pl.pallas_call(
        functools.partial(_attn_body, tk=tk, nsplit=nsplit),
        grid=(B, N_KV_HEADS, L // tq),
        in_specs=[
            pl.BlockSpec((None, tq, gw), lambda b, h, i: (b, i, h)),
            pl.BlockSpec((kw, L), lambda b, h, i: (0, b)),
            pl.BlockSpec((None, None, L, kw), lambda b, h, i: (h, b, 0, 0)),
        ],
        out_specs=pl.BlockSpec((None, tq, gw), lambda b, h, i: (b, i, h)),
        out_shape=jax.ShapeDtypeStruct((B, L, qw), BF16),
        scratch_shapes=[
            pltpu.VMEM((rows, kw), BF16),
            pltpu.VMEM((rows, LANES), F32),
            pltpu.VMEM((rows, kw), F32),
        ],
        compiler_params=_cparams(("parallel", "parallel", "parallel")),
        name="attention",
    )(q, kt, v_aug)


def _hdot(a, b):
    ah, al = _split_bf16(a)
    bh, bl = _split_bf16(b)
    return (jnp.dot(ah, bh, preferred_element_type=F32)
            + jnp.dot(al, bh, preferred_element_type=F32)
            + jnp.dot(ah, bl, preferred_element_type=F32))


def _filter_body(feat_ref, t_ref, w1_ref, b1_ref, f1_ref, w2_ref, b2_ref, f2_ref, w3_ref, b3_ref,
                 delta_ref, bwd_ref, h_ref, sum_ref):
    i = pl.program_id(0)
    h = jnp.sin(f1_ref[...] * (_hdot(feat_ref[...], w1_ref[...]) + b1_ref[...]))
    h = jnp.sin(f2_ref[...] * (_hdot(h, w2_ref[...]) + b2_ref[...]))
    h = _hdot(h, w3_ref[...]) + b3_ref[...]
    t = t_ref[...]
    h = h * jnp.exp(-t * delta_ref[...])
    width = h_ref.shape[-1]
    for j in range(h_ref.shape[0]):
        h_ref[j] = h[:, j * width:(j + 1) * width]
    row = lax.broadcasted_iota(jnp.int32, h.shape, 0) + i * h.shape[0]
    a = jnp.where((row == 0) & (bwd_ref[...] > 0.5), 0.0, jnp.abs(h))
    part = jnp.sum(a, axis=0, keepdims=True)

    @pl.when(i == 0)
    def _():
        sum_ref[...] = jnp.zeros(sum_ref.shape, F32)

    sum_ref[...] += jnp.broadcast_to(part, sum_ref.shape)


def _hyena_filter(L, width, fw1, fb1, ff1, fw2, fb2, ff2, fw3, fb3, tl=512):
    bands = (FILTER_EMB - 1) // 2
    t = jnp.linspace(0.0, 1.0, L, dtype=F32)[:, None]
    w = 2.0 * math.pi * jnp.arange(L, dtype=F32)[:, None] / L
    fr = jnp.linspace(1e-4, bands - 1, bands, dtype=F32)[None, :]
    feats = jnp.concatenate([t, jnp.cos(w * fr), -jnp.sin(w * fr)], axis=-1)
    max_decay = math.log(DECAY_TARGET) / FAST_DECAY_PCT
    min_decay = math.log(DECAY_TARGET) / SLOW_DECAY_PCT
    deltas = jnp.abs(jnp.linspace(min_decay, max_decay, width, dtype=F32))
    ncol = fw3.shape[1]
    delta_cols = jnp.tile(deltas, ncol // width)[None, :]
    is_bwd = ((jnp.arange(ncol) // width) % 2).astype(F32)[None, :]
    emb = hid = LANES

    def pad2(a, r, c):
        a = a.astype(F32)
        return jnp.zeros((r, c), F32).at[:a.shape[0], :a.shape[1]].set(a)

    row = lambda a: pad2(a.reshape(1, -1), 1, hid)
    feats = pad2(feats, L, emb)
    fw1, fw2, fw3 = pad2(fw1, emb, hid), pad2(fw2, hid, hid), pad2(fw3, hid, ncol)
    fb3 = fb3.reshape(1, ncol).astype(F32)
    const = lambda shape: pl.BlockSpec(shape, lambda i: (0, 0))
    h, sums = pl.pallas_call(
        _filter_body,
        grid=(L // tl,),
        in_specs=[
            pl.BlockSpec((tl, emb), lambda i: (i, 0)),
            pl.BlockSpec((tl, 1), lambda i: (i, 0)),
            const((emb, hid)), const((1, hid)), const((1, hid)),
            const((hid, hid)), const((1, hid)), const((1, hid)),
            const((hid, ncol)), const((1, ncol)), const((1, ncol)), const((1, ncol)),
        ],
        out_specs=[pl.BlockSpec((ncol // width, tl, width), lambda i: (0, i, 0)),
                   pl.BlockSpec((SUBLANES, ncol), lambda i: (0, 0))],
        out_shape=[jax.ShapeDtypeStruct((ncol // width, L, width), F32),
                   jax.ShapeDtypeStruct((SUBLANES, ncol), F32)],
        compiler_params=_cparams(("arbitrary",)),
        name="hyena_filter",
    )(feats, t, fw1, row(fb1), row(ff1), fw2, row(fb2), row(ff2), fw3, fb3, delta_cols, is_bwd)
    return h, sums[0]


def _shortconv_rows(z_ref, w_ref, b_ref, c, chunk):
    L = z_ref.shape[0]
    row = lax.broadcasted_iota(jnp.int32, (chunk, z_ref.shape[1]), 0)
    r0 = pl.multiple_of(c * chunk, chunk)
    cur = z_ref[pl.ds(r0, chunk), :]
    prev_row = z_ref[pl.ds(jnp.maximum(r0 - 1, 0), 1), :]
    next_row = z_ref[pl.ds(jnp.minimum(r0 + chunk, L - 1), 1), :]
    prev_row = jnp.where(c == 0, 0.0, prev_row)
    next_row = jnp.where(c == L // chunk - 1, 0.0, next_row)
    down = jnp.where(row == 0, prev_row, pltpu.roll(cur, 1, 0))
    up = jnp.where(row == chunk - 1, next_row, pltpu.roll(cur, chunk - 1, 0))
    return b_ref[...] + down * w_ref[0:1, :] + cur * w_ref[1:2, :] + up * w_ref[2:3, :]


def _dft_tables(n1_len, n2_len):
    n = n1_len * n2_len

    def root(num, den):
        ang = (2.0 * math.pi / den) * (num % den).astype(F32)
        return jnp.cos(ang), -jnp.sin(ang)

    i1 = jnp.arange(n1_len, dtype=jnp.int32)
    i2 = jnp.arange(n2_len, dtype=jnp.int32)
    f1r, f1i = root(i1[:, None] * i1[None, :], n1_len)
    f2r, f2i = root(i2[:, None] * i2[None, :], n2_len)
    twr, twi = root(i1[:, None] * i2[None, :], n)
    return (f1r, f1i), (f2r, f2i), (twr, twi)


def _stacked_inner_dft(f2r, f2i, twr_row, twi_row):
    gr = f2r * twr_row - f2i * twi_row
    gi = f2r * twi_row + f2i * twr_row
    top = jnp.concatenate([gr, -gi], axis=1)
    bot = jnp.concatenate([gi, gr], axis=1)
    return jnp.concatenate([top, bot], axis=0)


def _fft_plan(n):
    n1_len = n // FFT_N2
    nk1 = n1_len // 2 + 1
    nk1_pad = -(-nk1 // SUBLANES) * SUBLANES
    chunk = max(c for c in range(1, FFT_K1_CHUNK_MAX + 1) if nk1 % c == 0)
    return n1_len, nk1, nk1_pad, chunk


def _outer_dft_to_scratch(load_rows, fa_ref, ar_ref, ai_ref, nk1_pad):
    fa = fa_ref[...]

    def step(n2, carry):
        a = jnp.dot(fa, load_rows(n2).astype(BF16), preferred_element_type=F32)
        ar_ref[pl.ds(n2, nk1_pad, stride=FFT_PITCH), :] = a[:nk1_pad]
        ai_ref[pl.ds(n2, nk1_pad, stride=FFT_PITCH), :] = a[nk1_pad:]
        return carry

    lax.fori_loop(0, FFT_N2, step, 0, unroll=FFT_UNROLL_OUTER)


def _spec_body(f_ref, b_ref, inv_ref, fa_ref, f2r_ref, f2i_ref, twr_ref, twi_ref, kr_ref, ki_ref,
               fr_ref, fi_ref, br_ref, bi_ref, *, n1_len, nk1_pad, chunk):
    kc = pl.program_id(2)
    half = n1_len // 2

    @pl.when(kc == 0)
    def _():
        inv = inv_ref[...]
        row = lax.broadcasted_iota(jnp.int32, (half, LANES), 0)
        _outer_dft_to_scratch(lambda n2: f_ref[pl.ds(n2, half, stride=FFT_N2), :] * inv,
                              fa_ref, fr_ref, fi_ref, nk1_pad)
        _outer_dft_to_scratch(
            lambda n2: jnp.where((row == 0) & (n2 == 0), 0.0,
                                 b_ref[pl.ds(n2, half, stride=FFT_N2), :] * inv),
            fa_ref, br_ref, bi_ref, nk1_pad)

    f2r = f2r_ref[...]
    f2i = f2i_ref[...]

    def step(t, carry):
        k1 = kc * chunk + t
        base = pl.multiple_of(k1 * FFT_PITCH, SUBLANES)
        mf = _stacked_inner_dft(f2r, f2i, twr_ref[pl.ds(k1, 1), :],
                                twi_ref[pl.ds(k1, 1), :]).astype(BF16)
        rows = pl.ds(base, FFT_N2)
        xf = jnp.dot(mf, jnp.concatenate([fr_ref[rows, :], fi_ref[rows, :]], axis=0).astype(BF16),
                     preferred_element_type=F32)
        xb = jnp.dot(mf, jnp.concatenate([br_ref[rows, :], bi_ref[rows, :]], axis=0).astype(BF16),
                     preferred_element_type=F32)
        o = pl.multiple_of(t * FFT_N2, FFT_N2)
        kr_ref[pl.ds(o, FFT_N2), :] = xf[:FFT_N2] + xb[:FFT_N2]
        ki_ref[pl.ds(o, FFT_N2), :] = xf[FFT_N2:] - xb[FFT_N2:]
        return carry

    lax.fori_loop(0, chunk, step, 0, unroll=FFT_UNROLL_INNER)


def _filter_spectrum(h, inv_norm, tables):
    O2, L, C = h.shape
    O = O2 // 2
    n1_len, nk1, nk1_pad, chunk = _fft_plan(2 * L)
    half = n1_len // 2
    (f1r, f1i), (f2r, f2i), (twr, twi) = tables
    fa = jnp.concatenate([f1r[:nk1_pad, :half], f1i[:nk1_pad, :half]], axis=0).astype(BF16)
    rows = chunk * FFT_N2
    const = lambda shape: pl.BlockSpec(shape, lambda o, c, k: (0, 0))
    out = jax.ShapeDtypeStruct((O, nk1 * FFT_N2, C), F32)
    scratch = pltpu.VMEM((nk1_pad * FFT_PITCH, LANES), F32)
    return pl.pallas_call(
        functools.partial(_spec_body, n1_len=n1_len, nk1_pad=nk1_pad, chunk=chunk),
        grid=(O, C // LANES, nk1 // chunk),
        in_specs=[
            pl.BlockSpec((None, L, LANES), lambda o, c, k: (2 * o, 0, c)),
            pl.BlockSpec((None, L, LANES), lambda o, c, k: (2 * o + 1, 0, c)),
            pl.BlockSpec((None, 1, LANES), lambda o, c, k: (o, 0, c)),
            const(fa.shape), const(f2r.shape), const(f2i.shape), const(twr.shape), const(twi.shape),
        ],
        out_specs=[pl.BlockSpec((None, rows, LANES), lambda o, c, k: (o, k, c)),
                   pl.BlockSpec((None, rows, LANES), lambda o, c, k: (o, k, c))],
        out_shape=[out, out],
        scratch_shapes=[scratch, scratch, scratch, scratch],
        compiler_params=_cparams(("parallel", "parallel", "arbitrary")),
        name="filter_spectrum",
    )(h, h, inv_norm, fa, f2r, f2i, twr, twi)


def _fftconv_body(y_ref, gate_ref, wy_ref, by_ref, wg_ref, bg_ref, d_ref, kr_ref, ki_ref, fa_ref,
                  fs_ref, f2r_ref, f2i_ref, twr_ref, twi_ref, o_ref, xs_ref, ar_ref, ai_ref, *,
                  n1_len, nk1_pad, chunk, conv_y):
    kc = pl.program_id(2)
    half = n1_len // 2

    @pl.when(kc == 0)
    def _():
        def copy(n1, carry):
            src = pl.multiple_of(n1 * FFT_N2, FFT_N2)
            dst = pl.multiple_of(n1 * FFT_PITCH, SUBLANES)
            if conv_y:
                xs_ref[pl.ds(dst, FFT_N2), :] = _shortconv_rows(y_ref, wy_ref, by_ref, n1, FFT_N2)
            else:
                xs_ref[pl.ds(dst, FFT_N2), :] = y_ref[pl.ds(src, FFT_N2), :]
            return carry

        lax.fori_loop(0, half, copy, 0)
        _outer_dft_to_scratch(lambda n2: xs_ref[pl.ds(n2, half, stride=FFT_PITCH), :],
                              fa_ref, ar_ref, ai_ref, nk1_pad)

    f2r = f2r_ref[...]
    f2i = f2i_ref[...]

    def step(t, carry):
        k1 = kc * chunk + t
        base = pl.multiple_of(k1 * FFT_PITCH, SUBLANES)
        rhs = jnp.concatenate([ar_ref[pl.ds(base, FFT_N2), :], ai_ref[pl.ds(base, FFT_N2), :]],
                              axis=0).astype(BF16)
        mf = _stacked_inner_dft(f2r, f2i, twr_ref[pl.ds(k1, 1), :], twi_ref[pl.ds(k1, 1), :])
        x = jnp.dot(mf.astype(BF16), rhs, preferred_element_type=F32)
        xr, xi = x[:FFT_N2], x[FFT_N2:]
        o = pl.multiple_of(t * FFT_N2, FFT_N2)
        kr = kr_ref[pl.ds(o, FFT_N2), :]
        ki = ki_ref[pl.ds(o, FFT_N2), :]
        z = jnp.concatenate([xr * kr - xi * ki, xr * ki + xi * kr], axis=0).astype(BF16)
        b = jnp.dot(mf.T.astype(BF16), z, preferred_element_type=F32)
        ar_ref[pl.ds(base, FFT_N2), :] = b[:FFT_N2]
        ai_ref[pl.ds(base, FFT_N2), :] = b[FFT_N2:]
        return carry

    lax.fori_loop(0, chunk, step, 0, unroll=FFT_UNROLL_INNER)

    @pl.when(kc == pl.num_programs(2) - 1)
    def _():
        fs = fs_ref[...]

        def inv_outer(n2, carry):
            rhs = jnp.concatenate([ar_ref[pl.ds(n2, nk1_pad, stride=FFT_PITCH), :],
                                   ai_ref[pl.ds(n2, nk1_pad, stride=FFT_PITCH), :]],
                                  axis=0).astype(BF16)
            conv = jnp.dot(fs, rhs, preferred_element_type=F32)
            ar_ref[pl.ds(n2, half, stride=FFT_PITCH), :] = conv
            return carry

        lax.fori_loop(0, FFT_N2, inv_outer, 0, unroll=FFT_UNROLL_OUTER)
        d = d_ref[...]

        def finish(n1, carry):
            src = pl.multiple_of(n1 * FFT_PITCH, SUBLANES)
            dst = pl.multiple_of(n1 * FFT_N2, FFT_N2)
            y = xs_ref[pl.ds(src, FFT_N2), :]
            gate = _shortconv_rows(gate_ref, wg_ref, bg_ref, n1, FFT_N2)
            o_ref[pl.ds(dst, FFT_N2), :] = (gate * (
                ar_ref[pl.ds(src, FFT_N2), :] + y * d)).astype(o_ref.dtype)
            return carry

        lax.fori_loop(0, half, finish, 0)


def _fftconv_gate(y, y_off, conv_y, z, gate_off, conv_w, conv_b, d, kr, ki, tables, out_dtype):
    B, L, _ = y.shape
    C = d.shape[-1]
    yo, go = y_off // LANES, gate_off // LANES
    wo = yo if conv_y else go
    N = 2 * L
    n1_len, nk1, nk1_pad, chunk = _fft_plan(N)
    half = n1_len // 2
    (f1r, f1i), (f2r, f2i), (twr, twi) = tables
    fa = jnp.concatenate([f1r[:nk1_pad, :half], f1i[:nk1_pad, :half]], axis=0).astype(BF16)
    wts = jnp.concatenate([jnp.ones((1,), F32), jnp.full((nk1 - 2,), 2.0, F32), jnp.ones((1,), F32),
                           jnp.zeros((nk1_pad - nk1,), F32)]) * (1.0 / N)
    fs = jnp.concatenate([f1r[:half, :nk1_pad] * wts, f1i[:half, :nk1_pad] * wts],
                         axis=1).astype(BF16)
    rows = chunk * FFT_N2
    const = lambda shape: pl.BlockSpec(shape, lambda c, b, k: (0, 0))
    return pl.pallas_call(
        functools.partial(_fftconv_body, n1_len=n1_len, nk1_pad=nk1_pad, chunk=chunk,
                          conv_y=conv_y),
        grid=(C // LANES, B, nk1 // chunk),
        in_specs=[
            pl.BlockSpec((None, L, LANES), lambda c, b, k: (b, 0, c + yo)),
            pl.BlockSpec((None, L, LANES), lambda c, b, k: (b, 0, c + go)),
            pl.BlockSpec((SHORT_CONV, LANES), lambda c, b, k: (0, c + wo)),
            pl.BlockSpec((1, LANES), lambda c, b, k: (0, c + wo)),
            pl.BlockSpec((SHORT_CONV, LANES), lambda c, b, k: (0, c + go)),
            pl.BlockSpec((1, LANES), lambda c, b, k: (0, c + go)),
            pl.BlockSpec((1, LANES), lambda c, b, k: (0, c)),
            pl.BlockSpec((rows, LANES), lambda c, b, k: (k, c)),
            pl.BlockSpec((rows, LANES), lambda c, b, k: (k, c)),
            const(fa.shape), const(fs.shape), const(f2r.shape), const(f2i.shape),
            const(twr.shape), const(twi.shape),
        ],
        out_specs=pl.BlockSpec((None, L, LANES), lambda c, b, k: (b, 0, c)),
        out_shape=jax.ShapeDtypeStruct((B, L, C), out_dtype),
        scratch_shapes=[pltpu.VMEM((half * FFT_PITCH, LANES), F32),
                        pltpu.VMEM((nk1_pad * FFT_PITCH, LANES), F32),
                        pltpu.VMEM((nk1_pad * FFT_PITCH, LANES), F32)],
        compiler_params=_cparams(("parallel", "parallel", "arbitrary")),
        name="fftconv_gate",
    )(y, z, conv_w, conv_b, conv_w, conv_b, d, kr, ki, fa, fs, f2r, f2i, twr, twi)


def _hyena(z, conv_w, conv_b, fw1, fb1, ff1, fw2, fb2, ff2, fw3, fb3, hyena_d):
    B, L, C3 = z.shape
    W = C3 // 3
    h, sums = _hyena_filter(L, W, fw1, fb1, ff1, fw2, fb2, ff2, fw3, fb3)
    sums = sums.reshape(HYENA_ORDER, 2, W)
    inv_norm = (1.0 / (sums[:, 0] + sums[:, 1]))[:, None, :]
    tables = _dft_tables(2 * L // FFT_N2, FFT_N2)
    kr, ki = _filter_spectrum(h, inv_norm, tables)
    conv_b = conv_b.reshape(1, C3)
    y = z
    for o in range(HYENA_ORDER):
        y = _fftconv_gate(y, 0, o == 0, z, (o + 1) * W, conv_w, conv_b,
                          hyena_d[o].reshape(1, W).astype(F32), kr[o], ki[o], tables,
                          BF16 if o == HYENA_ORDER - 1 else F32)
    return y


def _merge_body(yh_ref, ya_ref, g_ref, x_ref, whu_ref, wau_ref, wo_ref, n2_ref, rwh_ref, rwl_ref,
                rb_ref, h_ref, hn_ref, lg_ref):
    D = x_ref.shape[-1]
    up_h = jnp.dot(yh_ref[...], whu_ref[...], preferred_element_type=F32)
    up_a = jnp.dot(ya_ref[...], wau_ref[...], preferred_element_type=F32)
    g = g_ref[...].astype(F32)
    merged = jax.nn.sigmoid(g[:, :D]) * up_h + jax.nn.sigmoid(g[:, D:]) * up_a
    h = x_ref[...] + jnp.dot(merged.astype(BF16), wo_ref[...], preferred_element_type=F32)
    h_ref[...] = h
    ms = jnp.mean(h * h, axis=-1, keepdims=True)
    hn = h * lax.rsqrt(ms + EPS) * n2_ref[...]
    hn_ref[...] = hn.astype(BF16)
    hh, hl = _split_bf16(hn)
    lg_ref[...] = (jnp.dot(hh, rwh_ref[...], preferred_element_type=F32)
                   + jnp.dot(hl, rwh_ref[...], preferred_element_type=F32)
                   + jnp.dot(hh, rwl_ref[...], preferred_element_type=F32)) + rb_ref[...]


def _merge(yh, ya, gates, xt, whu, wau, wo, n2g, rw, rb, tm=1024):
    T, D = xt.shape
    W = yh.shape[1]
    E = rw.shape[1]
    rwp = jnp.zeros((D, LOGIT_PAD), F32).at[:, :E].set(rw)
    rwh, rwl = _split_bf16(rwp)
    rbp = jnp.zeros((1, LOGIT_PAD), F32).at[0, :E].set(rb)
    rowblk = lambda w: pl.BlockSpec((tm, w), lambda i: (i, 0))
    const = lambda shape: pl.BlockSpec(shape, lambda i: (0, 0))
    return pl.pallas_call(
        _merge_body,
        grid=(T // tm,),
        in_specs=[rowblk(W), rowblk(W), rowblk(2 * D), rowblk(D),
                  const((W, D)), const((W, D)), const((D, D)), const((1, D)),
                  const((D, LOGIT_PAD)), const((D, LOGIT_PAD)), const((1, LOGIT_PAD))],
        out_specs=[rowblk(D), rowblk(D), rowblk(LOGIT_PAD)],
        out_shape=[jax.ShapeDtypeStruct((T, D), F32), jax.ShapeDtypeStruct((T, D), BF16),
                   jax.ShapeDtypeStruct((T, LOGIT_PAD), F32)],
        compiler_params=_cparams(("parallel",)),
        name="merge",
    )(yh, ya, gates, xt, whu.astype(BF16), wau.astype(BF16), wo.astype(BF16), n2g.reshape(1, D),
      rwh, rwl, rbp)


def _moe_body(be_ref, nused_ref, x_ref, w1g_ref, w1l_ref, b1g_ref, b1l_ref, w2_ref, b2_ref, o_ref):
    i = pl.program_id(0)

    @pl.when(i < nused_ref[0])
    def _():
        x = x_ref[...]
        nt = (((1,), (1,)), ((), ()))
        glu = lax.dot_general(x, w1g_ref[...], nt, preferred_element_type=F32) + b1g_ref[...]
        lin = lax.dot_general(x, w1l_ref[...], nt, preferred_element_type=F32) + b1l_ref[...]
        glu = jnp.minimum(glu, SWIGLU_LIMIT)
        lin = jnp.clip(lin, -SWIGLU_LIMIT, SWIGLU_LIMIT)
        act = glu * jax.nn.sigmoid(SWIGLU_ALPHA * glu) * (lin + 1.0)
        y = jnp.dot(act.astype(BF16), w2_ref[...].astype(BF16),
                    preferred_element_type=F32) + b2_ref[...]
        o_ref[...] = y.astype(o_ref.dtype)

    @pl.when(i >= nused_ref[0])
    def _():
        o_ref[...] = jnp.zeros(o_ref.shape, o_ref.dtype)


def _moe_experts(xg, block_expert, n_used, w1g, w1l, b1g, b1l, w2, b2):
    P, D = xg.shape
    dff = w2.shape[1]
    nb = P // MOE_TM
    wspec = lambda k, n: pl.BlockSpec((None, k, n), lambda i, be, nu: (be[i], 0, 0))
    grid_spec = pltpu.PrefetchScalarGridSpec(
        num_scalar_prefetch=2,
        grid=(nb,),
        in_specs=[
            pl.BlockSpec((MOE_TM, D), lambda i, be, nu: (i, 0)),
            wspec(dff, D), wspec(dff, D), wspec(1, dff), wspec(1, dff),
            wspec(dff, D), wspec(1, D),
        ],
        out_specs=pl.BlockSpec((MOE_TM, D), lambda i, be, nu: (i, 0)),
    )
    return pl.pallas_call(
        _moe_body,
        grid_spec=grid_spec,
        out_shape=jax.ShapeDtypeStruct((P, D), BF16),
        compiler_params=_cparams(("arbitrary",)),
        name="moe_experts",
    )(block_expert, n_used, xg, w1g, w1l, b1g, b1l, w2, b2)


def _prep_w1_body(w_ref, sel_ref, g_ref, l_ref):
    half = g_ref.shape[0]
    picked = lax.dot_general(sel_ref[...], w_ref[...].astype(BF16), (((1,), (1,)), ((), ())),
                             preferred_element_type=F32)
    g_ref[...] = picked[:half].astype(BF16)
    l_ref[...] = picked[half:].astype(BF16)


def _prep_w1(w1, tc=8 * LANES):
    E, D, F2 = w1.shape
    tc = min(tc, F2)
    half = tc // 2
    row = jnp.arange(tc)
    picks = jnp.where(row < half, 2 * row, 2 * (row - half) + 1)
    sel = (picks[:, None] == jnp.arange(tc)[None, :]).astype(BF16)
    out = jax.ShapeDtypeStruct((E, F2 // 2, D), BF16)
    return pl.pallas_call(
        _prep_w1_body,
        grid=(E, F2 // tc),
        in_specs=[pl.BlockSpec((None, D, tc), lambda e, j: (e, 0, j)),
                  pl.BlockSpec((tc, tc), lambda e, j: (0, 0))],
        out_specs=[pl.BlockSpec((None, half, D), lambda e, j: (e, j, 0)),
                   pl.BlockSpec((None, half, D), lambda e, j: (e, j, 0))],
        out_shape=[out, out],
        compiler_params=_cparams(("parallel", "parallel")),
        name="prep_w1",
    )(w1, sel)


def _combine_body(h_ref, y_ref, g_ref, o_ref):
    acc = h_ref[...]
    g = g_ref[...]
    for k in range(y_ref.shape[0]):
        acc = acc + g[:, k:k + 1] * y_ref[k].astype(F32)
    o_ref[...] = acc


def _combine(h, yk, gates, tm=512):
    T, D = h.shape
    K = yk.shape[0]
    return pl.pallas_call(
        _combine_body,
        grid=(T // tm,),
        in_specs=[pl.BlockSpec((tm, D), lambda i: (i, 0)),
                  pl.BlockSpec((K, tm, D), lambda i: (0, i, 0)),
                  pl.BlockSpec((tm, K), lambda i: (i, 0))],
        out_specs=pl.BlockSpec((tm, D), lambda i: (i, 0)),
        out_shape=jax.ShapeDtypeStruct((T, D), F32),
        compiler_params=_cparams(("parallel",)),
        name="moe_combine",
    )(h, yk, gates)


def _lookup(table, idx):
    n = table.shape[0]
    hit = idx[None, :] == jnp.arange(n, dtype=idx.dtype)[:, None]
    return jnp.sum(jnp.where(hit, table[:, None], 0), axis=0)


def _moe(h, hn_bf16, logits, w1, b1, w2, b2):
    T, D = hn_bf16.shape
    E = w1.shape[0]
    top_val, top_idx = lax.top_k(logits, TOP_K)
    gates = jax.nn.softmax(top_val, axis=-1)
    TK = T * TOP_K
    e_flat = top_idx.reshape(TK).astype(jnp.int32)
    order = jnp.argsort(e_flat).astype(jnp.int32)
    rank = jnp.argsort(order).astype(jnp.int32)
    counts = jnp.sum(jnp.arange(E, dtype=jnp.int32)[:, None] == e_flat[None, :], axis=1,
                     dtype=jnp.int32)
    starts = jnp.cumsum(counts) - counts
    padded = ((counts + MOE_TM - 1) // MOE_TM) * MOE_TM
    pad_ends = jnp.cumsum(padded)
    pad_starts = pad_ends - padded
    nb = (TK + E * (MOE_TM - 1) + MOE_TM - 1) // MOE_TM
    block_start = jnp.arange(nb, dtype=jnp.int32) * MOE_TM
    block_expert = jnp.minimum(jnp.sum(block_start[:, None] >= pad_ends[None, :], axis=1),
                               E - 1).astype(jnp.int32)
    n_used = (pad_ends[-1] // MOE_TM).astype(jnp.int32).reshape(1)
    blk_first = block_start - pad_starts[block_expert]
    within = blk_first[:, None] + jnp.arange(MOE_TM, dtype=jnp.int32)[None, :]
    valid = within < counts[block_expert][:, None]
    sorted_idx = jnp.where(valid, starts[block_expert][:, None] + within, 0).reshape(nb * MOE_TM)
    filler = jnp.arange(nb * MOE_TM, dtype=jnp.int32) % T
    src = jnp.where(valid.reshape(nb * MOE_TM), order[sorted_idx] // TOP_K, filler)
    pos = _lookup(pad_starts - starts, e_flat) + rank
    w1g, w1l = _prep_w1(w1)
    y = _moe_experts(hn_bf16[src], block_expert, n_used, w1g, w1l,
                     b1[:, None, 0::2].astype(F32), b1[:, None, 1::2].astype(F32),
                     w2, b2[:, None, :].astype(F32))
    return _combine(h, y[pos.reshape(T, TOP_K).T], gates)


def _rope_tables(L):
    rows = L // GRID_W
    row = jnp.repeat(jnp.arange(rows, dtype=F32), GRID_W)
    col = jnp.tile(jnp.arange(GRID_W, dtype=F32), rows)
    half = HEAD_DIM // 2
    freqs = ROPE_THETA ** (-jnp.arange(0, half, 2, dtype=F32) / half)
    ang = jnp.concatenate([row[:, None] * freqs, col[:, None] * freqs], axis=-1)
    cos = jnp.repeat(jnp.cos(ang), 2, axis=-1)
    sin = jnp.repeat(jnp.sin(ang), 2, axis=-1)
    sign = jnp.tile(jnp.array([-1.0, 1.0], F32), HEAD_DIM // 2)
    reps = LANES // HEAD_DIM
    return jnp.tile(cos, (1, reps)), jnp.tile(sin * sign, (1, reps))


def kernel(x, norm1_g, w_in, conv_w, conv_b, filt_w1, filt_b1, filt_freq1, filt_w2, filt_b2, filt_freq2, filt_w3, filt_b3, hyena_d, q_norm_g, k_norm_g, w_hyena_up, w_attn_up, w_out, norm2_g, router_w, router_b, expert_w1, expert_b1, expert_w2, expert_b2):
    B, L, D = x.shape
    T = B * L
    depth = w_in.shape[0]
    hw = conv_w.shape[-1] // 3
    aw = N_Q_HEADS * HEAD_DIM
    kvw = N_KV_HEADS * HEAD_DIM
    widths = (3 * hw, aw, 2 * kvw, 2 * D)
    cosf, sinf = _rope_tables(L)
    for l in range(depth):
        xt = x.reshape(T, D)
        gq = jnp.tile(q_norm_g[l].astype(F32), N_Q_HEADS)[None, :]
        gk = jnp.tile(k_norm_g[l].astype(F32), N_KV_HEADS)[None, :]
        z, qr, kr, va, gates = _inproj(xt, norm1_g[l], w_in[l].astype(BF16), widths, gq, gk,
                                       cosf, sinf, L)
        y_hy = _hyena(z.reshape(B, L, 3 * hw), conv_w[l], conv_b[l], filt_w1[l], filt_b1[l],
                      filt_freq1[l], filt_w2[l], filt_b2[l], filt_freq2[l], filt_w3[l], filt_b3[l],
                      hyena_d[l])
        y_at = _attention(qr.reshape(B, L, aw), kr, va.reshape(N_KV_HEADS, B, L, kvw))
        h, hn, logits = _merge(y_hy.reshape(T, hw), y_at.reshape(T, aw), gates, xt,
                               w_hyena_up[l], w_attn_up[l], w_out[l], norm2_g[l],
                               router_w[l], router_b[l])
        x = _moe(h, hn, logits[:, :N_EXPERTS], expert_w1[l], expert_b1[l], expert_w2[l],
                 expert_b2[l]).reshape(B, L, D)
    return x
```

```python
import functools
import math

import jax
import jax.numpy as jnp
from jax import lax
from jax.experimental import pallas as pl
from jax.experimental.pallas import tpu as pltpu

F32 = jnp.float32
BF16 = jnp.bfloat16

GRID_W = 64
HEAD_DIM = 64
N_Q_HEADS = 8
N_KV_HEADS = 2
Q_PER_KV = N_Q_HEADS // N_KV_HEADS
ROPE_THETA = 10000.0
HYENA_ORDER = 2
SHORT_CONV = 3
FILTER_EMB = 33
FAST_DECAY_PCT = 0.3
SLOW_DECAY_PCT = 1.5
DECAY_TARGET = 1e-2
N_EXPERTS = 32
TOP_K = 4
SWIGLU_LIMIT = 7.0
SWIGLU_ALPHA = 1.702
EPS = 1e-6

LANES = 128
SUBLANES = 8
VMEM_LIMIT = 56 * 1024 * 1024

FFT_N2 = LANES
FFT_PITCH = FFT_N2 + SUBLANES
FFT_K1_CHUNK_MAX = 16
FFT_UNROLL_OUTER = 16
FFT_UNROLL_INNER = True
MOE_TM = 1024
LOGIT_PAD = LANES


def _cparams(sem):
    return pltpu.CompilerParams(dimension_semantics=sem, vmem_limit_bytes=VMEM_LIMIT)


def _split_bf16(x):
    hi = x.astype(BF16)
    lo = (x - hi.astype(F32)).astype(BF16)
    return hi, lo


def _inproj_body(x_ref, g_ref, w_ref, gq_ref, gk_ref, cos_ref, sin_ref, oq_ref, ok_ref,
                 z_ref, qo_ref, ko_ref, vo_ref, gate_ref, *, widths):
    x = x_ref[...]
    ms = jnp.mean(x * x, axis=-1, keepdims=True)
    u = (x * lax.rsqrt(ms + EPS) * g_ref[...]).astype(BF16)
    offs = [sum(widths[:i]) for i in range(len(widths) + 1)]
    proj = lambda i: jnp.dot(u, w_ref[:, offs[i]:offs[i + 1]], preferred_element_type=F32)
    z_ref[...] = proj(0)
    gate_ref[...] = proj(3).astype(gate_ref.dtype)
    _qk_epilogue(proj(1), proj(2), gq_ref[...], gk_ref[...], cos_ref[...], sin_ref[...],
                 oq_ref[...], ok_ref[...], qo_ref, ko_ref, vo_ref)


def _inproj(xt, g, w_bf16, widths, gq, gk, cosf, sinf, seq_len, tm=512):
    T, D = xt.shape
    n = w_bf16.shape[1]
    zw, qw, kvw, gw = widths
    kw = kvw // 2
    assert kw == LANES and N_KV_HEADS * HEAD_DIM == LANES and N_KV_HEADS == 2
    nl = seq_len // tm

    def blk_ones(w):
        r = jnp.arange(w) // HEAD_DIM
        return (r[:, None] == r[None, :]).astype(BF16)

    const = lambda shape: pl.BlockSpec(shape, lambda i: (0, 0))
    return pl.pallas_call(
        functools.partial(_inproj_body, widths=widths),
        grid=(T // tm,),
        in_specs=[
            pl.BlockSpec((tm, D), lambda i: (i, 0)),
            const((1, D)), const((D, n)), const((1, qw)), const((1, kw)),
            pl.BlockSpec((tm, LANES), lambda i: (i % nl, 0)),
            pl.BlockSpec((tm, LANES), lambda i: (i % nl, 0)),
            const((qw, qw)), const((kw, kw)),
        ],
        out_specs=[
            pl.BlockSpec((tm, zw), lambda i: (i, 0)),
            pl.BlockSpec((tm, qw), lambda i: (i, 0)),
            pl.BlockSpec((kw, tm), lambda i: (0, i)),
            pl.BlockSpec((N_KV_HEADS, tm, kw), lambda i: (0, i, 0)),
            pl.BlockSpec((tm, gw), lambda i: (i, 0)),
        ],
        out_shape=[
            jax.ShapeDtypeStruct((T, zw), F32),
            jax.ShapeDtypeStruct((T, qw), BF16),
            jax.ShapeDtypeStruct((kw, T), BF16),
            jax.ShapeDtypeStruct((N_KV_HEADS, T, kw), BF16),
            jax.ShapeDtypeStruct((T, gw), BF16),
        ],
        compiler_params=_cparams(("parallel",)),
        name="inproj",
    )(xt, g.reshape(1, D), w_bf16, gq, gk, cosf, sinf, blk_ones(qw), blk_ones(kw))


def _head_norm_rope(x, gain, cosf, sinf, ones_blk):
    w = x.shape[-1]
    hi, lo = _split_bf16(x * x)
    ss = (jnp.dot(hi, ones_blk, preferred_element_type=F32)
          + jnp.dot(lo, ones_blk, preferred_element_type=F32))
    xn = x * lax.rsqrt(ss * (1.0 / HEAD_DIM) + EPS) * gain
    lane = lax.broadcasted_iota(jnp.int32, (x.shape[0], LANES), 1)
    cols = []
    for c in range(w // LANES):
        col = xn[:, c * LANES:(c + 1) * LANES]
        cols.append(jnp.where(lane % 2 == 0, pltpu.roll(col, LANES - 1, 1), pltpu.roll(col, 1, 1)))
    swapped = cols[0] if len(cols) == 1 else jnp.concatenate(cols, axis=1)
    return xn * cosf + swapped * sinf


def _qk_epilogue(q, kv, gq, gk, cosf, sinf, ones_q, ones_k, qo_ref, ko_ref, vo_ref):
    nq = q.shape[-1] // LANES
    q = _head_norm_rope(q, gq, jnp.tile(cosf, (1, nq)), jnp.tile(sinf, (1, nq)), ones_q)
    qo_ref[...] = (q * (HEAD_DIM ** -0.5 * math.log2(math.e))).astype(BF16)
    kw = kv.shape[-1] // 2
    k = _head_norm_rope(kv[:, :kw], gk, cosf, sinf, ones_k)
    ko_ref[...] = k.T.astype(BF16)
    v = kv[:, kw:]
    lane = lax.broadcasted_iota(jnp.int32, v.shape, 1)
    for h in range(N_KV_HEADS):
        vo_ref[h] = jnp.where((lane // HEAD_DIM) == h, v, 1.0).astype(BF16)


def _attn_body(q_ref, k_ref, v_ref, o_ref, qs_ref, m_ref, acc_ref, *, tk, nsplit):
    kvh = pl.program_id(1)
    tq = q_ref.shape[0]
    seq = k_ref.shape[1]
    rows = Q_PER_KV * tq
    lane = lax.broadcasted_iota(jnp.int32, (tq, LANES), 1)
    in_head = (lane // HEAD_DIM) == kvh

    for g in range(Q_PER_KV):
        col = q_ref[:, (g // 2) * LANES:(g // 2 + 1) * LANES].astype(F32)
        col = jnp.where((g % 2) == kvh, col, pltpu.roll(col, HEAD_DIM, 1))
        qs_ref[g * tq:(g + 1) * tq, :] = jnp.where(in_head, col, 0.0).astype(BF16)
    m_ref[...] = jnp.full(m_ref.shape, -jnp.inf, F32)
    acc_ref[...] = jnp.zeros(acc_ref.shape, F32)
    part = rows // nsplit

    def step(c, carry):
        r0 = pl.multiple_of(c * tk, tk)
        kc = k_ref[:, pl.ds(r0, tk)]
        vc = v_ref[pl.ds(r0, tk), :]
        for h in range(nsplit):
            sl = slice(h * part, (h + 1) * part)
            s = jnp.dot(qs_ref[sl, :], kc, preferred_element_type=F32)
            m_prev = m_ref[sl, :]
            m_new = jnp.maximum(m_prev, jnp.max(s, axis=-1, keepdims=True))
            p = jnp.exp2(s - jnp.tile(m_new, (1, tk // LANES)))
            acc_ref[sl, :] = jnp.exp2(m_prev - m_new) * acc_ref[sl, :] + jnp.dot(
                p.astype(BF16), vc, preferred_element_type=F32)
            m_ref[sl, :] = m_new
        return carry

    lax.fori_loop(0, seq // tk, step, 0)

    acc = acc_ref[...]
    o = acc / pltpu.roll(acc, HEAD_DIM, 1)
    for c in range(Q_PER_KV // 2):
        even = o[(2 * c) * tq:(2 * c + 1) * tq, :]
        odd = o[(2 * c + 1) * tq:(2 * c + 2) * tq, :]
        even = jnp.where(kvh == 0, even, pltpu.roll(even, HEAD_DIM, 1))
        odd = jnp.where(kvh == 1, odd, pltpu.roll(odd, HEAD_DIM, 1))
        o_ref[:, c * LANES:(c + 1) * LANES] = jnp.where(lane < HEAD_DIM, even, odd).astype(o_ref.dtype)


def _attention(q, kt, v_aug, tq=512, tk=2048, nsplit=2):
    B, L, qw = q.shape
    gw = Q_PER_KV * HEAD_DIM
    kw = kt.shape[0]
    rows = Q_PER_KV * tq
    return pl.pallas_call(
        functools.partial(_attn_body, tk=tk, nsplit=nsplit),
        grid=(B, N_KV_HEADS, L // tq),
        in_specs=[
            pl.BlockSpec((None, tq, gw), lambda b, h, i: (b, i, h)),
            pl.BlockSpec((kw, L), lambda b, h, i: (0, b)),
            pl.BlockSpec((None, None, L, kw), lambda b, h, i: (h, b, 0, 0)),
        ],
        out_specs=pl.BlockSpec((None, tq, gw), lambda b, h, i: (b, i, h)),
        out_shape=jax.ShapeDtypeStruct((B, L, qw), BF16),
        scratch_shapes=[
            pltpu.VMEM((rows, kw), BF16),
            pltpu.VMEM((rows, LANES), F32),
            pltpu.VMEM((rows, kw), F32),
        ],
        compiler_params=_cparams(("parallel", "parallel", "parallel")),
        name="attention",
    )(q, kt, v_aug)


def _hdot(a, b):
    ah, al = _split_bf16(a)
    bh, bl = _split_bf16(b)
    return (jnp.dot(ah, bh, preferred_element_type=F32)
            + jnp.dot(al, bh, preferred_element_type=F32)
            + jnp.dot(ah, bl, preferred_element_type=F32))


def _filter_body(feat_ref, t_ref, w1_ref, b1_ref, f1_ref, w2_ref, b2_ref, f2_ref, w3_ref, b3_ref,
                 delta_ref, bwd_ref, h_ref, sum_ref):
    i = pl.program_id(0)
    h = jnp.sin(f1_ref[...] * (_hdot(feat_ref[...], w1_ref[...]) + b1_ref[...]))
    h = jnp.sin(f2_ref[...] * (_hdot(h, w2_ref[...]) + b2_ref[...]))
    h = _hdot(h, w3_ref[...]) + b3_ref[...]
    t = t_ref[...]
    h = h * jnp.exp(-t * delta_ref[...])
    width = h_ref.shape[-1]
    for j in range(h_ref.shape[0]):
        h_ref[j] = h[:, j * width:(j + 1) * width]
    row = lax.broadcasted_iota(jnp.int32, h.shape, 0) + i * h.shape[0]
    a = jnp.where((row == 0) & (bwd_ref[...] > 0.5), 0.0, jnp.abs(h))
    part = jnp.sum(a, axis=0, keepdims=True)

    @pl.when(i == 0)
    def _():
        sum_ref[...] = jnp.zeros(sum_ref.shape, F32)

    sum_ref[...] += jnp.broadcast_to(part, sum_ref.shape)


def _hyena_filter(L, width, fw1, fb1, ff1, fw2, fb2, ff2, fw3, fb3, tl=512):
    bands = (FILTER_EMB - 1) // 2
    t = jnp.linspace(0.0, 1.0, L, dtype=F32)[:, None]
    w = 2.0 * math.pi * jnp.arange(L, dtype=F32)[:, None] / L
    fr = jnp.linspace(1e-4, bands - 1, bands, dtype=F32)[None, :]
    feats = jnp.concatenate([t, jnp.cos(w * fr), -jnp.sin(w * fr)], axis=-1)
    max_decay = math.log(DECAY_TARGET) / FAST_DECAY_PCT
    min_decay = math.log(DECAY_TARGET) / SLOW_DECAY_PCT
    deltas = jnp.abs(jnp.linspace(min_decay, max_decay, width, dtype=F32))
    ncol = fw3.shape[1]
    delta_cols = jnp.tile(deltas, ncol // width)[None, :]
    is_bwd = ((jnp.arange(ncol) // width) % 2).astype(F32)[None, :]
    emb = hid = LANES

    def pad2(a, r, c):
        a = a.astype(F32)
        return jnp.zeros((r, c), F32).at[:a.shape[0], :a.shape[1]].set(a)

    row = lambda a: pad2(a.reshape(1, -1), 1, hid)
    feats = pad2(feats, L, emb)
    fw1, fw2, fw3 = pad2(fw1, emb, hid), pad2(fw2, hid, hid), pad2(fw3, hid, ncol)
    fb3 = fb3.reshape(1, ncol).astype(F32)
    const = lambda shape: pl.BlockSpec(shape, lambda i: (0, 0))
    h, sums = pl.pallas_call(
        _filter_body,
        grid=(L // tl,),
        in_specs=[
            pl.BlockSpec((tl, emb), lambda i: (i, 0)),
            pl.BlockSpec((tl, 1), lambda i: (i, 0)),
            const((emb, hid)), const((1, hid)), const((1, hid)),
            const((hid, hid)), const((1, hid)), const((1, hid)),
            const((hid, ncol)), const((1, ncol)), const((1, ncol)), const((1, ncol)),
        ],
        out_specs=[pl.BlockSpec((ncol // width, tl, width), lambda i: (0, i, 0)),
                   pl.BlockSpec((SUBLANES, ncol), lambda i: (0, 0))],
        out_shape=[jax.ShapeDtypeStruct((ncol // width, L, width), F32),
                   jax.ShapeDtypeStruct((SUBLANES, ncol), F32)],
        compiler_params=_cparams(("arbitrary",)),
        name="hyena_filter",
    )(feats, t, fw1, row(fb1), row(ff1), fw2, row(fb2), row(ff2), fw3, fb3, delta_cols, is_bwd)
    return h, sums[0]


def _shortconv_rows(z_ref, w_ref, b_ref, c, chunk):
    L = z_ref.shape[0]
    row = lax.broadcasted_iota(jnp.int32, (chunk, z_ref.shape[1]), 0)
    r0 = pl.multiple_of(c * chunk, chunk)
    cur = z_ref[pl.ds(r0, chunk), :]
    prev_row = z_ref[pl.ds(jnp.maximum(r0 - 1, 0), 1), :]
    next_row = z_ref[pl.ds(jnp.minimum(r0 + chunk, L - 1), 1), :]
    prev_row = jnp.where(c == 0, 0.0, prev_row)
    next_row = jnp.where(c == L // chunk - 1, 0.0, next_row)
    down = jnp.where(row == 0, prev_row, pltpu.roll(cur, 1, 0))
    up = jnp.where(row == chunk - 1, next_row, pltpu.roll(cur, chunk - 1, 0))
    return b_ref[...] + down * w_ref[0:1, :] + cur * w_ref[1:2, :] + up * w_ref[2:3, :]


def _dft_tables(n1_len, n2_len):
    n = n1_len * n2_len

    def root(num, den):
        ang = (2.0 * math.pi / den) * (num % den).astype(F32)
        return jnp.cos(ang), -jnp.sin(ang)

    i1 = jnp.arange(n1_len, dtype=jnp.int32)
    i2 = jnp.arange(n2_len, dtype=jnp.int32)
    f1r, f1i = root(i1[:, None] * i1[None, :], n1_len)
    f2r, f2i = root(i2[:, None] * i2[None, :], n2_len)
    twr, twi = root(i1[:, None] * i2[None, :], n)
    return (f1r, f1i), (f2r, f2i), (twr, twi)


def _stacked_inner_dft(f2r, f2i, twr_row, twi_row):
    gr = f2r * twr_row - f2i * twi_row
    gi = f2r * twi_row + f2i * twr_row
    top = jnp.concatenate([gr, -gi], axis=1)
    bot = jnp.concatenate([gi, gr], axis=1)
    return jnp.concatenate([top, bot], axis=0)


def _fft_plan(n):
    n1_len = n // FFT_N2
    nk1 = n1_len // 2 + 1
    nk1_pad = -(-nk1 // SUBLANES) * SUBLANES
    chunk = max(c for c in range(1, FFT_K1_CHUNK_MAX + 1) if nk1 % c == 0)
    return n1_len, nk1, nk1_pad, chunk


def _outer_dft_to_scratch(load_rows, fa_ref, ar_ref, ai_ref, nk1_pad):
    fa = fa_ref[...]

    def step(n2, carry):
        a = jnp.dot(fa, load_rows(n2).astype(BF16), preferred_element_type=F32)
        ar_ref[pl.ds(n2, nk1_pad, stride=FFT_PITCH), :] = a[:nk1_pad]
        ai_ref[pl.ds(n2, nk1_pad, stride=FFT_PITCH), :] = a[nk1_pad:]
        return carry

    lax.fori_loop(0, FFT_N2, step, 0, unroll=FFT_UNROLL_OUTER)


def _spec_body(f_ref, b_ref, inv_ref, fa_ref, f2r_ref, f2i_ref, twr_ref, twi_ref, kr_ref, ki_ref,
               fr_ref, fi_ref, br_ref, bi_ref, *, n1_len, nk1_pad, chunk):
    kc = pl.program_id(2)
    half = n1_len // 2

    @pl.when(kc == 0)
    def _():
        inv = inv_ref[...]
        row = lax.broadcasted_iota(jnp.int32, (half, LANES), 0)
        _outer_dft_to_scratch(lambda n2: f_ref[pl.ds(n2, half, stride=FFT_N2), :] * inv,
                              fa_ref, fr_ref, fi_ref, nk1_pad)
        _outer_dft_to_scratch(
            lambda n2: jnp.where((row == 0) & (n2 == 0), 0.0,
                                 b_ref[pl.ds(n2, half, stride=FFT_N2), :] * inv),
            fa_ref, br_ref, bi_ref, nk1_pad)

    f2r = f2r_ref[...]
    f2i = f2i_ref[...]

    def step(t, carry):
        k1 = kc * chunk + t
        base = pl.multiple_of(k1 * FFT_PITCH, SUBLANES)
        mf = _stacked_inner_dft(f2r, f2i, twr_ref[pl.ds(k1, 1), :],
                                twi_ref[pl.ds(k1, 1), :]).astype(BF16)
        rows = pl.ds(base, FFT_N2)
        xf = jnp.dot(mf, jnp.concatenate([fr_ref[rows, :], fi_ref[rows, :]], axis=0).astype(BF16),
                     preferred_element_type=F32)
        xb = jnp.dot(mf, jnp.concatenate([br_ref[rows, :], bi_ref[rows, :]], axis=0).astype(BF16),
                     preferred_element_type=F32)
        o = pl.multiple_of(t * FFT_N2, FFT_N2)
        kr_ref[pl.ds(o, FFT_N2), :] = xf[:FFT_N2] + xb[:FFT_N2]
        ki_ref[pl.ds(o, FFT_N2), :] = xf[FFT_N2:] - xb[FFT_N2:]
        return carry

    lax.fori_loop(0, chunk, step, 0, unroll=FFT_UNROLL_INNER)


def _filter_spectrum(h, inv_norm, tables):
    O2, L, C = h.shape
    O = O2 // 2
    n1_len, nk1, nk1_pad, chunk = _fft_plan(2 * L)
    half = n1_len // 2
    (f1r, f1i), (f2r, f2i), (twr, twi) = tables
    fa = jnp.concatenate([f1r[:nk1_pad, :half], f1i[:nk1_pad, :half]], axis=0).astype(BF16)
    rows = chunk * FFT_N2
    const = lambda shape: pl.BlockSpec(shape, lambda o, c, k: (0, 0))
    out = jax.ShapeDtypeStruct((O, nk1 * FFT_N2, C), F32)
    scratch = pltpu.VMEM((nk1_pad * FFT_PITCH, LANES), F32)
    return pl.pallas_call(
        functools.partial(_spec_body, n1_len=n1_len, nk1_pad=nk1_pad, chunk=chunk),
        grid=(O, C // LANES, nk1 // chunk),
        in_specs=[
            pl.BlockSpec((None, L, LANES), lambda o, c, k: (2 * o, 0, c)),
            pl.BlockSpec((None, L, LANES), lambda o, c, k: (2 * o + 1, 0, c)),
            pl.BlockSpec((None, 1, LANES), lambda o, c, k: (o, 0, c)),
            const(fa.shape), const(f2r.shape), const(f2i.shape), const(twr.shape), const(twi.shape),
        ],
        out_specs=[pl.BlockSpec((None, rows, LANES), lambda o, c, k: (o, k, c)),
                   pl.BlockSpec((None, rows, LANES), lambda o, c, k: (o, k, c))],
        out_shape=[out, out],
        scratch_shapes=[scratch, scratch, scratch, scratch],
        compiler_params=_cparams(("parallel", "parallel", "arbitrary")),
        name="filter_spectrum",
    )(h, h, inv_norm, fa, f2r, f2i, twr, twi)


def _fftconv_body(y_ref, gate_ref, wy_ref, by_ref, wg_ref, bg_ref, d_ref, kr_ref, ki_ref, fa_ref,
                  fs_ref, f2r_ref, f2i_ref, twr_ref, twi_ref, o_ref, xs_ref, ar_ref, ai_ref, *,
                  n1_len, nk1_pad, chunk, conv_y):
    kc = pl.program_id(2)
    half = n1_len // 2

    @pl.when(kc == 0)
    def _():
        def copy(n1, carry):
            src = pl.multiple_of(n1 * FFT_N2, FFT_N2)
            dst = pl.multiple_of(n1 * FFT_PITCH, SUBLANES)
            if conv_y:
                xs_ref[pl.ds(dst, FFT_N2), :] = _shortconv_rows(y_ref, wy_ref, by_ref, n1, FFT_N2)
            else:
                xs_ref[pl.ds(dst, FFT_N2), :] = y_ref[pl.ds(src, FFT_N2), :]
            return carry

        lax.fori_loop(0, half, copy, 0)
        _outer_dft_to_scratch(lambda n2: xs_ref[pl.ds(n2, half, stride=FFT_PITCH), :],
                              fa_ref, ar_ref, ai_ref, nk1_pad)

    f2r = f2r_ref[...]
    f2i = f2i_ref[...]

    def step(t, carry):
        k1 = kc * chunk + t
        base = pl.multiple_of(k1 * FFT_PITCH, SUBLANES)
        rhs = jnp.concatenate([ar_ref[pl.ds(base, FFT_N2), :], ai_ref[pl.ds(base, FFT_N2), :]],
                              axis=0).astype(BF16)
        mf = _stacked_inner_dft(f2r, f2i, twr_ref[pl.ds(k1, 1), :], twi_ref[pl.ds(k1, 1), :])
        x = jnp.dot(mf.astype(BF16), rhs, preferred_element_type=F32)
        xr, xi = x[:FFT_N2], x[FFT_N2:]
        o = pl.multiple_of(t * FFT_N2, FFT_N2)
        kr = kr_ref[pl.ds(o, FFT_N2), :]
        ki = ki_ref[pl.ds(o, FFT_N2), :]
        z = jnp.concatenate([xr * kr - xi * ki, xr * ki + xi * kr], axis=0).astype(BF16)
        b = jnp.dot(mf.T.astype(BF16), z, preferred_element_type=F32)
        ar_ref[pl.ds(base, FFT_N2), :] = b[:FFT_N2]
        ai_ref[pl.ds(base, FFT_N2), :] = b[FFT_N2:]
        return carry

    lax.fori_loop(0, chunk, step, 0, unroll=FFT_UNROLL_INNER)

    @pl.when(kc == pl.num_programs(2) - 1)
    def _():
        fs = fs_ref[...]

        def inv_outer(n2, carry):
            rhs = jnp.concatenate([ar_ref[pl.ds(n2, nk1_pad, stride=FFT_PITCH), :],
                                   ai_ref[pl.ds(n2, nk1_pad, stride=FFT_PITCH), :]],
                                  axis=0).astype(BF16)
            conv = jnp.dot(fs, rhs, preferred_element_type=F32)
            ar_ref[pl.ds(n2, half, stride=FFT_PITCH), :] = conv
            return carry

        lax.fori_loop(0, FFT_N2, inv_outer, 0, unroll=FFT_UNROLL_OUTER)
        d = d_ref[...]

        def finish(n1, carry):
            src = pl.multiple_of(n1 * FFT_PITCH, SUBLANES)
            dst = pl.multiple_of(n1 * FFT_N2, FFT_N2)
            y = xs_ref[pl.ds(src, FFT_N2), :]
            gate = _shortconv_rows(gate_ref, wg_ref, bg_ref, n1, FFT_N2)
            o_ref[pl.ds(dst, FFT_N2), :] = (gate * (
                ar_ref[pl.ds(src, FFT_N2), :] + y * d)).astype(o_ref.dtype)
            return carry

        lax.fori_loop(0, half, finish, 0)


def _fftconv_gate(y, y_off, conv_y, z, gate_off, conv_w, conv_b, d, kr, ki, tables, out_dtype):
    B, L, _ = y.shape
    C = d.shape[-1]
    yo, go = y_off // LANES, gate_off // LANES
    wo = yo if conv_y else go
    N = 2 * L
    n1_len, nk1, nk1_pad, chunk = _fft_plan(N)
    half = n1_len // 2
    (f1r, f1i), (f2r, f2i), (twr, twi) = tables
    fa = jnp.concatenate([f1r[:nk1_pad, :half], f1i[:nk1_pad, :half]], axis=0).astype(BF16)
    wts = jnp.concatenate([jnp.ones((1,), F32), jnp.full((nk1 - 2,), 2.0, F32), jnp.ones((1,), F32),
                           jnp.zeros((nk1_pad - nk1,), F32)]) * (1.0 / N)
    fs = jnp.concatenate([f1r[:half, :nk1_pad] * wts, f1i[:half, :nk1_pad] * wts],
                         axis=1).astype(BF16)
    rows = chunk * FFT_N2
    const = lambda shape: pl.BlockSpec(shape, lambda c, b, k: (0, 0))
    return pl.pallas_call(
        functools.partial(_fftconv_body, n1_len=n1_len, nk1_pad=nk1_pad, chunk=chunk,
                          conv_y=conv_y),
        grid=(C // LANES, B, nk1 // chunk),
        in_specs=[
            pl.BlockSpec((None, L, LANES), lambda c, b, k: (b, 0, c + yo)),
            pl.BlockSpec((None, L, LANES), lambda c, b, k: (b, 0, c + go)),
            pl.BlockSpec((SHORT_CONV, LANES), lambda c, b, k: (0, c + wo)),
            pl.BlockSpec((1, LANES), lambda c, b, k: (0, c + wo)),
            pl.BlockSpec((SHORT_CONV, LANES), lambda c, b, k: (0, c + go)),
            pl.BlockSpec((1, LANES), lambda c, b, k: (0, c + go)),
            pl.BlockSpec((1, LANES), lambda c, b, k: (0, c)),
            pl.BlockSpec((rows, LANES), lambda c, b, k: (k, c)),
            pl.BlockSpec((rows, LANES), lambda c, b, k: (k, c)),
            const(fa.shape), const(fs.shape), const(f2r.shape), const(f2i.shape),
            const(twr.shape), const(twi.shape),
        ],
        out_specs=pl.BlockSpec((None, L, LANES), lambda c, b, k: (b, 0, c)),
        out_shape=jax.ShapeDtypeStruct((B, L, C), out_dtype),
        scratch_shapes=[pltpu.VMEM((half * FFT_PITCH, LANES), F32),
                        pltpu.VMEM((nk1_pad * FFT_PITCH, LANES), F32),
                        pltpu.VMEM((nk1_pad * FFT_PITCH, LANES), F32)],
        compiler_params=_cparams(("parallel", "parallel", "arbitrary")),
        name="fftconv_gate",
    )(y, z, conv_w, conv_b, conv_w, conv_b, d, kr, ki, fa, fs, f2r, f2i, twr, twi)


def _hyena(z, conv_w, conv_b, fw1, fb1, ff1, fw2, fb2, ff2, fw3, fb3, hyena_d):
    B, L, C3 = z.shape
    W = C3 // 3
    h, sums = _hyena_filter(L, W, fw1, fb1, ff1, fw2, fb2, ff2, fw3, fb3)
    sums = sums.reshape(HYENA_ORDER, 2, W)
    inv_norm = (1.0 / (sums[:, 0] + sums[:, 1]))[:, None, :]
    tables = _dft_tables(2 * L // FFT_N2, FFT_N2)
    kr, ki = _filter_spectrum(h, inv_norm, tables)
    conv_b = conv_b.reshape(1, C3)
    y = z
    for o in range(HYENA_ORDER):
        y = _fftconv_gate(y, 0, o == 0, z, (o + 1) * W, conv_w, conv_b,
                          hyena_d[o].reshape(1, W).astype(F32), kr[o], ki[o], tables,
                          BF16 if o == HYENA_ORDER - 1 else F32)
    return y


def _merge_body(yh_ref, ya_ref, g_ref, x_ref, whu_ref, wau_ref, wo_ref, n2_ref, rwh_ref, rwl_ref,
                rb_ref, h_ref, hn_ref, lg_ref):
    D = x_ref.shape[-1]
    up_h = jnp.dot(yh_ref[...], whu_ref[...], preferred_element_type=F32)
    up_a = jnp.dot(ya_ref[...], wau_ref[...], preferred_element_type=F32)
    g = g_ref[...].astype(F32)
    merged = jax.nn.sigmoid(g[:, :D]) * up_h + jax.nn.sigmoid(g[:, D:]) * up_a
    h = x_ref[...] + jnp.dot(merged.astype(BF16), wo_ref[...], preferred_element_type=F32)
    h_ref[...] = h
    ms = jnp.mean(h * h, axis=-1, keepdims=True)
    hn = h * lax.rsqrt(ms + EPS) * n2_ref[...]
    hn_ref[...] = hn.astype(BF16)
    hh, hl = _split_bf16(hn)
    lg_ref[...] = (jnp.dot(hh, rwh_ref[...], preferred_element_type=F32)
                   + jnp.dot(hl, rwh_ref[...], preferred_element_type=F32)
                   + jnp.dot(hh, rwl_ref[...], preferred_element_type=F32)) + rb_ref[...]


def _merge(yh, ya, gates, xt, whu, wau, wo, n2g, rw, rb, tm=1024):
    T, D = xt.shape
    W = yh.shape[1]
    E = rw.shape[1]
    rwp = jnp.zeros((D, LOGIT_PAD), F32).at[:, :E].set(rw)
    rwh, rwl = _split_bf16(rwp)
    rbp = jnp.zeros((1, LOGIT_PAD), F32).at[0, :E].set(rb)
    rowblk = lambda w: pl.BlockSpec((tm, w), lambda i: (i, 0))
    const = lambda shape: pl.BlockSpec(shape, lambda i: (0, 0))
    return pl.pallas_call(
        _merge_body,
        grid=(T // tm,),
        in_specs=[rowblk(W), rowblk(W), rowblk(2 * D), rowblk(D),
                  const((W, D)), const((W, D)), const((D, D)), const((1, D)),
                  const((D, LOGIT_PAD)), const((D, LOGIT_PAD)), const((1, LOGIT_PAD))],
        out_specs=[rowblk(D), rowblk(D), rowblk(LOGIT_PAD)],
        out_shape=[jax.ShapeDtypeStruct((T, D), F32), jax.ShapeDtypeStruct((T, D), BF16),
                   jax.ShapeDtypeStruct((T, LOGIT_PAD), F32)],
        compiler_params=_cparams(("parallel",)),
        name="merge",
    )(yh, ya, gates, xt, whu.astype(BF16), wau.astype(BF16), wo.astype(BF16), n2g.reshape(1, D),
      rwh, rwl, rbp)


def _moe_body(be_ref, nused_ref, x_ref, w1g_ref, w1l_ref, b1g_ref, b1l_ref, w2_ref, b2_ref, o_ref):
    i = pl.program_id(0)

    @pl.when(i < nused_ref[0])
    def _():
        x = x_ref[...]
        nt = (((1,), (1,)), ((), ()))
        glu = lax.dot_general(x, w1g_ref[...], nt, preferred_element_type=F32) + b1g_ref[...]
        lin = lax.dot_general(x, w1l_ref[...], nt, preferred_element_type=F32) + b1l_ref[...]
        glu = jnp.minimum(glu, SWIGLU_LIMIT)
        lin = jnp.clip(lin, -SWIGLU_LIMIT, SWIGLU_LIMIT)
        act = glu * jax.nn.sigmoid(SWIGLU_ALPHA * glu) * (lin + 1.0)
        y = jnp.dot(act.astype(BF16), w2_ref[...].astype(BF16),
                    preferred_element_type=F32) + b2_ref[...]
        o_ref[...] = y.astype(o_ref.dtype)

    @pl.when(i >= nused_ref[0])
    def _():
        o_ref[...] = jnp.zeros(o_ref.shape, o_ref.dtype)


def _moe_experts(xg, block_expert, n_used, w1g, w1l, b1g, b1l, w2, b2):
    P, D = xg.shape
    dff = w2.shape[1]
    nb = P // MOE_TM
    wspec = lambda k, n: pl.BlockSpec((None, k, n), lambda i, be, nu: (be[i], 0, 0))
    grid_spec = pltpu.PrefetchScalarGridSpec(
        num_scalar_prefetch=2,
        grid=(nb,),
        in_specs=[
            pl.BlockSpec((MOE_TM, D), lambda i, be, nu: (i, 0)),
            wspec(dff, D), wspec(dff, D), wspec(1, dff), wspec(1, dff),
            wspec(dff, D), wspec(1, D),
        ],
        out_specs=pl.BlockSpec((MOE_TM, D), lambda i, be, nu: (i, 0)),
    )
    return pl.pallas_call(
        _moe_body,
        grid_spec=grid_spec,
        out_shape=jax.ShapeDtypeStruct((P, D), BF16),
        compiler_params=_cparams(("arbitrary",)),
        name="moe_experts",
    )(block_expert, n_used, xg, w1g, w1l, b1g, b1l, w2, b2)


def _prep_w1_body(w_ref, sel_ref, g_ref, l_ref):
    half = g_ref.shape[0]
    picked = lax.dot_general(sel_ref[...], w_ref[...].astype(BF16), (((1,), (1,)), ((), ())),
                             preferred_element_type=F32)
    g_ref[...] = picked[:half].astype(BF16)
    l_ref[...] = picked[half:].astype(BF16)


def _prep_w1(w1, tc=8 * LANES):
    E, D, F2 = w1.shape
    tc = min(tc, F2)
    half = tc // 2
    row = jnp.arange(tc)
    picks = jnp.where(row < half, 2 * row, 2 * (row - half) + 1)
    sel = (picks[:, None] == jnp.arange(tc)[None, :]).astype(BF16)
    out = jax.ShapeDtypeStruct((E, F2 // 2, D), BF16)
    return pl.pallas_call(
        _prep_w1_body,
        grid=(E, F2 // tc),
        in_specs=[pl.BlockSpec((None, D, tc), lambda e, j: (e, 0, j)),
                  pl.BlockSpec((tc, tc), lambda e, j: (0, 0))],
        out_specs=[pl.BlockSpec((None, half, D), lambda e, j: (e, j, 0)),
                   pl.BlockSpec((None, half, D), lambda e, j: (e, j, 0))],
        out_shape=[out, out],
        compiler_params=_cparams(("parallel", "parallel")),
        name="prep_w1",
    )(w1, sel)


def _combine_body(h_ref, y_ref, g_ref, o_ref):
    acc = h_ref[...]
    g = g_ref[...]
    for k in range(y_ref.shape[0]):
        acc = acc + g[:, k:k + 1] * y_ref[k].astype(F32)
    o_ref[...] = acc


def _combine(h, yk, gates, tm=512):
    T, D = h.shape
    K = yk.shape[0]
    return pl.pallas_call(
        _combine_body,
        grid=(T // tm,),
        in_specs=[pl.BlockSpec((tm, D), lambda i: (i, 0)),
                  pl.BlockSpec((K, tm, D), lambda i: (0, i, 0)),
                  pl.BlockSpec((tm, K), lambda i: (i, 0))],
        out_specs=pl.BlockSpec((tm, D), lambda i: (i, 0)),
        out_shape=jax.ShapeDtypeStruct((T, D), F32),
        compiler_params=_cparams(("parallel",)),
        name="moe_combine",
    )(h, yk, gates)


def _lookup(table, idx):
    n = table.shape[0]
    hit = idx[None, :] == jnp.arange(n, dtype=idx.dtype)[:, None]
    return jnp.sum(jnp.where(hit, table[:, None], 0), axis=0)


def _moe(h, hn_bf16, logits, w1, b1, w2, b2):
    T, D = hn_bf16.shape
    E = w1.shape[0]
    top_val, top_idx = lax.top_k(logits, TOP_K)
    gates = jax.nn.softmax(top_val, axis=-1)
    TK = T * TOP_K
    e_flat = top_idx.reshape(TK).astype(jnp.int32)
    order = jnp.argsort(e_flat).astype(jnp.int32)
    rank = jnp.argsort(order).astype(jnp.int32)
    counts = jnp.sum(jnp.arange(E, dtype=jnp.int32)[:, None] == e_flat[None, :], axis=1,
                     dtype=jnp.int32)
    starts = jnp.cumsum(counts) - counts
    padded = ((counts + MOE_TM - 1) // MOE_TM) * MOE_TM
    pad_ends = jnp.cumsum(padded)
    pad_starts = pad_ends - padded
    nb = (TK + E * (MOE_TM - 1) + MOE_TM - 1) // MOE_TM
    block_start = jnp.arange(nb, dtype=jnp.int32) * MOE_TM
    block_expert = jnp.minimum(jnp.sum(block_start[:, None] >= pad_ends[None, :], axis=1),
                               E - 1).astype(jnp.int32)
    n_used = (pad_ends[-1] // MOE_TM).astype(jnp.int32).reshape(1)
    blk_first = block_start - pad_starts[block_expert]
    within = blk_first[:, None] + jnp.arange(MOE_TM, dtype=jnp.int32)[None, :]
    valid = within < counts[block_expert][:, None]
    sorted_idx = jnp.where(valid, starts[block_expert][:, None] + within, 0).reshape(nb * MOE_TM)
    filler = jnp.arange(nb * MOE_TM, dtype=jnp.int32) % T
    src = jnp.where(valid.reshape(nb * MOE_TM), order[sorted_idx] // TOP_K, filler)
    pos = _lookup(pad_starts - starts, e_flat) + rank
    w1g, w1l = _prep_w1(w1)
    y = _moe_experts(hn_bf16[src], block_expert, n_used, w1g, w1l,
                     b1[:, None, 0::2].astype(F32), b1[:, None, 1::2].astype(F32),
                     w2, b2[:, None, :].astype(F32))
    return _combine(h, y[pos.reshape(T, TOP_K).T], gates)


def _rope_tables(L):
    rows = L // GRID_W
    row = jnp.repeat(jnp.arange(rows, dtype=F32), GRID_W)
    col = jnp.tile(jnp.arange(GRID_W, dtype=F32), rows)
    half = HEAD_DIM // 2
    freqs = ROPE_THETA ** (-jnp.arange(0, half, 2, dtype=F32) / half)
    ang = jnp.concatenate([row[:, None] * freqs, col[:, None] * freqs], axis=-1)
    cos = jnp.repeat(jnp.cos(ang), 2, axis=-1)
    sin = jnp.repeat(jnp.sin(ang), 2, axis=-1)
    sign = jnp.tile(jnp.array([-1.0, 1.0], F32), HEAD_DIM // 2)
    reps = LANES // HEAD_DIM
    return jnp.tile(cos, (1, reps)), jnp.tile(sin * sign, (1, reps))


def kernel(x, norm1_g, w_in, conv_w, conv_b, filt_w1, filt_b1, filt_freq1, filt_w2, filt_b2, filt_freq2, filt_w3, filt_b3, hyena_d, q_norm_g, k_norm_g, w_hyena_up, w_attn_up, w_out, norm2_g, router_w, router_b, expert_w1, expert_b1, expert_w2, expert_b2):
    B, L, D = x.shape
    T = B * L
    depth = w_in.shape[0]
    hw = conv_w.shape[-1] // 3
    aw = N_Q_HEADS * HEAD_DIM
    kvw = N_KV_HEADS * HEAD_DIM
    widths = (3 * hw, aw, 2 * kvw, 2 * D)
    cosf, sinf = _rope_tables(L)
    for l in range(depth):
        xt = x.reshape(T, D)
        gq = jnp.tile(q_norm_g[l].astype(F32), N_Q_HEADS)[None, :]
        gk = jnp.tile(k_norm_g[l].astype(F32), N_KV_HEADS)[None, :]
        z, qr, kr, va, gates = _inproj(xt, norm1_g[l], w_in[l].astype(BF16), widths, gq, gk,
                                       cosf, sinf, L)
        y_hy = _hyena(z.reshape(B, L, 3 * hw), conv_w[l], conv_b[l], filt_w1[l], filt_b1[l],
                      filt_freq1[l], filt_w2[l], filt_b2[l], filt_freq2[l], filt_w3[l], filt_b3[l],
                      hyena_d[l])
        y_at = _attention(qr.reshape(B, L, aw), kr, va.reshape(N_KV_HEADS, B, L, kvw))
        h, hn, logits = _merge(y_hy.reshape(T, hw), y_at.reshape(T, aw), gates, xt,
                               w_hyena_up[l], w_attn_up[l], w_out[l], norm2_g[l],
                               router_w[l], router_b[l])
        x = _moe(h, hn, logits[:, :N_EXPERTS], expert_w1[l], expert_b1[l], expert_w2[l],
                 expert_b2[l]).reshape(B, L, D)
    return x
```

```python
import functools
import math

import jax
import jax.numpy as jnp
from jax import lax
from jax.experimental import pallas as pl
from jax.experimental.pallas import tpu as pltpu

F32 = jnp.float32
BF16 = jnp.bfloat16

GRID_W = 64
HEAD_DIM = 64
N_Q_HEADS = 8
N_KV_HEADS = 2
Q_PER_KV = N_Q_HEADS // N_KV_HEADS
ROPE_THETA = 10000.0
HYENA_ORDER = 2
SHORT_CONV = 3
FILTER_EMB = 33
FAST_DECAY_PCT = 0.3
SLOW_DECAY_PCT = 1.5
DECAY_TARGET = 1e-2
N_EXPERTS = 32
TOP_K = 4
SWIGLU_LIMIT = 7.0
SWIGLU_ALPHA = 1.702
EPS = 1e-6

LANES = 128
SUBLANES = 8
VMEM_LIMIT = 56 * 1024 * 1024

FFT_N2 = LANES
FFT_PITCH = FFT_N2 + SUBLANES
FFT_K1_CHUNK_MAX = 16
FFT_UNROLL_OUTER = 16
FFT_UNROLL_INNER = True
MOE_TM = 512
LOGIT_PAD = LANES


def _cparams(sem):
    return pltpu.CompilerParams(dimension_semantics=sem, vmem_limit_bytes=VMEM_LIMIT)


def _split_bf16(x):
    hi = x.astype(BF16)
    lo = (x - hi.astype(F32)).astype(BF16)
    return hi, lo


def _inproj_body(x_ref, g_ref, w_ref, gq_ref, gk_ref, cos_ref, sin_ref, oq_ref, ok_ref,
                 z_ref, qo_ref, ko_ref, vo_ref, gate_ref, q_scr, kv_scr, *, widths):
    @pl.when(pl.program_id(0) == 0)
    def _():
        q_scr[...] = jnp.zeros(q_scr.shape, F32)
        kv_scr[...] = jnp.zeros(kv_scr.shape, F32)

    x = x_ref[...]
    ms = jnp.mean(x * x, axis=-1, keepdims=True)
    u = (x * lax.rsqrt(ms + EPS) * g_ref[...]).astype(BF16)
    offs = [sum(widths[:i]) for i in range(len(widths) + 1)]
    proj = lambda i: jnp.dot(u, w_ref[:, offs[i]:offs[i + 1]], preferred_element_type=F32)
    z_ref[...] = proj(0)
    gate_ref[...] = proj(3).astype(gate_ref.dtype)
    _qk_epilogue(q_scr[...], kv_scr[...], gq_ref[...], gk_ref[...], cos_ref[...], sin_ref[...],
                 oq_ref[...], ok_ref[...], qo_ref, ko_ref, vo_ref)
    q_scr[...] = proj(1)
    kv_scr[...] = proj(2)


def _inproj(xt, g, w_bf16, widths, gq, gk, cosf, sinf, seq_len, tm=512):
    T, D = xt.shape
    n = w_bf16.shape[1]
    zw, qw, kvw, gw = widths
    kw = kvw // 2
    assert kw == LANES and N_KV_HEADS * HEAD_DIM == LANES and N_KV_HEADS == 2
    nl = seq_len // tm
    nt = T // tm
    cur = lambda i: jnp.minimum(i, nt - 1)
    prev = lambda i: jnp.maximum(i - 1, 0)

    def blk_ones(w):
        r = jnp.arange(w) // HEAD_DIM
        return (r[:, None] == r[None, :]).astype(BF16)

    const = lambda shape: pl.BlockSpec(shape, lambda i: (0, 0))
    return pl.pallas_call(
        functools.partial(_inproj_body, widths=widths),
        grid=(nt + 1,),
        in_specs=[
            pl.BlockSpec((tm, D), lambda i: (cur(i), 0)),
            const((1, D)), const((D, n)), const((1, qw)), const((1, kw)),
            pl.BlockSpec((tm, LANES), lambda i: (prev(i) % nl, 0)),
            pl.BlockSpec((tm, LANES), lambda i: (prev(i) % nl, 0)),
            const((qw, qw)), const((kw, kw)),
        ],
        out_specs=[
            pl.BlockSpec((tm, zw), lambda i: (cur(i), 0)),
            pl.BlockSpec((tm, qw), lambda i: (prev(i), 0)),
            pl.BlockSpec((kw, tm), lambda i: (0, prev(i))),
            pl.BlockSpec((N_KV_HEADS, tm, kw), lambda i: (0, prev(i), 0)),
            pl.BlockSpec((tm, gw), lambda i: (cur(i), 0)),
        ],
        out_shape=[
            jax.ShapeDtypeStruct((T, zw), F32),
            jax.ShapeDtypeStruct((T, qw), BF16),
            jax.ShapeDtypeStruct((kw, T), BF16),
            jax.ShapeDtypeStruct((N_KV_HEADS, T, kw), BF16),
            jax.ShapeDtypeStruct((T, gw), BF16),
        ],
        scratch_shapes=[pltpu.VMEM((tm, qw), F32), pltpu.VMEM((tm, kvw), F32)],
        compiler_params=_cparams(("arbitrary",)),
        name="inproj",
    )(xt, g.reshape(1, D), w_bf16, gq, gk, cosf, sinf, blk_ones(qw), blk_ones(kw))


def _head_norm_rope(x, gain, cosf, sinf, ones_blk):
    w = x.shape[-1]
    hi, lo = _split_bf16(x * x)
    ss = (jnp.dot(hi, ones_blk, preferred_element_type=F32)
          + jnp.dot(lo, ones_blk, preferred_element_type=F32))
    xn = x * lax.rsqrt(ss * (1.0 / HEAD_DIM) + EPS) * gain
    lane = lax.broadcasted_iota(jnp.int32, (x.shape[0], LANES), 1)
    cols = []
    for c in range(w // LANES):
        col = xn[:, c * LANES:(c + 1) * LANES]
        cols.append(jnp.where(lane % 2 == 0, pltpu.roll(col, LANES - 1, 1), pltpu.roll(col, 1, 1)))
    swapped = cols[0] if len(cols) == 1 else jnp.concatenate(cols, axis=1)
    return xn * cosf + swapped * sinf


def _qk_epilogue(q, kv, gq, gk, cosf, sinf, ones_q, ones_k, qo_ref, ko_ref, vo_ref):
    nq = q.shape[-1] // LANES
    q = _head_norm_rope(q, gq, jnp.tile(cosf, (1, nq)), jnp.tile(sinf, (1, nq)), ones_q)
    qo_ref[...] = (q * (HEAD_DIM ** -0.5 * math.log2(math.e))).astype(BF16)
    kw = kv.shape[-1] // 2
    k = _head_norm_rope(kv[:, :kw], gk, cosf, sinf, ones_k)
    ko_ref[...] = k.T.astype(BF16)
    v = kv[:, kw:]
    lane = lax.broadcasted_iota(jnp.int32, v.shape, 1)
    for h in range(N_KV_HEADS):
        vo_ref[h] = jnp.where((lane // HEAD_DIM) == h, v, 1.0).astype(BF16)


def _attn_body(q_ref, k_ref, v_ref, o_ref, qs_ref, m_ref, acc_ref, *, tk, nsplit):
    kvh = pl.program_id(1)
    tq = q_ref.shape[0]
    seq = k_ref.shape[1]
    rows = Q_PER_KV * tq
    lane = lax.broadcasted_iota(jnp.int32, (tq, LANES), 1)
    in_head = (lane // HEAD_DIM) == kvh

    for g in range(Q_PER_KV):
        col = q_ref[:, (g // 2) * LANES:(g // 2 + 1) * LANES].astype(F32)
        col = jnp.where((g % 2) == kvh, col, pltpu.roll(col, HEAD_DIM, 1))
        qs_ref[g * tq:(g + 1) * tq, :] = jnp.where(in_head, col, 0.0).astype(BF16)
    m_ref[...] = jnp.full(m_ref.shape, -jnp.inf, F32)
    acc_ref[...] = jnp.zeros(acc_ref.shape, F32)
    part = rows // nsplit

    def step(c, carry):
        r0 = pl.multiple_of(c * tk, tk)
        kc = k_ref[:, pl.ds(r0, tk)]
        vc = v_ref[pl.ds(r0, tk), :]
        for h in range(nsplit):
            sl = slice(h * part, (h + 1) * part)
            s = jnp.dot(qs_ref[sl, :], kc, preferred_element_type=F32)
            m_prev = m_ref[sl, :]
            m_new = jnp.maximum(m_prev, jnp.max(s, axis=-1, keepdims=True))
            p = jnp.exp2(s - jnp.tile(m_new, (1, tk // LANES)))
            acc_ref[sl, :] = jnp.exp2(m_prev - m_new) * acc_ref[sl, :] + jnp.dot(
                p.astype(BF16), vc, preferred_element_type=F32)
            m_ref[sl, :] = m_new
        return carry

    lax.fori_loop(0, seq // tk, step, 0)

    acc = acc_ref[...]
    o = acc / pltpu.roll(acc, HEAD_DIM, 1)
    for c in range(Q_PER_KV // 2):
        even = o[(2 * c) * tq:(2 * c + 1) * tq, :]
        odd = o[(2 * c + 1) * tq:(2 * c + 2) * tq, :]
        even = jnp.where(kvh == 0, even, pltpu.roll(even, HEAD_DIM, 1))
        odd = jnp.where(kvh == 1, odd, pltpu.roll(odd, HEAD_DIM, 1))
        o_ref[:, c * LANES:(c + 1) * LANES] = jnp.where(lane < HEAD_DIM, even, odd).astype(o_ref.dtype)


def _attention(q, kt, v_aug, tq=512, tk=2048, nsplit=2):
    B, L, qw = q.shape
    gw = Q_PER_KV * HEAD_DIM
    kw = kt.shape[0]
    rows = Q_PER_KV * tq
    return pl.pallas_call(
        functools.partial(_attn_body, tk=tk, nsplit=nsplit),
        grid=(B, N_KV_HEADS, L // tq),
        in_specs=[
            pl.BlockSpec((None, tq, gw), lambda b, h, i: (b, i, h)),
            pl.BlockSpec((kw, L), lambda b, h, i: (0, b)),
            pl.BlockSpec((None, None, L, kw), lambda b, h, i: (h, b, 0, 0)),
        ],
        out_specs=pl.BlockSpec((None, tq, gw), lambda b, h, i: (b, i, h)),
        out_shape=jax.ShapeDtypeStruct((B, L, qw), BF16),
        scratch_shapes=[
            pltpu.VMEM((rows, kw), BF16),
            pltpu.VMEM((rows, LANES), F32),
            pltpu.VMEM((rows, kw), F32),
        ],
        compiler_params=_cparams(("parallel", "parallel", "parallel")),
        name="attention",
    )(q, kt, v_aug)


def _hdot(a, b):
    ah, al = _split_bf16(a)
    bh, bl = _split_bf16(b)
    return (jnp.dot(ah, bh, preferred_element_type=F32)
            + jnp.dot(al, bh, preferred_element_type=F32)
            + jnp.dot(ah, bl, preferred_element_type=F32))


def _filter_body(feat_ref, t_ref, w1_ref, b1_ref, f1_ref, w2_ref, b2_ref, f2_ref, w3_ref, b3_ref,
                 delta_ref, bwd_ref, h_ref, sum_ref):
    i = pl.program_id(0)
    h = jnp.sin(f1_ref[...] * (_hdot(feat_ref[...], w1_ref[...]) + b1_ref[...]))
    h = jnp.sin(f2_ref[...] * (_hdot(h, w2_ref[...]) + b2_ref[...]))
    h = _hdot(h, w3_ref[...]) + b3_ref[...]
    t = t_ref[...]
    h = h * jnp.exp(-t * delta_ref[...])
    width = h_ref.shape[-1]
    for j in range(h_ref.shape[0]):
        h_ref[j] = h[:, j * width:(j + 1) * width]
    row = lax.broadcasted_iota(jnp.int32, h.shape, 0) + i * h.shape[0]
    a = jnp.where((row == 0) & (bwd_ref[...] > 0.5), 0.0, jnp.abs(h))
    part = jnp.sum(a, axis=0, keepdims=True)

    @pl.when(i == 0)
    def _():
        sum_ref[...] = jnp.zeros(sum_ref.shape, F32)

    sum_ref[...] += jnp.broadcast_to(part, sum_ref.shape)


def _hyena_filter(L, width, fw1, fb1, ff1, fw2, fb2, ff2, fw3, fb3, tl=512):
    bands = (FILTER_EMB - 1) // 2
    t = jnp.linspace(0.0, 1.0, L, dtype=F32)[:, None]
    w = 2.0 * math.pi * jnp.arange(L, dtype=F32)[:, None] / L
    fr = jnp.linspace(1e-4, bands - 1, bands, dtype=F32)[None, :]
    feats = jnp.concatenate([t, jnp.cos(w * fr), -jnp.sin(w * fr)], axis=-1)
    max_decay = math.log(DECAY_TARGET) / FAST_DECAY_PCT
    min_decay = math.log(DECAY_TARGET) / SLOW_DECAY_PCT
    deltas = jnp.abs(jnp.linspace(min_decay, max_decay, width, dtype=F32))
    ncol = fw3.shape[1]
    delta_cols = jnp.tile(deltas, ncol // width)[None, :]
    is_bwd = ((jnp.arange(ncol) // width) % 2).astype(F32)[None, :]
    emb = hid = LANES

    def pad2(a, r, c):
        a = a.astype(F32)
        return jnp.zeros((r, c), F32).at[:a.shape[0], :a.shape[1]].set(a)

    row = lambda a: pad2(a.reshape(1, -1), 1, hid)
    feats = pad2(feats, L, emb)
    fw1, fw2, fw3 = pad2(fw1, emb, hid), pad2(fw2, hid, hid), pad2(fw3, hid, ncol)
    fb3 = fb3.reshape(1, ncol).astype(F32)
    const = lambda shape: pl.BlockSpec(shape, lambda i: (0, 0))
    h, sums = pl.pallas_call(
        _filter_body,
        grid=(L // tl,),
        in_specs=[
            pl.BlockSpec((tl, emb), lambda i: (i, 0)),
            pl.BlockSpec((tl, 1), lambda i: (i, 0)),
            const((emb, hid)), const((1, hid)), const((1, hid)),
            const((hid, hid)), const((1, hid)), const((1, hid)),
            const((hid, ncol)), const((1, ncol)), const((1, ncol)), const((1, ncol)),
        ],
        out_specs=[pl.BlockSpec((ncol // width, tl, width), lambda i: (0, i, 0)),
                   pl.BlockSpec((SUBLANES, ncol), lambda i: (0, 0))],
        out_shape=[jax.ShapeDtypeStruct((ncol // width, L, width), F32),
                   jax.ShapeDtypeStruct((SUBLANES, ncol), F32)],
        compiler_params=_cparams(("arbitrary",)),
        name="hyena_filter",
    )(feats, t, fw1, row(fb1), row(ff1), fw2, row(fb2), row(ff2), fw3, fb3, delta_cols, is_bwd)
    return h, sums[0]


def _shortconv_rows(z_ref, w_ref, b_ref, c, chunk):
    L = z_ref.shape[0]
    row = lax.broadcasted_iota(jnp.int32, (chunk, z_ref.shape[1]), 0)
    r0 = pl.multiple_of(c * chunk, chunk)
    cur = z_ref[pl.ds(r0, chunk), :]
    prev_row = z_ref[pl.ds(jnp.maximum(r0 - 1, 0), 1), :]
    next_row = z_ref[pl.ds(jnp.minimum(r0 + chunk, L - 1), 1), :]
    prev_row = jnp.where(c == 0, 0.0, prev_row)
    next_row = jnp.where(c == L // chunk - 1, 0.0, next_row)
    down = jnp.where(row == 0, prev_row, pltpu.roll(cur, 1, 0))
    up = jnp.where(row == chunk - 1, next_row, pltpu.roll(cur, chunk - 1, 0))
    return b_ref[...] + down * w_ref[0:1, :] + cur * w_ref[1:2, :] + up * w_ref[2:3, :]


def _dft_tables(n1_len, n2_len):
    n = n1_len * n2_len

    def root(num, den):
        ang = (2.0 * math.pi / den) * (num % den).astype(F32)
        return jnp.cos(ang), -jnp.sin(ang)

    i1 = jnp.arange(n1_len, dtype=jnp.int32)
    i2 = jnp.arange(n2_len, dtype=jnp.int32)
    f1r, f1i = root(i1[:, None] * i1[None, :], n1_len)
    f2r, f2i = root(i2[:, None] * i2[None, :], n2_len)
    twr, twi = root(i1[:, None] * i2[None, :], n)
    return (f1r, f1i), (f2r, f2i), (twr, twi)


def _stacked_inner_dft(f2r, f2i, twr_row, twi_row):
    gr = f2r * twr_row - f2i * twi_row
    gi = f2r * twi_row + f2i * twr_row
    top = jnp.concatenate([gr, -gi], axis=1)
    bot = jnp.concatenate([gi, gr], axis=1)
    return jnp.concatenate([top, bot], axis=0)


def _fft_plan(n):
    n1_len = n // FFT_N2
    nk1 = n1_len // 2 + 1
    nk1_pad = -(-nk1 // SUBLANES) * SUBLANES
    chunk = max(c for c in range(1, FFT_K1_CHUNK_MAX + 1) if nk1 % c == 0)
    return n1_len, nk1, nk1_pad, chunk


def _outer_dft_to_scratch(load_rows, fa_ref, ar_ref, ai_ref, nk1_pad):
    fa = fa_ref[...]

    def step(n2, carry):
        a = jnp.dot(fa, load_rows(n2).astype(BF16), preferred_element_type=F32)
        ar_ref[pl.ds(n2, nk1_pad, stride=FFT_PITCH), :] = a[:nk1_pad]
        ai_ref[pl.ds(n2, nk1_pad, stride=FFT_PITCH), :] = a[nk1_pad:]
        return carry

    lax.fori_loop(0, FFT_N2, step, 0, unroll=FFT_UNROLL_OUTER)


def _spec_body(f_ref, b_ref, inv_ref, fa_ref, f2r_ref, f2i_ref, twr_ref, twi_ref, kr_ref, ki_ref,
               fr_ref, fi_ref, br_ref, bi_ref, *, n1_len, nk1_pad, chunk):
    kc = pl.program_id(2)
    half = n1_len // 2

    @pl.when(kc == 0)
    def _():
        inv = inv_ref[...]
        row = lax.broadcasted_iota(jnp.int32, (half, LANES), 0)
        _outer_dft_to_scratch(lambda n2: f_ref[pl.ds(n2, half, stride=FFT_N2), :] * inv,
                              fa_ref, fr_ref, fi_ref, nk1_pad)
        _outer_dft_to_scratch(
            lambda n2: jnp.where((row == 0) & (n2 == 0), 0.0,
                                 b_ref[pl.ds(n2, half, stride=FFT_N2), :] * inv),
            fa_ref, br_ref, bi_ref, nk1_pad)

    f2r = f2r_ref[...]
    f2i = f2i_ref[...]

    def step(t, carry):
        k1 = kc * chunk + t
        base = pl.multiple_of(k1 * FFT_PITCH, SUBLANES)
        mf = _stacked_inner_dft(f2r, f2i, twr_ref[pl.ds(k1, 1), :],
                                twi_ref[pl.ds(k1, 1), :]).astype(BF16)
        rows = pl.ds(base, FFT_N2)
        xf = jnp.dot(mf, jnp.concatenate([fr_ref[rows, :], fi_ref[rows, :]], axis=0).astype(BF16),
                     preferred_element_type=F32)
        xb = jnp.dot(mf, jnp.concatenate([br_ref[rows, :], bi_ref[rows, :]], axis=0).astype(BF16),
                     preferred_element_type=F32)
        o = pl.multiple_of(t * FFT_N2, FFT_N2)
        kr_ref[pl.ds(o, FFT_N2), :] = xf[:FFT_N2] + xb[:FFT_N2]
        ki_ref[pl.ds(o, FFT_N2), :] = xf[FFT_N2:] - xb[FFT_N2:]
        return carry

    lax.fori_loop(0, chunk, step, 0, unroll=FFT_UNROLL_INNER)


def _filter_spectrum(h, inv_norm, tables):
    O2, L, C = h.shape
    O = O2 // 2
    n1_len, nk1, nk1_pad, chunk = _fft_plan(2 * L)
    half = n1_len // 2
    (f1r, f1i), (f2r, f2i), (twr, twi) = tables
    fa = jnp.concatenate([f1r[:nk1_pad, :half], f1i[:nk1_pad, :half]], axis=0).astype(BF16)
    rows = chunk * FFT_N2
    const = lambda shape: pl.BlockSpec(shape, lambda o, c, k: (0, 0))
    out = jax.ShapeDtypeStruct((O, nk1 * FFT_N2, C), F32)
    scratch = pltpu.VMEM((nk1_pad * FFT_PITCH, LANES), F32)
    return pl.pallas_call(
        functools.partial(_spec_body, n1_len=n1_len, nk1_pad=nk1_pad, chunk=chunk),
        grid=(O, C // LANES, nk1 // chunk),
        in_specs=[
            pl.BlockSpec((None, L, LANES), lambda o, c, k: (2 * o, 0, c)),
            pl.BlockSpec((None, L, LANES), lambda o, c, k: (2 * o + 1, 0, c)),
            pl.BlockSpec((None, 1, LANES), lambda o, c, k: (o, 0, c)),
            const(fa.shape), const(f2r.shape), const(f2i.shape), const(twr.shape), const(twi.shape),
        ],
        out_specs=[pl.BlockSpec((None, rows, LANES), lambda o, c, k: (o, k, c)),
                   pl.BlockSpec((None, rows, LANES), lambda o, c, k: (o, k, c))],
        out_shape=[out, out],
        scratch_shapes=[scratch, scratch, scratch, scratch],
        compiler_params=_cparams(("parallel", "parallel", "arbitrary")),
        name="filter_spectrum",
    )(h, h, inv_norm, fa, f2r, f2i, twr, twi)


def _fftconv_body(y_ref, gate_ref, wy_ref, by_ref, wg_ref, bg_ref, d_ref, kr_ref, ki_ref, fa_ref,
                  fs_ref, f2r_ref, f2i_ref, twr_ref, twi_ref, o_ref, xs_ref, ar_ref, ai_ref, *,
                  n1_len, nk1_pad, chunk, conv_y):
    kc = pl.program_id(2)
    half = n1_len // 2

    @pl.when(kc == 0)
    def _():
        def copy(n1, carry):
            src = pl.multiple_of(n1 * FFT_N2, FFT_N2)
            dst = pl.multiple_of(n1 * FFT_PITCH, SUBLANES)
            if conv_y:
                xs_ref[pl.ds(dst, FFT_N2), :] = _shortconv_rows(y_ref, wy_ref, by_ref, n1, FFT_N2)
            else:
                xs_ref[pl.ds(dst, FFT_N2), :] = y_ref[pl.ds(src, FFT_N2), :]
            return carry

        lax.fori_loop(0, half, copy, 0)
        _outer_dft_to_scratch(lambda n2: xs_ref[pl.ds(n2, half, stride=FFT_PITCH), :],
                              fa_ref, ar_ref, ai_ref, nk1_pad)

    f2r = f2r_ref[...]
    f2i = f2i_ref[...]

    def step(t, carry):
        k1 = kc * chunk + t
        base = pl.multiple_of(k1 * FFT_PITCH, SUBLANES)
        rhs = jnp.concatenate([ar_ref[pl.ds(base, FFT_N2), :], ai_ref[pl.ds(base, FFT_N2), :]],
                              axis=0).astype(BF16)
        mf = _stacked_inner_dft(f2r, f2i, twr_ref[pl.ds(k1, 1), :], twi_ref[pl.ds(k1, 1), :])
        x = jnp.dot(mf.astype(BF16), rhs, preferred_element_type=F32)
        xr, xi = x[:FFT_N2], x[FFT_N2:]
        o = pl.multiple_of(t * FFT_N2, FFT_N2)
        kr = kr_ref[pl.ds(o, FFT_N2), :]
        ki = ki_ref[pl.ds(o, FFT_N2), :]
        z = jnp.concatenate([xr * kr - xi * ki, xr * ki + xi * kr], axis=0).astype(BF16)
        b = jnp.dot(mf.T.astype(BF16), z, preferred_element_type=F32)
        ar_ref[pl.ds(base, FFT_N2), :] = b[:FFT_N2]
        ai_ref[pl.ds(base, FFT_N2), :] = b[FFT_N2:]
        return carry

    lax.fori_loop(0, chunk, step, 0, unroll=FFT_UNROLL_INNER)

    @pl.when(kc == pl.num_programs(2) - 1)
    def _():
        fs = fs_ref[...]

        def inv_outer(n2, carry):
            rhs = jnp.concatenate([ar_ref[pl.ds(n2, nk1_pad, stride=FFT_PITCH), :],
                                   ai_ref[pl.ds(n2, nk1_pad, stride=FFT_PITCH), :]],
                                  axis=0).astype(BF16)
            conv = jnp.dot(fs, rhs, preferred_element_type=F32)
            ar_ref[pl.ds(n2, half, stride=FFT_PITCH), :] = conv
            return carry

        lax.fori_loop(0, FFT_N2, inv_outer, 0, unroll=FFT_UNROLL_OUTER)
        d = d_ref[...]

        def finish(n1, carry):
            src = pl.multiple_of(n1 * FFT_PITCH, SUBLANES)
            dst = pl.multiple_of(n1 * FFT_N2, FFT_N2)
            y = xs_ref[pl.ds(src, FFT_N2), :]
            gate = _shortconv_rows(gate_ref, wg_ref, bg_ref, n1, FFT_N2)
            o_ref[pl.ds(dst, FFT_N2), :] = (gate * (
                ar_ref[pl.ds(src, FFT_N2), :] + y * d)).astype(o_ref.dtype)
            return carry

        lax.fori_loop(0, half, finish, 0)


def _fftconv_gate(y, y_off, conv_y, z, gate_off, conv_w, conv_b, d, kr, ki, tables, out_dtype):
    B, L, _ = y.shape
    C = d.shape[-1]
    yo, go = y_off // LANES, gate_off // LANES
    wo = yo if conv_y else go
    N = 2 * L
    n1_len, nk1, nk1_pad, chunk = _fft_plan(N)
    half = n1_len // 2
    (f1r, f1i), (f2r, f2i), (twr, twi) = tables
    fa = jnp.concatenate([f1r[:nk1_pad, :half], f1i[:nk1_pad, :half]], axis=0).astype(BF16)
    wts = jnp.concatenate([jnp.ones((1,), F32), jnp.full((nk1 - 2,), 2.0, F32), jnp.ones((1,), F32),
                           jnp.zeros((nk1_pad - nk1,), F32)]) * (1.0 / N)
    fs = jnp.concatenate([f1r[:half, :nk1_pad] * wts, f1i[:half, :nk1_pad] * wts],
                         axis=1).astype(BF16)
    rows = chunk * FFT_N2
    const = lambda shape: pl.BlockSpec(shape, lambda c, b, k: (0, 0))
    return pl.pallas_call(
        functools.partial(_fftconv_body, n1_len=n1_len, nk1_pad=nk1_pad, chunk=chunk,
                          conv_y=conv_y),
        grid=(C // LANES, B, nk1 // chunk),
        in_specs=[
            pl.BlockSpec((None, L, LANES), lambda c, b, k: (b, 0, c + yo)),
            pl.BlockSpec((None, L, LANES), lambda c, b, k: (b, 0, c + go)),
            pl.BlockSpec((SHORT_CONV, LANES), lambda c, b, k: (0, c + wo)),
            pl.BlockSpec((1, LANES), lambda c, b, k: (0, c + wo)),
            pl.BlockSpec((SHORT_CONV, LANES), lambda c, b, k: (0, c + go)),
            pl.BlockSpec((1, LANES), lambda c, b, k: (0, c + go)),
            pl.BlockSpec((1, LANES), lambda c, b, k: (0, c)),
            pl.BlockSpec((rows, LANES), lambda c, b, k: (k, c)),
            pl.BlockSpec((rows, LANES), lambda c, b, k: (k, c)),
            const(fa.shape), const(fs.shape), const(f2r.shape), const(f2i.shape),
            const(twr.shape), const(twi.shape),
        ],
        out_specs=pl.BlockSpec((None, L, LANES), lambda c, b, k: (b, 0, c)),
        out_shape=jax.ShapeDtypeStruct((B, L, C), out_dtype),
        scratch_shapes=[pltpu.VMEM((half * FFT_PITCH, LANES), F32),
                        pltpu.VMEM((nk1_pad * FFT_PITCH, LANES), F32),
                        pltpu.VMEM((nk1_pad * FFT_PITCH, LANES), F32)],
        compiler_params=_cparams(("parallel", "parallel", "arbitrary")),
        name="fftconv_gate",
    )(y, z, conv_w, conv_b, conv_w, conv_b, d, kr, ki, fa, fs, f2r, f2i, twr, twi)


def _hyena(z, conv_w, conv_b, fw1, fb1, ff1, fw2, fb2, ff2, fw3, fb3, hyena_d):
    B, L, C3 = z.shape
    W = C3 // 3
    h, sums = _hyena_filter(L, W, fw1, fb1, ff1, fw2, fb2, ff2, fw3, fb3)
    sums = sums.reshape(HYENA_ORDER, 2, W)
    inv_norm = (1.0 / (sums[:, 0] + sums[:, 1]))[:, None, :]
    tables = _dft_tables(2 * L // FFT_N2, FFT_N2)
    kr, ki = _filter_spectrum(h, inv_norm, tables)
    conv_b = conv_b.reshape(1, C3)
    y = z
    for o in range(HYENA_ORDER):
        y = _fftconv_gate(y, 0, o == 0, z, (o + 1) * W, conv_w, conv_b,
                          hyena_d[o].reshape(1, W).astype(F32), kr[o], ki[o], tables,
                          BF16 if o == HYENA_ORDER - 1 else F32)
    return y


def _merge_body(yh_ref, ya_ref, g_ref, x_ref, whu_ref, wau_ref, wo_ref, n2_ref, rwh_ref, rwl_ref,
                rb_ref, h_ref, hn_ref, lg_ref):
    D = x_ref.shape[-1]
    up_h = jnp.dot(yh_ref[...], whu_ref[...], preferred_element_type=F32)
    up_a = jnp.dot(ya_ref[...], wau_ref[...], preferred_element_type=F32)
    g = g_ref[...].astype(F32)
    merged = jax.nn.sigmoid(g[:, :D]) * up_h + jax.nn.sigmoid(g[:, D:]) * up_a
    h = x_ref[...] + jnp.dot(merged.astype(BF16), wo_ref[...], preferred_element_type=F32)
    h_ref[...] = h
    ms = jnp.mean(h * h, axis=-1, keepdims=True)
    hn = h * lax.rsqrt(ms + EPS) * n2_ref[...]
    hn_ref[...] = hn.astype(BF16)
    hh, hl = _split_bf16(hn)
    lg_ref[...] = (jnp.dot(hh, rwh_ref[...], preferred_element_type=F32)
                   + jnp.dot(hl, rwh_ref[...], preferred_element_type=F32)
                   + jnp.dot(hh, rwl_ref[...], preferred_element_type=F32)) + rb_ref[...]


def _merge(yh, ya, gates, xt, whu, wau, wo, n2g, rw, rb, tm=1024):
    T, D = xt.shape
    W = yh.shape[1]
    E = rw.shape[1]
    rwp = jnp.zeros((D, LOGIT_PAD), F32).at[:, :E].set(rw)
    rwh, rwl = _split_bf16(rwp)
    rbp = jnp.zeros((1, LOGIT_PAD), F32).at[0, :E].set(rb)
    rowblk = lambda w: pl.BlockSpec((tm, w), lambda i: (i, 0))
    const = lambda shape: pl.BlockSpec(shape, lambda i: (0, 0))
    return pl.pallas_call(
        _merge_body,
        grid=(T // tm,),
        in_specs=[rowblk(W), rowblk(W), rowblk(2 * D), rowblk(D),
                  const((W, D)), const((W, D)), const((D, D)), const((1, D)),
                  const((D, LOGIT_PAD)), const((D, LOGIT_PAD)), const((1, LOGIT_PAD))],
        out_specs=[rowblk(D), rowblk(D), rowblk(LOGIT_PAD)],
        out_shape=[jax.ShapeDtypeStruct((T, D), F32), jax.ShapeDtypeStruct((T, D), BF16),
                   jax.ShapeDtypeStruct((T, LOGIT_PAD), F32)],
        compiler_params=_cparams(("parallel",)),
        name="merge",
    )(yh, ya, gates, xt, whu.astype(BF16), wau.astype(BF16), wo.astype(BF16), n2g.reshape(1, D),
      rwh, rwl, rbp)


def _moe_body(be_ref, nused_ref, x_ref, w1g_ref, w1l_ref, b1g_ref, b1l_ref, w2_ref, b2_ref, o_ref):
    i = pl.program_id(0)

    @pl.when(i < nused_ref[0])
    def _():
        x = x_ref[...]
        nt = (((1,), (1,)), ((), ()))
        glu = lax.dot_general(x, w1g_ref[...], nt, preferred_element_type=F32) + b1g_ref[...]
        lin = lax.dot_general(x, w1l_ref[...], nt, preferred_element_type=F32) + b1l_ref[...]
        glu = jnp.minimum(glu, SWIGLU_LIMIT)
        lin = jnp.clip(lin, -SWIGLU_LIMIT, SWIGLU_LIMIT)
        act = glu * jax.nn.sigmoid(SWIGLU_ALPHA * glu) * (lin + 1.0)
        y = jnp.dot(act.astype(BF16), w2_ref[...].astype(BF16),
                    preferred_element_type=F32) + b2_ref[...]
        o_ref[...] = y.astype(o_ref.dtype)

    @pl.when(i >= nused_ref[0])
    def _():
        o_ref[...] = jnp.zeros(o_ref.shape, o_ref.dtype)


def _moe_experts(xg, block_expert, n_used, w1g, w1l, b1g, b1l, w2, b2):
    P, D = xg.shape
    dff = w2.shape[1]
    nb = P // MOE_TM
    wspec = lambda k, n: pl.BlockSpec((None, k, n), lambda i, be, nu: (be[i], 0, 0))
    grid_spec = pltpu.PrefetchScalarGridSpec(
        num_scalar_prefetch=2,
        grid=(nb,),
        in_specs=[
            pl.BlockSpec((MOE_TM, D), lambda i, be, nu: (i, 0)),
            wspec(dff, D), wspec(dff, D), wspec(1, dff), wspec(1, dff),
            wspec(dff, D), wspec(1, D),
        ],
        out_specs=pl.BlockSpec((MOE_TM, D), lambda i, be, nu: (i, 0)),
    )
    return pl.pallas_call(
        _moe_body,
        grid_spec=grid_spec,
        out_shape=jax.ShapeDtypeStruct((P, D), BF16),
        compiler_params=_cparams(("arbitrary",)),
        name="moe_experts",
    )(block_expert, n_used, xg, w1g, w1l, b1g, b1l, w2, b2)


def _prep_w1_body(w_ref, sel_ref, g_ref, l_ref):
    half = g_ref.shape[0]
    picked = lax.dot_general(sel_ref[...], w_ref[...].astype(BF16), (((1,), (1,)), ((), ())),
                             preferred_element_type=F32)
    g_ref[...] = picked[:half].astype(BF16)
    l_ref[...] = picked[half:].astype(BF16)


def _prep_w1(w1, tc=8 * LANES):
    E, D, F2 = w1.shape
    tc = min(tc, F2)
    half = tc // 2
    row = jnp.arange(tc)
    picks = jnp.where(row < half, 2 * row, 2 * (row - half) + 1)
    sel = (picks[:, None] == jnp.arange(tc)[None, :]).astype(BF16)
    out = jax.ShapeDtypeStruct((E, F2 // 2, D), BF16)
    return pl.pallas_call(
        _prep_w1_body,
        grid=(E, F2 // tc),
        in_specs=[pl.BlockSpec((None, D, tc), lambda e, j: (e, 0, j)),
                  pl.BlockSpec((tc, tc), lambda e, j: (0, 0))],
        out_specs=[pl.BlockSpec((None, half, D), lambda e, j: (e, j, 0)),
                   pl.BlockSpec((None, half, D), lambda e, j: (e, j, 0))],
        out_shape=[out, out],
        compiler_params=_cparams(("parallel", "parallel")),
        name="prep_w1",
    )(w1, sel)


def _combine_body(h_ref, y_ref, g_ref, o_ref):
    acc = h_ref[...]
    g = g_ref[...]
    for k in range(y_ref.shape[0]):
        acc = acc + g[:, k:k + 1] * y_ref[k].astype(F32)
    o_ref[...] = acc


def _combine(h, yk, gates, tm=512):
    T, D = h.shape
    K = yk.shape[0]
    return pl.pallas_call(
        _combine_body,
        grid=(T // tm,),
        in_specs=[pl.BlockSpec((tm, D), lambda i: (i, 0)),
                  pl.BlockSpec((K, tm, D), lambda i: (0, i, 0)),
                  pl.BlockSpec((tm, K), lambda i: (i, 0))],
        out_specs=pl.BlockSpec((tm, D), lambda i: (i, 0)),
        out_shape=jax.ShapeDtypeStruct((T, D), F32),
        compiler_params=_cparams(("parallel",)),
        name="moe_combine",
    )(h, yk, gates)


def _lookup(table, idx):
    n = table.shape[0]
    hit = idx[None, :] == jnp.arange(n, dtype=idx.dtype)[:, None]
    return jnp.sum(jnp.where(hit, table[:, None], 0), axis=0)


def _moe(h, hn_bf16, logits, w1, b1, w2, b2):
    T, D = hn_bf16.shape
    E = w1.shape[0]
    top_val, top_idx = lax.top_k(logits, TOP_K)
    gates = jax.nn.softmax(top_val, axis=-1)
    TK = T * TOP_K
    e_flat = top_idx.reshape(TK).astype(jnp.int32)
    order = jnp.argsort(e_flat).astype(jnp.int32)
    rank = jnp.argsort(order).astype(jnp.int32)
    counts = jnp.sum(jnp.arange(E, dtype=jnp.int32)[:, None] == e_flat[None, :], axis=1,
                     dtype=jnp.int32)
    starts = jnp.cumsum(counts) - counts
    padded = ((counts + MOE_TM - 1) // MOE_TM) * MOE_TM
    pad_ends = jnp.cumsum(padded)
    pad_starts = pad_ends - padded
    nb = (TK + E * (MOE_TM - 1) + MOE_TM - 1) // MOE_TM
    block_start = jnp.arange(nb, dtype=jnp.int32) * MOE_TM
    block_expert = jnp.minimum(jnp.sum(block_start[:, None] >= pad_ends[None, :], axis=1),
                               E - 1).astype(jnp.int32)
    n_used = (pad_ends[-1] // MOE_TM).astype(jnp.int32).reshape(1)
    blk_first = block_start - pad_starts[block_expert]
    within = blk_first[:, None] + jnp.arange(MOE_TM, dtype=jnp.int32)[None, :]
    valid = within < counts[block_expert][:, None]
    sorted_idx = jnp.where(valid, starts[block_expert][:, None] + within, 0).reshape(nb * MOE_TM)
    filler = jnp.arange(nb * MOE_TM, dtype=jnp.int32) % T
    src = jnp.where(valid.reshape(nb * MOE_TM), order[sorted_idx] // TOP_K, filler)
    pos = _lookup(pad_starts - starts, e_flat) + rank
    w1g, w1l = _prep_w1(w1)
    y = _moe_experts(hn_bf16[src], block_expert, n_used, w1g, w1l,
                     b1[:, None, 0::2].astype(F32), b1[:, None, 1::2].astype(F32),
                     w2, b2[:, None, :].astype(F32))
    return _combine(h, y[pos.reshape(T, TOP_K).T], gates)


def _rope_tables(L):
    rows = L // GRID_W
    row = jnp.repeat(jnp.arange(rows, dtype=F32), GRID_W)
    col = jnp.tile(jnp.arange(GRID_W, dtype=F32), rows)
    half = HEAD_DIM // 2
    freqs = ROPE_THETA ** (-jnp.arange(0, half, 2, dtype=F32) / half)
    ang = jnp.concatenate([row[:, None] * freqs, col[:, None] * freqs], axis=-1)
    cos = jnp.repeat(jnp.cos(ang), 2, axis=-1)
    sin = jnp.repeat(jnp.sin(ang), 2, axis=-1)
    sign = jnp.tile(jnp.array([-1.0, 1.0], F32), HEAD_DIM // 2)
    reps = LANES // HEAD_DIM
    return jnp.tile(cos, (1, reps)), jnp.tile(sin * sign, (1, reps))


def kernel(x, norm1_g, w_in, conv_w, conv_b, filt_w1, filt_b1, filt_freq1, filt_w2, filt_b2, filt_freq2, filt_w3, filt_b3, hyena_d, q_norm_g, k_norm_g, w_hyena_up, w_attn_up, w_out, norm2_g, router_w, router_b, expert_w1, expert_b1, expert_w2, expert_b2):
    B, L, D = x.shape
    T = B * L
    depth = w_in.shape[0]
    hw = conv_w.shape[-1] // 3
    aw = N_Q_HEADS * HEAD_DIM
    kvw = N_KV_HEADS * HEAD_DIM
    widths = (3 * hw, aw, 2 * kvw, 2 * D)
    cosf, sinf = _rope_tables(L)
    for l in range(depth):
        xt = x.reshape(T, D)
        gq = jnp.tile(q_norm_g[l].astype(F32), N_Q_HEADS)[None, :]
        gk = jnp.tile(k_norm_g[l].astype(F32), N_KV_HEADS)[None, :]
        z, qr, kr, va, gates = _inproj(xt, norm1_g[l], w_in[l].astype(BF16), widths, gq, gk,
                                       cosf, sinf, L)
        y_hy = _hyena(z.reshape(B, L, 3 * hw), conv_w[l], conv_b[l], filt_w1[l], filt_b1[l],
                      filt_freq1[l], filt_w2[l], filt_b2[l], filt_freq2[l], filt_w3[l], filt_b3[l],
                      hyena_d[l])
        y_at = _attention(qr.reshape(B, L, aw), kr, va.reshape(N_KV_HEADS, B, L, kvw))
        h, hn, logits = _merge(y_hy.reshape(T, hw), y_at.reshape(T, aw), gates, xt,
                               w_hyena_up[l], w_attn_up[l], w_out[l], norm2_g[l],
                               router_w[l], router_b[l])
        x = _moe(h, hn, logits[:, :N_EXPERTS], expert_w1[l], expert_b1[l], expert_w2[l],
                 expert_b2[l]).reshape(B, L, D)
    return x
```

```python
import functools
import math

import jax
import jax.numpy as jnp
from jax import lax
from jax.experimental import pallas as pl
from jax.experimental.pallas import tpu as pltpu

F32 = jnp.float32
BF16 = jnp.bfloat16

GRID_W = 64
HEAD_DIM = 64
N_Q_HEADS = 8
N_KV_HEADS = 2
Q_PER_KV = N_Q_HEADS // N_KV_HEADS
ROPE_THETA = 10000.0
HYENA_ORDER = 2
SHORT_CONV = 3
FILTER_EMB = 33
FAST_DECAY_PCT = 0.3
SLOW_DECAY_PCT = 1.5
DECAY_TARGET = 1e-2
N_EXPERTS = 32
TOP_K = 4
SWIGLU_LIMIT = 7.0
SWIGLU_ALPHA = 1.702
EPS = 1e-6

LANES = 128
SUBLANES = 8
VMEM_LIMIT = 56 * 1024 * 1024

FFT_N2 = LANES
FFT_PITCH = FFT_N2 + SUBLANES
FFT_K1_CHUNK_MAX = 16
FFT_UNROLL_OUTER = 16
FFT_UNROLL_INNER = True
MOE_TM = 512
LOGIT_PAD = LANES


def _cparams(sem):
    return pltpu.CompilerParams(dimension_semantics=sem, vmem_limit_bytes=VMEM_LIMIT)


def _split_bf16(x):
    hi = x.astype(BF16)
    lo = (x - hi.astype(F32)).astype(BF16)
    return hi, lo


def _inproj_body(x_ref, g_ref, w_ref, gq_ref, gk_ref, cos_ref, sin_ref, oq_ref, ok_ref,
                 z_ref, qo_ref, ko_ref, vo_ref, gate_ref, q_scr, kv_scr, *, widths):
    @pl.when(pl.program_id(0) == 0)
    def _():
        q_scr[...] = jnp.zeros(q_scr.shape, F32)
        kv_scr[...] = jnp.zeros(kv_scr.shape, F32)

    x = x_ref[...]
    ms = jnp.mean(x * x, axis=-1, keepdims=True)
    u = (x * lax.rsqrt(ms + EPS) * g_ref[...]).astype(BF16)
    offs = [sum(widths[:i]) for i in range(len(widths) + 1)]
    proj = lambda i: jnp.dot(u, w_ref[:, offs[i]:offs[i + 1]], preferred_element_type=F32)
    z_ref[...] = proj(0)
    gate_ref[...] = proj(3).astype(gate_ref.dtype)
    _qk_epilogue(q_scr[...], kv_scr[...], gq_ref[...], gk_ref[...], cos_ref[...], sin_ref[...],
                 oq_ref[...], ok_ref[...], qo_ref, ko_ref, vo_ref)
    q_scr[...] = proj(1)
    kv_scr[...] = proj(2)


def _inproj(xt, g, w_bf16, widths, gq, gk, cosf, sinf, seq_len, tm=512):
    T, D = xt.shape
    n = w_bf16.shape[1]
    zw, qw, kvw, gw = widths
    kw = kvw // 2
    assert kw == LANES and N_KV_HEADS * HEAD_DIM == LANES and N_KV_HEADS == 2
    nl = seq_len // tm
    nt = T // tm
    cur = lambda i: jnp.minimum(i, nt - 1)
    prev = lambda i: jnp.maximum(i - 1, 0)

    def blk_ones(w):
        r = jnp.arange(w) // HEAD_DIM
        return (r[:, None] == r[None, :]).astype(BF16)

    const = lambda shape: pl.BlockSpec(shape, lambda i: (0, 0))
    return pl.pallas_call(
        functools.partial(_inproj_body, widths=widths),
        grid=(nt + 1,),
        in_specs=[
            pl.BlockSpec((tm, D), lambda i: (cur(i), 0)),
            const((1, D)), const((D, n)), const((1, qw)), const((1, kw)),
            pl.BlockSpec((tm, LANES), lambda i: (prev(i) % nl, 0)),
            pl.BlockSpec((tm, LANES), lambda i: (prev(i) % nl, 0)),
            const((qw, qw)), const((kw, kw)),
        ],
        out_specs=[
            pl.BlockSpec((tm, zw), lambda i: (cur(i), 0)),
            pl.BlockSpec((tm, qw), lambda i: (prev(i), 0)),
            pl.BlockSpec((kw, tm), lambda i: (0, prev(i))),
            pl.BlockSpec((N_KV_HEADS, tm, kw), lambda i: (0, prev(i), 0)),
            pl.BlockSpec((tm, gw), lambda i: (cur(i), 0)),
        ],
        out_shape=[
            jax.ShapeDtypeStruct((T, zw), F32),
            jax.ShapeDtypeStruct((T, qw), BF16),
            jax.ShapeDtypeStruct((kw, T), BF16),
            jax.ShapeDtypeStruct((N_KV_HEADS, T, kw), BF16),
            jax.ShapeDtypeStruct((T, gw), BF16),
        ],
        scratch_shapes=[pltpu.VMEM((tm, qw), F32), pltpu.VMEM((tm, kvw), F32)],
        compiler_params=_cparams(("arbitrary",)),
        name="inproj",
    )(xt, g.reshape(1, D), w_bf16, gq, gk, cosf, sinf, blk_ones(qw), blk_ones(kw))


def _head_norm_rope(x, gain, cosf, sinf, ones_blk):
    w = x.shape[-1]
    hi, lo = _split_bf16(x * x)
    ss = (jnp.dot(hi, ones_blk, preferred_element_type=F32)
          + jnp.dot(lo, ones_blk, preferred_element_type=F32))
    xn = x * lax.rsqrt(ss * (1.0 / HEAD_DIM) + EPS) * gain
    lane = lax.broadcasted_iota(jnp.int32, (x.shape[0], LANES), 1)
    cols = []
    for c in range(w // LANES):
        col = xn[:, c * LANES:(c + 1) * LANES]
        cols.append(jnp.where(lane % 2 == 0, pltpu.roll(col, LANES - 1, 1), pltpu.roll(col, 1, 1)))
    swapped = cols[0] if len(cols) == 1 else jnp.concatenate(cols, axis=1)
    return xn * cosf + swapped * sinf


def _qk_epilogue(q, kv, gq, gk, cosf, sinf, ones_q, ones_k, qo_ref, ko_ref, vo_ref):
    nq = q.shape[-1] // LANES
    q = _head_norm_rope(q, gq, jnp.tile(cosf, (1, nq)), jnp.tile(sinf, (1, nq)), ones_q)
    qo_ref[...] = (q * (HEAD_DIM ** -0.5 * math.log2(math.e))).astype(BF16)
    kw = kv.shape[-1] // 2
    k = _head_norm_rope(kv[:, :kw], gk, cosf, sinf, ones_k)
    ko_ref[...] = k.T.astype(BF16)
    v = kv[:, kw:]
    lane = lax.broadcasted_iota(jnp.int32, v.shape, 1)
    for h in range(N_KV_HEADS):
        vo_ref[h] = jnp.where((lane // HEAD_DIM) == h, v, 1.0).astype(BF16)


def _attn_body(q_ref, k_ref, v_ref, o_ref, qs_ref, m_ref, acc_ref, *, tk, nsplit):
    kvh = pl.program_id(1)
    tq = q_ref.shape[0]
    seq = k_ref.shape[1]
    rows = Q_PER_KV * tq
    lane = lax.broadcasted_iota(jnp.int32, (tq, LANES), 1)
    in_head = (lane // HEAD_DIM) == kvh

    for g in range(Q_PER_KV):
        col = q_ref[:, (g // 2) * LANES:(g // 2 + 1) * LANES].astype(F32)
        col = jnp.where((g % 2) == kvh, col, pltpu.roll(col, HEAD_DIM, 1))
        qs_ref[g * tq:(g + 1) * tq, :] = jnp.where(in_head, col, 0.0).astype(BF16)
    m_ref[...] = jnp.full(m_ref.shape, -jnp.inf, F32)
    acc_ref[...] = jnp.zeros(acc_ref.shape, F32)
    part = rows // nsplit

    def step(c, carry):
        r0 = pl.multiple_of(c * tk, tk)
        kc = k_ref[:, pl.ds(r0, tk)]
        vc = v_ref[pl.ds(r0, tk), :]
        for h in range(nsplit):
            sl = slice(h * part, (h + 1) * part)
            s = jnp.dot(qs_ref[sl, :], kc, preferred_element_type=F32)
            m_prev = m_ref[sl, :]
            m_new = jnp.maximum(m_prev, jnp.max(s, axis=-1, keepdims=True))
            p = jnp.exp2(s - jnp.tile(m_new, (1, tk // LANES)))
            acc_ref[sl, :] = jnp.exp2(m_prev - m_new) * acc_ref[sl, :] + jnp.dot(
                p.astype(BF16), vc, preferred_element_type=F32)
            m_ref[sl, :] = m_new
        return carry

    lax.fori_loop(0, seq // tk, step, 0)

    acc = acc_ref[...]
    o = acc / pltpu.roll(acc, HEAD_DIM, 1)
    for c in range(Q_PER_KV // 2):
        even = o[(2 * c) * tq:(2 * c + 1) * tq, :]
        odd = o[(2 * c + 1) * tq:(2 * c + 2) * tq, :]
        even = jnp.where(kvh == 0, even, pltpu.roll(even, HEAD_DIM, 1))
        odd = jnp.where(kvh == 1, odd, pltpu.roll(odd, HEAD_DIM, 1))
        o_ref[:, c * LANES:(c + 1) * LANES] = jnp.where(lane < HEAD_DIM, even, odd).astype(o_ref.dtype)


def _attention(q, kt, v_aug, tq=512, tk=2048, nsplit=2):
    B, L, qw = q.shape
    gw = Q_PER_KV * HEAD_DIM
    kw = kt.shape[0]
    rows = Q_PER_KV * tq
    return pl.pallas_call(
        functools.partial(_attn_body, tk=tk, nsplit=nsplit),
        grid=(B, N_KV_HEADS, L // tq),
        in_specs=[
            pl.BlockSpec((None, tq, gw), lambda b, h, i: (b, i, h)),
            pl.BlockSpec((kw, L), lambda b, h, i: (0, b)),
            pl.BlockSpec((None, None, L, kw), lambda b, h, i: (h, b, 0, 0)),
        ],
        out_specs=pl.BlockSpec((None, tq, gw), lambda b, h, i: (b, i, h)),
        out_shape=jax.ShapeDtypeStruct((B, L, qw), BF16),
        scratch_shapes=[
            pltpu.VMEM((rows, kw), BF16),
            pltpu.VMEM((rows, LANES), F32),
            pltpu.VMEM((rows, kw), F32),
        ],
        compiler_params=_cparams(("parallel", "parallel", "parallel")),
        name="attention",
    )(q, kt, v_aug)


def _hdot(a, b):
    ah, al = _split_bf16(a)
    bh, bl = _split_bf16(b)
    return (jnp.dot(ah, bh, preferred_element_type=F32)
            + jnp.dot(al, bh, preferred_element_type=F32)
            + jnp.dot(ah, bl, preferred_element_type=F32))


def _filter_body(feat_ref, t_ref, w1_ref, b1_ref, f1_ref, w2_ref, b2_ref, f2_ref, w3_ref, b3_ref,
                 delta_ref, bwd_ref, h_ref, sum_ref):
    i = pl.program_id(0)
    h = jnp.sin(f1_ref[...] * (_hdot(feat_ref[...], w1_ref[...]) + b1_ref[...]))
    h = jnp.sin(f2_ref[...] * (_hdot(h, w2_ref[...]) + b2_ref[...]))
    h = _hdot(h, w3_ref[...]) + b3_ref[...]
    t = t_ref[...]
    h = h * jnp.exp(-t * delta_ref[...])
    width = h_ref.shape[-1]
    for j in range(h_ref.shape[0]):
        h_ref[j] = h[:, j * width:(j + 1) * width]
    row = lax.broadcasted_iota(jnp.int32, h.shape, 0) + i * h.shape[0]
    a = jnp.where((row == 0) & (bwd_ref[...] > 0.5), 0.0, jnp.abs(h))
    part = jnp.sum(a, axis=0, keepdims=True)

    @pl.when(i == 0)
    def _():
        sum_ref[...] = jnp.zeros(sum_ref.shape, F32)

    sum_ref[...] += jnp.broadcast_to(part, sum_ref.shape)


def _hyena_filter(L, width, fw1, fb1, ff1, fw2, fb2, ff2, fw3, fb3, tl=512):
    bands = (FILTER_EMB - 1) // 2
    t = jnp.linspace(0.0, 1.0, L, dtype=F32)[:, None]
    w = 2.0 * math.pi * jnp.arange(L, dtype=F32)[:, None] / L
    fr = jnp.linspace(1e-4, bands - 1, bands, dtype=F32)[None, :]
    feats = jnp.concatenate([t, jnp.cos(w * fr), -jnp.sin(w * fr)], axis=-1)
    max_decay = math.log(DECAY_TARGET) / FAST_DECAY_PCT
    min_decay = math.log(DECAY_TARGET) / SLOW_DECAY_PCT
    deltas = jnp.abs(jnp.linspace(min_decay, max_decay, width, dtype=F32))
    ncol = fw3.shape[1]
    delta_cols = jnp.tile(deltas, ncol // width)[None, :]
    is_bwd = ((jnp.arange(ncol) // width) % 2).astype(F32)[None, :]
    emb = hid = LANES

    def pad2(a, r, c):
        a = a.astype(F32)
        return jnp.zeros((r, c), F32).at[:a.shape[0], :a.shape[1]].set(a)

    row = lambda a: pad2(a.reshape(1, -1), 1, hid)
    feats = pad2(feats, L, emb)
    fw1, fw2, fw3 = pad2(fw1, emb, hid), pad2(fw2, hid, hid), pad2(fw3, hid, ncol)
    fb3 = fb3.reshape(1, ncol).astype(F32)
    const = lambda shape: pl.BlockSpec(shape, lambda i: (0, 0))
    h, sums = pl.pallas_call(
        _filter_body,
        grid=(L // tl,),
        in_specs=[
            pl.BlockSpec((tl, emb), lambda i: (i, 0)),
            pl.BlockSpec((tl, 1), lambda i: (i, 0)),
            const((emb, hid)), const((1, hid)), const((1, hid)),
            const((hid, hid)), const((1, hid)), const((1, hid)),
            const((hid, ncol)), const((1, ncol)), const((1, ncol)), const((1, ncol)),
        ],
        out_specs=[pl.BlockSpec((ncol // width, tl, width), lambda i: (0, i, 0)),
                   pl.BlockSpec((SUBLANES, ncol), lambda i: (0, 0))],
        out_shape=[jax.ShapeDtypeStruct((ncol // width, L, width), F32),
                   jax.ShapeDtypeStruct((SUBLANES, ncol), F32)],
        compiler_params=_cparams(("arbitrary",)),
        name="hyena_filter",
    )(feats, t, fw1, row(fb1), row(ff1), fw2, row(fb2), row(ff2), fw3, fb3, delta_cols, is_bwd)
    return h, sums[0]


def _shortconv_rows(z_ref, w_ref, b_ref, c, chunk, interior=False):
    L = z_ref.shape[0]
    r0 = pl.multiple_of(c * chunk, chunk)
    cur = z_ref[pl.ds(r0, chunk), :]
    if interior:
        down = z_ref[pl.ds(r0 - 1, chunk), :]
        up = z_ref[pl.ds(r0 + 1, chunk), :]
    else:
        row = lax.broadcasted_iota(jnp.int32, (chunk, z_ref.shape[1]), 0)
        prev_row = z_ref[pl.ds(jnp.maximum(r0 - 1, 0), 1), :]
        next_row = z_ref[pl.ds(jnp.minimum(r0 + chunk, L - 1), 1), :]
        prev_row = jnp.where(c == 0, 0.0, prev_row)
        next_row = jnp.where(c == L // chunk - 1, 0.0, next_row)
        down = jnp.where(row == 0, prev_row, pltpu.roll(cur, 1, 0))
        up = jnp.where(row == chunk - 1, next_row, pltpu.roll(cur, chunk - 1, 0))
    return b_ref[...] + down * w_ref[0:1, :] + cur * w_ref[1:2, :] + up * w_ref[2:3, :]


def _for_slabs(n, body):
    body(0, False)
    lax.fori_loop(1, n - 1, lambda c, carry: (body(c, True), carry)[1], 0)
    body(n - 1, False)


def _dft_tables(n1_len, n2_len):
    n = n1_len * n2_len

    def root(num, den):
        ang = (2.0 * math.pi / den) * (num % den).astype(F32)
        return jnp.cos(ang), -jnp.sin(ang)

    i1 = jnp.arange(n1_len, dtype=jnp.int32)
    i2 = jnp.arange(n2_len, dtype=jnp.int32)
    f1r, f1i = root(i1[:, None] * i1[None, :], n1_len)
    f2r, f2i = root(i2[:, None] * i2[None, :], n2_len)
    twr, twi = root(i1[:, None] * i2[None, :], n)
    return (f1r, f1i), (f2r, f2i), (twr, twi)


def _stacked_inner_dft(f2r, f2i, twr_row, twi_row):
    gr = f2r * twr_row - f2i * twi_row
    gi = f2r * twi_row + f2i * twr_row
    top = jnp.concatenate([gr, -gi], axis=1)
    bot = jnp.concatenate([gi, gr], axis=1)
    return jnp.concatenate([top, bot], axis=0)


def _fft_plan(n):
    n1_len = n // FFT_N2
    nk1 = n1_len // 2 + 1
    nk1_pad = -(-nk1 // SUBLANES) * SUBLANES
    chunk = max(c for c in range(1, FFT_K1_CHUNK_MAX + 1) if nk1 % c == 0)
    return n1_len, nk1, nk1_pad, chunk


def _outer_dft_to_scratch(load_rows, fa_ref, ar_ref, ai_ref, nk1_pad):
    fa = fa_ref[...]

    def step(n2, carry):
        a = jnp.dot(fa, load_rows(n2).astype(BF16), preferred_element_type=F32)
        ar_ref[pl.ds(n2, nk1_pad, stride=FFT_PITCH), :] = a[:nk1_pad]
        ai_ref[pl.ds(n2, nk1_pad, stride=FFT_PITCH), :] = a[nk1_pad:]
        return carry

    lax.fori_loop(0, FFT_N2, step, 0, unroll=FFT_UNROLL_OUTER)


def _spec_body(f_ref, b_ref, inv_ref, fa_ref, f2r_ref, f2i_ref, twr_ref, twi_ref, kr_ref, ki_ref,
               fr_ref, fi_ref, br_ref, bi_ref, *, n1_len, nk1_pad, chunk):
    kc = pl.program_id(2)
    half = n1_len // 2

    @pl.when(kc == 0)
    def _():
        inv = inv_ref[...]
        row = lax.broadcasted_iota(jnp.int32, (half, LANES), 0)
        _outer_dft_to_scratch(lambda n2: f_ref[pl.ds(n2, half, stride=FFT_N2), :] * inv,
                              fa_ref, fr_ref, fi_ref, nk1_pad)
        _outer_dft_to_scratch(
            lambda n2: jnp.where((row == 0) & (n2 == 0), 0.0,
                                 b_ref[pl.ds(n2, half, stride=FFT_N2), :] * inv),
            fa_ref, br_ref, bi_ref, nk1_pad)

    f2r = f2r_ref[...]
    f2i = f2i_ref[...]

    def step(t, carry):
        k1 = kc * chunk + t
        base = pl.multiple_of(k1 * FFT_PITCH, SUBLANES)
        mf = _stacked_inner_dft(f2r, f2i, twr_ref[pl.ds(k1, 1), :],
                                twi_ref[pl.ds(k1, 1), :]).astype(BF16)
        rows = pl.ds(base, FFT_N2)
        xf = jnp.dot(mf, jnp.concatenate([fr_ref[rows, :], fi_ref[rows, :]], axis=0).astype(BF16),
                     preferred_element_type=F32)
        xb = jnp.dot(mf, jnp.concatenate([br_ref[rows, :], bi_ref[rows, :]], axis=0).astype(BF16),
                     preferred_element_type=F32)
        o = pl.multiple_of(t * FFT_N2, FFT_N2)
        kr_ref[pl.ds(o, FFT_N2), :] = xf[:FFT_N2] + xb[:FFT_N2]
        ki_ref[pl.ds(o, FFT_N2), :] = xf[FFT_N2:] - xb[FFT_N2:]
        return carry

    lax.fori_loop(0, chunk, step, 0, unroll=FFT_UNROLL_INNER)


def _filter_spectrum(h, inv_norm, tables):
    O2, L, C = h.shape
    O = O2 // 2
    n1_len, nk1, nk1_pad, chunk = _fft_plan(2 * L)
    half = n1_len // 2
    (f1r, f1i), (f2r, f2i), (twr, twi) = tables
    fa = jnp.concatenate([f1r[:nk1_pad, :half], f1i[:nk1_pad, :half]], axis=0).astype(BF16)
    rows = chunk * FFT_N2
    const = lambda shape: pl.BlockSpec(shape, lambda o, c, k: (0, 0))
    out = jax.ShapeDtypeStruct((O, nk1 * FFT_N2, C), F32)
    scratch = pltpu.VMEM((nk1_pad * FFT_PITCH, LANES), F32)
    return pl.pallas_call(
        functools.partial(_spec_body, n1_len=n1_len, nk1_pad=nk1_pad, chunk=chunk),
        grid=(O, C // LANES, nk1 // chunk),
        in_specs=[
            pl.BlockSpec((None, L, LANES), lambda o, c, k: (2 * o, 0, c)),
            pl.BlockSpec((None, L, LANES), lambda o, c, k: (2 * o + 1, 0, c)),
            pl.BlockSpec((None, 1, LANES), lambda o, c, k: (o, 0, c)),
            const(fa.shape), const(f2r.shape), const(f2i.shape), const(twr.shape), const(twi.shape),
        ],
        out_specs=[pl.BlockSpec((None, rows, LANES), lambda o, c, k: (o, k, c)),
                   pl.BlockSpec((None, rows, LANES), lambda o, c, k: (o, k, c))],
        out_shape=[out, out],
        scratch_shapes=[scratch, scratch, scratch, scratch],
        compiler_params=_cparams(("parallel", "parallel", "arbitrary")),
        name="filter_spectrum",
    )(h, h, inv_norm, fa, f2r, f2i, twr, twi)


def _fftconv_body(y_ref, gate_ref, wy_ref, by_ref, wg_ref, bg_ref, d_ref, kr_ref, ki_ref, fa_ref,
                  fs_ref, f2r_ref, f2i_ref, twr_ref, twi_ref, o_ref, xs_ref, ar_ref, ai_ref, *,
                  n1_len, nk1_pad, chunk, conv_y):
    kc = pl.program_id(2)
    half = n1_len // 2

    @pl.when(kc == 0)
    def _():
        def copy(n1, interior):
            src = pl.multiple_of(n1 * FFT_N2, FFT_N2)
            dst = pl.multiple_of(n1 * FFT_PITCH, SUBLANES)
            if conv_y:
                xs_ref[pl.ds(dst, FFT_N2), :] = _shortconv_rows(y_ref, wy_ref, by_ref, n1, FFT_N2,
                                                                interior)
            else:
                xs_ref[pl.ds(dst, FFT_N2), :] = y_ref[pl.ds(src, FFT_N2), :]

        _for_slabs(half, copy)
        _outer_dft_to_scratch(lambda n2: xs_ref[pl.ds(n2, half, stride=FFT_PITCH), :],
                              fa_ref, ar_ref, ai_ref, nk1_pad)

    f2r = f2r_ref[...]
    f2i = f2i_ref[...]

    def step(t, carry):
        k1 = kc * chunk + t
        base = pl.multiple_of(k1 * FFT_PITCH, SUBLANES)
        rhs = jnp.concatenate([ar_ref[pl.ds(base, FFT_N2), :], ai_ref[pl.ds(base, FFT_N2), :]],
                              axis=0).astype(BF16)
        mf = _stacked_inner_dft(f2r, f2i, twr_ref[pl.ds(k1, 1), :], twi_ref[pl.ds(k1, 1), :])
        x = jnp.dot(mf.astype(BF16), rhs, preferred_element_type=F32)
        xr, xi = x[:FFT_N2], x[FFT_N2:]
        o = pl.multiple_of(t * FFT_N2, FFT_N2)
        kr = kr_ref[pl.ds(o, FFT_N2), :]
        ki = ki_ref[pl.ds(o, FFT_N2), :]
        z = jnp.concatenate([xr * kr - xi * ki, xr * ki + xi * kr], axis=0).astype(BF16)
        b = jnp.dot(mf.T.astype(BF16), z, preferred_element_type=F32)
        ar_ref[pl.ds(base, FFT_N2), :] = b[:FFT_N2]
        ai_ref[pl.ds(base, FFT_N2), :] = b[FFT_N2:]
        return carry

    lax.fori_loop(0, chunk, step, 0, unroll=FFT_UNROLL_INNER)

    @pl.when(kc == pl.num_programs(2) - 1)
    def _():
        fs = fs_ref[...]

        def inv_outer(n2, carry):
            rhs = jnp.concatenate([ar_ref[pl.ds(n2, nk1_pad, stride=FFT_PITCH), :],
                                   ai_ref[pl.ds(n2, nk1_pad, stride=FFT_PITCH), :]],
                                  axis=0).astype(BF16)
            conv = jnp.dot(fs, rhs, preferred_element_type=F32)
            ar_ref[pl.ds(n2, half, stride=FFT_PITCH), :] = conv
            return carry

        lax.fori_loop(0, FFT_N2, inv_outer, 0, unroll=FFT_UNROLL_OUTER)
        d = d_ref[...]

        def finish(n1, interior):
            src = pl.multiple_of(n1 * FFT_PITCH, SUBLANES)
            dst = pl.multiple_of(n1 * FFT_N2, FFT_N2)
            y = xs_ref[pl.ds(src, FFT_N2), :]
            gate = _shortconv_rows(gate_ref, wg_ref, bg_ref, n1, FFT_N2, interior)
            o_ref[pl.ds(dst, FFT_N2), :] = (gate * (
                ar_ref[pl.ds(src, FFT_N2), :] + y * d)).astype(o_ref.dtype)

        _for_slabs(half, finish)


def _fftconv_gate(y, y_off, conv_y, z, gate_off, conv_w, conv_b, d, kr, ki, tables, out_dtype):
    B, L, _ = y.shape
    C = d.shape[-1]
    yo, go = y_off // LANES, gate_off // LANES
    wo = yo if conv_y else go
    N = 2 * L
    n1_len, nk1, nk1_pad, chunk = _fft_plan(N)
    half = n1_len // 2
    (f1r, f1i), (f2r, f2i), (twr, twi) = tables
    fa = jnp.concatenate([f1r[:nk1_pad, :half], f1i[:nk1_pad, :half]], axis=0).astype(BF16)
    wts = jnp.concatenate([jnp.ones((1,), F32), jnp.full((nk1 - 2,), 2.0, F32), jnp.ones((1,), F32),
                           jnp.zeros((nk1_pad - nk1,), F32)]) * (1.0 / N)
    fs = jnp.concatenate([f1r[:half, :nk1_pad] * wts, f1i[:half, :nk1_pad] * wts],
                         axis=1).astype(BF16)
    rows = chunk * FFT_N2
    const = lambda shape: pl.BlockSpec(shape, lambda c, b, k: (0, 0))
    return pl.pallas_call(
        functools.partial(_fftconv_body, n1_len=n1_len, nk1_pad=nk1_pad, chunk=chunk,
                          conv_y=conv_y),
        grid=(C // LANES, B, nk1 // chunk),
        in_specs=[
            pl.BlockSpec((None, L, LANES), lambda c, b, k: (b, 0, c + yo)),
            pl.BlockSpec((None, L, LANES), lambda c, b, k: (b, 0, c + go)),
            pl.BlockSpec((SHORT_CONV, LANES), lambda c, b, k: (0, c + wo)),
            pl.BlockSpec((1, LANES), lambda c, b, k: (0, c + wo)),
            pl.BlockSpec((SHORT_CONV, LANES), lambda c, b, k: (0, c + go)),
            pl.BlockSpec((1, LANES), lambda c, b, k: (0, c + go)),
            pl.BlockSpec((1, LANES), lambda c, b, k: (0, c)),
            pl.BlockSpec((rows, LANES), lambda c, b, k: (k, c)),
            pl.BlockSpec((rows, LANES), lambda c, b, k: (k, c)),
            const(fa.shape), const(fs.shape), const(f2r.shape), const(f2i.shape),
            const(twr.shape), const(twi.shape),
        ],
        out_specs=pl.BlockSpec((None, L, LANES), lambda c, b, k: (b, 0, c)),
        out_shape=jax.ShapeDtypeStruct((B, L, C), out_dtype),
        scratch_shapes=[pltpu.VMEM((half * FFT_PITCH, LANES), F32),
                        pltpu.VMEM((nk1_pad * FFT_PITCH, LANES), F32),
                        pltpu.VMEM((nk1_pad * FFT_PITCH, LANES), F32)],
        compiler_params=_cparams(("parallel", "parallel", "arbitrary")),
        name="fftconv_gate",
    )(y, z, conv_w, conv_b, conv_w, conv_b, d, kr, ki, fa, fs, f2r, f2i, twr, twi)


def _hyena(z, conv_w, conv_b, fw1, fb1, ff1, fw2, fb2, ff2, fw3, fb3, hyena_d):
    B, L, C3 = z.shape
    W = C3 // 3
    h, sums = _hyena_filter(L, W, fw1, fb1, ff1, fw2, fb2, ff2, fw3, fb3)
    sums = sums.reshape(HYENA_ORDER, 2, W)
    inv_norm = (1.0 / (sums[:, 0] + sums[:, 1]))[:, None, :]
    tables = _dft_tables(2 * L // FFT_N2, FFT_N2)
    kr, ki = _filter_spectrum(h, inv_norm, tables)
    conv_b = conv_b.reshape(1, C3)
    y = z
    for o in range(HYENA_ORDER):
        y = _fftconv_gate(y, 0, o == 0, z, (o + 1) * W, conv_w, conv_b,
                          hyena_d[o].reshape(1, W).astype(F32), kr[o], ki[o], tables,
                          BF16 if o == HYENA_ORDER - 1 else F32)
    return y


def _merge_body(yh_ref, ya_ref, g_ref, x_ref, whu_ref, wau_ref, wo_ref, n2_ref, rwh_ref, rwl_ref,
                rb_ref, h_ref, hn_ref, lg_ref):
    D = x_ref.shape[-1]
    up_h = jnp.dot(yh_ref[...], whu_ref[...], preferred_element_type=F32)
    up_a = jnp.dot(ya_ref[...], wau_ref[...], preferred_element_type=F32)
    g = g_ref[...].astype(F32)
    merged = jax.nn.sigmoid(g[:, :D]) * up_h + jax.nn.sigmoid(g[:, D:]) * up_a
    h = x_ref[...] + jnp.dot(merged.astype(BF16), wo_ref[...], preferred_element_type=F32)
    h_ref[...] = h
    ms = jnp.mean(h * h, axis=-1, keepdims=True)
    hn = h * lax.rsqrt(ms + EPS) * n2_ref[...]
    hn_ref[...] = hn.astype(BF16)
    hh, hl = _split_bf16(hn)
    lg_ref[...] = (jnp.dot(hh, rwh_ref[...], preferred_element_type=F32)
                   + jnp.dot(hl, rwh_ref[...], preferred_element_type=F32)
                   + jnp.dot(hh, rwl_ref[...], preferred_element_type=F32)) + rb_ref[...]


def _merge(yh, ya, gates, xt, whu, wau, wo, n2g, rw, rb, tm=1024):
    T, D = xt.shape
    W = yh.shape[1]
    E = rw.shape[1]
    rwp = jnp.zeros((D, LOGIT_PAD), F32).at[:, :E].set(rw)
    rwh, rwl = _split_bf16(rwp)
    rbp = jnp.zeros((1, LOGIT_PAD), F32).at[0, :E].set(rb)
    rowblk = lambda w: pl.BlockSpec((tm, w), lambda i: (i, 0))
    const = lambda shape: pl.BlockSpec(shape, lambda i: (0, 0))
    return pl.pallas_call(
        _merge_body,
        grid=(T // tm,),
        in_specs=[rowblk(W), rowblk(W), rowblk(2 * D), rowblk(D),
                  const((W, D)), const((W, D)), const((D, D)), const((1, D)),
                  const((D, LOGIT_PAD)), const((D, LOGIT_PAD)), const((1, LOGIT_PAD))],
        out_specs=[rowblk(D), rowblk(D), rowblk(LOGIT_PAD)],
        out_shape=[jax.ShapeDtypeStruct((T, D), F32), jax.ShapeDtypeStruct((T, D), BF16),
                   jax.ShapeDtypeStruct((T, LOGIT_PAD), F32)],
        compiler_params=_cparams(("parallel",)),
        name="merge",
    )(yh, ya, gates, xt, whu.astype(BF16), wau.astype(BF16), wo.astype(BF16), n2g.reshape(1, D),
      rwh, rwl, rbp)


def _moe_body(be_ref, nused_ref, x_ref, w1g_ref, w1l_ref, b1g_ref, b1l_ref, w2_ref, b2_ref, o_ref):
    i = pl.program_id(0)

    @pl.when(i < nused_ref[0])
    def _():
        x = x_ref[...]
        nt = (((1,), (1,)), ((), ()))
        glu = lax.dot_general(x, w1g_ref[...], nt, preferred_element_type=F32) + b1g_ref[...]
        lin = lax.dot_general(x, w1l_ref[...], nt, preferred_element_type=F32) + b1l_ref[...]
        glu = jnp.minimum(glu, SWIGLU_LIMIT)
        lin = jnp.clip(lin, -SWIGLU_LIMIT, SWIGLU_LIMIT)
        act = glu * jax.nn.sigmoid(SWIGLU_ALPHA * glu) * (lin + 1.0)
        y = jnp.dot(act.astype(BF16), w2_ref[...].astype(BF16),
                    preferred_element_type=F32) + b2_ref[...]
        o_ref[...] = y.astype(o_ref.dtype)

    @pl.when(i >= nused_ref[0])
    def _():
        o_ref[...] = jnp.zeros(o_ref.shape, o_ref.dtype)


def _moe_experts(xg, block_expert, n_used, w1g, w1l, b1g, b1l, w2, b2):
    P, D = xg.shape
    dff = w2.shape[1]
    nb = P // MOE_TM
    wspec = lambda k, n: pl.BlockSpec((None, k, n), lambda i, be, nu: (be[i], 0, 0))
    grid_spec = pltpu.PrefetchScalarGridSpec(
        num_scalar_prefetch=2,
        grid=(nb,),
        in_specs=[
            pl.BlockSpec((MOE_TM, D), lambda i, be, nu: (i, 0)),
            wspec(dff, D), wspec(dff, D), wspec(1, dff), wspec(1, dff),
            wspec(dff, D), wspec(1, D),
        ],
        out_specs=pl.BlockSpec((MOE_TM, D), lambda i, be, nu: (i, 0)),
    )
    return pl.pallas_call(
        _moe_body,
        grid_spec=grid_spec,
        out_shape=jax.ShapeDtypeStruct((P, D), BF16),
        compiler_params=_cparams(("arbitrary",)),
        name="moe_experts",
    )(block_expert, n_used, xg, w1g, w1l, b1g, b1l, w2, b2)


def _prep_w1_body(w_ref, sel_ref, g_ref, l_ref):
    half = g_ref.shape[0]
    picked = lax.dot_general(sel_ref[...], w_ref[...].astype(BF16), (((1,), (1,)), ((), ())),
                             preferred_element_type=F32)
    g_ref[...] = picked[:half].astype(BF16)
    l_ref[...] = picked[half:].astype(BF16)


def _prep_w1(w1, tc=8 * LANES):
    E, D, F2 = w1.shape
    tc = min(tc, F2)
    half = tc // 2
    row = jnp.arange(tc)
    picks = jnp.where(row < half, 2 * row, 2 * (row - half) + 1)
    sel = (picks[:, None] == jnp.arange(tc)[None, :]).astype(BF16)
    out = jax.ShapeDtypeStruct((E, F2 // 2, D), BF16)
    return pl.pallas_call(
        _prep_w1_body,
        grid=(E, F2 // tc),
        in_specs=[pl.BlockSpec((None, D, tc), lambda e, j: (e, 0, j)),
                  pl.BlockSpec((tc, tc), lambda e, j: (0, 0))],
        out_specs=[pl.BlockSpec((None, half, D), lambda e, j: (e, j, 0)),
                   pl.BlockSpec((None, half, D), lambda e, j: (e, j, 0))],
        out_shape=[out, out],
        compiler_params=_cparams(("parallel", "parallel")),
        name="prep_w1",
    )(w1, sel)


def _combine_body(h_ref, y_ref, g_ref, o_ref):
    acc = h_ref[...]
    g = g_ref[...]
    for k in range(y_ref.shape[0]):
        acc = acc + g[:, k:k + 1] * y_ref[k].astype(F32)
    o_ref[...] = acc


def _combine(h, yk, gates, tm=512):
    T, D = h.shape
    K = yk.shape[0]
    return pl.pallas_call(
        _combine_body,
        grid=(T // tm,),
        in_specs=[pl.BlockSpec((tm, D), lambda i: (i, 0)),
                  pl.BlockSpec((K, tm, D), lambda i: (0, i, 0)),
                  pl.BlockSpec((tm, K), lambda i: (i, 0))],
        out_specs=pl.BlockSpec((tm, D), lambda i: (i, 0)),
        out_shape=jax.ShapeDtypeStruct((T, D), F32),
        compiler_params=_cparams(("parallel",)),
        name="moe_combine",
    )(h, yk, gates)


def _lookup(table, idx):
    n = table.shape[0]
    hit = idx[None, :] == jnp.arange(n, dtype=idx.dtype)[:, None]
    return jnp.sum(jnp.where(hit, table[:, None], 0), axis=0)


def _moe(h, hn_bf16, logits, w1, b1, w2, b2):
    T, D = hn_bf16.shape
    E = w1.shape[0]
    top_val, top_idx = lax.top_k(logits, TOP_K)
    gates = jax.nn.softmax(top_val, axis=-1)
    TK = T * TOP_K
    e_flat = top_idx.reshape(TK).astype(jnp.int32)
    order = jnp.argsort(e_flat).astype(jnp.int32)
    rank = jnp.argsort(order).astype(jnp.int32)
    counts = jnp.sum(jnp.arange(E, dtype=jnp.int32)[:, None] == e_flat[None, :], axis=1,
                     dtype=jnp.int32)
    starts = jnp.cumsum(counts) - counts
    padded = ((counts + MOE_TM - 1) // MOE_TM) * MOE_TM
    pad_ends = jnp.cumsum(padded)
    pad_starts = pad_ends - padded
    nb = (TK + E * (MOE_TM - 1) + MOE_TM - 1) // MOE_TM
    block_start = jnp.arange(nb, dtype=jnp.int32) * MOE_TM
    block_expert = jnp.minimum(jnp.sum(block_start[:, None] >= pad_ends[None, :], axis=1),
                               E - 1).astype(jnp.int32)
    n_used = (pad_ends[-1] // MOE_TM).astype(jnp.int32).reshape(1)
    blk_first = block_start - pad_starts[block_expert]
    within = blk_first[:, None] + jnp.arange(MOE_TM, dtype=jnp.int32)[None, :]
    valid = within < counts[block_expert][:, None]
    sorted_idx = jnp.where(valid, starts[block_expert][:, None] + within, 0).reshape(nb * MOE_TM)
    filler = jnp.arange(nb * MOE_TM, dtype=jnp.int32) % T
    src = jnp.where(valid.reshape(nb * MOE_TM), order[sorted_idx] // TOP_K, filler)
    pos = _lookup(pad_starts - starts, e_flat) + rank
    w1g, w1l = _prep_w1(w1)
    y = _moe_experts(hn_bf16[src], block_expert, n_used, w1g, w1l,
                     b1[:, None, 0::2].astype(F32), b1[:, None, 1::2].astype(F32),
                     w2, b2[:, None, :].astype(F32))
    return _combine(h, y[pos.reshape(T, TOP_K).T], gates)


def _rope_tables(L):
    rows = L // GRID_W
    row = jnp.repeat(jnp.arange(rows, dtype=F32), GRID_W)
    col = jnp.tile(jnp.arange(GRID_W, dtype=F32), rows)
    half = HEAD_DIM // 2
    freqs = ROPE_THETA ** (-jnp.arange(0, half, 2, dtype=F32) / half)
    ang = jnp.concatenate([row[:, None] * freqs, col[:, None] * freqs], axis=-1)
    cos = jnp.repeat(jnp.cos(ang), 2, axis=-1)
    sin = jnp.repeat(jnp.sin(ang), 2, axis=-1)
    sign = jnp.tile(jnp.array([-1.0, 1.0], F32), HEAD_DIM // 2)
    reps = LANES // HEAD_DIM
    return jnp.tile(cos, (1, reps)), jnp.tile(sin * sign, (1, reps))


def kernel(x, norm1_g, w_in, conv_w, conv_b, filt_w1, filt_b1, filt_freq1, filt_w2, filt_b2, filt_freq2, filt_w3, filt_b3, hyena_d, q_norm_g, k_norm_g, w_hyena_up, w_attn_up, w_out, norm2_g, router_w, router_b, expert_w1, expert_b1, expert_w2, expert_b2):
    B, L, D = x.shape
    T = B * L
    depth = w_in.shape[0]
    hw = conv_w.shape[-1] // 3
    aw = N_Q_HEADS * HEAD_DIM
    kvw = N_KV_HEADS * HEAD_DIM
    widths = (3 * hw, aw, 2 * kvw, 2 * D)
    cosf, sinf = _rope_tables(L)
    for l in range(depth):
        xt = x.reshape(T, D)
        gq = jnp.tile(q_norm_g[l].astype(F32), N_Q_HEADS)[None, :]
        gk = jnp.tile(k_norm_g[l].astype(F32), N_KV_HEADS)[None, :]
        z, qr, kr, va, gates = _inproj(xt, norm1_g[l], w_in[l].astype(BF16), widths, gq, gk,
                                       cosf, sinf, L)
        y_hy = _hyena(z.reshape(B, L, 3 * hw), conv_w[l], conv_b[l], filt_w1[l], filt_b1[l],
                      filt_freq1[l], filt_w2[l], filt_b2[l], filt_freq2[l], filt_w3[l], filt_b3[l],
                      hyena_d[l])
        y_at = _attention(qr.reshape(B, L, aw), kr, va.reshape(N_KV_HEADS, B, L, kvw))
        h, hn, logits = _merge(y_hy.reshape(T, hw), y_at.reshape(T, aw), gates, xt,
                               w_hyena_up[l], w_attn_up[l], w_out[l], norm2_g[l],
                               router_w[l], router_b[l])
        x = _moe(h, hn, logits[:, :N_EXPERTS], expert_w1[l], expert_b1[l], expert_w2[l],
                 expert_b2[l]).reshape(B, L, D)
    return x
```

```python
import functools
import math

import jax
import jax.numpy as jnp
from jax import lax
from jax.experimental import pallas as pl
from jax.experimental.pallas import tpu as pltpu

F32 = jnp.float32
BF16 = jnp.bfloat16

GRID_W = 64
HEAD_DIM = 64
N_Q_HEADS = 8
N_KV_HEADS = 2
Q_PER_KV = N_Q_HEADS // N_KV_HEADS
ROPE_THETA = 10000.0
HYENA_ORDER = 2
SHORT_CONV = 3
FILTER_EMB = 33
FAST_DECAY_PCT = 0.3
SLOW_DECAY_PCT = 1.5
DECAY_TARGET = 1e-2
N_EXPERTS = 32
TOP_K = 4
SWIGLU_LIMIT = 7.0
SWIGLU_ALPHA = 1.702
EPS = 1e-6

LANES = 128
SUBLANES = 8
VMEM_LIMIT = 56 * 1024 * 1024

FFT_N2 = LANES
FFT_PITCH = FFT_N2 + SUBLANES
FFT_K1_CHUNK_MAX = 16
FFT_UNROLL_OUTER = 32
FFT_UNROLL_INNER = True
MOE_TM = 512
LOGIT_PAD = LANES


def _cparams(sem):
    return pltpu.CompilerParams(dimension_semantics=sem, vmem_limit_bytes=VMEM_LIMIT)


def _split_bf16(x):
    hi = x.astype(BF16)
    lo = (x - hi.astype(F32)).astype(BF16)
    return hi, lo


def _inproj_body(x_ref, g_ref, w_ref, gq_ref, gk_ref, cos_ref, sin_ref, oq_ref, ok_ref,
                 z_ref, qo_ref, ko_ref, vo_ref, gate_ref, q_scr, kv_scr, *, widths):
    @pl.when(pl.program_id(0) == 0)
    def _():
        q_scr[...] = jnp.zeros(q_scr.shape, F32)
        kv_scr[...] = jnp.zeros(kv_scr.shape, F32)

    x = x_ref[...]
    ms = jnp.mean(x * x, axis=-1, keepdims=True)
    u = (x * lax.rsqrt(ms + EPS) * g_ref[...]).astype(BF16)
    offs = [sum(widths[:i]) for i in range(len(widths) + 1)]
    proj = lambda i: jnp.dot(u, w_ref[:, offs[i]:offs[i + 1]], preferred_element_type=F32)
    z_ref[...] = proj(0)
    gate_ref[...] = proj(3).astype(gate_ref.dtype)
    _qk_epilogue(q_scr[...], kv_scr[...], gq_ref[...], gk_ref[...], cos_ref[...], sin_ref[...],
                 oq_ref[...], ok_ref[...], qo_ref, ko_ref, vo_ref)
    q_scr[...] = proj(1)
    kv_scr[...] = proj(2)


def _inproj(xt, g, w_bf16, widths, gq, gk, cosf, sinf, seq_len, tm=512):
    T, D = xt.shape
    n = w_bf16.shape[1]
    zw, qw, kvw, gw = widths
    kw = kvw // 2
    assert kw == LANES and N_KV_HEADS * HEAD_DIM == LANES and N_KV_HEADS == 2
    nl = seq_len // tm
    nt = T // tm
    cur = lambda i: jnp.minimum(i, nt - 1)
    prev = lambda i: jnp.maximum(i - 1, 0)

    def blk_ones(w):
        r = jnp.arange(w) // HEAD_DIM
        return (r[:, None] == r[None, :]).astype(BF16)

    const = lambda shape: pl.BlockSpec(shape, lambda i: (0, 0))
    return pl.pallas_call(
        functools.partial(_inproj_body, widths=widths),
        grid=(nt + 1,),
        in_specs=[
            pl.BlockSpec((tm, D), lambda i: (cur(i), 0)),
            const((1, D)), const((D, n)), const((1, qw)), const((1, kw)),
            pl.BlockSpec((tm, LANES), lambda i: (prev(i) % nl, 0)),
            pl.BlockSpec((tm, LANES), lambda i: (prev(i) % nl, 0)),
            const((qw, qw)), const((kw, kw)),
        ],
        out_specs=[
            pl.BlockSpec((tm, zw), lambda i: (cur(i), 0)),
            pl.BlockSpec((tm, qw), lambda i: (prev(i), 0)),
            pl.BlockSpec((kw, tm), lambda i: (0, prev(i))),
            pl.BlockSpec((N_KV_HEADS, tm, kw), lambda i: (0, prev(i), 0)),
            pl.BlockSpec((tm, gw), lambda i: (cur(i), 0)),
        ],
        out_shape=[
            jax.ShapeDtypeStruct((T, zw), F32),
            jax.ShapeDtypeStruct((T, qw), BF16),
            jax.ShapeDtypeStruct((kw, T), BF16),
            jax.ShapeDtypeStruct((N_KV_HEADS, T, kw), BF16),
            jax.ShapeDtypeStruct((T, gw), BF16),
        ],
        scratch_shapes=[pltpu.VMEM((tm, qw), F32), pltpu.VMEM((tm, kvw), F32)],
        compiler_params=_cparams(("arbitrary",)),
        name="inproj",
    )(xt, g.reshape(1, D), w_bf16, gq, gk, cosf, sinf, blk_ones(qw), blk_ones(kw))


def _head_norm_rope(x, gain, cosf, sinf, ones_blk):
    w = x.shape[-1]
    hi, lo = _split_bf16(x * x)
    ss = (jnp.dot(hi, ones_blk, preferred_element_type=F32)
          + jnp.dot(lo, ones_blk, preferred_element_type=F32))
    xn = x * lax.rsqrt(ss * (1.0 / HEAD_DIM) + EPS) * gain
    lane = lax.broadcasted_iota(jnp.int32, (x.shape[0], LANES), 1)
    cols = []
    for c in range(w // LANES):
        col = xn[:, c * LANES:(c + 1) * LANES]
        cols.append(jnp.where(lane % 2 == 0, pltpu.roll(col, LANES - 1, 1), pltpu.roll(col, 1, 1)))
    swapped = cols[0] if len(cols) == 1 else jnp.concatenate(cols, axis=1)
    return xn * cosf + swapped * sinf


def _qk_epilogue(q, kv, gq, gk, cosf, sinf, ones_q, ones_k, qo_ref, ko_ref, vo_ref):
    nq = q.shape[-1] // LANES
    q = _head_norm_rope(q, gq, jnp.tile(cosf, (1, nq)), jnp.tile(sinf, (1, nq)), ones_q)
    qo_ref[...] = (q * (HEAD_DIM ** -0.5 * math.log2(math.e))).astype(BF16)
    kw = kv.shape[-1] // 2
    k = _head_norm_rope(kv[:, :kw], gk, cosf, sinf, ones_k)
    ko_ref[...] = k.T.astype(BF16)
    v = kv[:, kw:]
    lane = lax.broadcasted_iota(jnp.int32, v.shape, 1)
    for h in range(N_KV_HEADS):
        vo_ref[h] = jnp.where((lane // HEAD_DIM) == h, v, 1.0).astype(BF16)


def _attn_body(q_ref, k_ref, v_ref, o_ref, qs_ref, m_ref, acc_ref, *, tk, nsplit):
    kvh = pl.program_id(1)
    tq = q_ref.shape[0]
    seq = k_ref.shape[1]
    rows = Q_PER_KV * tq
    lane = lax.broadcasted_iota(jnp.int32, (tq, LANES), 1)
    in_head = (lane // HEAD_DIM) == kvh

    for g in range(Q_PER_KV):
        col = q_ref[:, (g // 2) * LANES:(g // 2 + 1) * LANES].astype(F32)
        col = jnp.where((g % 2) == kvh, col, pltpu.roll(col, HEAD_DIM, 1))
        qs_ref[g * tq:(g + 1) * tq, :] = jnp.where(in_head, col, 0.0).astype(BF16)
    m_ref[...] = jnp.full(m_ref.shape, -jnp.inf, F32)
    acc_ref[...] = jnp.zeros(acc_ref.shape, F32)
    part = rows // nsplit

    def step(c, carry):
        r0 = pl.multiple_of(c * tk, tk)
        kc = k_ref[:, pl.ds(r0, tk)]
        vc = v_ref[pl.ds(r0, tk), :]
        for h in range(nsplit):
            sl = slice(h * part, (h + 1) * part)
            s = jnp.dot(qs_ref[sl, :], kc, preferred_element_type=F32)
            m_prev = m_ref[sl, :]
            m_new = jnp.maximum(m_prev, jnp.max(s, axis=-1, keepdims=True))
            p = jnp.exp2(s - jnp.tile(m_new, (1, tk // LANES)))
            acc_ref[sl, :] = jnp.exp2(m_prev - m_new) * acc_ref[sl, :] + jnp.dot(
                p.astype(BF16), vc, preferred_element_type=F32)
            m_ref[sl, :] = m_new
        return carry

    lax.fori_loop(0, seq // tk, step, 0)

    acc = acc_ref[...]
    o = acc / pltpu.roll(acc, HEAD_DIM, 1)
    for c in range(Q_PER_KV // 2):
        even = o[(2 * c) * tq:(2 * c + 1) * tq, :]
        odd = o[(2 * c + 1) * tq:(2 * c + 2) * tq, :]
        even = jnp.where(kvh == 0, even, pltpu.roll(even, HEAD_DIM, 1))
        odd = jnp.where(kvh == 1, odd, pltpu.roll(odd, HEAD_DIM, 1))
        o_ref[:, c * LANES:(c + 1) * LANES] = jnp.where(lane < HEAD_DIM, even, odd).astype(o_ref.dtype)


def _attention(q, kt, v_aug, tq=512, tk=2048, nsplit=2):
    B, L, qw = q.shape
    gw = Q_PER_KV * HEAD_DIM
    kw = kt.shape[0]
    rows = Q_PER_KV * tq
    return pl.pallas_call(
        functools.partial(_attn_body, tk=tk, nsplit=nsplit),
        grid=(B, N_KV_HEADS, L // tq),
        in_specs=[
            pl.BlockSpec((None, tq, gw), lambda b, h, i: (b, i, h)),
            pl.BlockSpec((kw, L), lambda b, h, i: (0, b)),
            pl.BlockSpec((None, None, L, kw), lambda b, h, i: (h, b, 0, 0)),
        ],
        out_specs=pl.BlockSpec((None, tq, gw), lambda b, h, i: (b, i, h)),
        out_shape=jax.ShapeDtypeStruct((B, L, qw), BF16),
        scratch_shapes=[
            pltpu.VMEM((rows, kw), BF16),
            pltpu.VMEM((rows, LANES), F32),
            pltpu.VMEM((rows, kw), F32),
        ],
        compiler_params=_cparams(("parallel", "parallel", "parallel")),
        name="attention",
    )(q, kt, v_aug)


def _hdot(a, b):
    ah, al = _split_bf16(a)
    bh, bl = _split_bf16(b)
    return (jnp.dot(ah, bh, preferred_element_type=F32)
            + jnp.dot(al, bh, preferred_element_type=F32)
            + jnp.dot(ah, bl, preferred_element_type=F32))


def _filter_body(feat_ref, t_ref, w1_ref, b1_ref, f1_ref, w2_ref, b2_ref, f2_ref, w3_ref, b3_ref,
                 delta_ref, bwd_ref, h_ref, sum_ref):
    i = pl.program_id(0)
    h = jnp.sin(f1_ref[...] * (_hdot(feat_ref[...], w1_ref[...]) + b1_ref[...]))
    h = jnp.sin(f2_ref[...] * (_hdot(h, w2_ref[...]) + b2_ref[...]))
    h = _hdot(h, w3_ref[...]) + b3_ref[...]
    t = t_ref[...]
    h = h * jnp.exp(-t * delta_ref[...])
    width = h_ref.shape[-1]
    for j in range(h_ref.shape[0]):
        h_ref[j] = h[:, j * width:(j + 1) * width]
    row = lax.broadcasted_iota(jnp.int32, h.shape, 0) + i * h.shape[0]
    a = jnp.where((row == 0) & (bwd_ref[...] > 0.5), 0.0, jnp.abs(h))
    part = jnp.sum(a, axis=0, keepdims=True)

    @pl.when(i == 0)
    def _():
        sum_ref[...] = jnp.zeros(sum_ref.shape, F32)

    sum_ref[...] += jnp.broadcast_to(part, sum_ref.shape)


def _hyena_filter(L, width, fw1, fb1, ff1, fw2, fb2, ff2, fw3, fb3, tl=512):
    bands = (FILTER_EMB - 1) // 2
    t = jnp.linspace(0.0, 1.0, L, dtype=F32)[:, None]
    w = 2.0 * math.pi * jnp.arange(L, dtype=F32)[:, None] / L
    fr = jnp.linspace(1e-4, bands - 1, bands, dtype=F32)[None, :]
    feats = jnp.concatenate([t, jnp.cos(w * fr), -jnp.sin(w * fr)], axis=-1)
    max_decay = math.log(DECAY_TARGET) / FAST_DECAY_PCT
    min_decay = math.log(DECAY_TARGET) / SLOW_DECAY_PCT
    deltas = jnp.abs(jnp.linspace(min_decay, max_decay, width, dtype=F32))
    ncol = fw3.shape[1]
    delta_cols = jnp.tile(deltas, ncol // width)[None, :]
    is_bwd = ((jnp.arange(ncol) // width) % 2).astype(F32)[None, :]
    emb = hid = LANES

    def pad2(a, r, c):
        a = a.astype(F32)
        return jnp.zeros((r, c), F32).at[:a.shape[0], :a.shape[1]].set(a)

    row = lambda a: pad2(a.reshape(1, -1), 1, hid)
    feats = pad2(feats, L, emb)
    fw1, fw2, fw3 = pad2(fw1, emb, hid), pad2(fw2, hid, hid), pad2(fw3, hid, ncol)
    fb3 = fb3.reshape(1, ncol).astype(F32)
    const = lambda shape: pl.BlockSpec(shape, lambda i: (0, 0))
    h, sums = pl.pallas_call(
        _filter_body,
        grid=(L // tl,),
        in_specs=[
            pl.BlockSpec((tl, emb), lambda i: (i, 0)),
            pl.BlockSpec((tl, 1), lambda i: (i, 0)),
            const((emb, hid)), const((1, hid)), const((1, hid)),
            const((hid, hid)), const((1, hid)), const((1, hid)),
            const((hid, ncol)), const((1, ncol)), const((1, ncol)), const((1, ncol)),
        ],
        out_specs=[pl.BlockSpec((ncol // width, tl, width), lambda i: (0, i, 0)),
                   pl.BlockSpec((SUBLANES, ncol), lambda i: (0, 0))],
        out_shape=[jax.ShapeDtypeStruct((ncol // width, L, width), F32),
                   jax.ShapeDtypeStruct((SUBLANES, ncol), F32)],
        compiler_params=_cparams(("arbitrary",)),
        name="hyena_filter",
    )(feats, t, fw1, row(fb1), row(ff1), fw2, row(fb2), row(ff2), fw3, fb3, delta_cols, is_bwd)
    return h, sums[0]


def _shortconv_rows(z_ref, w_ref, b_ref, c, chunk, interior=False):
    L = z_ref.shape[0]
    r0 = pl.multiple_of(c * chunk, chunk)
    cur = z_ref[pl.ds(r0, chunk), :]
    if interior:
        down = z_ref[pl.ds(r0 - 1, chunk), :]
        up = z_ref[pl.ds(r0 + 1, chunk), :]
    else:
        row = lax.broadcasted_iota(jnp.int32, (chunk, z_ref.shape[1]), 0)
        prev_row = z_ref[pl.ds(jnp.maximum(r0 - 1, 0), 1), :]
        next_row = z_ref[pl.ds(jnp.minimum(r0 + chunk, L - 1), 1), :]
        prev_row = jnp.where(c == 0, 0.0, prev_row)
        next_row = jnp.where(c == L // chunk - 1, 0.0, next_row)
        down = jnp.where(row == 0, prev_row, pltpu.roll(cur, 1, 0))
        up = jnp.where(row == chunk - 1, next_row, pltpu.roll(cur, chunk - 1, 0))
    return b_ref[...] + down * w_ref[0:1, :] + cur * w_ref[1:2, :] + up * w_ref[2:3, :]


def _for_slabs(n, body):
    body(0, False)
    lax.fori_loop(1, n - 1, lambda c, carry: (body(c, True), carry)[1], 0)
    body(n - 1, False)


def _dft_tables(n1_len, n2_len):
    n = n1_len * n2_len

    def root(num, den):
        ang = (2.0 * math.pi / den) * (num % den).astype(F32)
        return jnp.cos(ang), -jnp.sin(ang)

    i1 = jnp.arange(n1_len, dtype=jnp.int32)
    i2 = jnp.arange(n2_len, dtype=jnp.int32)
    f1r, f1i = root(i1[:, None] * i1[None, :], n1_len)
    f2r, f2i = root(i2[:, None] * i2[None, :], n2_len)
    twr, twi = root(i1[:, None] * i2[None, :], n)
    return (f1r, f1i), (f2r, f2i), (twr, twi)


def _stacked_inner_dft(f2r, f2i, twr_row, twi_row):
    gr = f2r * twr_row - f2i * twi_row
    gi = f2r * twi_row + f2i * twr_row
    top = jnp.concatenate([gr, -gi], axis=1)
    bot = jnp.concatenate([gi, gr], axis=1)
    return jnp.concatenate([top, bot], axis=0)


def _fft_plan(n):
    n1_len = n // FFT_N2
    nk1 = n1_len // 2 + 1
    nk1_pad = -(-nk1 // SUBLANES) * SUBLANES
    chunk = max(c for c in range(1, FFT_K1_CHUNK_MAX + 1) if nk1 % c == 0)
    return n1_len, nk1, nk1_pad, chunk


def _outer_dft_to_scratch(load_rows, fa_ref, ar_ref, ai_ref, nk1_pad):
    fa = fa_ref[...]

    def step(n2, carry):
        a = jnp.dot(fa, load_rows(n2).astype(BF16), preferred_element_type=F32)
        ar_ref[pl.ds(n2, nk1_pad, stride=FFT_PITCH), :] = a[:nk1_pad]
        ai_ref[pl.ds(n2, nk1_pad, stride=FFT_PITCH), :] = a[nk1_pad:]
        return carry

    lax.fori_loop(0, FFT_N2, step, 0, unroll=FFT_UNROLL_OUTER)


def _spec_body(f_ref, b_ref, inv_ref, fa_ref, f2r_ref, f2i_ref, twr_ref, twi_ref, kr_ref, ki_ref,
               fr_ref, fi_ref, br_ref, bi_ref, *, n1_len, nk1_pad, chunk):
    kc = pl.program_id(2)
    half = n1_len // 2

    @pl.when(kc == 0)
    def _():
        inv = inv_ref[...]
        row = lax.broadcasted_iota(jnp.int32, (half, LANES), 0)
        _outer_dft_to_scratch(lambda n2: f_ref[pl.ds(n2, half, stride=FFT_N2), :] * inv,
                              fa_ref, fr_ref, fi_ref, nk1_pad)
        _outer_dft_to_scratch(
            lambda n2: jnp.where((row == 0) & (n2 == 0), 0.0,
                                 b_ref[pl.ds(n2, half, stride=FFT_N2), :] * inv),
            fa_ref, br_ref, bi_ref, nk1_pad)

    f2r = f2r_ref[...]
    f2i = f2i_ref[...]

    def step(t, carry):
        k1 = kc * chunk + t
        base = pl.multiple_of(k1 * FFT_PITCH, SUBLANES)
        mf = _stacked_inner_dft(f2r, f2i, twr_ref[pl.ds(k1, 1), :],
                                twi_ref[pl.ds(k1, 1), :]).astype(BF16)
        rows = pl.ds(base, FFT_N2)
        xf = jnp.dot(mf, jnp.concatenate([fr_ref[rows, :], fi_ref[rows, :]], axis=0).astype(BF16),
                     preferred_element_type=F32)
        xb = jnp.dot(mf, jnp.concatenate([br_ref[rows, :], bi_ref[rows, :]], axis=0).astype(BF16),
                     preferred_element_type=F32)
        o = pl.multiple_of(t * FFT_N2, FFT_N2)
        kr_ref[pl.ds(o, FFT_N2), :] = xf[:FFT_N2] + xb[:FFT_N2]
        ki_ref[pl.ds(o, FFT_N2), :] = xf[FFT_N2:] - xb[FFT_N2:]
        return carry

    lax.fori_loop(0, chunk, step, 0, unroll=FFT_UNROLL_INNER)


def _filter_spectrum(h, inv_norm, tables):
    O2, L, C = h.shape
    O = O2 // 2
    n1_len, nk1, nk1_pad, chunk = _fft_plan(2 * L)
    half = n1_len // 2
    (f1r, f1i), (f2r, f2i), (twr, twi) = tables
    fa = jnp.concatenate([f1r[:nk1_pad, :half], f1i[:nk1_pad, :half]], axis=0).astype(BF16)
    rows = chunk * FFT_N2
    const = lambda shape: pl.BlockSpec(shape, lambda o, c, k: (0, 0))
    out = jax.ShapeDtypeStruct((O, nk1 * FFT_N2, C), F32)
    scratch = pltpu.VMEM((nk1_pad * FFT_PITCH, LANES), F32)
    return pl.pallas_call(
        functools.partial(_spec_body, n1_len=n1_len, nk1_pad=nk1_pad, chunk=chunk),
        grid=(O, C // LANES, nk1 // chunk),
        in_specs=[
            pl.BlockSpec((None, L, LANES), lambda o, c, k: (2 * o, 0, c)),
            pl.BlockSpec((None, L, LANES), lambda o, c, k: (2 * o + 1, 0, c)),
            pl.BlockSpec((None, 1, LANES), lambda o, c, k: (o, 0, c)),
            const(fa.shape), const(f2r.shape), const(f2i.shape), const(twr.shape), const(twi.shape),
        ],
        out_specs=[pl.BlockSpec((None, rows, LANES), lambda o, c, k: (o, k, c)),
                   pl.BlockSpec((None, rows, LANES), lambda o, c, k: (o, k, c))],
        out_shape=[out, out],
        scratch_shapes=[scratch, scratch, scratch, scratch],
        compiler_params=_cparams(("parallel", "parallel", "arbitrary")),
        name="filter_spectrum",
    )(h, h, inv_norm, fa, f2r, f2i, twr, twi)


def _fftconv_body(y_ref, gate_ref, wy_ref, by_ref, wg_ref, bg_ref, d_ref, kr_ref, ki_ref, fa_ref,
                  fs_ref, f2r_ref, f2i_ref, twr_ref, twi_ref, o_ref, xs_ref, ar_ref, ai_ref, *,
                  n1_len, nk1_pad, chunk, conv_y):
    kc = pl.program_id(2)
    half = n1_len // 2

    @pl.when(kc == 0)
    def _():
        def copy(n1, interior):
            src = pl.multiple_of(n1 * FFT_N2, FFT_N2)
            dst = pl.multiple_of(n1 * FFT_PITCH, SUBLANES)
            if conv_y:
                xs_ref[pl.ds(dst, FFT_N2), :] = _shortconv_rows(y_ref, wy_ref, by_ref, n1, FFT_N2,
                                                                interior)
            else:
                xs_ref[pl.ds(dst, FFT_N2), :] = y_ref[pl.ds(src, FFT_N2), :]

        _for_slabs(half, copy)
        _outer_dft_to_scratch(lambda n2: xs_ref[pl.ds(n2, half, stride=FFT_PITCH), :],
                              fa_ref, ar_ref, ai_ref, nk1_pad)

    f2r = f2r_ref[...]
    f2i = f2i_ref[...]

    def step(t, carry):
        k1 = kc * chunk + t
        base = pl.multiple_of(k1 * FFT_PITCH, SUBLANES)
        rhs = jnp.concatenate([ar_ref[pl.ds(base, FFT_N2), :], ai_ref[pl.ds(base, FFT_N2), :]],
                              axis=0).astype(BF16)
        mf = _stacked_inner_dft(f2r, f2i, twr_ref[pl.ds(k1, 1), :], twi_ref[pl.ds(k1, 1), :])
        x = jnp.dot(mf.astype(BF16), rhs, preferred_element_type=F32)
        xr, xi = x[:FFT_N2], x[FFT_N2:]
        o = pl.multiple_of(t * FFT_N2, FFT_N2)
        kr = kr_ref[pl.ds(o, FFT_N2), :]
        ki = ki_ref[pl.ds(o, FFT_N2), :]
        z = jnp.concatenate([xr * kr - xi * ki, xr * ki + xi * kr], axis=0).astype(BF16)
        b = jnp.dot(mf.T.astype(BF16), z, preferred_element_type=F32)
        ar_ref[pl.ds(base, FFT_N2), :] = b[:FFT_N2]
        ai_ref[pl.ds(base, FFT_N2), :] = b[FFT_N2:]
        return carry

    lax.fori_loop(0, chunk, step, 0, unroll=FFT_UNROLL_INNER)

    @pl.when(kc == pl.num_programs(2) - 1)
    def _():
        fs = fs_ref[...]

        def inv_outer(n2, carry):
            rhs = jnp.concatenate([ar_ref[pl.ds(n2, nk1_pad, stride=FFT_PITCH), :],
                                   ai_ref[pl.ds(n2, nk1_pad, stride=FFT_PITCH), :]],
                                  axis=0).astype(BF16)
            conv = jnp.dot(fs, rhs, preferred_element_type=F32)
            ar_ref[pl.ds(n2, half, stride=FFT_PITCH), :] = conv
            return carry

        lax.fori_loop(0, FFT_N2, inv_outer, 0, unroll=FFT_UNROLL_OUTER)
        d = d_ref[...]

        def finish(n1, interior):
            src = pl.multiple_of(n1 * FFT_PITCH, SUBLANES)
            dst = pl.multiple_of(n1 * FFT_N2, FFT_N2)
            y = xs_ref[pl.ds(src, FFT_N2), :]
            gate = _shortconv_rows(gate_ref, wg_ref, bg_ref, n1, FFT_N2, interior)
            o_ref[pl.ds(dst, FFT_N2), :] = (gate * (
                ar_ref[pl.ds(src, FFT_N2), :] + y * d)).astype(o_ref.dtype)

        _for_slabs(half, finish)


def _fftconv_gate(y, y_off, conv_y, z, gate_off, conv_w, conv_b, d, kr, ki, tables, out_dtype):
    B, L, _ = y.shape
    C = d.shape[-1]
    yo, go = y_off // LANES, gate_off // LANES
    wo = yo if conv_y else go
    N = 2 * L
    n1_len, nk1, nk1_pad, chunk = _fft_plan(N)
    half = n1_len // 2
    (f1r, f1i), (f2r, f2i), (twr, twi) = tables
    fa = jnp.concatenate([f1r[:nk1_pad, :half], f1i[:nk1_pad, :half]], axis=0).astype(BF16)
    wts = jnp.concatenate([jnp.ones((1,), F32), jnp.full((nk1 - 2,), 2.0, F32), jnp.ones((1,), F32),
                           jnp.zeros((nk1_pad - nk1,), F32)]) * (1.0 / N)
    fs = jnp.concatenate([f1r[:half, :nk1_pad] * wts, f1i[:half, :nk1_pad] * wts],
                         axis=1).astype(BF16)
    rows = chunk * FFT_N2
    const = lambda shape: pl.BlockSpec(shape, lambda c, b, k: (0, 0))
    return pl.pallas_call(
        functools.partial(_fftconv_body, n1_len=n1_len, nk1_pad=nk1_pad, chunk=chunk,
                          conv_y=conv_y),
        grid=(C // LANES, B, nk1 // chunk),
        in_specs=[
            pl.BlockSpec((None, L, LANES), lambda c, b, k: (b, 0, c + yo)),
            pl.BlockSpec((None, L, LANES), lambda c, b, k: (b, 0, c + go)),
            pl.BlockSpec((SHORT_CONV, LANES), lambda c, b, k: (0, c + wo)),
            pl.BlockSpec((1, LANES), lambda c, b, k: (0, c + wo)),
            pl.BlockSpec((SHORT_CONV, LANES), lambda c, b, k: (0, c + go)),
            pl.BlockSpec((1, LANES), lambda c, b, k: (0, c + go)),
            pl.BlockSpec((1, LANES), lambda c, b, k: (0, c)),
            pl.BlockSpec((rows, LANES), lambda c, b, k: (k, c)),
            pl.BlockSpec((rows, LANES), lambda c, b, k: (k, c)),
            const(fa.shape), const(fs.shape), const(f2r.shape), const(f2i.shape),
            const(twr.shape), const(twi.shape),
        ],
        out_specs=pl.BlockSpec((None, L, LANES), lambda c, b, k: (b, 0, c)),
        out_shape=jax.ShapeDtypeStruct((B, L, C), out_dtype),
        scratch_shapes=[pltpu.VMEM((half * FFT_PITCH, LANES), F32),
                        pltpu.VMEM((nk1_pad * FFT_PITCH, LANES), F32),
                        pltpu.VMEM((nk1_pad * FFT_PITCH, LANES), F32)],
        compiler_params=_cparams(("parallel", "parallel", "arbitrary")),
        name="fftconv_gate",
    )(y, z, conv_w, conv_b, conv_w, conv_b, d, kr, ki, fa, fs, f2r, f2i, twr, twi)


def _hyena(z, conv_w, conv_b, fw1, fb1, ff1, fw2, fb2, ff2, fw3, fb3, hyena_d):
    B, L, C3 = z.shape
    W = C3 // 3
    h, sums = _hyena_filter(L, W, fw1, fb1, ff1, fw2, fb2, ff2, fw3, fb3)
    sums = sums.reshape(HYENA_ORDER, 2, W)
    inv_norm = (1.0 / (sums[:, 0] + sums[:, 1]))[:, None, :]
    tables = _dft_tables(2 * L // FFT_N2, FFT_N2)
    kr, ki = _filter_spectrum(h, inv_norm, tables)
    conv_b = conv_b.reshape(1, C3)
    y = z
    for o in range(HYENA_ORDER):
        y = _fftconv_gate(y, 0, o == 0, z, (o + 1) * W, conv_w, conv_b,
                          hyena_d[o].reshape(1, W).astype(F32), kr[o], ki[o], tables,
                          BF16 if o == HYENA_ORDER - 1 else F32)
    return y


def _merge_body(yh_ref, ya_ref, g_ref, x_ref, whu_ref, wau_ref, wo_ref, n2_ref, rwh_ref, rwl_ref,
                rb_ref, h_ref, hn_ref, lg_ref):
    D = x_ref.shape[-1]
    up_h = jnp.dot(yh_ref[...], whu_ref[...], preferred_element_type=F32)
    up_a = jnp.dot(ya_ref[...], wau_ref[...], preferred_element_type=F32)
    g = g_ref[...].astype(F32)
    merged = jax.nn.sigmoid(g[:, :D]) * up_h + jax.nn.sigmoid(g[:, D:]) * up_a
    h = x_ref[...] + jnp.dot(merged.astype(BF16), wo_ref[...], preferred_element_type=F32)
    h_ref[...] = h
    ms = jnp.mean(h * h, axis=-1, keepdims=True)
    hn = h * lax.rsqrt(ms + EPS) * n2_ref[...]
    hn_ref[...] = hn.astype(BF16)
    hh, hl = _split_bf16(hn)
    lg_ref[...] = (jnp.dot(hh, rwh_ref[...], preferred_element_type=F32)
                   + jnp.dot(hl, rwh_ref[...], preferred_element_type=F32)
                   + jnp.dot(hh, rwl_ref[...], preferred_element_type=F32)) + rb_ref[...]


def _merge(yh, ya, gates, xt, whu, wau, wo, n2g, rw, rb, tm=1024):
    T, D = xt.shape
    W = yh.shape[1]
    E = rw.shape[1]
    rwp = jnp.zeros((D, LOGIT_PAD), F32).at[:, :E].set(rw)
    rwh, rwl = _split_bf16(rwp)
    rbp = jnp.zeros((1, LOGIT_PAD), F32).at[0, :E].set(rb)
    rowblk = lambda w: pl.BlockSpec((tm, w), lambda i: (i, 0))
    const = lambda shape: pl.BlockSpec(shape, lambda i: (0, 0))
    return pl.pallas_call(
        _merge_body,
        grid=(T // tm,),
        in_specs=[rowblk(W), rowblk(W), rowblk(2 * D), rowblk(D),
                  const((W, D)), const((W, D)), const((D, D)), const((1, D)),
                  const((D, LOGIT_PAD)), const((D, LOGIT_PAD)), const((1, LOGIT_PAD))],
        out_specs=[rowblk(D), rowblk(D), rowblk(LOGIT_PAD)],
        out_shape=[jax.ShapeDtypeStruct((T, D), F32), jax.ShapeDtypeStruct((T, D), BF16),
                   jax.ShapeDtypeStruct((T, LOGIT_PAD), F32)],
        compiler_params=_cparams(("parallel",)),
        name="merge",
    )(yh, ya, gates, xt, whu.astype(BF16), wau.astype(BF16), wo.astype(BF16), n2g.reshape(1, D),
      rwh, rwl, rbp)


def _moe_body(be_ref, nused_ref, x_ref, w1g_ref, w1l_ref, b1g_ref, b1l_ref, w2_ref, b2_ref, o_ref):
    i = pl.program_id(0)

    @pl.when(i < nused_ref[0])
    def _():
        x = x_ref[...]
        nt = (((1,), (1,)), ((), ()))
        glu = lax.dot_general(x, w1g_ref[...], nt, preferred_element_type=F32) + b1g_ref[...]
        lin = lax.dot_general(x, w1l_ref[...], nt, preferred_element_type=F32) + b1l_ref[...]
        glu = jnp.minimum(glu, SWIGLU_LIMIT)
        lin = jnp.clip(lin, -SWIGLU_LIMIT, SWIGLU_LIMIT)
        act = glu * jax.nn.sigmoid(SWIGLU_ALPHA * glu) * (lin + 1.0)
        y = jnp.dot(act.astype(BF16), w2_ref[...].astype(BF16),
                    preferred_element_type=F32) + b2_ref[...]
        o_ref[...] = y.astype(o_ref.dtype)

    @pl.when(i >= nused_ref[0])
    def _():
        o_ref[...] = jnp.zeros(o_ref.shape, o_ref.dtype)


def _moe_experts(xg, block_expert, n_used, w1g, w1l, b1g, b1l, w2, b2):
    P, D = xg.shape
    dff = w2.shape[1]
    nb = P // MOE_TM
    wspec = lambda k, n: pl.BlockSpec((None, k, n), lambda i, be, nu: (be[i], 0, 0))
    grid_spec = pltpu.PrefetchScalarGridSpec(
        num_scalar_prefetch=2,
        grid=(nb,),
        in_specs=[
            pl.BlockSpec((MOE_TM, D), lambda i, be, nu: (i, 0)),
            wspec(dff, D), wspec(dff, D), wspec(1, dff), wspec(1, dff),
            wspec(dff, D), wspec(1, D),
        ],
        out_specs=pl.BlockSpec((MOE_TM, D), lambda i, be, nu: (i, 0)),
    )
    return pl.pallas_call(
        _moe_body,
        grid_spec=grid_spec,
        out_shape=jax.ShapeDtypeStruct((P, D), BF16),
        compiler_params=_cparams(("arbitrary",)),
        name="moe_experts",
    )(block_expert, n_used, xg, w1g, w1l, b1g, b1l, w2, b2)


def _prep_w1_body(w_ref, sel_ref, g_ref, l_ref):
    half = g_ref.shape[0]
    picked = lax.dot_general(sel_ref[...], w_ref[...].astype(BF16), (((1,), (1,)), ((), ())),
                             preferred_element_type=F32)
    g_ref[...] = picked[:half].astype(BF16)
    l_ref[...] = picked[half:].astype(BF16)


def _prep_w1(w1, tc=8 * LANES):
    E, D, F2 = w1.shape
    tc = min(tc, F2)
    half = tc // 2
    row = jnp.arange(tc)
    picks = jnp.where(row < half, 2 * row, 2 * (row - half) + 1)
    sel = (picks[:, None] == jnp.arange(tc)[None, :]).astype(BF16)
    out = jax.ShapeDtypeStruct((E, F2 // 2, D), BF16)
    return pl.pallas_call(
        _prep_w1_body,
        grid=(E, F2 // tc),
        in_specs=[pl.BlockSpec((None, D, tc), lambda e, j: (e, 0, j)),
                  pl.BlockSpec((tc, tc), lambda e, j: (0, 0))],
        out_specs=[pl.BlockSpec((None, half, D), lambda e, j: (e, j, 0)),
                   pl.BlockSpec((None, half, D), lambda e, j: (e, j, 0))],
        out_shape=[out, out],
        compiler_params=_cparams(("parallel", "parallel")),
        name="prep_w1",
    )(w1, sel)


def _combine_body(h_ref, y_ref, g_ref, o_ref):
    acc = h_ref[...]
    g = g_ref[...]
    for k in range(y_ref.shape[0]):
        acc = acc + g[:, k:k + 1] * y_ref[k].astype(F32)
    o_ref[...] = acc


def _combine(h, yk, gates, tm=512):
    T, D = h.shape
    K = yk.shape[0]
    return pl.pallas_call(
        _combine_body,
        grid=(T // tm,),
        in_specs=[pl.BlockSpec((tm, D), lambda i: (i, 0)),
                  pl.BlockSpec((K, tm, D), lambda i: (0, i, 0)),
                  pl.BlockSpec((tm, K), lambda i: (i, 0))],
        out_specs=pl.BlockSpec((tm, D), lambda i: (i, 0)),
        out_shape=jax.ShapeDtypeStruct((T, D), F32),
        compiler_params=_cparams(("parallel",)),
        name="moe_combine",
    )(h, yk, gates)


def _lookup(table, idx):
    n = table.shape[0]
    hit = idx[None, :] == jnp.arange(n, dtype=idx.dtype)[:, None]
    return jnp.sum(jnp.where(hit, table[:, None], 0), axis=0)


def _moe(h, hn_bf16, logits, w1, b1, w2, b2):
    T, D = hn_bf16.shape
    E = w1.shape[0]
    top_val, top_idx = lax.top_k(logits, TOP_K)
    gates = jax.nn.softmax(top_val, axis=-1)
    TK = T * TOP_K
    e_flat = top_idx.reshape(TK).astype(jnp.int32)
    order = jnp.argsort(e_flat).astype(jnp.int32)
    rank = jnp.argsort(order).astype(jnp.int32)
    counts = jnp.sum(jnp.arange(E, dtype=jnp.int32)[:, None] == e_flat[None, :], axis=1,
                     dtype=jnp.int32)
    starts = jnp.cumsum(counts) - counts
    padded = ((counts + MOE_TM - 1) // MOE_TM) * MOE_TM
    pad_ends = jnp.cumsum(padded)
    pad_starts = pad_ends - padded
    nb = (TK + E * (MOE_TM - 1) + MOE_TM - 1) // MOE_TM
    block_start = jnp.arange(nb, dtype=jnp.int32) * MOE_TM
    block_expert = jnp.minimum(jnp.sum(block_start[:, None] >= pad_ends[None, :], axis=1),
                               E - 1).astype(jnp.int32)
    n_used = (pad_ends[-1] // MOE_TM).astype(jnp.int32).reshape(1)
    blk_first = block_start - pad_starts[block_expert]
    within = blk_first[:, None] + jnp.arange(MOE_TM, dtype=jnp.int32)[None, :]
    valid = within < counts[block_expert][:, None]
    sorted_idx = jnp.where(valid, starts[block_expert][:, None] + within, 0).reshape(nb * MOE_TM)
    filler = jnp.arange(nb * MOE_TM, dtype=jnp.int32) % T
    src = jnp.where(valid.reshape(nb * MOE_TM), order[sorted_idx] // TOP_K, filler)
    pos = _lookup(pad_starts - starts, e_flat) + rank
    w1g, w1l = _prep_w1(w1)
    y = _moe_experts(hn_bf16[src], block_expert, n_used, w1g, w1l,
                     b1[:, None, 0::2].astype(F32), b1[:, None, 1::2].astype(F32),
                     w2, b2[:, None, :].astype(F32))
    return _combine(h, y[pos.reshape(T, TOP_K).T], gates)


def _rope_tables(L):
    rows = L // GRID_W
    row = jnp.repeat(jnp.arange(rows, dtype=F32), GRID_W)
    col = jnp.tile(jnp.arange(GRID_W, dtype=F32), rows)
    half = HEAD_DIM // 2
    freqs = ROPE_THETA ** (-jnp.arange(0, half, 2, dtype=F32) / half)
    ang = jnp.concatenate([row[:, None] * freqs, col[:, None] * freqs], axis=-1)
    cos = jnp.repeat(jnp.cos(ang), 2, axis=-1)
    sin = jnp.repeat(jnp.sin(ang), 2, axis=-1)
    sign = jnp.tile(jnp.array([-1.0, 1.0], F32), HEAD_DIM // 2)
    reps = LANES // HEAD_DIM
    return jnp.tile(cos, (1, reps)), jnp.tile(sin * sign, (1, reps))


def kernel(x, norm1_g, w_in, conv_w, conv_b, filt_w1, filt_b1, filt_freq1, filt_w2, filt_b2, filt_freq2, filt_w3, filt_b3, hyena_d, q_norm_g, k_norm_g, w_hyena_up, w_attn_up, w_out, norm2_g, router_w, router_b, expert_w1, expert_b1, expert_w2, expert_b2):
    B, L, D = x.shape
    T = B * L
    depth = w_in.shape[0]
    hw = conv_w.shape[-1] // 3
    aw = N_Q_HEADS * HEAD_DIM
    kvw = N_KV_HEADS * HEAD_DIM
    widths = (3 * hw, aw, 2 * kvw, 2 * D)
    cosf, sinf = _rope_tables(L)
    for l in range(depth):
        xt = x.reshape(T, D)
        gq = jnp.tile(q_norm_g[l].astype(F32), N_Q_HEADS)[None, :]
        gk = jnp.tile(k_norm_g[l].astype(F32), N_KV_HEADS)[None, :]
        z, qr, kr, va, gates = _inproj(xt, norm1_g[l], w_in[l].astype(BF16), widths, gq, gk,
                                       cosf, sinf, L)
        y_hy = _hyena(z.reshape(B, L, 3 * hw), conv_w[l], conv_b[l], filt_w1[l], filt_b1[l],
                      filt_freq1[l], filt_w2[l], filt_b2[l], filt_freq2[l], filt_w3[l], filt_b3[l],
                      hyena_d[l])
        y_at = _attention(qr.reshape(B, L, aw), kr, va.reshape(N_KV_HEADS, B, L, kvw))
        h, hn, logits = _merge(y_hy.reshape(T, hw), y_at.reshape(T, aw), gates, xt,
                               w_hyena_up[l], w_attn_up[l], w_out[l], norm2_g[l],
                               router_w[l], router_b[l])
        x = _moe(h, hn, logits[:, :N_EXPERTS], expert_w1[l], expert_b1[l], expert_w2[l],
                 expert_b2[l]).reshape(B, L, D)
    return x
```

```python
import functools
import math

import jax
import jax.numpy as jnp
from jax import lax
from jax.experimental import pallas as pl
from jax.experimental.pallas import tpu as pltpu

F32 = jnp.float32
BF16 = jnp.bfloat16

GRID_W = 64
HEAD_DIM = 64
N_Q_HEADS = 8
N_KV_HEADS = 2
Q_PER_KV = N_Q_HEADS // N_KV_HEADS
ROPE_THETA = 10000.0
HYENA_ORDER = 2
SHORT_CONV = 3
FILTER_EMB = 33
FAST_DECAY_PCT = 0.3
SLOW_DECAY_PCT = 1.5
DECAY_TARGET = 1e-2
N_EXPERTS = 32
TOP_K = 4
SWIGLU_LIMIT = 7.0
SWIGLU_ALPHA = 1.702
EPS = 1e-6

LANES = 128
SUBLANES = 8
VMEM_LIMIT = 56 * 1024 * 1024

FFT_N2 = LANES
FFT_PITCH = FFT_N2 + SUBLANES
FFT_K1_CHUNK_MAX = 16
FFT_UNROLL_OUTER = 32
FFT_UNROLL_INNER = True
MOE_TM = 512
LOGIT_PAD = LANES


def _cparams(sem):
    return pltpu.CompilerParams(dimension_semantics=sem, vmem_limit_bytes=VMEM_LIMIT)


def _split_bf16(x):
    hi = x.astype(BF16)
    lo = (x - hi.astype(F32)).astype(BF16)
    return hi, lo


def _inproj_body(x_ref, g_ref, w_ref, gq_ref, gk_ref, cos_ref, sin_ref, oq_ref, ok_ref,
                 z_ref, qo_ref, ko_ref, vo_ref, gate_ref, q_scr, kv_scr, *, widths):
    @pl.when(pl.program_id(0) == 0)
    def _():
        q_scr[...] = jnp.zeros(q_scr.shape, F32)
        kv_scr[...] = jnp.zeros(kv_scr.shape, F32)

    x = x_ref[...]
    ms = jnp.mean(x * x, axis=-1, keepdims=True)
    u = (x * lax.rsqrt(ms + EPS) * g_ref[...]).astype(BF16)
    offs = [sum(widths[:i]) for i in range(len(widths) + 1)]
    proj = lambda i: jnp.dot(u, w_ref[:, offs[i]:offs[i + 1]], preferred_element_type=F32)
    z_ref[...] = proj(0)
    gate_ref[...] = proj(3).astype(gate_ref.dtype)
    _qk_epilogue(q_scr[...], kv_scr[...], gq_ref[...], gk_ref[...], cos_ref[...], sin_ref[...],
                 oq_ref[...], ok_ref[...], qo_ref, ko_ref, vo_ref)
    q_scr[...] = proj(1)
    kv_scr[...] = proj(2)


def _inproj(xt, g, w_bf16, widths, gq, gk, cosf, sinf, seq_len, tm=512):
    T, D = xt.shape
    n = w_bf16.shape[1]
    zw, qw, kvw, gw = widths
    kw = kvw // 2
    assert kw == LANES and N_KV_HEADS * HEAD_DIM == LANES and N_KV_HEADS == 2
    nl = seq_len // tm
    nt = T // tm
    cur = lambda i: jnp.minimum(i, nt - 1)
    prev = lambda i: jnp.maximum(i - 1, 0)

    def blk_ones(w):
        r = jnp.arange(w) // HEAD_DIM
        return (r[:, None] == r[None, :]).astype(BF16)

    const = lambda shape: pl.BlockSpec(shape, lambda i: (0, 0))
    return pl.pallas_call(
        functools.partial(_inproj_body, widths=widths),
        grid=(nt + 1,),
        in_specs=[
            pl.BlockSpec((tm, D), lambda i: (cur(i), 0)),
            const((1, D)), const((D, n)), const((1, qw)), const((1, kw)),
            pl.BlockSpec((tm, LANES), lambda i: (prev(i) % nl, 0)),
            pl.BlockSpec((tm, LANES), lambda i: (prev(i) % nl, 0)),
            const((qw, qw)), const((kw, kw)),
        ],
        out_specs=[
            pl.BlockSpec((tm, zw), lambda i: (cur(i), 0)),
            pl.BlockSpec((tm, qw), lambda i: (prev(i), 0)),
            pl.BlockSpec((kw, tm), lambda i: (0, prev(i))),
            pl.BlockSpec((N_KV_HEADS, tm, kw), lambda i: (0, prev(i), 0)),
            pl.BlockSpec((tm, gw), lambda i: (cur(i), 0)),
        ],
        out_shape=[
            jax.ShapeDtypeStruct((T, zw), F32),
            jax.ShapeDtypeStruct((T, qw), BF16),
            jax.ShapeDtypeStruct((kw, T), BF16),
            jax.ShapeDtypeStruct((N_KV_HEADS, T, kw), BF16),
            jax.ShapeDtypeStruct((T, gw), BF16),
        ],
        scratch_shapes=[pltpu.VMEM((tm, qw), F32), pltpu.VMEM((tm, kvw), F32)],
        compiler_params=_cparams(("arbitrary",)),
        name="inproj",
    )(xt, g.reshape(1, D), w_bf16, gq, gk, cosf, sinf, blk_ones(qw), blk_ones(kw))


def _head_norm_rope(x, gain, cosf, sinf, ones_blk):
    w = x.shape[-1]
    hi, lo = _split_bf16(x * x)
    ss = (jnp.dot(hi, ones_blk, preferred_element_type=F32)
          + jnp.dot(lo, ones_blk, preferred_element_type=F32))
    xn = x * lax.rsqrt(ss * (1.0 / HEAD_DIM) + EPS) * gain
    lane = lax.broadcasted_iota(jnp.int32, (x.shape[0], LANES), 1)
    cols = []
    for c in range(w // LANES):
        col = xn[:, c * LANES:(c + 1) * LANES]
        cols.append(jnp.where(lane % 2 == 0, pltpu.roll(col, LANES - 1, 1), pltpu.roll(col, 1, 1)))
    swapped = cols[0] if len(cols) == 1 else jnp.concatenate(cols, axis=1)
    return xn * cosf + swapped * sinf


def _qk_epilogue(q, kv, gq, gk, cosf, sinf, ones_q, ones_k, qo_ref, ko_ref, vo_ref):
    nq = q.shape[-1] // LANES
    q = _head_norm_rope(q, gq, jnp.tile(cosf, (1, nq)), jnp.tile(sinf, (1, nq)), ones_q)
    qo_ref[...] = (q * (HEAD_DIM ** -0.5 * math.log2(math.e))).astype(BF16)
    kw = kv.shape[-1] // 2
    k = _head_norm_rope(kv[:, :kw], gk, cosf, sinf, ones_k)
    ko_ref[...] = k.T.astype(BF16)
    v = kv[:, kw:]
    lane = lax.broadcasted_iota(jnp.int32, v.shape, 1)
    for h in range(N_KV_HEADS):
        vo_ref[h] = jnp.where((lane // HEAD_DIM) == h, v, 1.0).astype(BF16)


def _attn_body(q_ref, k_ref, v_ref, o_ref, qs_ref, m_ref, acc_ref, *, tk, nsplit):
    kvh = pl.program_id(1)
    tq = q_ref.shape[0]
    seq = k_ref.shape[1]
    rows = Q_PER_KV * tq
    lane = lax.broadcasted_iota(jnp.int32, (tq, LANES), 1)
    in_head = (lane // HEAD_DIM) == kvh

    for g in range(Q_PER_KV):
        col = q_ref[:, (g // 2) * LANES:(g // 2 + 1) * LANES].astype(F32)
        col = jnp.where((g % 2) == kvh, col, pltpu.roll(col, HEAD_DIM, 1))
        qs_ref[g * tq:(g + 1) * tq, :] = jnp.where(in_head, col, 0.0).astype(BF16)
    m_ref[...] = jnp.full(m_ref.shape, -jnp.inf, F32)
    acc_ref[...] = jnp.zeros(acc_ref.shape, F32)
    part = rows // nsplit

    def step(c, carry):
        r0 = pl.multiple_of(c * tk, tk)
        kc = k_ref[:, pl.ds(r0, tk)]
        vc = v_ref[pl.ds(r0, tk), :]
        for h in range(nsplit):
            sl = slice(h * part, (h + 1) * part)
            s = jnp.dot(qs_ref[sl, :], kc, preferred_element_type=F32)
            m_prev = m_ref[sl, :]
            m_new = jnp.maximum(m_prev, jnp.max(s, axis=-1, keepdims=True))
            p = jnp.exp2(s - jnp.tile(m_new, (1, tk // LANES)))
            acc_ref[sl, :] = jnp.exp2(m_prev - m_new) * acc_ref[sl, :] + jnp.dot(
                p.astype(BF16), vc, preferred_element_type=F32)
            m_ref[sl, :] = m_new
        return carry

    lax.fori_loop(0, seq // tk, step, 0)

    acc = acc_ref[...]
    o = acc / pltpu.roll(acc, HEAD_DIM, 1)
    for c in range(Q_PER_KV // 2):
        even = o[(2 * c) * tq:(2 * c + 1) * tq, :]
        odd = o[(2 * c + 1) * tq:(2 * c + 2) * tq, :]
        even = jnp.where(kvh == 0, even, pltpu.roll(even, HEAD_DIM, 1))
        odd = jnp.where(kvh == 1, odd, pltpu.roll(odd, HEAD_DIM, 1))
        o_ref[:, c * LANES:(c + 1) * LANES] = jnp.where(lane < HEAD_DIM, even, odd).astype(o_ref.dtype)


def _attention(q, kt, v_aug, tq=512, tk=2048, nsplit=2):
    B, L, qw = q.shape
    gw = Q_PER_KV * HEAD_DIM
    kw = kt.shape[0]
    rows = Q_PER_KV * tq
    return pl.pallas_call(
        functools.partial(_attn_body, tk=tk, nsplit=nsplit),
        grid=(B, N_KV_HEADS, L // tq),
        in_specs=[
            pl.BlockSpec((None, tq, gw), lambda b, h, i: (b, i, h)),
            pl.BlockSpec((kw, L), lambda b, h, i: (0, b)),
            pl.BlockSpec((None, None, L, kw), lambda b, h, i: (h, b, 0, 0)),
        ],
        out_specs=pl.BlockSpec((None, tq, gw), lambda b, h, i: (b, i, h)),
        out_shape=jax.ShapeDtypeStruct((B, L, qw), BF16),
        scratch_shapes=[
            pltpu.VMEM((rows, kw), BF16),
            pltpu.VMEM((rows, LANES), F32),
            pltpu.VMEM((rows, kw), F32),
        ],
        compiler_params=_cparams(("parallel", "parallel", "parallel")),
        name="attention",
    )(q, kt, v_aug)


def _hdot(a, b):
    ah, al = _split_bf16(a)
    bh, bl = _split_bf16(b)
    return (jnp.dot(ah, bh, preferred_element_type=F32)
            + jnp.dot(al, bh, preferred_element_type=F32)
            + jnp.dot(ah, bl, preferred_element_type=F32))


def _filter_body(feat_ref, t_ref, w1_ref, b1_ref, f1_ref, w2_ref, b2_ref, f2_ref, w3_ref, b3_ref,
                 delta_ref, bwd_ref, h_ref, sum_ref):
    i = pl.program_id(0)
    h = jnp.sin(f1_ref[...] * (_hdot(feat_ref[...], w1_ref[...]) + b1_ref[...]))
    h = jnp.sin(f2_ref[...] * (_hdot(h, w2_ref[...]) + b2_ref[...]))
    h = _hdot(h, w3_ref[...]) + b3_ref[...]
    t = t_ref[...]
    h = h * jnp.exp(-t * delta_ref[...])
    width = h_ref.shape[-1]
    for j in range(h_ref.shape[0]):
        h_ref[j] = h[:, j * width:(j + 1) * width]
    row = lax.broadcasted_iota(jnp.int32, h.shape, 0) + i * h.shape[0]
    a = jnp.where((row == 0) & (bwd_ref[...] > 0.5), 0.0, jnp.abs(h))
    part = jnp.sum(a, axis=0, keepdims=True)

    @pl.when(i == 0)
    def _():
        sum_ref[...] = jnp.zeros(sum_ref.shape, F32)

    sum_ref[...] += jnp.broadcast_to(part, sum_ref.shape)


def _hyena_filter(L, width, fw1, fb1, ff1, fw2, fb2, ff2, fw3, fb3, tl=512):
    bands = (FILTER_EMB - 1) // 2
    t = jnp.linspace(0.0, 1.0, L, dtype=F32)[:, None]
    w = 2.0 * math.pi * jnp.arange(L, dtype=F32)[:, None] / L
    fr = jnp.linspace(1e-4, bands - 1, bands, dtype=F32)[None, :]
    feats = jnp.concatenate([t, jnp.cos(w * fr), -jnp.sin(w * fr)], axis=-1)
    max_decay = math.log(DECAY_TARGET) / FAST_DECAY_PCT
    min_decay = math.log(DECAY_TARGET) / SLOW_DECAY_PCT
    deltas = jnp.abs(jnp.linspace(min_decay, max_decay, width, dtype=F32))
    ncol = fw3.shape[1]
    delta_cols = jnp.tile(deltas, ncol // width)[None, :]
    is_bwd = ((jnp.arange(ncol) // width) % 2).astype(F32)[None, :]
    emb = hid = LANES

    def pad2(a, r, c):
        a = a.astype(F32)
        return jnp.zeros((r, c), F32).at[:a.shape[0], :a.shape[1]].set(a)

    row = lambda a: pad2(a.reshape(1, -1), 1, hid)
    feats = pad2(feats, L, emb)
    fw1, fw2, fw3 = pad2(fw1, emb, hid), pad2(fw2, hid, hid), pad2(fw3, hid, ncol)
    fb3 = fb3.reshape(1, ncol).astype(F32)
    const = lambda shape: pl.BlockSpec(shape, lambda i: (0, 0))
    h, sums = pl.pallas_call(
        _filter_body,
        grid=(L // tl,),
        in_specs=[
            pl.BlockSpec((tl, emb), lambda i: (i, 0)),
            pl.BlockSpec((tl, 1), lambda i: (i, 0)),
            const((emb, hid)), const((1, hid)), const((1, hid)),
            const((hid, hid)), const((1, hid)), const((1, hid)),
            const((hid, ncol)), const((1, ncol)), const((1, ncol)), const((1, ncol)),
        ],
        out_specs=[pl.BlockSpec((ncol // width, tl, width), lambda i: (0, i, 0)),
                   pl.BlockSpec((SUBLANES, ncol), lambda i: (0, 0))],
        out_shape=[jax.ShapeDtypeStruct((ncol // width, L, width), F32),
                   jax.ShapeDtypeStruct((SUBLANES, ncol), F32)],
        compiler_params=_cparams(("arbitrary",)),
        name="hyena_filter",
    )(feats, t, fw1, row(fb1), row(ff1), fw2, row(fb2), row(ff2), fw3, fb3, delta_cols, is_bwd)
    return h, sums[0]


def _shortconv_rows(z_ref, w_ref, b_ref, c, chunk, interior=False):
    L = z_ref.shape[0]
    r0 = pl.multiple_of(c * chunk, chunk)
    cur = z_ref[pl.ds(r0, chunk), :]
    if interior:
        down = z_ref[pl.ds(r0 - 1, chunk), :]
        up = z_ref[pl.ds(r0 + 1, chunk), :]
    else:
        row = lax.broadcasted_iota(jnp.int32, (chunk, z_ref.shape[1]), 0)
        prev_row = z_ref[pl.ds(jnp.maximum(r0 - 1, 0), 1), :]
        next_row = z_ref[pl.ds(jnp.minimum(r0 + chunk, L - 1), 1), :]
        prev_row = jnp.where(c == 0, 0.0, prev_row)
        next_row = jnp.where(c == L // chunk - 1, 0.0, next_row)
        down = jnp.where(row == 0, prev_row, pltpu.roll(cur, 1, 0))
        up = jnp.where(row == chunk - 1, next_row, pltpu.roll(cur, chunk - 1, 0))
    return b_ref[...] + down * w_ref[0:1, :] + cur * w_ref[1:2, :] + up * w_ref[2:3, :]


def _for_slabs(n, body):
    body(0, False)
    lax.fori_loop(1, n - 1, lambda c, carry: (body(c, True), carry)[1], 0)
    body(n - 1, False)


def _dft_tables(n1_len, n2_len):
    n = n1_len * n2_len

    def root(num, den):
        ang = (2.0 * math.pi / den) * (num % den).astype(F32)
        return jnp.cos(ang), -jnp.sin(ang)

    i1 = jnp.arange(n1_len, dtype=jnp.int32)
    i2 = jnp.arange(n2_len, dtype=jnp.int32)
    f1r, f1i = root(i1[:, None] * i1[None, :], n1_len)
    f2r, f2i = root(i2[:, None] * i2[None, :], n2_len)
    twr, twi = root(i1[:, None] * i2[None, :], n)
    return (f1r, f1i), (f2r, f2i), (twr, twi)


def _stacked_inner_dft(f2r, f2i, twr_row, twi_row):
    gr = f2r * twr_row - f2i * twi_row
    gi = f2r * twi_row + f2i * twr_row
    top = jnp.concatenate([gr, -gi], axis=1)
    bot = jnp.concatenate([gi, gr], axis=1)
    return jnp.concatenate([top, bot], axis=0)


def _fft_plan(n):
    n1_len = n // FFT_N2
    nk1 = n1_len // 2 + 1
    nk1_pad = -(-nk1 // SUBLANES) * SUBLANES
    chunk = max(c for c in range(1, FFT_K1_CHUNK_MAX + 1) if nk1 % c == 0)
    return n1_len, nk1, nk1_pad, chunk


def _outer_dft_to_scratch(load_rows, fa_ref, ar_ref, ai_ref, nk1_pad):
    fa = fa_ref[...]

    def step(n2, carry):
        a = jnp.dot(fa, load_rows(n2).astype(BF16), preferred_element_type=F32)
        ar_ref[pl.ds(n2, nk1_pad, stride=FFT_PITCH), :] = a[:nk1_pad]
        ai_ref[pl.ds(n2, nk1_pad, stride=FFT_PITCH), :] = a[nk1_pad:]
        return carry

    lax.fori_loop(0, FFT_N2, step, 0, unroll=FFT_UNROLL_OUTER)


def _spec_body(f_ref, b_ref, inv_ref, fa_ref, f2r_ref, f2i_ref, twr_ref, twi_ref, kr_ref, ki_ref,
               fr_ref, fi_ref, br_ref, bi_ref, *, n1_len, nk1_pad, chunk):
    kc = pl.program_id(2)
    half = n1_len // 2

    @pl.when(kc == 0)
    def _():
        inv = inv_ref[...]
        row = lax.broadcasted_iota(jnp.int32, (half, LANES), 0)
        _outer_dft_to_scratch(lambda n2: f_ref[pl.ds(n2, half, stride=FFT_N2), :] * inv,
                              fa_ref, fr_ref, fi_ref, nk1_pad)
        _outer_dft_to_scratch(
            lambda n2: jnp.where((row == 0) & (n2 == 0), 0.0,
                                 b_ref[pl.ds(n2, half, stride=FFT_N2), :] * inv),
            fa_ref, br_ref, bi_ref, nk1_pad)

    f2r = f2r_ref[...]
    f2i = f2i_ref[...]

    def step(t, carry):
        k1 = kc * chunk + t
        base = pl.multiple_of(k1 * FFT_PITCH, SUBLANES)
        mf = _stacked_inner_dft(f2r, f2i, twr_ref[pl.ds(k1, 1), :],
                                twi_ref[pl.ds(k1, 1), :]).astype(BF16)
        rows = pl.ds(base, FFT_N2)
        xf = jnp.dot(mf, jnp.concatenate([fr_ref[rows, :], fi_ref[rows, :]], axis=0).astype(BF16),
                     preferred_element_type=F32)
        xb = jnp.dot(mf, jnp.concatenate([br_ref[rows, :], bi_ref[rows, :]], axis=0).astype(BF16),
                     preferred_element_type=F32)
        o = pl.multiple_of(t * FFT_N2, FFT_N2)
        kr_ref[pl.ds(o, FFT_N2), :] = xf[:FFT_N2] + xb[:FFT_N2]
        ki_ref[pl.ds(o, FFT_N2), :] = xf[FFT_N2:] - xb[FFT_N2:]
        return carry

    lax.fori_loop(0, chunk, step, 0, unroll=FFT_UNROLL_INNER)


def _filter_spectrum(h, inv_norm, tables):
    O2, L, C = h.shape
    O = O2 // 2
    n1_len, nk1, nk1_pad, chunk = _fft_plan(2 * L)
    half = n1_len // 2
    (f1r, f1i), (f2r, f2i), (twr, twi) = tables
    fa = jnp.concatenate([f1r[:nk1_pad, :half], f1i[:nk1_pad, :half]], axis=0).astype(BF16)
    rows = chunk * FFT_N2
    const = lambda shape: pl.BlockSpec(shape, lambda o, c, k: (0, 0))
    out = jax.ShapeDtypeStruct((O, nk1 * FFT_N2, C), F32)
    scratch = pltpu.VMEM((nk1_pad * FFT_PITCH, LANES), F32)
    return pl.pallas_call(
        functools.partial(_spec_body, n1_len=n1_len, nk1_pad=nk1_pad, chunk=chunk),
        grid=(O, C // LANES, nk1 // chunk),
        in_specs=[
            pl.BlockSpec((None, L, LANES), lambda o, c, k: (2 * o, 0, c)),
            pl.BlockSpec((None, L, LANES), lambda o, c, k: (2 * o + 1, 0, c)),
            pl.BlockSpec((None, 1, LANES), lambda o, c, k: (o, 0, c)),
            const(fa.shape), const(f2r.shape), const(f2i.shape), const(twr.shape), const(twi.shape),
        ],
        out_specs=[pl.BlockSpec((None, rows, LANES), lambda o, c, k: (o, k, c)),
                   pl.BlockSpec((None, rows, LANES), lambda o, c, k: (o, k, c))],
        out_shape=[out, out],
        scratch_shapes=[scratch, scratch, scratch, scratch],
        compiler_params=_cparams(("parallel", "parallel", "arbitrary")),
        name="filter_spectrum",
    )(h, h, inv_norm, fa, f2r, f2i, twr, twi)


def _fftconv_body(y_ref, gate_ref, wy_ref, by_ref, wg_ref, bg_ref, d_ref, kr_ref, ki_ref, fa_ref,
                  fs_ref, f2r_ref, f2i_ref, twr_ref, twi_ref, o_ref, xs_ref, ar_ref, ai_ref, *,
                  n1_len, nk1_pad, chunk, conv_y):
    kc = pl.program_id(2)
    half = n1_len // 2

    @pl.when(kc == 0)
    def _():
        def copy(n1, interior):
            src = pl.multiple_of(n1 * FFT_N2, FFT_N2)
            dst = pl.multiple_of(n1 * FFT_PITCH, SUBLANES)
            if conv_y:
                xs_ref[pl.ds(dst, FFT_N2), :] = _shortconv_rows(y_ref, wy_ref, by_ref, n1, FFT_N2,
                                                                interior)
            else:
                xs_ref[pl.ds(dst, FFT_N2), :] = y_ref[pl.ds(src, FFT_N2), :]

        _for_slabs(half, copy)
        _outer_dft_to_scratch(lambda n2: xs_ref[pl.ds(n2, half, stride=FFT_PITCH), :],
                              fa_ref, ar_ref, ai_ref, nk1_pad)

    f2r = f2r_ref[...]
    f2i = f2i_ref[...]

    def step(t, carry):
        k1 = kc * chunk + t
        base = pl.multiple_of(k1 * FFT_PITCH, SUBLANES)
        rhs = jnp.concatenate([ar_ref[pl.ds(base, FFT_N2), :], ai_ref[pl.ds(base, FFT_N2), :]],
                              axis=0).astype(BF16)
        mf = _stacked_inner_dft(f2r, f2i, twr_ref[pl.ds(k1, 1), :], twi_ref[pl.ds(k1, 1), :])
        x = jnp.dot(mf.astype(BF16), rhs, preferred_element_type=F32)
        xr, xi = x[:FFT_N2], x[FFT_N2:]
        o = pl.multiple_of(t * FFT_N2, FFT_N2)
        kr = kr_ref[pl.ds(o, FFT_N2), :]
        ki = ki_ref[pl.ds(o, FFT_N2), :]
        z = jnp.concatenate([xr * kr - xi * ki, xr * ki + xi * kr], axis=0).astype(BF16)
        b = jnp.dot(mf.T.astype(BF16), z, preferred_element_type=F32)
        ar_ref[pl.ds(base, FFT_N2), :] = b[:FFT_N2]
        ai_ref[pl.ds(base, FFT_N2), :] = b[FFT_N2:]
        return carry

    lax.fori_loop(0, chunk, step, 0, unroll=FFT_UNROLL_INNER)

    @pl.when(kc == pl.num_programs(2) - 1)
    def _():
        fs = fs_ref[...]

        def inv_outer(n2, carry):
            rhs = jnp.concatenate([ar_ref[pl.ds(n2, nk1_pad, stride=FFT_PITCH), :],
                                   ai_ref[pl.ds(n2, nk1_pad, stride=FFT_PITCH), :]],
                                  axis=0).astype(BF16)
            conv = jnp.dot(fs, rhs, preferred_element_type=F32)
            ar_ref[pl.ds(n2, half, stride=FFT_PITCH), :] = conv
            return carry

        lax.fori_loop(0, FFT_N2, inv_outer, 0, unroll=FFT_UNROLL_OUTER)
        d = d_ref[...]

        def finish(n1, interior):
            src = pl.multiple_of(n1 * FFT_PITCH, SUBLANES)
            dst = pl.multiple_of(n1 * FFT_N2, FFT_N2)
            y = xs_ref[pl.ds(src, FFT_N2), :]
            gate = _shortconv_rows(gate_ref, wg_ref, bg_ref, n1, FFT_N2, interior)
            o_ref[pl.ds(dst, FFT_N2), :] = (gate * (
                ar_ref[pl.ds(src, FFT_N2), :] + y * d)).astype(o_ref.dtype)

        _for_slabs(half, finish)


def _fftconv_gate(y, y_off, conv_y, z, gate_off, conv_w, conv_b, d, kr, ki, tables, out_dtype):
    B, L, _ = y.shape
    C = d.shape[-1]
    yo, go = y_off // LANES, gate_off // LANES
    wo = yo if conv_y else go
    N = 2 * L
    n1_len, nk1, nk1_pad, chunk = _fft_plan(N)
    half = n1_len // 2
    (f1r, f1i), (f2r, f2i), (twr, twi) = tables
    fa = jnp.concatenate([f1r[:nk1_pad, :half], f1i[:nk1_pad, :half]], axis=0).astype(BF16)
    wts = jnp.concatenate([jnp.ones((1,), F32), jnp.full((nk1 - 2,), 2.0, F32), jnp.ones((1,), F32),
                           jnp.zeros((nk1_pad - nk1,), F32)]) * (1.0 / N)
    fs = jnp.concatenate([f1r[:half, :nk1_pad] * wts, f1i[:half, :nk1_pad] * wts],
                         axis=1).astype(BF16)
    rows = chunk * FFT_N2
    const = lambda shape: pl.BlockSpec(shape, lambda c, b, k: (0, 0))
    return pl.pallas_call(
        functools.partial(_fftconv_body, n1_len=n1_len, nk1_pad=nk1_pad, chunk=chunk,
                          conv_y=conv_y),
        grid=(C // LANES, B, nk1 // chunk),
        in_specs=[
            pl.BlockSpec((None, L, LANES), lambda c, b, k: (b, 0, c + yo)),
            pl.BlockSpec((None, L, LANES), lambda c, b, k: (b, 0, c + go)),
            pl.BlockSpec((SHORT_CONV, LANES), lambda c, b, k: (0, c + wo)),
            pl.BlockSpec((1, LANES), lambda c, b, k: (0, c + wo)),
            pl.BlockSpec((SHORT_CONV, LANES), lambda c, b, k: (0, c + go)),
            pl.BlockSpec((1, LANES), lambda c, b, k: (0, c + go)),
            pl.BlockSpec((1, LANES), lambda c, b, k: (0, c)),
            pl.BlockSpec((rows, LANES), lambda c, b, k: (k, c)),
            pl.BlockSpec((rows, LANES), lambda c, b, k: (k, c)),
            const(fa.shape), const(fs.shape), const(f2r.shape), const(f2i.shape),
            const(twr.shape), const(twi.shape),
        ],
        out_specs=pl.BlockSpec((None, L, LANES), lambda c, b, k: (b, 0, c)),
        out_shape=jax.ShapeDtypeStruct((B, L, C), out_dtype),
        scratch_shapes=[pltpu.VMEM((half * FFT_PITCH, LANES), F32),
                        pltpu.VMEM((nk1_pad * FFT_PITCH, LANES), F32),
                        pltpu.VMEM((nk1_pad * FFT_PITCH, LANES), F32)],
        compiler_params=_cparams(("parallel", "parallel", "arbitrary")),
        name="fftconv_gate",
    )(y, z, conv_w, conv_b, conv_w, conv_b, d, kr, ki, fa, fs, f2r, f2i, twr, twi)


def _hyena(z, conv_w, conv_b, fw1, fb1, ff1, fw2, fb2, ff2, fw3, fb3, hyena_d):
    B, L, C3 = z.shape
    W = C3 // 3
    h, sums = _hyena_filter(L, W, fw1, fb1, ff1, fw2, fb2, ff2, fw3, fb3)
    sums = sums.reshape(HYENA_ORDER, 2, W)
    inv_norm = (1.0 / (sums[:, 0] + sums[:, 1]))[:, None, :]
    tables = _dft_tables(2 * L // FFT_N2, FFT_N2)
    kr, ki = _filter_spectrum(h, inv_norm, tables)
    conv_b = conv_b.reshape(1, C3)
    y = z
    for o in range(HYENA_ORDER):
        y = _fftconv_gate(y, 0, o == 0, z, (o + 1) * W, conv_w, conv_b,
                          hyena_d[o].reshape(1, W).astype(F32), kr[o], ki[o], tables,
                          BF16 if o == HYENA_ORDER - 1 else F32)
    return y


def _merge_body(yh_ref, ya_ref, g_ref, x_ref, whu_ref, wau_ref, wo_ref, n2_ref, rwh_ref, rwl_ref,
                rb_ref, h_ref, hn_ref, lg_ref):
    D = x_ref.shape[-1]
    up_h = jnp.dot(yh_ref[...], whu_ref[...], preferred_element_type=F32)
    up_a = jnp.dot(ya_ref[...], wau_ref[...], preferred_element_type=F32)
    g = g_ref[...].astype(F32)
    merged = jax.nn.sigmoid(g[:, :D]) * up_h + jax.nn.sigmoid(g[:, D:]) * up_a
    h = x_ref[...] + jnp.dot(merged.astype(BF16), wo_ref[...], preferred_element_type=F32)
    h_ref[...] = h
    ms = jnp.mean(h * h, axis=-1, keepdims=True)
    hn = h * lax.rsqrt(ms + EPS) * n2_ref[...]
    hn_ref[...] = hn.astype(BF16)
    hh, hl = _split_bf16(hn)
    lg_ref[...] = (jnp.dot(hh, rwh_ref[...], preferred_element_type=F32)
                   + jnp.dot(hl, rwh_ref[...], preferred_element_type=F32)
                   + jnp.dot(hh, rwl_ref[...], preferred_element_type=F32)) + rb_ref[...]


def _merge(yh, ya, gates, xt, whu, wau, wo, n2g, rw, rb, tm=1024):
    T, D = xt.shape
    W = yh.shape[1]
    E = rw.shape[1]
    rwp = jnp.zeros((D, LOGIT_PAD), F32).at[:, :E].set(rw)
    rwh, rwl = _split_bf16(rwp)
    rbp = jnp.zeros((1, LOGIT_PAD), F32).at[0, :E].set(rb)
    rowblk = lambda w: pl.BlockSpec((tm, w), lambda i: (i, 0))
    const = lambda shape: pl.BlockSpec(shape, lambda i: (0, 0))
    return pl.pallas_call(
        _merge_body,
        grid=(T // tm,),
        in_specs=[rowblk(W), rowblk(W), rowblk(2 * D), rowblk(D),
                  const((W, D)), const((W, D)), const((D, D)), const((1, D)),
                  const((D, LOGIT_PAD)), const((D, LOGIT_PAD)), const((1, LOGIT_PAD))],
        out_specs=[rowblk(D), rowblk(D), rowblk(LOGIT_PAD)],
        out_shape=[jax.ShapeDtypeStruct((T, D), F32), jax.ShapeDtypeStruct((T, D), BF16),
                   jax.ShapeDtypeStruct((T, LOGIT_PAD), F32)],
        compiler_params=_cparams(("parallel",)),
        name="merge",
    )(yh, ya, gates, xt, whu.astype(BF16), wau.astype(BF16), wo.astype(BF16), n2g.reshape(1, D),
      rwh, rwl, rbp)


def _moe_body(be_ref, nused_ref, x_ref, w1g_ref, w1l_ref, b1g_ref, b1l_ref, w2_ref, b2_ref, o_ref):
    i = pl.program_id(0)

    @pl.when(i < nused_ref[0])
    def _():
        x = x_ref[...]
        nt = (((1,), (1,)), ((), ()))
        glu = lax.dot_general(x, w1g_ref[...], nt, preferred_element_type=F32) + b1g_ref[...]
        lin = lax.dot_general(x, w1l_ref[...], nt, preferred_element_type=F32) + b1l_ref[...]
        glu = jnp.minimum(glu, SWIGLU_LIMIT)
        lin = jnp.clip(lin, -SWIGLU_LIMIT, SWIGLU_LIMIT)
        act = glu * jax.nn.sigmoid(SWIGLU_ALPHA * glu) * (lin + 1.0)
        y = jnp.dot(act.astype(BF16), w2_ref[...].astype(BF16),
                    preferred_element_type=F32) + b2_ref[...]
        o_ref[...] = y.astype(o_ref.dtype)

    @pl.when(i >= nused_ref[0])
    def _():
        o_ref[...] = jnp.zeros(o_ref.shape, o_ref.dtype)


def _moe_experts(xg, block_expert, n_used, w1g, w1l, b1g, b1l, w2, b2):
    P, D = xg.shape
    dff = w2.shape[1]
    nb = P // MOE_TM
    wspec = lambda k, n: pl.BlockSpec((None, k, n), lambda i, be, nu: (be[i], 0, 0))
    grid_spec = pltpu.PrefetchScalarGridSpec(
        num_scalar_prefetch=2,
        grid=(nb,),
        in_specs=[
            pl.BlockSpec((MOE_TM, D), lambda i, be, nu: (i, 0)),
            wspec(dff, D), wspec(dff, D), wspec(1, dff), wspec(1, dff),
            wspec(dff, D), wspec(1, D),
        ],
        out_specs=pl.BlockSpec((MOE_TM, D), lambda i, be, nu: (i, 0)),
    )
    return pl.pallas_call(
        _moe_body,
        grid_spec=grid_spec,
        out_shape=jax.ShapeDtypeStruct((P, D), BF16),
        compiler_params=_cparams(("arbitrary",)),
        name="moe_experts",
    )(block_expert, n_used, xg, w1g, w1l, b1g, b1l, w2, b2)


def _prep_w1_body(w_ref, sel_ref, g_ref, l_ref):
    half = g_ref.shape[0]
    picked = lax.dot_general(sel_ref[...], w_ref[...].astype(BF16), (((1,), (1,)), ((), ())),
                             preferred_element_type=F32)
    g_ref[...] = picked[:half].astype(BF16)
    l_ref[...] = picked[half:].astype(BF16)


def _prep_w1(w1, tc=8 * LANES):
    E, D, F2 = w1.shape
    tc = min(tc, F2)
    half = tc // 2
    row = jnp.arange(tc)
    picks = jnp.where(row < half, 2 * row, 2 * (row - half) + 1)
    sel = (picks[:, None] == jnp.arange(tc)[None, :]).astype(BF16)
    out = jax.ShapeDtypeStruct((E, F2 // 2, D), BF16)
    half_spec = pl.BlockSpec((None, half, D), lambda e, j: (e, j, 0))

    def outer(w_hbm, sel_ref, g_hbm, l_hbm):
        pltpu.emit_pipeline(
            lambda w_ref, g_ref, l_ref: _prep_w1_body(w_ref, sel_ref, g_ref, l_ref),
            grid=(E, F2 // tc),
            in_specs=[pl.BlockSpec((None, D, tc), lambda e, j: (e, 0, j),
                                   pipeline_mode=pl.Buffered(3))],
            out_specs=[half_spec, half_spec],
        )(w_hbm, g_hbm, l_hbm)

    return pl.pallas_call(
        outer,
        in_specs=[pl.BlockSpec(memory_space=pl.ANY), pl.BlockSpec(memory_space=pltpu.VMEM)],
        out_specs=[pl.BlockSpec(memory_space=pl.ANY), pl.BlockSpec(memory_space=pl.ANY)],
        out_shape=[out, out],
        compiler_params=pltpu.CompilerParams(vmem_limit_bytes=VMEM_LIMIT),
        name="prep_w1",
    )(w1, sel)


def _combine_body(h_ref, y_ref, g_ref, o_ref):
    acc = h_ref[...]
    g = g_ref[...]
    for k in range(y_ref.shape[0]):
        acc = acc + g[:, k:k + 1] * y_ref[k].astype(F32)
    o_ref[...] = acc


def _combine(h, yk, gates, tm=512):
    T, D = h.shape
    K = yk.shape[0]
    return pl.pallas_call(
        _combine_body,
        grid=(T // tm,),
        in_specs=[pl.BlockSpec((tm, D), lambda i: (i, 0)),
                  pl.BlockSpec((K, tm, D), lambda i: (0, i, 0)),
                  pl.BlockSpec((tm, K), lambda i: (i, 0))],
        out_specs=pl.BlockSpec((tm, D), lambda i: (i, 0)),
        out_shape=jax.ShapeDtypeStruct((T, D), F32),
        compiler_params=_cparams(("parallel",)),
        name="moe_combine",
    )(h, yk, gates)


def _lookup(table, idx):
    n = table.shape[0]
    hit = idx[None, :] == jnp.arange(n, dtype=idx.dtype)[:, None]
    return jnp.sum(jnp.where(hit, table[:, None], 0), axis=0)


def _moe(h, hn_bf16, logits, w1, b1, w2, b2):
    T, D = hn_bf16.shape
    E = w1.shape[0]
    top_val, top_idx = lax.top_k(logits, TOP_K)
    gates = jax.nn.softmax(top_val, axis=-1)
    TK = T * TOP_K
    e_flat = top_idx.reshape(TK).astype(jnp.int32)
    order = jnp.argsort(e_flat).astype(jnp.int32)
    rank = jnp.argsort(order).astype(jnp.int32)
    counts = jnp.sum(jnp.arange(E, dtype=jnp.int32)[:, None] == e_flat[None, :], axis=1,
                     dtype=jnp.int32)
    starts = jnp.cumsum(counts) - counts
    padded = ((counts + MOE_TM - 1) // MOE_TM) * MOE_TM
    pad_ends = jnp.cumsum(padded)
    pad_starts = pad_ends - padded
    nb = (TK + E * (MOE_TM - 1) + MOE_TM - 1) // MOE_TM
    block_start = jnp.arange(nb, dtype=jnp.int32) * MOE_TM
    block_expert = jnp.minimum(jnp.sum(block_start[:, None] >= pad_ends[None, :], axis=1),
                               E - 1).astype(jnp.int32)
    n_used = (pad_ends[-1] // MOE_TM).astype(jnp.int32).reshape(1)
    blk_first = block_start - pad_starts[block_expert]
    within = blk_first[:, None] + jnp.arange(MOE_TM, dtype=jnp.int32)[None, :]
    valid = within < counts[block_expert][:, None]
    sorted_idx = jnp.where(valid, starts[block_expert][:, None] + within, 0).reshape(nb * MOE_TM)
    filler = jnp.arange(nb * MOE_TM, dtype=jnp.int32) % T
    src = jnp.where(valid.reshape(nb * MOE_TM), order[sorted_idx] // TOP_K, filler)
    pos = _lookup(pad_starts - starts, e_flat) + rank
    w1g, w1l = _prep_w1(w1)
    y = _moe_experts(hn_bf16[src], block_expert, n_used, w1g, w1l,
                     b1[:, None, 0::2].astype(F32), b1[:, None, 1::2].astype(F32),
                     w2, b2[:, None, :].astype(F32))
    return _combine(h, y[pos.reshape(T, TOP_K).T], gates)


def _rope_tables(L):
    rows = L // GRID_W
    row = jnp.repeat(jnp.arange(rows, dtype=F32), GRID_W)
    col = jnp.tile(jnp.arange(GRID_W, dtype=F32), rows)
    half = HEAD_DIM // 2
    freqs = ROPE_THETA ** (-jnp.arange(0, half, 2, dtype=F32) / half)
    ang = jnp.concatenate([row[:, None] * freqs, col[:, None] * freqs], axis=-1)
    cos = jnp.repeat(jnp.cos(ang), 2, axis=-1)
    sin = jnp.repeat(jnp.sin(ang), 2, axis=-1)
    sign = jnp.tile(jnp.array([-1.0, 1.0], F32), HEAD_DIM // 2)
    reps = LANES // HEAD_DIM
    return jnp.tile(cos, (1, reps)), jnp.tile(sin * sign, (1, reps))


def kernel(x, norm1_g, w_in, conv_w, conv_b, filt_w1, filt_b1, filt_freq1, filt_w2, filt_b2, filt_freq2, filt_w3, filt_b3, hyena_d, q_norm_g, k_norm_g, w_hyena_up, w_attn_up, w_out, norm2_g, router_w, router_b, expert_w1, expert_b1, expert_w2, expert_b2):
    B, L, D = x.shape
    T = B * L
    depth = w_in.shape[0]
    hw = conv_w.shape[-1] // 3
    aw = N_Q_HEADS * HEAD_DIM
    kvw = N_KV_HEADS * HEAD_DIM
    widths = (3 * hw, aw, 2 * kvw, 2 * D)
    cosf, sinf = _rope_tables(L)
    for l in range(depth):
        xt = x.reshape(T, D)
        gq = jnp.tile(q_norm_g[l].astype(F32), N_Q_HEADS)[None, :]
        gk = jnp.tile(k_norm_g[l].astype(F32), N_KV_HEADS)[None, :]
        z, qr, kr, va, gates = _inproj(xt, norm1_g[l], w_in[l].astype(BF16), widths, gq, gk,
                                       cosf, sinf, L)
        y_hy = _hyena(z.reshape(B, L, 3 * hw), conv_w[l], conv_b[l], filt_w1[l], filt_b1[l],
                      filt_freq1[l], filt_w2[l], filt_b2[l], filt_freq2[l], filt_w3[l], filt_b3[l],
                      hyena_d[l])
        y_at = _attention(qr.reshape(B, L, aw), kr, va.reshape(N_KV_HEADS, B, L, kvw))
        h, hn, logits = _merge(y_hy.reshape(T, hw), y_at.reshape(T, aw), gates, xt,
                               w_hyena_up[l], w_attn_up[l], w_out[l], norm2_g[l],
                               router_w[l], router_b[l])
        x = _moe(h, hn, logits[:, :N_EXPERTS], expert_w1[l], expert_b1[l], expert_w2[l],
                 expert_b2[l]).reshape(B, L, D)
    return x
```
